```python
import jax, jax.numpy as jnp
from jax import lax
import numpy as np

D_MODEL = 2048
BATCH = 8
SEQ = 8192
DEPTH = 1

N_MEM = 256
MLA_HEADS = 16
QK_NOPE_DIM = 128
QK_ROPE_DIM = 64
V_HEAD_DIM = 128
KV_LORA_RANK = 512
QK_HEAD_DIM = QK_NOPE_DIM + QK_ROPE_DIM
SOFTMAX_SCALE = QK_HEAD_DIM ** -0.5
ROPE_THETA = 10000.0
Q_BLOCK = 128
LRU_WIDTH = D_MODEL
LRU_BLOCKS = 16
LRU_BLOCK_DIM = LRU_WIDTH // LRU_BLOCKS
LRU_CONV_WIDTH = 4
LRU_C = 8.0
X_HEADS = 4
X_HEAD_DIM = 128
D_FF = 5632
FFN_CONV_WIDTH = 3
EPS = 1e-6

Q_COLS = MLA_HEADS * QK_HEAD_DIM
IN_SIZES = (Q_COLS, KV_LORA_RANK, QK_ROPE_DIM, LRU_WIDTH, LRU_WIDTH, D_MODEL, D_MODEL)
IN_COLS = sum(IN_SIZES)
IN_SPLIT_IDX = tuple(int(v) for v in np.cumsum(IN_SIZES)[:-1])

kernel_name = "hybrid_mla_rglru_gated_merge_layer"


def rmsnorm(x, g):
    x32 = x.astype(jnp.float32)
    y = x32 * lax.rsqrt(jnp.mean(x32 * x32, axis=-1, keepdims=True) + EPS)
    return (y * g.astype(jnp.float32)).astype(x.dtype)


def rope_tables(positions):
    inv_freq = ROPE_THETA ** (-jnp.arange(0, QK_ROPE_DIM, 2, dtype=jnp.float32) / QK_ROPE_DIM)
    ang = positions.astype(jnp.float32)[..., None] * inv_freq
    return jnp.cos(ang), jnp.sin(ang)


def apply_rope(x, cos, sin):
    x32 = x.astype(jnp.float32)
    x1, x2 = jnp.split(x32, 2, axis=-1)
    out = jnp.concatenate([x1 * cos - x2 * sin, x2 * cos + x1 * sin], axis=-1)
    return out.astype(x.dtype)


def causal_dwconv(x, w, b):
    k_width = w.shape[0]
    s = x.shape[1]
    xp = jnp.pad(x, ((0, 0), (k_width - 1, 0), (0, 0)))
    acc = b
    for k in range(k_width):
        acc = acc + xp[:, k:k + s] * w[k]
    return acc


def mla_attention(q_in, ckv, k_rope, positions, g_ckv, w_ukv):
    b, s, _ = q_in.shape
    cos, sin = rope_tables(positions)
    q = q_in.reshape(b, s, MLA_HEADS, QK_HEAD_DIM)
    q_nope, q_rope = q[..., :QK_NOPE_DIM], q[..., QK_NOPE_DIM:]
    q_rope = apply_rope(q_rope, cos[:, :, None], sin[:, :, None])
    q = jnp.concatenate([q_nope, q_rope], axis=-1)
    k_rope = apply_rope(k_rope, cos, sin)
    kv = (rmsnorm(ckv, g_ckv) @ w_ukv).reshape(b, s, MLA_HEADS, QK_NOPE_DIM + V_HEAD_DIM)
    k_nope, v = kv[..., :QK_NOPE_DIM], kv[..., QK_NOPE_DIM:]
    k = jnp.concatenate([k_nope, jnp.broadcast_to(k_rope[:, :, None], (b, s, MLA_HEADS, QK_ROPE_DIM))], axis=-1)
    n_blocks = s // Q_BLOCK
    q_blocks = q.reshape(b, n_blocks, Q_BLOCK, MLA_HEADS, QK_HEAD_DIM).transpose(1, 0, 2, 3, 4)
    starts = jnp.arange(n_blocks, dtype=jnp.int32) * Q_BLOCK
    k_idx = jnp.arange(s, dtype=jnp.int32)

    def one_block(args):
        qb, start = args
        sc = jnp.einsum('bqhd,bkhd->bhqk', qb, k).astype(jnp.float32) * SOFTMAX_SCALE
        q_idx = start + jnp.arange(Q_BLOCK, dtype=jnp.int32)
        mask = q_idx[:, None] >= k_idx[None, :]
        p = jax.nn.softmax(jnp.where(mask, sc, -jnp.inf), axis=-1)
        return jnp.einsum('bhqk,bkhd->bqhd', p.astype(v.dtype), v)

    o = lax.map(one_block, (q_blocks, starts))
    return o.transpose(1, 0, 2, 3, 4).reshape(b, s, MLA_HEADS * V_HEAD_DIM)


def rg_lru(x, w_rg, b_rg, w_ig, b_ig, lam):
    b, s, w = x.shape
    xh = x.reshape(b, s, LRU_BLOCKS, LRU_BLOCK_DIM)
    r = jax.nn.sigmoid(jnp.einsum('bshi,hij->bshj', xh, w_rg) + b_rg).reshape(b, s, w)
    i = jax.nn.sigmoid(jnp.einsum('bshi,hij->bshj', xh, w_ig) + b_ig).reshape(b, s, w)
    log_a = -LRU_C * r.astype(jnp.float32) * jax.nn.softplus(-lam.astype(jnp.float32))
    a = jnp.exp(log_a)
    u = jnp.sqrt(-jnp.expm1(2.0 * log_a)) * (i * x).astype(jnp.float32)

    def combine(left, right):
        a1, b1 = left
        a2, b2 = right
        return a1 * a2, a2 * b1 + b2

    _, h = lax.associative_scan(combine, (a, u), axis=1)
    return h.astype(x.dtype)


def _fwd_setup_inputs(seed: int = 0) -> dict:
    key = jax.random.key(seed)
    ks = jax.random.split(key, 32)
    f32 = jnp.float32

    def nrm(k, shape, fan_in):
        return jax.random.normal(k, shape, f32) * (fan_in ** -0.5)

    def gain(k, shape):
        return 1.0 + 0.05 * jax.random.normal(k, shape, f32)

    def bias(k, shape):
        return 0.01 * jax.random.normal(k, shape, f32)

    L = DEPTH
    x = jax.random.normal(ks[0], (BATCH, SEQ, D_MODEL), f32)
    mem = jax.random.normal(ks[1], (BATCH, N_MEM, D_MODEL), f32)
    offset = jax.random.randint(ks[2], (BATCH, 1), 0, 1024, dtype=jnp.int32)
    positions = offset + jnp.arange(SEQ, dtype=jnp.int32)[None, :]
    a0 = jax.random.uniform(ks[3], (L, LRU_WIDTH), f32, 0.9, 0.999)
    sig = a0 ** (1.0 / LRU_C)
    lru_lambda = jnp.log(sig) - jnp.log1p(-sig)
    return {
        "x": x,
        "mem": mem,
        "positions": positions,
        "g_pre_mix": gain(ks[4], (L, D_MODEL)),
        "g_post_mix": gain(ks[5], (L, D_MODEL)),
        "w_in": nrm(ks[6], (L, D_MODEL, IN_COLS), D_MODEL),
        "g_ckv": gain(ks[7], (L, KV_LORA_RANK)),
        "w_ukv": nrm(ks[8], (L, KV_LORA_RANK, MLA_HEADS * (QK_NOPE_DIM + V_HEAD_DIM)), KV_LORA_RANK),
        "w_o_mla": nrm(ks[9], (L, MLA_HEADS * V_HEAD_DIM, D_MODEL), MLA_HEADS * V_HEAD_DIM),
        "w_conv_lru": nrm(ks[10], (L, LRU_CONV_WIDTH, LRU_WIDTH), LRU_CONV_WIDTH),
        "b_conv_lru": bias(ks[11], (L, LRU_WIDTH)),
        "w_rg": nrm(ks[12], (L, LRU_BLOCKS, LRU_BLOCK_DIM, LRU_BLOCK_DIM), LRU_BLOCK_DIM),
        "b_rg": bias(ks[13], (L, LRU_BLOCKS, LRU_BLOCK_DIM)),
        "w_ig": nrm(ks[14], (L, LRU_BLOCKS, LRU_BLOCK_DIM, LRU_BLOCK_DIM), LRU_BLOCK_DIM),
        "b_ig": bias(ks[15], (L, LRU_BLOCKS, LRU_BLOCK_DIM)),
        "lru_lambda": lru_lambda,
        "w_o_lru": nrm(ks[16], (L, LRU_WIDTH, D_MODEL), LRU_WIDTH),
        "w_out": nrm(ks[17], (L, D_MODEL, D_MODEL), D_MODEL),
        "g_pre_x": gain(ks[18], (L, D_MODEL)),
        "g_post_x": gain(ks[19], (L, D_MODEL)),
        "g_mem": gain(ks[20], (L, D_MODEL)),
        "w_cq": nrm(ks[21], (L, D_MODEL, X_HEADS * X_HEAD_DIM), D_MODEL),
        "w_ck": nrm(ks[22], (L, D_MODEL, X_HEADS * X_HEAD_DIM), D_MODEL),
        "w_cv": nrm(ks[23], (L, D_MODEL, X_HEADS * X_HEAD_DIM), D_MODEL),
        "w_co": nrm(ks[24], (L, X_HEADS * X_HEAD_DIM, D_MODEL), X_HEADS * X_HEAD_DIM),
        "g_pre_ffn": gain(ks[25], (L, D_MODEL)),
        "g_post_ffn": gain(ks[26], (L, D_MODEL)),
        "w_up": nrm(ks[27], (L, D_MODEL, 2 * D_FF), D_MODEL),
        "w_fconv": nrm(ks[28], (L, FFN_CONV_WIDTH, 2 * D_FF), FFN_CONV_WIDTH),
        "b_fconv": bias(ks[29], (L, 2 * D_FF)),
        "w_down": nrm(ks[30], (L, D_FF, D_MODEL), D_FF),
    }


def _fwd_reference(x, mem, positions, g_pre_mix, g_post_mix, w_in, g_ckv, w_ukv, w_o_mla,
              w_conv_lru, b_conv_lru, w_rg, b_rg, w_ig, b_ig, lru_lambda, w_o_lru, w_out,
              g_pre_x, g_post_x, g_mem, w_cq, w_ck, w_cv, w_co,
              g_pre_ffn, g_post_ffn, w_up, w_fconv, b_fconv, w_down):
    b, s, _ = x.shape
    m = mem.shape[1]
    for l in range(DEPTH):
        h = rmsnorm(x, g_pre_mix[l])
        proj = h @ w_in[l]
        q_in, ckv, k_rope, lru_x, lru_y, gate_mla, gate_lru = jnp.split(proj, IN_SPLIT_IDX, axis=-1)
        o_mla = mla_attention(q_in, ckv, k_rope, positions, g_ckv[l], w_ukv[l]) @ w_o_mla[l]
        xc = causal_dwconv(lru_x, w_conv_lru[l], b_conv_lru[l])
        hr = rg_lru(xc, w_rg[l], b_rg[l], w_ig[l], b_ig[l], lru_lambda[l])
        o_lru = (hr * jax.nn.gelu(lru_y, approximate=True)) @ w_o_lru[l]
        merged = jax.nn.sigmoid(gate_mla) * o_mla + jax.nn.sigmoid(gate_lru) * o_lru
        x = x + rmsnorm(merged @ w_out[l], g_post_mix[l])
        h = rmsnorm(x, g_pre_x[l])
        mn = rmsnorm(mem, g_mem[l])
        cq = (h @ w_cq[l]).reshape(b, s, X_HEADS, X_HEAD_DIM)
        ck = (mn @ w_ck[l]).reshape(b, m, X_HEADS, X_HEAD_DIM)
        cv = (mn @ w_cv[l]).reshape(b, m, X_HEADS, X_HEAD_DIM)
        sc = jnp.einsum('bshd,bmhd->bhsm', cq, ck).astype(jnp.float32) * (X_HEAD_DIM ** -0.5)
        p = jax.nn.softmax(sc, axis=-1).astype(cv.dtype)
        co = jnp.einsum('bhsm,bmhd->bshd', p, cv).reshape(b, s, X_HEADS * X_HEAD_DIM)
        x = x + rmsnorm(co @ w_co[l], g_post_x[l])
        h = rmsnorm(x, g_pre_ffn[l])
        up = causal_dwconv(h @ w_up[l], w_fconv[l], b_fconv[l])
        gate, val = jnp.split(up, 2, axis=-1)
        x = x + rmsnorm((jax.nn.gelu(gate, approximate=True) * val) @ w_down[l], g_post_ffn[l])
    return x


import jax as _jax
import jax.numpy as _jnp

TWIN_FORMAT = 'train_step'
FWD_PARAMS = ['x', 'mem', 'positions', 'g_pre_mix', 'g_post_mix', 'w_in', 'g_ckv', 'w_ukv', 'w_o_mla', 'w_conv_lru', 'b_conv_lru', 'w_rg', 'b_rg', 'w_ig', 'b_ig', 'lru_lambda', 'w_o_lru', 'w_out', 'g_pre_x', 'g_post_x', 'g_mem', 'w_cq', 'w_ck', 'w_cv', 'w_co', 'g_pre_ffn', 'g_post_ffn', 'w_up', 'w_fconv', 'b_fconv', 'w_down']
TWIN_WEIGHTS = ['g_pre_mix', 'g_post_mix', 'w_in', 'g_ckv', 'w_ukv', 'w_o_mla', 'w_conv_lru', 'b_conv_lru', 'w_rg', 'b_rg', 'w_ig', 'b_ig', 'lru_lambda', 'w_o_lru', 'w_out', 'g_pre_x', 'g_post_x', 'g_mem', 'w_cq', 'w_ck', 'w_cv', 'w_co', 'g_pre_ffn', 'g_post_ffn', 'w_up', 'w_fconv', 'b_fconv', 'w_down']
TWIN_DIFF_INPUT = 'x'
TWIN_INPUTS = ['x', 'mem', 'positions', 'g_pre_mix', 'g_post_mix', 'w_in', 'g_ckv', 'w_ukv', 'w_o_mla', 'w_conv_lru', 'b_conv_lru', 'w_rg', 'b_rg', 'w_ig', 'b_ig', 'lru_lambda', 'w_o_lru', 'w_out', 'g_pre_x', 'g_post_x', 'g_mem', 'w_cq', 'w_ck', 'w_cv', 'w_co', 'g_pre_ffn', 'g_post_ffn', 'w_up', 'w_fconv', 'b_fconv', 'w_down', 'loss_target', 'm_g_pre_mix', 'm_g_post_mix', 'm_w_in', 'm_g_ckv', 'm_w_ukv', 'm_w_o_mla', 'm_w_conv_lru', 'm_b_conv_lru', 'm_w_rg', 'm_b_rg', 'm_w_ig', 'm_b_ig', 'm_lru_lambda', 'm_w_o_lru', 'm_w_out', 'm_g_pre_x', 'm_g_post_x', 'm_g_mem', 'm_w_cq', 'm_w_ck', 'm_w_cv', 'm_w_co', 'm_g_pre_ffn', 'm_g_post_ffn', 'm_w_up', 'm_w_fconv', 'm_b_fconv', 'm_w_down', 'v_g_pre_mix', 'v_g_post_mix', 'v_w_in', 'v_g_ckv', 'v_w_ukv', 'v_w_o_mla', 'v_w_conv_lru', 'v_b_conv_lru', 'v_w_rg', 'v_b_rg', 'v_w_ig', 'v_b_ig', 'v_lru_lambda', 'v_w_o_lru', 'v_w_out', 'v_g_pre_x', 'v_g_post_x', 'v_g_mem', 'v_w_cq', 'v_w_ck', 'v_w_cv', 'v_w_co', 'v_g_pre_ffn', 'v_g_post_ffn', 'v_w_up', 'v_w_fconv', 'v_b_fconv', 'v_w_down']
TWIN_OUTPUTS = ['loss', 'grad_x', 'grad_g_pre_mix', 'grad_g_post_mix', 'grad_w_in', 'grad_g_ckv', 'grad_w_ukv', 'grad_w_o_mla', 'grad_w_conv_lru', 'grad_b_conv_lru', 'grad_w_rg', 'grad_b_rg', 'grad_w_ig', 'grad_b_ig', 'grad_lru_lambda', 'grad_w_o_lru', 'grad_w_out', 'grad_g_pre_x', 'grad_g_post_x', 'grad_g_mem', 'grad_w_cq', 'grad_w_ck', 'grad_w_cv', 'grad_w_co', 'grad_g_pre_ffn', 'grad_g_post_ffn', 'grad_w_up', 'grad_w_fconv', 'grad_b_fconv', 'grad_w_down', 'delta_g_pre_mix', 'delta_g_post_mix', 'delta_w_in', 'delta_g_ckv', 'delta_w_ukv', 'delta_w_o_mla', 'delta_w_conv_lru', 'delta_b_conv_lru', 'delta_w_rg', 'delta_b_rg', 'delta_w_ig', 'delta_b_ig', 'delta_lru_lambda', 'delta_w_o_lru', 'delta_w_out', 'delta_g_pre_x', 'delta_g_post_x', 'delta_g_mem', 'delta_w_cq', 'delta_w_ck', 'delta_w_cv', 'delta_w_co', 'delta_g_pre_ffn', 'delta_g_post_ffn', 'delta_w_up', 'delta_w_fconv', 'delta_b_fconv', 'delta_w_down', 'new_m_g_pre_mix', 'new_m_g_post_mix', 'new_m_w_in', 'new_m_g_ckv', 'new_m_w_ukv', 'new_m_w_o_mla', 'new_m_w_conv_lru', 'new_m_b_conv_lru', 'new_m_w_rg', 'new_m_b_rg', 'new_m_w_ig', 'new_m_b_ig', 'new_m_lru_lambda', 'new_m_w_o_lru', 'new_m_w_out', 'new_m_g_pre_x', 'new_m_g_post_x', 'new_m_g_mem', 'new_m_w_cq', 'new_m_w_ck', 'new_m_w_cv', 'new_m_w_co', 'new_m_g_pre_ffn', 'new_m_g_post_ffn', 'new_m_w_up', 'new_m_w_fconv', 'new_m_b_fconv', 'new_m_w_down', 'new_v_g_pre_mix', 'new_v_g_post_mix', 'new_v_w_in', 'new_v_g_ckv', 'new_v_w_ukv', 'new_v_w_o_mla', 'new_v_w_conv_lru', 'new_v_b_conv_lru', 'new_v_w_rg', 'new_v_b_rg', 'new_v_w_ig', 'new_v_b_ig', 'new_v_lru_lambda', 'new_v_w_o_lru', 'new_v_w_out', 'new_v_g_pre_x', 'new_v_g_post_x', 'new_v_g_mem', 'new_v_w_cq', 'new_v_w_ck', 'new_v_w_cv', 'new_v_w_co', 'new_v_g_pre_ffn', 'new_v_g_post_ffn', 'new_v_w_up', 'new_v_w_fconv', 'new_v_b_fconv', 'new_v_w_down']
TWIN_LEAF_KINDS = {'loss': 'loss', 'grad_x': 'grad_x', 'grad_g_pre_mix': 'grad_w', 'grad_g_post_mix': 'grad_w', 'grad_w_in': 'grad_w', 'grad_g_ckv': 'grad_w', 'grad_w_ukv': 'grad_w', 'grad_w_o_mla': 'grad_w', 'grad_w_conv_lru': 'grad_w', 'grad_b_conv_lru': 'grad_w', 'grad_w_rg': 'grad_w', 'grad_b_rg': 'grad_w', 'grad_w_ig': 'grad_w', 'grad_b_ig': 'grad_w', 'grad_lru_lambda': 'grad_w', 'grad_w_o_lru': 'grad_w', 'grad_w_out': 'grad_w', 'grad_g_pre_x': 'grad_w', 'grad_g_post_x': 'grad_w', 'grad_g_mem': 'grad_w', 'grad_w_cq': 'grad_w', 'grad_w_ck': 'grad_w', 'grad_w_cv': 'grad_w', 'grad_w_co': 'grad_w', 'grad_g_pre_ffn': 'grad_w', 'grad_g_post_ffn': 'grad_w', 'grad_w_up': 'grad_w', 'grad_w_fconv': 'grad_w', 'grad_b_fconv': 'grad_w', 'grad_w_down': 'grad_w', 'delta_g_pre_mix': 'delta_w', 'delta_g_post_mix': 'delta_w', 'delta_w_in': 'delta_w', 'delta_g_ckv': 'delta_w', 'delta_w_ukv': 'delta_w', 'delta_w_o_mla': 'delta_w', 'delta_w_conv_lru': 'delta_w', 'delta_b_conv_lru': 'delta_w', 'delta_w_rg': 'delta_w', 'delta_b_rg': 'delta_w', 'delta_w_ig': 'delta_w', 'delta_b_ig': 'delta_w', 'delta_lru_lambda': 'delta_w', 'delta_w_o_lru': 'delta_w', 'delta_w_out': 'delta_w', 'delta_g_pre_x': 'delta_w', 'delta_g_post_x': 'delta_w', 'delta_g_mem': 'delta_w', 'delta_w_cq': 'delta_w', 'delta_w_ck': 'delta_w', 'delta_w_cv': 'delta_w', 'delta_w_co': 'delta_w', 'delta_g_pre_ffn': 'delta_w', 'delta_g_post_ffn': 'delta_w', 'delta_w_up': 'delta_w', 'delta_w_fconv': 'delta_w', 'delta_b_fconv': 'delta_w', 'delta_w_down': 'delta_w', 'new_m_g_pre_mix': 'new_m', 'new_m_g_post_mix': 'new_m', 'new_m_w_in': 'new_m', 'new_m_g_ckv': 'new_m', 'new_m_w_ukv': 'new_m', 'new_m_w_o_mla': 'new_m', 'new_m_w_conv_lru': 'new_m', 'new_m_b_conv_lru': 'new_m', 'new_m_w_rg': 'new_m', 'new_m_b_rg': 'new_m', 'new_m_w_ig': 'new_m', 'new_m_b_ig': 'new_m', 'new_m_lru_lambda': 'new_m', 'new_m_w_o_lru': 'new_m', 'new_m_w_out': 'new_m', 'new_m_g_pre_x': 'new_m', 'new_m_g_post_x': 'new_m', 'new_m_g_mem': 'new_m', 'new_m_w_cq': 'new_m', 'new_m_w_ck': 'new_m', 'new_m_w_cv': 'new_m', 'new_m_w_co': 'new_m', 'new_m_g_pre_ffn': 'new_m', 'new_m_g_post_ffn': 'new_m', 'new_m_w_up': 'new_m', 'new_m_w_fconv': 'new_m', 'new_m_b_fconv': 'new_m', 'new_m_w_down': 'new_m', 'new_v_g_pre_mix': 'new_v', 'new_v_g_post_mix': 'new_v', 'new_v_w_in': 'new_v', 'new_v_g_ckv': 'new_v', 'new_v_w_ukv': 'new_v', 'new_v_w_o_mla': 'new_v', 'new_v_w_conv_lru': 'new_v', 'new_v_b_conv_lru': 'new_v', 'new_v_w_rg': 'new_v', 'new_v_b_rg': 'new_v', 'new_v_w_ig': 'new_v', 'new_v_b_ig': 'new_v', 'new_v_lru_lambda': 'new_v', 'new_v_w_o_lru': 'new_v', 'new_v_w_out': 'new_v', 'new_v_g_pre_x': 'new_v', 'new_v_g_post_x': 'new_v', 'new_v_g_mem': 'new_v', 'new_v_w_cq': 'new_v', 'new_v_w_ck': 'new_v', 'new_v_w_cv': 'new_v', 'new_v_w_co': 'new_v', 'new_v_g_pre_ffn': 'new_v', 'new_v_g_post_ffn': 'new_v', 'new_v_w_up': 'new_v', 'new_v_w_fconv': 'new_v', 'new_v_b_fconv': 'new_v', 'new_v_w_down': 'new_v'}


def _forward(args):
    return _fwd_reference(*[args[k] for k in FWD_PARAMS])


def _output_shape():
    def fwd():
        inp = _fwd_setup_inputs(0)
        return _fwd_reference(*[inp[k] for k in FWD_PARAMS])
    out = _jax.eval_shape(fwd)
    return out.shape, out.dtype

N_MICROBATCH = 1
ADAM_LR = 0.001
ADAM_B1 = 0.9
ADAM_B2 = 0.999
ADAM_EPS = 1e-08
ADAM_WD = 0.01
ADAM_STEP = 10
PER_EXAMPLE_BATCH_AXIS = {'x': 0, 'mem': 0, 'positions': 0, 'loss_target': 0}
SHARED_INPUTS = []
_WEIGHT_DTYPES = {'g_pre_mix': _jnp.float32, 'g_post_mix': _jnp.float32, 'w_in': _jnp.float32, 'g_ckv': _jnp.float32, 'w_ukv': _jnp.float32, 'w_o_mla': _jnp.float32, 'w_conv_lru': _jnp.float32, 'b_conv_lru': _jnp.float32, 'w_rg': _jnp.float32, 'b_rg': _jnp.float32, 'w_ig': _jnp.float32, 'b_ig': _jnp.float32, 'lru_lambda': _jnp.float32, 'w_o_lru': _jnp.float32, 'w_out': _jnp.float32, 'g_pre_x': _jnp.float32, 'g_post_x': _jnp.float32, 'g_mem': _jnp.float32, 'w_cq': _jnp.float32, 'w_ck': _jnp.float32, 'w_cv': _jnp.float32, 'w_co': _jnp.float32, 'g_pre_ffn': _jnp.float32, 'g_post_ffn': _jnp.float32, 'w_up': _jnp.float32, 'w_fconv': _jnp.float32, 'b_fconv': _jnp.float32, 'w_down': _jnp.float32}
MOMENT_SCALE = {'g_pre_mix': 8.164521e-01, 'g_post_mix': 3.190852e+01, 'w_in': 3.391416e-01, 'g_ckv': 7.790879e-01, 'w_ukv': 2.684048e-01, 'w_o_mla': 3.536939e-01, 'w_conv_lru': 1.078353e+00, 'b_conv_lru': 2.498836e+01, 'w_rg': 6.460811e-01, 'b_rg': 4.234392e-01, 'w_ig': 1.223959e+00, 'b_ig': 3.265342e-01, 'lru_lambda': 6.210790e-01, 'w_o_lru': 1.310608e+00, 'w_out': 1.179556e+00, 'g_pre_x': 4.614711e-01, 'g_post_x': 3.260239e+01, 'g_mem': 2.011587e+00, 'w_cq': 9.520511e-01, 'w_ck': 9.610719e-01, 'w_cv': 3.936324e+00, 'w_co': 2.000725e+00, 'g_pre_ffn': 1.251697e+00, 'g_post_ffn': 3.198357e+01, 'w_up': 5.433915e-01, 'w_fconv': 6.296003e-01, 'b_fconv': 2.033303e+00, 'w_down': 1.193666e+00}


def _to_microbatches(a, axis):
    t = _jnp.moveaxis(a, axis, 0)
    t = t.reshape((N_MICROBATCH, t.shape[0] // N_MICROBATCH) + t.shape[1:])
    return _jnp.moveaxis(t, 1, axis + 1)


def setup_inputs(seed: int = 0) -> dict:
    inp = _fwd_setup_inputs(seed)
    key = _jax.random.fold_in(_jax.random.key(seed), 7919)
    shape, _ = _output_shape()
    out = dict(inp)
    out["loss_target"] = _jax.random.normal(_jax.random.fold_in(key, 0), shape, _jnp.float32)
    for i, name in enumerate(TWIN_WEIGHTS):
        w = inp[name].astype(_jnp.float32)
        if MOMENT_SCALE is None:
            s = _jnp.sqrt(_jnp.mean(_jnp.square(w)) + 1e-30)
        else:
            s = MOMENT_SCALE[name]
        km, kv = _jax.random.split(_jax.random.fold_in(key, i + 1))
        out[name] = w
        out["m_" + name] = s * _jax.random.normal(km, w.shape, _jnp.float32)
        out["v_" + name] = (s * s) * _jax.random.uniform(kv, w.shape, _jnp.float32, 0.5, 1.5)
    if N_MICROBATCH > 1:
        for name, axis in PER_EXAMPLE_BATCH_AXIS.items():
            out[name] = _to_microbatches(out[name], axis)
    return {'x': out['x'], 'mem': out['mem'], 'positions': out['positions'], 'g_pre_mix': out['g_pre_mix'], 'g_post_mix': out['g_post_mix'], 'w_in': out['w_in'], 'g_ckv': out['g_ckv'], 'w_ukv': out['w_ukv'], 'w_o_mla': out['w_o_mla'], 'w_conv_lru': out['w_conv_lru'], 'b_conv_lru': out['b_conv_lru'], 'w_rg': out['w_rg'], 'b_rg': out['b_rg'], 'w_ig': out['w_ig'], 'b_ig': out['b_ig'], 'lru_lambda': out['lru_lambda'], 'w_o_lru': out['w_o_lru'], 'w_out': out['w_out'], 'g_pre_x': out['g_pre_x'], 'g_post_x': out['g_post_x'], 'g_mem': out['g_mem'], 'w_cq': out['w_cq'], 'w_ck': out['w_ck'], 'w_cv': out['w_cv'], 'w_co': out['w_co'], 'g_pre_ffn': out['g_pre_ffn'], 'g_post_ffn': out['g_post_ffn'], 'w_up': out['w_up'], 'w_fconv': out['w_fconv'], 'b_fconv': out['b_fconv'], 'w_down': out['w_down'], 'loss_target': out['loss_target'], 'm_g_pre_mix': out['m_g_pre_mix'], 'm_g_post_mix': out['m_g_post_mix'], 'm_w_in': out['m_w_in'], 'm_g_ckv': out['m_g_ckv'], 'm_w_ukv': out['m_w_ukv'], 'm_w_o_mla': out['m_w_o_mla'], 'm_w_conv_lru': out['m_w_conv_lru'], 'm_b_conv_lru': out['m_b_conv_lru'], 'm_w_rg': out['m_w_rg'], 'm_b_rg': out['m_b_rg'], 'm_w_ig': out['m_w_ig'], 'm_b_ig': out['m_b_ig'], 'm_lru_lambda': out['m_lru_lambda'], 'm_w_o_lru': out['m_w_o_lru'], 'm_w_out': out['m_w_out'], 'm_g_pre_x': out['m_g_pre_x'], 'm_g_post_x': out['m_g_post_x'], 'm_g_mem': out['m_g_mem'], 'm_w_cq': out['m_w_cq'], 'm_w_ck': out['m_w_ck'], 'm_w_cv': out['m_w_cv'], 'm_w_co': out['m_w_co'], 'm_g_pre_ffn': out['m_g_pre_ffn'], 'm_g_post_ffn': out['m_g_post_ffn'], 'm_w_up': out['m_w_up'], 'm_w_fconv': out['m_w_fconv'], 'm_b_fconv': out['m_b_fconv'], 'm_w_down': out['m_w_down'], 'v_g_pre_mix': out['v_g_pre_mix'], 'v_g_post_mix': out['v_g_post_mix'], 'v_w_in': out['v_w_in'], 'v_g_ckv': out['v_g_ckv'], 'v_w_ukv': out['v_w_ukv'], 'v_w_o_mla': out['v_w_o_mla'], 'v_w_conv_lru': out['v_w_conv_lru'], 'v_b_conv_lru': out['v_b_conv_lru'], 'v_w_rg': out['v_w_rg'], 'v_b_rg': out['v_b_rg'], 'v_w_ig': out['v_w_ig'], 'v_b_ig': out['v_b_ig'], 'v_lru_lambda': out['v_lru_lambda'], 'v_w_o_lru': out['v_w_o_lru'], 'v_w_out': out['v_w_out'], 'v_g_pre_x': out['v_g_pre_x'], 'v_g_post_x': out['v_g_post_x'], 'v_g_mem': out['v_g_mem'], 'v_w_cq': out['v_w_cq'], 'v_w_ck': out['v_w_ck'], 'v_w_cv': out['v_w_cv'], 'v_w_co': out['v_w_co'], 'v_g_pre_ffn': out['v_g_pre_ffn'], 'v_g_post_ffn': out['v_g_post_ffn'], 'v_w_up': out['v_w_up'], 'v_w_fconv': out['v_w_fconv'], 'v_b_fconv': out['v_b_fconv'], 'v_w_down': out['v_w_down']}


def _loss(weights, diff, rest, loss_target):
    with _jax.named_scope("forward"):
        args = {**rest, TWIN_DIFF_INPUT: diff, **{k: w.astype(_WEIGHT_DTYPES[k]) for k, w in weights.items()}}
        y = _forward(args)
    with _jax.named_scope("loss_head"):
        err = _jnp.square(y.astype(_jnp.float32) - loss_target)
        return 0.5 * _jnp.sum(_jnp.mean(err, axis=-1)) if err.ndim else 0.5 * err


def _adamw(w, g, m, v):
    m = ADAM_B1 * m + (1.0 - ADAM_B1) * g
    v = ADAM_B2 * v + (1.0 - ADAM_B2) * _jnp.square(g)
    m_hat = m / (1.0 - ADAM_B1 ** ADAM_STEP)
    v_hat = v / (1.0 - ADAM_B2 ** ADAM_STEP)
    delta = -ADAM_LR * (m_hat / (_jnp.sqrt(v_hat) + ADAM_EPS) + ADAM_WD * w)
    return delta, m, v


def reference(x, mem, positions, g_pre_mix, g_post_mix, w_in, g_ckv, w_ukv, w_o_mla, w_conv_lru, b_conv_lru, w_rg, b_rg, w_ig, b_ig, lru_lambda, w_o_lru, w_out, g_pre_x, g_post_x, g_mem, w_cq, w_ck, w_cv, w_co, g_pre_ffn, g_post_ffn, w_up, w_fconv, b_fconv, w_down, loss_target, m_g_pre_mix, m_g_post_mix, m_w_in, m_g_ckv, m_w_ukv, m_w_o_mla, m_w_conv_lru, m_b_conv_lru, m_w_rg, m_b_rg, m_w_ig, m_b_ig, m_lru_lambda, m_w_o_lru, m_w_out, m_g_pre_x, m_g_post_x, m_g_mem, m_w_cq, m_w_ck, m_w_cv, m_w_co, m_g_pre_ffn, m_g_post_ffn, m_w_up, m_w_fconv, m_b_fconv, m_w_down, v_g_pre_mix, v_g_post_mix, v_w_in, v_g_ckv, v_w_ukv, v_w_o_mla, v_w_conv_lru, v_b_conv_lru, v_w_rg, v_b_rg, v_w_ig, v_b_ig, v_lru_lambda, v_w_o_lru, v_w_out, v_g_pre_x, v_g_post_x, v_g_mem, v_w_cq, v_w_ck, v_w_cv, v_w_co, v_g_pre_ffn, v_g_post_ffn, v_w_up, v_w_fconv, v_b_fconv, v_w_down):
    given = dict(x=x, mem=mem, positions=positions, g_pre_mix=g_pre_mix, g_post_mix=g_post_mix, w_in=w_in, g_ckv=g_ckv, w_ukv=w_ukv, w_o_mla=w_o_mla, w_conv_lru=w_conv_lru, b_conv_lru=b_conv_lru, w_rg=w_rg, b_rg=b_rg, w_ig=w_ig, b_ig=b_ig, lru_lambda=lru_lambda, w_o_lru=w_o_lru, w_out=w_out, g_pre_x=g_pre_x, g_post_x=g_post_x, g_mem=g_mem, w_cq=w_cq, w_ck=w_ck, w_cv=w_cv, w_co=w_co, g_pre_ffn=g_pre_ffn, g_post_ffn=g_post_ffn, w_up=w_up, w_fconv=w_fconv, b_fconv=b_fconv, w_down=w_down, loss_target=loss_target, m_g_pre_mix=m_g_pre_mix, m_g_post_mix=m_g_post_mix, m_w_in=m_w_in, m_g_ckv=m_g_ckv, m_w_ukv=m_w_ukv, m_w_o_mla=m_w_o_mla, m_w_conv_lru=m_w_conv_lru, m_b_conv_lru=m_b_conv_lru, m_w_rg=m_w_rg, m_b_rg=m_b_rg, m_w_ig=m_w_ig, m_b_ig=m_b_ig, m_lru_lambda=m_lru_lambda, m_w_o_lru=m_w_o_lru, m_w_out=m_w_out, m_g_pre_x=m_g_pre_x, m_g_post_x=m_g_post_x, m_g_mem=m_g_mem, m_w_cq=m_w_cq, m_w_ck=m_w_ck, m_w_cv=m_w_cv, m_w_co=m_w_co, m_g_pre_ffn=m_g_pre_ffn, m_g_post_ffn=m_g_post_ffn, m_w_up=m_w_up, m_w_fconv=m_w_fconv, m_b_fconv=m_b_fconv, m_w_down=m_w_down, v_g_pre_mix=v_g_pre_mix, v_g_post_mix=v_g_post_mix, v_w_in=v_w_in, v_g_ckv=v_g_ckv, v_w_ukv=v_w_ukv, v_w_o_mla=v_w_o_mla, v_w_conv_lru=v_w_conv_lru, v_b_conv_lru=v_b_conv_lru, v_w_rg=v_w_rg, v_b_rg=v_b_rg, v_w_ig=v_w_ig, v_b_ig=v_b_ig, v_lru_lambda=v_lru_lambda, v_w_o_lru=v_w_o_lru, v_w_out=v_w_out, v_g_pre_x=v_g_pre_x, v_g_post_x=v_g_post_x, v_g_mem=v_g_mem, v_w_cq=v_w_cq, v_w_ck=v_w_ck, v_w_cv=v_w_cv, v_w_co=v_w_co, v_g_pre_ffn=v_g_pre_ffn, v_g_post_ffn=v_g_post_ffn, v_w_up=v_w_up, v_w_fconv=v_w_fconv, v_b_fconv=v_b_fconv, v_w_down=v_w_down)
    weights = {n: given[n] for n in TWIN_WEIGHTS}
    shared = {n: given[n] for n in SHARED_INPUTS}
    per_example = {n: given[n] for n in ['x', 'mem', 'positions']}
    grad_fn = _jax.value_and_grad(_loss, argnums=(0, 1))

    def one_microbatch(ex, loss_target):
        ex = dict(ex)
        diff = ex.pop(TWIN_DIFF_INPUT)
        return grad_fn(weights, diff, {**shared, **ex}, loss_target)

    if N_MICROBATCH == 1:
        loss, (grad_w, grad_x) = one_microbatch(per_example, given["loss_target"])
    else:
        def body(carry, xs):
            loss_sum, grad_sum = carry
            l_k, (gw_k, gx_k) = one_microbatch(xs[0], xs[1])
            with _jax.named_scope("update"):
                return (loss_sum + l_k, _jax.tree.map(_jnp.add, grad_sum, gw_k)), gx_k

        init = (_jnp.zeros((), _jnp.float32), _jax.tree.map(_jnp.zeros_like, weights))
        (loss, grad_w), grad_x = _jax.lax.scan(body, init, (per_example, given["loss_target"]))
    with _jax.named_scope("update"):
        delta_w, new_m, new_v = {}, {}, {}
        for n in TWIN_WEIGHTS:
            delta_w[n], new_m[n], new_v[n] = _adamw(weights[n], grad_w[n], given["m_" + n], given["v_" + n])
    return (loss, grad_x, *[grad_w[n] for n in TWIN_WEIGHTS], *[delta_w[n] for n in TWIN_WEIGHTS],
            *[new_m[n] for n in TWIN_WEIGHTS], *[new_v[n] for n in TWIN_WEIGHTS])
```

```python
import functools
import numpy as np
import jax
import jax.numpy as jnp
from jax import lax
from jax.experimental import pallas as pl
from jax.experimental.pallas import tpu as pltpu

MD = jnp.bfloat16
F32 = jnp.float32
EPS = 1e-6
N_DEV = 8
HEADS = 16
QK_NOPE = 128
QK_ROPE = 64
V_DIM = 128
KV_RANK = 512
SOFTMAX_SCALE = (QK_NOPE + QK_ROPE) ** -0.5
ROPE_THETA = 10000.0
LRU_BLOCKS = 16
LRU_C = 8.0
X_HEADS = 4
X_DIM = 128
LRU_TAPS = 4
FFN_TAPS = 3
ADAM_LR, ADAM_B1, ADAM_B2, ADAM_EPS, ADAM_WD, ADAM_STEP = 0.001, 0.9, 0.999, 1e-08, 0.01, 10
LANES = 128
VMEM_LIMIT = 52 * 1024 * 1024
MESH = pl.DeviceIdType.MESH

NN = (((1,), (0,)), ((), ()))
NT = (((1,), (1,)), ((), ()))
TN = (((0,), (0,)), ((), ()))


def _tile(n, pref, align=LANES):
    if n <= pref:
        return n
    d = (pref // align) * align
    while d > align and n % d:
        d -= align
    assert n % d == 0, (n, pref, align)
    return d


def _params(sem):
    return pltpu.CompilerParams(dimension_semantics=sem, vmem_limit_bytes=VMEM_LIMIT)


def _dot(a, b, dims=NN):
    return lax.dot_general(a, b, dims, preferred_element_type=F32)


def _sigmoid(x):
    return 1.0 / (1.0 + jnp.exp(-x))


_GELU_C = 0.7978845608028654
_GELU_A = 0.044715


def _gelu(x):
    return 0.5 * x * (1.0 + jnp.tanh(_GELU_C * (x + _GELU_A * x * x * x)))


def _gelu_and_grad(x):
    t = jnp.tanh(_GELU_C * (x + _GELU_A * x * x * x))
    g = 0.5 * x * (1.0 + t)
    dg = 0.5 * (1.0 + t) + 0.5 * x * (1.0 - t * t) * _GELU_C * (1.0 + 3.0 * _GELU_A * x * x)
    return g, dg


def _expm1(x):
    small = x * (1.0 + x * (0.5 + x * (1.0 / 6.0 + x * (1.0 / 24.0 + x * (1.0 / 120.0)))))
    return jnp.where(jnp.abs(x) < 0.05, small, jnp.exp(x) - 1.0)


def _softplus(x):
    return jnp.maximum(x, 0.0) + jnp.log(1.0 + jnp.exp(-jnp.abs(x)))


def _rms_hat(x):
    r = lax.rsqrt(jnp.mean(x * x, axis=-1, keepdims=True) + EPS)
    return x * r, r


def _rms_bwd(x, g, dn):
    xh, r = _rms_hat(x)
    dg = jnp.sum(dn * xh, axis=0, keepdims=True)
    dxh = dn * g
    dx = r * (dxh - xh * jnp.mean(dxh * xh, axis=-1, keepdims=True))
    return dx, dg


def _acc_out(ref, val, first):
    @pl.when(first)
    def _():
        ref[...] = val

    @pl.when(jnp.logical_not(first))
    def _():
        ref[...] += val


def mm(pairs, mode, out_dtype, name, tm=1024, tn=1024, tk=2048):
    dims = {"nn": NN, "nt": NT, "tn": TN}[mode]
    shapes = []
    for a, b in pairs:
        if mode == "nn":
            (m, k), (k2, n) = a.shape, b.shape
        elif mode == "nt":
            (m, k), (n, k2) = a.shape, b.shape
        else:
            (k, m), (k2, n) = a.shape, b.shape
        assert k == k2, (name, a.shape, b.shape)
        shapes.append((m, n, k))
    m, n = shapes[0][0], shapes[0][1]
    assert all(s[0] == m and s[1] == n for s in shapes)
    tm, tn = _tile(m, tm), _tile(n, tn)
    tks = [_tile(s[2], tk) for s in shapes]
    nks = [s[2] // t for s, t in zip(shapes, tks)]
    starts = [sum(nks[:p]) for p in range(len(pairs))]
    nk_total = sum(nks)

    in_specs, args = [], []
    for p, (a, b) in enumerate(pairs):
        def kk(k, p=p):
            return jnp.clip(k - starts[p], 0, nks[p] - 1)
        if mode == "tn":
            in_specs.append(pl.BlockSpec((tks[p], tm), lambda i, j, k, kk=kk: (kk(k), i)))
        else:
            in_specs.append(pl.BlockSpec((tm, tks[p]), lambda i, j, k, kk=kk: (i, kk(k))))
        if mode == "nt":
            in_specs.append(pl.BlockSpec((tn, tks[p]), lambda i, j, k, kk=kk: (j, kk(k))))
        else:
            in_specs.append(pl.BlockSpec((tks[p], tn), lambda i, j, k, kk=kk: (kk(k), j)))
        args += [a, b]

    def body(*refs):
        ins, o_ref, scratch = refs[:2 * len(pairs)], refs[2 * len(pairs)], refs[2 * len(pairs) + 1:]
        if nk_total == 1:
            o_ref[...] = _dot(ins[0][...], ins[1][...], dims).astype(o_ref.dtype)
            return
        acc_ref, = scratch
        k = pl.program_id(2)
        for p in range(len(pairs)):
            @pl.when(jnp.logical_and(k >= starts[p], k < starts[p] + nks[p]))
            def _(p=p):
                prod = _dot(ins[2 * p][...], ins[2 * p + 1][...], dims)

                @pl.when(k == 0)
                def _():
                    acc_ref[...] = prod

                @pl.when(k > 0)
                def _():
                    acc_ref[...] += prod

        @pl.when(k == nk_total - 1)
        def _():
            o_ref[...] = acc_ref[...].astype(o_ref.dtype)

    return pl.pallas_call(
        body, name=name, grid=(m // tm, n // tn, nk_total),
        in_specs=in_specs, out_specs=pl.BlockSpec((tm, tn), lambda i, j, k: (i, j)),
        out_shape=jax.ShapeDtypeStruct((m, n), out_dtype),
        scratch_shapes=[] if nk_total == 1 else [pltpu.VMEM((tm, tn), F32)],
        compiler_params=_params(("parallel", "parallel", "arbitrary")),
    )(*args)


def rms_fwd(x, g, name):
    s, d = x.shape
    ts = _tile(s, 512, 16)

    def body(x_ref, g_ref, o_ref):
        xh, _ = _rms_hat(x_ref[...].astype(F32))
        o_ref[...] = (xh * g_ref[...]).astype(o_ref.dtype)

    return pl.pallas_call(
        body, name=name, grid=(s // ts,),
        in_specs=[pl.BlockSpec((ts, d), lambda i: (i, 0)), pl.BlockSpec((1, d), lambda i: (0, 0))],
        out_specs=pl.BlockSpec((ts, d), lambda i: (i, 0)),
        out_shape=jax.ShapeDtypeStruct((s, d), MD),
        compiler_params=_params(("parallel",)),
    )(x, g)


def rms_bwd(x, g, dn, name):
    s, d = x.shape
    ts = _tile(s, 256, 16)
    out_dtype = MD

    def body(x_ref, g_ref, dn_ref, dx_ref, dg_ref):
        dx, dg = _rms_bwd(x_ref[...].astype(F32), g_ref[...], dn_ref[...].astype(F32))
        dx_ref[...] = dx.astype(dx_ref.dtype)
        _acc_out(dg_ref, dg, pl.program_id(0) == 0)

    row = pl.BlockSpec((ts, d), lambda i: (i, 0))
    vec = pl.BlockSpec((1, d), lambda i: (0, 0))
    return pl.pallas_call(
        body, name=name, grid=(s // ts,), in_specs=[row, vec, row], out_specs=[row, vec],
        out_shape=[jax.ShapeDtypeStruct((s, d), out_dtype), jax.ShapeDtypeStruct((1, d), F32)],
        compiler_params=_params(("arbitrary",)),
    )(x, g, dn)


def post_pre_fwd(x, y, g_post, g_pre, name):
    s, d = x.shape
    ts = _tile(s, 256, 16)

    def body(x_ref, y_ref, gp_ref, gq_ref, xn_ref, h_ref):
        yh, _ = _rms_hat(y_ref[...])
        xn = x_ref[...] + yh * gp_ref[...]
        xn_ref[...] = xn
        xh, _ = _rms_hat(xn)
        h_ref[...] = (xh * gq_ref[...]).astype(h_ref.dtype)

    row = pl.BlockSpec((ts, d), lambda i: (i, 0))
    vec = pl.BlockSpec((1, d), lambda i: (0, 0))
    return pl.pallas_call(
        body, name=name, grid=(s // ts,), in_specs=[row, row, vec, vec], out_specs=[row, row],
        out_shape=[jax.ShapeDtypeStruct((s, d), F32), jax.ShapeDtypeStruct((s, d), MD)],
        compiler_params=_params(("parallel",)),
    )(x, y, g_post, g_pre)


def loss_head(x, y, target, g_post, name):
    s, d = x.shape
    ts = _tile(s, 256, 16)

    def body(x_ref, y_ref, t_ref, g_ref, loss_ref, dx_ref, dy_ref, dg_ref):
        first = pl.program_id(0) == 0
        y = y_ref[...]
        yh, _ = _rms_hat(y)
        diff = x_ref[...] + yh * g_ref[...] - t_ref[...]
        part = 0.5 * jnp.sum(jnp.sum(diff * diff, axis=1, keepdims=True) * (1.0 / d), axis=0, keepdims=True)
        _acc_out(loss_ref, part, first)
        dx = diff * (1.0 / d)
        dx_ref[...] = dx
        dy, dg = _rms_bwd(y, g_ref[...], dx)
        dy_ref[...] = dy.astype(dy_ref.dtype)
        _acc_out(dg_ref, dg, first)

    row = pl.BlockSpec((ts, d), lambda i: (i, 0))
    vec = pl.BlockSpec((1, d), lambda i: (0, 0))
    one = pl.BlockSpec((1, 1), lambda i: (0, 0))
    return pl.pallas_call(
        body, name=name, grid=(s // ts,), in_specs=[row, row, row, vec], out_specs=[one, row, row, vec],
        out_shape=[jax.ShapeDtypeStruct((1, 1), F32), jax.ShapeDtypeStruct((s, d), F32),
                   jax.ShapeDtypeStruct((s, d), MD), jax.ShapeDtypeStruct((1, d), F32)],
        compiler_params=_params(("arbitrary",)),
    )(x, y, target, g_post)


def pre_post_bwd(x, y, dx_res, dh, g_pre, g_post, name):
    s, d = x.shape
    ts = _tile(s, 256, 16)

    def body(x_ref, y_ref, dr_ref, dh_ref, gq_ref, gp_ref, dx_ref, dy_ref, dgq_ref, dgp_ref):
        first = pl.program_id(0) == 0
        dxa, dgq = _rms_bwd(x_ref[...], gq_ref[...], dh_ref[...])
        dx = dr_ref[...] + dxa
        dx_ref[...] = dx
        dy, dgp = _rms_bwd(y_ref[...], gp_ref[...], dx)
        dy_ref[...] = dy.astype(dy_ref.dtype)
        _acc_out(dgq_ref, dgq, first)
        _acc_out(dgp_ref, dgp, first)

    row = pl.BlockSpec((ts, d), lambda i: (i, 0))
    vec = pl.BlockSpec((1, d), lambda i: (0, 0))
    return pl.pallas_call(
        body, name=name, grid=(s // ts,), in_specs=[row, row, row, row, vec, vec], out_specs=[row, row, vec, vec],
        out_shape=[jax.ShapeDtypeStruct((s, d), F32), jax.ShapeDtypeStruct((s, d), MD),
                   jax.ShapeDtypeStruct((1, d), F32), jax.ShapeDtypeStruct((1, d), F32)],
        compiler_params=_params(("arbitrary",)),
    )(x, y, dx_res, dh, g_pre, g_post)


def pre_bwd(x, dx_res, dh, g_pre, name):
    s, d = x.shape
    ts = _tile(s, 256, 16)

    def body(x_ref, dr_ref, dh_ref, g_ref, dx_ref, dg_ref):
        dxa, dg = _rms_bwd(x_ref[...], g_ref[...], dh_ref[...])
        dx_ref[...] = dr_ref[...] + dxa
        _acc_out(dg_ref, dg, pl.program_id(0) == 0)

    row = pl.BlockSpec((ts, d), lambda i: (i, 0))
    vec = pl.BlockSpec((1, d), lambda i: (0, 0))
    return pl.pallas_call(
        body, name=name, grid=(s // ts,), in_specs=[row, row, row, vec], out_specs=[row, vec],
        out_shape=[jax.ShapeDtypeStruct((s, d), F32), jax.ShapeDtypeStruct((1, d), F32)],
        compiler_params=_params(("arbitrary",)),
    )(x, dx_res, dh, g_pre)


def _swap_halves(x):
    lane = lax.broadcasted_iota(jnp.int32, x.shape, 1)
    return jnp.where((lane % QK_ROPE) < QK_ROPE // 2, pltpu.roll(x, LANES - QK_ROPE // 2, 1),
                     pltpu.roll(x, QK_ROPE // 2, 1))


def _rope_tables(ang_ref, sign):
    lane = lax.broadcasted_iota(jnp.int32, ang_ref.shape, 1)
    sgn = jnp.where((lane % QK_ROPE) < QK_ROPE // 2, -sign, sign)
    ang = ang_ref[...]
    return jnp.cos(ang), jnp.sin(ang) * sgn


def rope_fwd(proj, kr, ang2, q_rope_block, name):
    s = proj.shape[0]
    ts = _tile(s, 512, 16)
    wq = HEADS * QK_ROPE

    def body(q_ref, kr_ref, ang_ref, qo_ref, ko_ref):
        c, sn = _rope_tables(ang_ref, 1.0)
        lane = lax.broadcasted_iota(jnp.int32, (ts, LANES), 1)
        lo = lane < QK_ROPE
        for j in range(HEADS // 2):
            x = q_ref[:, j * LANES:(j + 1) * LANES].astype(F32)
            r = x * c + _swap_halves(x) * sn
            qo_ref[:, (2 * j) * LANES:(2 * j + 1) * LANES] = jnp.where(lo, r, 0.0).astype(qo_ref.dtype)
            qo_ref[:, (2 * j + 1) * LANES:(2 * j + 2) * LANES] = jnp.where(lo, pltpu.roll(r, QK_ROPE, 1), 0.0).astype(qo_ref.dtype)
        x = kr_ref[...].astype(F32)
        ko_ref[...] = jnp.where(lo, x * c + _swap_halves(x) * sn, 0.0).astype(ko_ref.dtype)

    return pl.pallas_call(
        body, name=name, grid=(s // ts,),
        in_specs=[pl.BlockSpec((ts, wq), lambda i: (i, q_rope_block)), pl.BlockSpec((ts, LANES), lambda i: (i, 0)),
                  pl.BlockSpec((ts, LANES), lambda i: (i, 0))],
        out_specs=[pl.BlockSpec((ts, HEADS * LANES), lambda i: (i, 0)), pl.BlockSpec((ts, LANES), lambda i: (i, 0))],
        out_shape=[jax.ShapeDtypeStruct((s, HEADS * LANES), MD), jax.ShapeDtypeStruct((s, LANES), MD)],
        compiler_params=_params(("parallel",)),
    )(proj, kr, ang2)


def rope_bwd(dqrp, dkrh, ang2, name):
    s = dqrp.shape[0]
    ts = _tile(s, 512, 16)
    wq = HEADS * QK_ROPE

    def body(dq_ref, dk_ref, ang_ref, qo_ref, ko_ref):
        c, sn = _rope_tables(ang_ref, -1.0)
        lane = lax.broadcasted_iota(jnp.int32, (ts, LANES), 1)
        lo = lane < QK_ROPE
        for j in range(HEADS // 2):
            a = jnp.where(lo, dq_ref[:, (2 * j) * LANES:(2 * j + 1) * LANES].astype(F32), 0.0)
            b = jnp.where(lo, dq_ref[:, (2 * j + 1) * LANES:(2 * j + 2) * LANES].astype(F32), 0.0)
            dy = a + pltpu.roll(b, QK_ROPE, 1)
            qo_ref[:, j * LANES:(j + 1) * LANES] = (dy * c + _swap_halves(dy) * sn).astype(qo_ref.dtype)
        dy = dk_ref[0]
        for h in range(1, HEADS):
            dy = dy + dk_ref[h]
        dy = jnp.where(lo, dy, 0.0)
        ko_ref[...] = jnp.where(lo, dy * c + _swap_halves(dy) * sn, 0.0).astype(ko_ref.dtype)

    return pl.pallas_call(
        body, name=name, grid=(s // ts,),
        in_specs=[pl.BlockSpec((ts, HEADS * LANES), lambda i: (i, 0)), pl.BlockSpec((HEADS, ts, LANES), lambda i: (0, i, 0)),
                  pl.BlockSpec((ts, LANES), lambda i: (i, 0))],
        out_specs=[pl.BlockSpec((ts, wq), lambda i: (i, 0)), pl.BlockSpec((ts, LANES), lambda i: (i, 0))],
        out_shape=[jax.ShapeDtypeStruct((s, wq), MD), jax.ShapeDtypeStruct((s, LANES), MD)],
        compiler_params=_params(("parallel",)),
    )(dqrp, dkrh, ang2)


def _col_to_row(col):
    t = col.shape[0]
    return jnp.transpose(jnp.broadcast_to(col, (t, LANES)))[0:1, :]


def flash_fwd(proj, qrp, kn, krp, v, name):
    s = proj.shape[0]
    t = _tile(s, 512)
    nb = s // t
    pairs = [(i, j) for i in range(nb) for j in range(i + 1)]
    it = jnp.asarray(np.array([p[0] for p in pairs], np.int32))
    jt = jnp.asarray(np.array([p[1] for p in pairs], np.int32))

    def body(it_ref, jt_ref, qn_ref, qr_ref, kn_ref, kr_ref, v_ref, o_ref, lse_ref, m_s, l_s, acc_s):
        p = pl.program_id(1)
        i, j = it_ref[p], jt_ref[p]

        @pl.when(j == 0)
        def _():
            m_s[...] = jnp.full(m_s.shape, -jnp.inf, F32)
            l_s[...] = jnp.zeros(l_s.shape, F32)
            acc_s[...] = jnp.zeros(acc_s.shape, F32)

        q = jnp.concatenate([qn_ref[...], qr_ref[...]], axis=1)
        k = jnp.concatenate([kn_ref[...], kr_ref[...]], axis=1)
        sc = _dot(q, k, NT) * SOFTMAX_SCALE
        row = lax.broadcasted_iota(jnp.int32, (t, t), 0) + i * t
        col = lax.broadcasted_iota(jnp.int32, (t, t), 1) + j * t
        sc = jnp.where(col <= row, sc, -jnp.inf)
        m_prev = m_s[...]
        m_new = jnp.maximum(m_prev, jnp.max(sc, axis=1, keepdims=True))
        alpha = jnp.exp(m_prev - m_new)
        pm = jnp.exp(sc - m_new[:, 0:1])
        l_s[...] = alpha * l_s[...] + jnp.sum(pm, axis=1, keepdims=True)
        acc_s[...] = acc_s[...] * alpha + _dot(pm.astype(MD), v_ref[...])
        m_s[...] = m_new

        @pl.when(j == i)
        def _():
            l = l_s[...]
            o_ref[...] = (acc_s[...] / l).astype(o_ref.dtype)
            lse_ref[...] = _col_to_row(m_s[:, 0:1] + jnp.log(l[:, 0:1]))

    blk = lambda tab, colf: pl.BlockSpec((t, LANES), lambda h, p, it, jt: ((it if tab == 0 else jt)[p], colf(h)))
    grid_spec = pltpu.PrefetchScalarGridSpec(
        num_scalar_prefetch=2, grid=(HEADS, len(pairs)),
        in_specs=[blk(0, lambda h: h), blk(0, lambda h: h), blk(1, lambda h: h), blk(1, lambda h: 0), blk(1, lambda h: h)],
        out_specs=[blk(0, lambda h: h), pl.BlockSpec((None, 1, t), lambda h, p, it, jt: (h, 0, it[p]))],
        scratch_shapes=[pltpu.VMEM((t, LANES), F32), pltpu.VMEM((t, LANES), F32), pltpu.VMEM((t, LANES), F32)])
    return pl.pallas_call(
        body, name=name, grid_spec=grid_spec,
        out_shape=[jax.ShapeDtypeStruct((s, HEADS * V_DIM), MD), jax.ShapeDtypeStruct((HEADS, 1, s), F32)],
        compiler_params=_params(("parallel", "arbitrary")),
    )(it, jt, proj, qrp, kn, krp, v)


def attn_delta(o, do, name):
    s = o.shape[0]
    t = _tile(s, 512)

    def body(o_ref, do_ref, d_ref):
        d = jnp.sum(o_ref[...].astype(F32) * do_ref[...].astype(F32), axis=1, keepdims=True)
        d_ref[...] = _col_to_row(d)

    blk = pl.BlockSpec((t, LANES), lambda h, i: (i, h))
    return pl.pallas_call(
        body, name=name, grid=(HEADS, s // t), in_specs=[blk, blk],
        out_specs=pl.BlockSpec((None, 1, t), lambda h, i: (h, 0, i)),
        out_shape=jax.ShapeDtypeStruct((HEADS, 1, s), F32),
        compiler_params=_params(("parallel", "parallel")),
    )(o, do)


def flash_bwd(proj, qrp, kn, krp, v, do, lse, delta, name):
    s = proj.shape[0]
    t = _tile(s, 512)
    nb = s // t
    pairs = [(i, j) for j in range(nb) for i in range(j, nb)]
    it = jnp.asarray(np.array([p[0] for p in pairs], np.int32))
    jt = jnp.asarray(np.array([p[1] for p in pairs], np.int32))
    n_pairs = len(pairs)

    def body(it_ref, jt_ref, qn_ref, qr_ref, kn_ref, kr_ref, v_ref, do_ref, lse_ref, dl_ref,
             dqn_ref, dqr_ref, dkn_ref, dkr_ref, dv_ref, dq_s, dk_s, dv_s):
        p = pl.program_id(1)
        i, j = it_ref[p], jt_ref[p]

        @pl.when(p == 0)
        def _():
            dq_s[...] = jnp.zeros(dq_s.shape, F32)

        @pl.when(i == j)
        def _():
            dk_s[...] = jnp.zeros(dk_s.shape, F32)
            dv_s[...] = jnp.zeros(dv_s.shape, F32)

        q = jnp.concatenate([qn_ref[...], qr_ref[...]], axis=1)
        k = jnp.concatenate([kn_ref[...], kr_ref[...]], axis=1)
        do_b = do_ref[...]
        st = _dot(k, q, NT) * SOFTMAX_SCALE
        krow = lax.broadcasted_iota(jnp.int32, (t, t), 0) + j * t
        qcol = lax.broadcasted_iota(jnp.int32, (t, t), 1) + i * t
        pt = jnp.where(krow <= qcol, jnp.exp(st - lse_ref[...]), 0.0)
        dv_s[...] += _dot(pt.astype(MD), do_b)
        dpt = _dot(v_ref[...], do_b, NT)
        dst = (pt * (dpt - dl_ref[...]) * SOFTMAX_SCALE).astype(MD)
        dk_s[...] += _dot(dst, q)
        row0 = pl.multiple_of(i * t, t)
        dq_s[pl.ds(row0, t), :] += _dot(dst, k, TN)

        @pl.when(i == nb - 1)
        def _():
            dkn_ref[...] = dk_s[:, :LANES].astype(dkn_ref.dtype)
            dkr_ref[...] = dk_s[:, LANES:]
            dv_ref[...] = dv_s[...].astype(dv_ref.dtype)

        @pl.when(p == n_pairs - 1)
        def _():
            dqn_ref[...] = dq_s[:, :LANES].astype(dqn_ref.dtype)
            dqr_ref[...] = dq_s[:, LANES:].astype(dqr_ref.dtype)

    blk = lambda tab, colf: pl.BlockSpec((t, LANES), lambda h, p, it, jt: ((it if tab == 0 else jt)[p], colf(h)))
    vec = pl.BlockSpec((None, 1, t), lambda h, p, it, jt: (h, 0, it[p]))
    whole = pl.BlockSpec((s, LANES), lambda h, p, it, jt: (0, h))
    grid_spec = pltpu.PrefetchScalarGridSpec(
        num_scalar_prefetch=2, grid=(HEADS, n_pairs),
        in_specs=[blk(0, lambda h: h), blk(0, lambda h: h), blk(1, lambda h: h), blk(1, lambda h: 0), blk(1, lambda h: h),
                  blk(0, lambda h: h), vec, vec],
        out_specs=[whole, whole, blk(1, lambda h: h), pl.BlockSpec((None, t, LANES), lambda h, p, it, jt: (h, jt[p], 0)),
                   blk(1, lambda h: h)],
        scratch_shapes=[pltpu.VMEM((s, 2 * LANES), F32), pltpu.VMEM((t, 2 * LANES), F32), pltpu.VMEM((t, LANES), F32)])
    wide = jax.ShapeDtypeStruct((s, HEADS * LANES), MD)
    return pl.pallas_call(
        body, name=name, grid_spec=grid_spec,
        out_shape=[wide, wide, wide, jax.ShapeDtypeStruct((HEADS, s, LANES), F32), wide],
        compiler_params=_params(("parallel", "arbitrary")),
    )(it, jt, proj, qrp, kn, krp, v, do, lse, delta)


def _halo_rows(dtype):
    return 8 if jnp.dtype(dtype).itemsize == 4 else 16


def _prev_spec(ts, tc, hr, col_off):
    return pl.BlockSpec((hr, tc), lambda j, i: (jnp.maximum(i * (ts // hr) - 1, 0), j + col_off))


def _next_spec(ts, tc, hr, col_off, n_rows):
    last = n_rows // hr - 1
    return pl.BlockSpec((hr, tc), lambda j, i: (jnp.minimum((i + 1) * (ts // hr), last), j + col_off))


def _conv_taps(x, halo, w, taps, first):
    hr, ts = halo.shape[0], x.shape[0]
    xe = jnp.concatenate([jnp.where(first, 0.0, halo), x], axis=0)
    shifted = []
    acc = None
    for k in range(taps):
        sh = taps - 1 - k
        xs = xe if sh == 0 else pltpu.roll(xe, sh, 0)
        xs = xs[hr:hr + ts, :]
        shifted.append(xs)
        term = xs * w[k:k + 1, :]
        acc = term if acc is None else acc + term
    return acc, shifted


def conv_bwd(x, dy, w8, taps, x_col_off, name, ts_pref=256):
    s, c = dy.shape
    tc = _tile(c, 512)
    ts = _tile(s, ts_pref, 16)
    hx, hy = _halo_rows(x.dtype), _halo_rows(dy.dtype)
    ns = s // ts
    xo = x_col_off // tc
    assert x_col_off % tc == 0

    def body(x_ref, xp_ref, dy_ref, dyn_ref, w_ref, dx_ref, dw_ref, db_ref):
        i = pl.program_id(1)
        first = i == 0
        xv = x_ref[...].astype(F32)
        dyv = dy_ref[...].astype(F32)
        w = w_ref[...]
        _, shifted = _conv_taps(xv, xp_ref[...].astype(F32), w, taps, first)
        dye = jnp.concatenate([dyv, jnp.where(i == ns - 1, 0.0, dyn_ref[...].astype(F32))], axis=0)
        dx = None
        for k in range(taps):
            sh = taps - 1 - k
            ds_ = dye if sh == 0 else pltpu.roll(dye, ts + hy - sh, 0)
            term = ds_[0:ts, :] * w[k:k + 1, :]
            dx = term if dx is None else dx + term
        dx_ref[...] = dx.astype(dx_ref.dtype)
        rows = [jnp.sum(dyv * shifted[k], axis=0, keepdims=True) for k in range(taps)]
        rows.append(jnp.zeros((8 - taps, tc), F32))
        _acc_out(dw_ref, jnp.concatenate(rows, axis=0), first)
        _acc_out(db_ref, jnp.sum(dyv, axis=0, keepdims=True), first)

    return pl.pallas_call(
        body, name=name, grid=(c // tc, ns),
        in_specs=[pl.BlockSpec((ts, tc), lambda j, i: (i, j + xo)), _prev_spec(ts, tc, hx, xo),
                  pl.BlockSpec((ts, tc), lambda j, i: (i, j)), _next_spec(ts, tc, hy, 0, s),
                  pl.BlockSpec((8, tc), lambda j, i: (0, j))],
        out_specs=[pl.BlockSpec((ts, tc), lambda j, i: (i, j)), pl.BlockSpec((8, tc), lambda j, i: (0, j)),
                   pl.BlockSpec((1, tc), lambda j, i: (0, j))],
        out_shape=[jax.ShapeDtypeStruct((s, c), MD), jax.ShapeDtypeStruct((8, c), F32), jax.ShapeDtypeStruct((1, c), F32)],
        compiler_params=_params(("parallel", "arbitrary")),
    )(x, x, dy, dy, w8)


def _lru_gates(xc, wr_ref, br, wi_ref, bi, sp, nblk):
    xb = xc.astype(MD)
    zr, zi = [], []
    for b in range(nblk):
        blk = xb[:, b * LANES:(b + 1) * LANES]
        zr.append(_dot(blk, wr_ref[b]))
        zi.append(_dot(blk, wi_ref[b]))
    r = _sigmoid(jnp.concatenate(zr, axis=1) + br)
    ig = _sigmoid(jnp.concatenate(zi, axis=1) + bi)
    log_a = -LRU_C * r * sp
    a = jnp.exp(log_a)
    mult = jnp.sqrt(-_expm1(2.0 * log_a))
    return r, ig, a, mult


def _lru_specs(ts, tc, xo, yo):
    nblk = tc // LANES
    tile = lambda off: pl.BlockSpec((ts, tc), lambda j, i: (i, j + off))
    vec = pl.BlockSpec((1, tc), lambda j, i: (0, j))
    wblk = pl.BlockSpec((nblk, LANES, LANES), lambda j, i: (j, 0, 0))
    return tile, vec, wblk


def lru_fwd(proj, x_off, y_off, w8, b_conv, w_rg, b_rg, w_ig, b_ig, lam, name):
    s = proj.shape[0]
    c = lam.shape[1]
    tc, ts = _tile(c, 512), _tile(s, 256, 16)
    nblk = tc // LANES
    xo, yo = x_off // tc, y_off // tc
    hr = _halo_rows(proj.dtype)

    def body(x_ref, xp_ref, y_ref, w_ref, bc_ref, wr_ref, br_ref, wi_ref, bi_ref, lam_ref, h_ref, hg_ref, carry):
        i = pl.program_id(1)
        first = i == 0
        xc, _ = _conv_taps(x_ref[...].astype(F32), xp_ref[...].astype(F32), w_ref[...], LRU_TAPS, first)
        xc = xc + bc_ref[...]
        sp = _softplus(-lam_ref[...])
        _, ig, a, mult = _lru_gates(xc, wr_ref, br_ref[...], wi_ref, bi_ref[...], sp, nblk)
        b = mult * (ig * xc)
        row = lax.broadcasted_iota(jnp.int32, (ts, tc), 0)
        d = 1
        while d < ts:
            keep = row >= d
            a_s, b_s = pltpu.roll(a, d, 0), pltpu.roll(b, d, 0)
            b = jnp.where(keep, a * b_s + b, b)
            a = jnp.where(keep, a * a_s, a)
            d *= 2
        h0 = jnp.where(first, 0.0, carry[7:8, :])
        h = a * h0 + b
        carry[...] = h[ts - 8:ts, :]
        h_ref[...] = h
        hg_ref[...] = (h * _gelu(y_ref[...].astype(F32))).astype(hg_ref.dtype)

    tile, vec, wblk = _lru_specs(ts, tc, xo, yo)
    out = pl.BlockSpec((ts, tc), lambda j, i: (i, j))
    return pl.pallas_call(
        body, name=name, grid=(c // tc, s // ts),
        in_specs=[tile(xo), _prev_spec(ts, tc, hr, xo), tile(yo), pl.BlockSpec((8, tc), lambda j, i: (0, j)), vec,
                  wblk, vec, wblk, vec, vec],
        out_specs=[out, out],
        out_shape=[jax.ShapeDtypeStruct((s, c), F32), jax.ShapeDtypeStruct((s, c), MD)],
        scratch_shapes=[pltpu.VMEM((8, tc), F32)],
        compiler_params=_params(("parallel", "arbitrary")),
    )(proj, proj, proj, w8, b_conv, w_rg, b_rg, w_ig, b_ig, lam)


def lru_bwd(proj, x_off, y_off, h, dhg, w8, b_conv, w_rg, b_rg, w_ig, b_ig, lam, name):
    s = proj.shape[0]
    c = lam.shape[1]
    tc, ts = _tile(c, 512), _tile(s, 256, 16)
    nblk = tc // LANES
    ns = s // ts
    xo, yo = x_off // tc, y_off // tc
    hr = _halo_rows(proj.dtype)
    rev = lambda i: ns - 1 - i

    def body(x_ref, xp_ref, y_ref, h_ref, hp_ref, dhg_ref, w_ref, bc_ref, wr_ref, br_ref, wi_ref, bi_ref, lam_ref,
             dxc_ref, dy_ref, dwr_ref, dwi_ref, dbr_ref, dbi_ref, dlam_ref, a_carry, g_carry):
        step = pl.program_id(1)
        i = rev(step)
        start = step == 0
        top = i == 0
        xc, _ = _conv_taps(x_ref[...].astype(F32), xp_ref[...].astype(F32), w_ref[...], LRU_TAPS, top)
        xc = xc + bc_ref[...]
        lam_v = lam_ref[...]
        sp = _softplus(-lam_v)
        r, ig, a, mult = _lru_gates(xc, wr_ref, br_ref[...], wi_ref, bi_ref[...], sp, nblk)
        hv = h_ref[...]
        gy, dgy = _gelu_and_grad(y_ref[...].astype(F32))
        dhg_v = dhg_ref[...]
        dy_ref[...] = (dhg_v * hv * dgy).astype(dy_ref.dtype)
        row = lax.broadcasted_iota(jnp.int32, (ts, tc), 0)
        a_next = jnp.where(row == ts - 1, jnp.where(start, 0.0, a_carry[0:1, :]), pltpu.roll(a, ts - 1, 0))
        ca, cb = a_next, dhg_v * gy
        d = 1
        while d < ts:
            keep = row < ts - d
            a_s, b_s = pltpu.roll(ca, ts - d, 0), pltpu.roll(cb, ts - d, 0)
            cb = jnp.where(keep, ca * b_s + cb, cb)
            ca = jnp.where(keep, ca * a_s, ca)
            d *= 2
        g = ca * jnp.where(start, 0.0, g_carry[0:1, :]) + cb
        a_carry[...] = a[0:8, :]
        g_carry[...] = g[0:8, :]
        h_prev = jnp.where(row == 0, jnp.where(top, 0.0, hp_ref[7:8, :]), pltpu.roll(hv, 1, 0))
        da = g * h_prev
        ixc = ig * xc
        d_ixc = g * mult
        dlog_a = da * a - g * ixc * (a * a) / mult
        dzr = dlog_a * (-LRU_C * sp) * r * (1.0 - r)
        dzi = d_ixc * xc * ig * (1.0 - ig)
        dxc = d_ixc * ig
        xb, dzr_b, dzi_b = xc.astype(MD), dzr.astype(MD), dzi.astype(MD)
        parts, dwr, dwi = [], [], []
        for b in range(nblk):
            sl = slice(b * LANES, (b + 1) * LANES)
            parts.append(_dot(dzr_b[:, sl], wr_ref[b], NT) + _dot(dzi_b[:, sl], wi_ref[b], NT))
            dwr.append(_dot(xb[:, sl], dzr_b[:, sl], TN)[None])
            dwi.append(_dot(xb[:, sl], dzi_b[:, sl], TN)[None])
        dxc_ref[...] = dxc + jnp.concatenate(parts, axis=1)
        _acc_out(dwr_ref, jnp.concatenate(dwr, axis=0), start)
        _acc_out(dwi_ref, jnp.concatenate(dwi, axis=0), start)
        _acc_out(dbr_ref, jnp.sum(dzr, axis=0, keepdims=True), start)
        _acc_out(dbi_ref, jnp.sum(dzi, axis=0, keepdims=True), start)
        dsp = jnp.sum(dlog_a * (-LRU_C) * r, axis=0, keepdims=True)
        _acc_out(dlam_ref, -dsp * _sigmoid(-lam_v), start)

    vec = pl.BlockSpec((1, tc), lambda j, i: (0, j))
    wblk = pl.BlockSpec((nblk, LANES, LANES), lambda j, i: (j, 0, 0))
    tile = lambda off: pl.BlockSpec((ts, tc), lambda j, i: (rev(i), j + off))
    x_prev = pl.BlockSpec((hr, tc), lambda j, i: (jnp.maximum(rev(i) * (ts // hr) - 1, 0), j + xo))
    h_prev = pl.BlockSpec((8, tc), lambda j, i: (jnp.maximum(rev(i) * (ts // 8) - 1, 0), j))
    return pl.pallas_call(
        body, name=name, grid=(c // tc, ns),
        in_specs=[tile(xo), x_prev, tile(yo), tile(0), h_prev, tile(0), pl.BlockSpec((8, tc), lambda j, i: (0, j)), vec,
                  wblk, vec, wblk, vec, vec],
        out_specs=[tile(0), tile(0), wblk, wblk, vec, vec, vec],
        out_shape=[jax.ShapeDtypeStruct((s, c), F32), jax.ShapeDtypeStruct((s, c), MD),
                   jax.ShapeDtypeStruct(w_rg.shape, F32), jax.ShapeDtypeStruct(w_ig.shape, F32),
                   jax.ShapeDtypeStruct((1, c), F32), jax.ShapeDtypeStruct((1, c), F32), jax.ShapeDtypeStruct((1, c), F32)],
        scratch_shapes=[pltpu.VMEM((8, tc), F32), pltpu.VMEM((8, tc), F32)],
        compiler_params=_params(("parallel", "arbitrary")),
    )(proj, proj, proj, h, h, dhg, w8, b_conv, w_rg, b_rg, w_ig, b_ig, lam)


def merge_fwd(proj, gm_off, gl_off, o_mla, o_lru, name):
    s, c = o_mla.shape
    tc, ts = _tile(c, 1024), _tile(s, 512, 16)

    def body(gm_ref, gl_ref, om_ref, ol_ref, out_ref):
        out_ref[...] = (_sigmoid(gm_ref[...].astype(F32)) * om_ref[...] + _sigmoid(gl_ref[...].astype(F32)) * ol_ref[...]).astype(out_ref.dtype)

    tile = lambda off: pl.BlockSpec((ts, tc), lambda i, j: (i, j + off // tc))
    return pl.pallas_call(
        body, name=name, grid=(s // ts, c // tc), in_specs=[tile(gm_off), tile(gl_off), tile(0), tile(0)], out_specs=tile(0),
        out_shape=jax.ShapeDtypeStruct((s, c), MD), compiler_params=_params(("parallel", "parallel")),
    )(proj, proj, o_mla, o_lru)


def merge_bwd(proj, gm_off, gl_off, o_mla, o_lru, dmerged, name):
    s, c = o_mla.shape
    tc, ts = _tile(c, 1024), _tile(s, 512, 16)

    def body(gm_ref, gl_ref, om_ref, ol_ref, dm_ref, dom_ref, dol_ref, dgm_ref, dgl_ref):
        sm, sl = _sigmoid(gm_ref[...].astype(F32)), _sigmoid(gl_ref[...].astype(F32))
        dm = dm_ref[...]
        dom_ref[...] = (dm * sm).astype(dom_ref.dtype)
        dol_ref[...] = (dm * sl).astype(dol_ref.dtype)
        dgm_ref[...] = (dm * om_ref[...] * sm * (1.0 - sm)).astype(dgm_ref.dtype)
        dgl_ref[...] = (dm * ol_ref[...] * sl * (1.0 - sl)).astype(dgl_ref.dtype)

    tile = lambda off: pl.BlockSpec((ts, tc), lambda i, j: (i, j + off // tc))
    out = jax.ShapeDtypeStruct((s, c), MD)
    return pl.pallas_call(
        body, name=name, grid=(s // ts, c // tc), in_specs=[tile(gm_off), tile(gl_off), tile(0), tile(0), tile(0)],
        out_specs=[tile(0)] * 4, out_shape=[out] * 4, compiler_params=_params(("parallel", "parallel")),
    )(proj, proj, o_mla, o_lru, dmerged)


def xattn_fwd(cq, ck, cv, name):
    s = cq.shape[0]
    m = ck.shape[0]
    ts = _tile(s, 512, 16)
    scale = X_DIM ** -0.5

    def body(q_ref, k_ref, v_ref, o_ref):
        for h in range(X_HEADS):
            sl = slice(h * X_DIM, (h + 1) * X_DIM)
            sc = _dot(q_ref[:, sl], k_ref[:, sl], NT) * scale
            e = jnp.exp(sc - jnp.max(sc, axis=1, keepdims=True))
            p = e / jnp.sum(e, axis=1, keepdims=True)
            o_ref[:, sl] = _dot(p.astype(MD), v_ref[:, sl]).astype(o_ref.dtype)

    w = X_HEADS * X_DIM
    return pl.pallas_call(
        body, name=name, grid=(s // ts,),
        in_specs=[pl.BlockSpec((ts, w), lambda i: (i, 0)), pl.BlockSpec((m, w), lambda i: (0, 0)), pl.BlockSpec((m, w), lambda i: (0, 0))],
        out_specs=pl.BlockSpec((ts, w), lambda i: (i, 0)), out_shape=jax.ShapeDtypeStruct((s, w), MD),
        compiler_params=_params(("parallel",)),
    )(cq, ck, cv)


def xattn_bwd(cq, ck, cv, dco, name):
    s = cq.shape[0]
    m = ck.shape[0]
    ts = _tile(s, 512, 16)
    scale = X_DIM ** -0.5

    def body(q_ref, k_ref, v_ref, do_ref, dq_ref, dk_ref, dv_ref):
        first = pl.program_id(0) == 0
        dks, dvs = [], []
        for h in range(X_HEADS):
            sl = slice(h * X_DIM, (h + 1) * X_DIM)
            q, k, v, do = q_ref[:, sl], k_ref[:, sl], v_ref[:, sl], do_ref[:, sl]
            sc = _dot(q, k, NT) * scale
            e = jnp.exp(sc - jnp.max(sc, axis=1, keepdims=True))
            p = e / jnp.sum(e, axis=1, keepdims=True)
            dvs.append(_dot(p.astype(MD), do, TN))
            dp = _dot(do, v, NT)
            ds = (p * (dp - jnp.sum(dp * p, axis=1, keepdims=True)) * scale).astype(MD)
            dq_ref[:, sl] = _dot(ds, k).astype(dq_ref.dtype)
            dks.append(_dot(ds, q, TN))
        _acc_out(dk_ref, jnp.concatenate(dks, axis=1), first)
        _acc_out(dv_ref, jnp.concatenate(dvs, axis=1), first)

    w = X_HEADS * X_DIM
    row = pl.BlockSpec((ts, w), lambda i: (i, 0))
    full = pl.BlockSpec((m, w), lambda i: (0, 0))
    return pl.pallas_call(
        body, name=name, grid=(s // ts,), in_specs=[row, full, full, row], out_specs=[row, full, full],
        out_shape=[jax.ShapeDtypeStruct((s, w), MD), jax.ShapeDtypeStruct((m, w), F32), jax.ShapeDtypeStruct((m, w), F32)],
        compiler_params=_params(("arbitrary",)),
    )(cq, ck, cv, dco)


def ffn_act_fwd(up_pre, w8, b, name):
    s, c2 = up_pre.shape
    f = c2 // 2
    tc, ts = _tile(f, 512), _tile(s, 512, 16)
    nf = f // tc
    hr = _halo_rows(up_pre.dtype)

    def body(g_ref, gp_ref, v_ref, vp_ref, wg_ref, wv_ref, bg_ref, bv_ref, o_ref):
        first = pl.program_id(1) == 0
        gate, _ = _conv_taps(g_ref[...].astype(F32), gp_ref[...].astype(F32), wg_ref[...], FFN_TAPS, first)
        val, _ = _conv_taps(v_ref[...].astype(F32), vp_ref[...].astype(F32), wv_ref[...], FFN_TAPS, first)
        o_ref[...] = (_gelu(gate + bg_ref[...]) * (val + bv_ref[...])).astype(o_ref.dtype)

    tile = lambda off: pl.BlockSpec((ts, tc), lambda j, i: (i, j + off))
    w = lambda off: pl.BlockSpec((8, tc), lambda j, i: (0, j + off))
    vec = lambda off: pl.BlockSpec((1, tc), lambda j, i: (0, j + off))
    return pl.pallas_call(
        body, name=name, grid=(nf, s // ts),
        in_specs=[tile(0), _prev_spec(ts, tc, hr, 0), tile(nf), _prev_spec(ts, tc, hr, nf), w(0), w(nf), vec(0), vec(nf)],
        out_specs=tile(0), out_shape=jax.ShapeDtypeStruct((s, f), MD),
        compiler_params=_params(("parallel", "parallel")),
    )(up_pre, up_pre, up_pre, up_pre, w8, w8, b, b)


def ffn_act_bwd(up_pre, w8, b, dact, name):
    s, c2 = up_pre.shape
    f = c2 // 2
    tc, ts = _tile(f, 512), _tile(s, 512, 16)
    nf = f // tc
    hr = _halo_rows(up_pre.dtype)

    def body(g_ref, gp_ref, v_ref, vp_ref, wg_ref, wv_ref, bg_ref, bv_ref, da_ref, o_ref):
        first = pl.program_id(1) == 0
        gate, _ = _conv_taps(g_ref[...].astype(F32), gp_ref[...].astype(F32), wg_ref[...], FFN_TAPS, first)
        val, _ = _conv_taps(v_ref[...].astype(F32), vp_ref[...].astype(F32), wv_ref[...], FFN_TAPS, first)
        ge, dge = _gelu_and_grad(gate + bg_ref[...])
        da = da_ref[...].astype(F32)
        is_gate = pl.program_id(0) < nf
        o_ref[...] = jnp.where(is_gate, da * (val + bv_ref[...]) * dge, da * ge).astype(o_ref.dtype)

    tile = lambda off: pl.BlockSpec((ts, tc), lambda j, i: (i, j % nf + off))
    prev = lambda off: pl.BlockSpec((hr, tc), lambda j, i: (jnp.maximum(i * (ts // hr) - 1, 0), j % nf + off))
    w = lambda off: pl.BlockSpec((8, tc), lambda j, i: (0, j % nf + off))
    vec = lambda off: pl.BlockSpec((1, tc), lambda j, i: (0, j % nf + off))
    return pl.pallas_call(
        body, name=name, grid=(2 * nf, s // ts),
        in_specs=[tile(0), prev(0), tile(nf), prev(nf), w(0), w(nf), vec(0), vec(nf), tile(0)],
        out_specs=pl.BlockSpec((ts, tc), lambda j, i: (i, j)), out_shape=jax.ShapeDtypeStruct((s, c2), MD),
        compiler_params=_params(("parallel", "parallel")),
    )(up_pre, up_pre, up_pre, up_pre, w8, w8, b, b, dact)


def _adam(w, g, m, v):
    m = ADAM_B1 * m + (1.0 - ADAM_B1) * g
    v = ADAM_B2 * v + (1.0 - ADAM_B2) * (g * g)
    m_hat = m / (1.0 - ADAM_B1 ** ADAM_STEP)
    v_hat = v / (1.0 - ADAM_B2 ** ADAM_STEP)
    delta = -ADAM_LR * (m_hat / (jnp.sqrt(v_hat) + ADAM_EPS) + ADAM_WD * w)
    return delta, m, v


def adamw_sharded(idx, g, recv_sib, recv_chips, w, m, v, name):
    r, c = w.shape
    tr = _tile(r, 512, 16)

    def body(idx_ref, g_ref, rs_ref, rc_ref, w_ref, m_ref, v_ref, go_ref, d_ref, mo_ref, vo_ref):
        grad = g_ref[...].astype(F32) + rs_ref[...].astype(F32)
        for j in range(3):
            grad = grad + rc_ref[j].astype(F32)
        go_ref[...] = grad
        d_ref[...], mo_ref[...], vo_ref[...] = _adam(w_ref[...], grad, m_ref[...], v_ref[...])

    row = pl.BlockSpec((tr, c), lambda i, idx: (i, 0))
    grid_spec = pltpu.PrefetchScalarGridSpec(
        num_scalar_prefetch=1, grid=(r // tr,),
        in_specs=[pl.BlockSpec((None, None, tr, c), lambda i, idx: (idx[1], idx[0], i, 0)),
                  pl.BlockSpec((None, tr, c), lambda i, idx: (idx[1], i, 0)),
                  pl.BlockSpec((3, tr, c), lambda i, idx: (0, i, 0)), row, row, row],
        out_specs=[row] * 4)
    out = jax.ShapeDtypeStruct((r, c), F32)
    return pl.pallas_call(body, name=name, grid_spec=grid_spec, out_shape=[out] * 4,
                          compiler_params=_params(("parallel",)))(idx, g, recv_sib, recv_chips, w, m, v)


def adamw_plain(g, w, m, v, name):
    r, c = w.shape
    tr = _tile(r, 512, 8)

    def body(g_ref, w_ref, m_ref, v_ref, d_ref, mo_ref, vo_ref):
        d_ref[...], mo_ref[...], vo_ref[...] = _adam(w_ref[...], g_ref[...], m_ref[...], v_ref[...])

    row = pl.BlockSpec((tr, c), lambda i: (i, 0))
    out = jax.ShapeDtypeStruct((r, c), F32)
    return pl.pallas_call(body, name=name, grid=(r // tr,), in_specs=[row] * 4, out_specs=[row] * 3, out_shape=[out] * 3,
                          compiler_params=_params(("parallel",)))(g, w, m, v)


def chip_partial(idx, g, recv_sib, name):
    _, _, r, c = g.shape
    tr = _tile(r, 512, 16)

    def body(idx_ref, g_ref, rs_ref, o_ref):
        o_ref[...] = (g_ref[...].astype(F32) + rs_ref[...].astype(F32)).astype(o_ref.dtype)

    grid_spec = pltpu.PrefetchScalarGridSpec(
        num_scalar_prefetch=1, grid=(3, r // tr),
        in_specs=[pl.BlockSpec((None, None, tr, c), lambda j, i, idx: (idx[2 + j], idx[0], i, 0)),
                  pl.BlockSpec((None, tr, c), lambda j, i, idx: (idx[2 + j], i, 0))],
        out_specs=pl.BlockSpec((None, tr, c), lambda j, i, idx: (j, i, 0)))
    return pl.pallas_call(body, name=name, grid_spec=grid_spec, out_shape=jax.ShapeDtypeStruct((3, r, c), MD),
                          compiler_params=_params(("parallel", "parallel")))(idx, g, recv_sib)


def sum_devices(parts, name):
    _, r, c = parts.shape
    tr = _tile(r, 256, 8)

    def body(p_ref, o_ref):
        acc = p_ref[0]
        for d in range(1, N_DEV):
            acc = acc + p_ref[d]
        o_ref[...] = acc

    return pl.pallas_call(
        body, name=name, grid=(r // tr,), in_specs=[pl.BlockSpec((N_DEV, tr, c), lambda i: (0, i, 0))],
        out_specs=pl.BlockSpec((tr, c), lambda i: (i, 0)), out_shape=jax.ShapeDtypeStruct((r, c), F32),
        compiler_params=_params(("parallel",)))(parts)


def _place():
    return lax.axis_index("x"), lax.axis_index("y"), lax.axis_index("c")


def _other_chips(x, y):
    return [(1 - x, y), (x, 1 - y), (1 - x, 1 - y)]


_ANY = pl.BlockSpec(memory_space=pl.ANY)


def all_gather(shard, name):
    r, c = shard.shape

    def body(x_ref, out_ref, send_sems, recv_sems, local_sem):
        x, y, cc = _place()
        me, sibling = (x, y, cc), (x, y, 1 - cc)
        chips = _other_chips(x, y)

        def slot(px, py, pc):
            return out_ref.at[4 * px + 2 * py + pc]

        def copy(k, block, to, src=None):
            return pltpu.make_async_remote_copy(
                src_ref=slot(*block) if src is None else src, dst_ref=slot(*block),
                send_sem=send_sems.at[k], recv_sem=recv_sems.at[k], device_id=to, device_id_type=MESH)

        mine = pltpu.make_async_copy(x_ref, slot(*me), local_sem)
        mine.start()
        first = [copy(0, me, sibling, src=x_ref)]
        first += [copy(1 + j, me, (*chip, cc), src=x_ref) for j, chip in enumerate(chips)]
        for cp in first:
            cp.start()
        passed = [copy(4 + j, (*chip, cc), sibling) for j, chip in enumerate(chips)]
        for j, chip in enumerate(chips):
            copy(1 + j, (*chip, cc), me).wait_recv()
            passed[j].start()
        copy(0, sibling, me).wait_recv()
        for j, chip in enumerate(chips):
            copy(4 + j, (*chip, 1 - cc), me).wait_recv()
        for cp in first + passed:
            cp.wait_send()
        mine.wait()

    return pl.pallas_call(
        body, name=name, out_shape=jax.ShapeDtypeStruct((N_DEV, r, c), shard.dtype),
        in_specs=[_ANY], out_specs=_ANY,
        scratch_shapes=[pltpu.SemaphoreType.DMA((7,)), pltpu.SemaphoreType.DMA((7,)), pltpu.SemaphoreType.DMA],
    )(shard)


def exchange_sibling(g, name):
    _, _, r, c = g.shape

    def body(g_ref, out_ref, send_sems, recv_sems):
        x, y, cc = _place()
        copies = [pltpu.make_async_remote_copy(
            src_ref=g_ref.at[k, 1 - cc], dst_ref=out_ref.at[k], send_sem=send_sems.at[k], recv_sem=recv_sems.at[k],
            device_id=(x, y, 1 - cc), device_id_type=MESH) for k in range(4)]
        for cp in copies:
            cp.start()
        for cp in copies:
            cp.wait_recv()
        for cp in copies:
            cp.wait_send()

    return pl.pallas_call(
        body, name=name, out_shape=jax.ShapeDtypeStruct((4, r, c), g.dtype), in_specs=[_ANY], out_specs=_ANY,
        scratch_shapes=[pltpu.SemaphoreType.DMA((4,)), pltpu.SemaphoreType.DMA((4,))],
    )(g)


def exchange_chips(p, name):
    _, r, c = p.shape

    def body(p_ref, out_ref, send_sems, recv_sems):
        x, y, cc = _place()
        copies = [pltpu.make_async_remote_copy(
            src_ref=p_ref.at[j], dst_ref=out_ref.at[j], send_sem=send_sems.at[j], recv_sem=recv_sems.at[j],
            device_id=(*chip, cc), device_id_type=MESH) for j, chip in enumerate(_other_chips(x, y))]
        for cp in copies:
            cp.start()
        for cp in copies:
            cp.wait_recv()
        for cp in copies:
            cp.wait_send()

    return pl.pallas_call(
        body, name=name, out_shape=jax.ShapeDtypeStruct((3, r, c), p.dtype), in_specs=[_ANY], out_specs=_ANY,
        scratch_shapes=[pltpu.SemaphoreType.DMA((3,)), pltpu.SemaphoreType.DMA((3,))],
    )(p)


def _step_local(x, mem, ang2, target, W):
    d = x.shape[1]
    wq = HEADS * QK_ROPE
    o_qr, o_lx, o_ly, o_gm, o_gl = d, d + wq, 2 * d + wq, 3 * d + wq, 4 * d + wq
    one = lambda a, b, mode, dt, name, **kw: mm([(a, b)], mode, dt, name, **kw)

    h1 = rms_fwd(x, W["g_pre_mix"], "h1")
    proj = one(h1, W["w_main"], "nn", MD, "proj")
    ckv = one(h1, W["w_ckv"], "nn", MD, "ckv")
    kr = one(h1, W["w_kr"], "nn", MD, "k_rope")
    ckv_n = rms_fwd(ckv, W["g_ckv"], "ckv_n")
    kn = one(ckv_n, W["w_uk"], "nn", MD, "k_nope")
    v = one(ckv_n, W["w_uv"], "nn", MD, "v")
    qrp, krp = rope_fwd(proj, kr, ang2, o_qr // wq, "rope")
    o, lse = flash_fwd(proj, qrp, kn, krp, v, "attn")
    o_mla = one(o, W["w_o_mla"], "nn", F32, "o_mla")
    lru_w = (W["w_conv8"], W["b_conv_lru"], W["w_rg"], W["b_rg"], W["w_ig"], W["b_ig"], W["lru_lambda"])
    hr, hg = lru_fwd(proj, o_lx, o_ly, *lru_w, "lru")
    o_lru = one(hg, W["w_o_lru"], "nn", F32, "o_lru")
    merged = merge_fwd(proj, o_gm, o_gl, o_mla, o_lru, "merge")
    y1 = one(merged, W["w_out"], "nn", F32, "y1")
    x1, h2 = post_pre_fwd(x, y1, W["g_post_mix"], W["g_pre_x"], "x1")
    mn = rms_fwd(mem, W["g_mem"], "mem_n")
    ck = one(mn, W["w_ck"], "nn", MD, "ck")
    cv = one(mn, W["w_cv"], "nn", MD, "cv")
    cq = one(h2, W["w_cq"], "nn", MD, "cq")
    co = xattn_fwd(cq, ck, cv, "xattn")
    y2 = one(co, W["w_co"], "nn", F32, "y2")
    x2, h3 = post_pre_fwd(x1, y2, W["g_post_x"], W["g_pre_ffn"], "x2")
    up_pre = one(h3, W["w_up"], "nn", MD, "up_pre")
    act = ffn_act_fwd(up_pre, W["w_fconv8"], W["b_fconv"], "act")
    y3 = one(act, W["w_down"], "nn", F32, "y3")
    G = {}
    loss, dx3, dy3, G["g_post_ffn"] = loss_head(x2, y3, target, W["g_post_ffn"], "loss")

    G["w_down"] = one(act, dy3, "tn", F32, "dw_down")
    dact = one(dy3, W["w_down"], "nt", MD, "dact")
    dup = ffn_act_bwd(up_pre, W["w_fconv8"], W["b_fconv"], dact, "dup")
    dup_pre, dwf8, G["b_fconv"] = conv_bwd(up_pre, dup, W["w_fconv8"], FFN_TAPS, 0, "ffn_conv_bwd")
    G["w_fconv"] = dwf8[:FFN_TAPS]
    G["w_up"] = one(h3, dup_pre, "tn", F32, "dw_up")
    dh3 = one(dup_pre, W["w_up"], "nt", F32, "dh3")
    dx2, dy2, G["g_pre_ffn"], G["g_post_x"] = pre_post_bwd(x2, y2, dx3, dh3, W["g_pre_ffn"], W["g_post_x"], "dx2")
    G["w_co"] = one(co, dy2, "tn", F32, "dw_co")
    dco = one(dy2, W["w_co"], "nt", MD, "dco")
    dcq, dck, dcv = xattn_bwd(cq, ck, cv, dco, "xattn_bwd")
    dck, dcv = dck.astype(MD), dcv.astype(MD)
    G["w_cq"] = one(h2, dcq, "tn", F32, "dw_cq")
    G["w_ck"] = one(mn, dck, "tn", F32, "dw_ck")
    G["w_cv"] = one(mn, dcv, "tn", F32, "dw_cv")
    dmn = mm([(dck, W["w_ck"]), (dcv, W["w_cv"])], "nt", F32, "dmem_n")
    _, G["g_mem"] = rms_bwd(mem, W["g_mem"], dmn, "dg_mem")
    dh2 = one(dcq, W["w_cq"], "nt", F32, "dh2")
    dx1, dy1, G["g_pre_x"], G["g_post_mix"] = pre_post_bwd(x1, y1, dx2, dh2, W["g_pre_x"], W["g_post_mix"], "dx1")
    G["w_out"] = one(merged, dy1, "tn", F32, "dw_out")
    dmerged = one(dy1, W["w_out"], "nt", F32, "dmerged")
    do_mla, do_lru, dgm, dgl = merge_bwd(proj, o_gm, o_gl, o_mla, o_lru, dmerged, "merge_bwd")
    G["w_o_mla"] = one(o, do_mla, "tn", F32, "dw_o_mla")
    do = one(do_mla, W["w_o_mla"], "nt", MD, "do")
    G["w_o_lru"] = one(hg, do_lru, "tn", F32, "dw_o_lru")
    dhg = one(do_lru, W["w_o_lru"], "nt", F32, "dhg")
    dxc, dly, G["w_rg"], G["w_ig"], G["b_rg"], G["b_ig"], G["lru_lambda"] = lru_bwd(proj, o_lx, o_ly, hr, dhg, *lru_w, "lru_bwd")
    dlx, dwc8, G["b_conv_lru"] = conv_bwd(proj, dxc, W["w_conv8"], LRU_TAPS, o_lx, "lru_conv_bwd")
    G["w_conv_lru"] = dwc8[:LRU_TAPS]
    delta = attn_delta(o, do, "attn_delta")
    dqn, dqrp, dkn, dkrh, dv = flash_bwd(proj, qrp, kn, krp, v, do, lse, delta, "attn_bwd")
    dqr, dkr = rope_bwd(dqrp, dkrh, ang2, "rope_bwd")
    G["w_uk"] = one(ckv_n, dkn, "tn", F32, "dw_uk")
    G["w_uv"] = one(ckv_n, dv, "tn", F32, "dw_uv")
    dckv_n = mm([(dkn, W["w_uk"]), (dv, W["w_uv"])], "nt", F32, "dckv_n")
    dckv, G["g_ckv"] = rms_bwd(ckv, W["g_ckv"], dckv_n, "dckv")
    dproj = jnp.concatenate([dqn, dqr, dlx, dly, dgm, dgl], axis=1)
    parts = [("w_main", dproj), ("w_ckv", dckv), ("w_kr", dkr)]
    for n, dpart in parts:
        G[n] = one(h1, dpart, "tn", F32, "d" + n)
    dh1 = mm([(dpart, W[n]) for n, dpart in parts], "nt", F32, "dh1", tk=1024)
    grad_x, G["g_pre_mix"] = pre_bwd(x, dx1, dh1, W["g_pre_mix"], "grad_x")
    return loss, grad_x, G


FLAT_COLS = 1024
BIG = (("w_in", 1), ("w_ukv", 1), ("w_o_mla", 0), ("w_o_lru", 0), ("w_out", 0), ("w_cq", 0), ("w_ck", 0), ("w_cv", 0),
       ("w_co", 1), ("w_up", 1), ("w_down", 0))
SMALL_SHARDED = ("w_conv_lru", "w_fconv")
REPLICATED = ("g_pre_mix", "g_post_mix", "g_ckv", "b_conv_lru", "w_rg", "b_rg", "w_ig", "b_ig", "lru_lambda",
              "g_pre_x", "g_post_x", "g_mem", "g_pre_ffn", "g_post_ffn", "b_fconv")
WEIGHTS = ("g_pre_mix", "g_post_mix", "w_in", "g_ckv", "w_ukv", "w_o_mla", "w_conv_lru", "b_conv_lru", "w_rg", "b_rg",
           "w_ig", "b_ig", "lru_lambda", "w_o_lru", "w_out", "g_pre_x", "g_post_x", "g_mem", "w_cq", "w_ck", "w_cv",
           "w_co", "g_pre_ffn", "g_post_ffn", "w_up", "w_fconv", "b_fconv", "w_down")


def _round_up(n, k):
    return (n + k - 1) // k * k


def _pack_rows(parts, row_mult, dtype):
    rows = [p.reshape(-1, FLAT_COLS) for p in parts]
    n = sum(r.shape[0] for r in rows)
    pad = _round_up(n, row_mult) - n
    if pad:
        rows.append(jnp.zeros((pad, FLAT_COLS), rows[0].dtype))
    return jnp.concatenate(rows, axis=0).astype(dtype)


def _pack_vec(parts, row_mult):
    flat = jnp.concatenate([p.reshape(-1) for p in parts])
    n = flat.shape[0]
    total = _round_up(n, row_mult * FLAT_COLS)
    return jnp.pad(flat, (0, total - n)).reshape(-1, FLAT_COLS)


def _unpack_vec(flat2d, shapes):
    flat = flat2d.reshape(-1)
    out, off = [], 0
    for shp in shapes:
        n = int(np.prod(shp))
        out.append(flat[off:off + n].reshape(shp))
        off += n
    return out


def _gathered_to_full(blocks, axis):
    n, r, c = blocks.shape
    return blocks.reshape(n * r, c) if axis == 0 else blocks.transpose(1, 0, 2).reshape(r, n * c)


def _full_to_shards(full, axis):
    r, c = full.shape
    if axis == 0:
        return full.reshape(N_DEV, r // N_DEV, c)
    return full.reshape(r, N_DEV, c // N_DEV).transpose(1, 0, 2)


def _split_w_in(w_in, d):
    hq = HEADS * (QK_NOPE + QK_ROPE)
    q = w_in[:, :hq].reshape(d, HEADS, QK_NOPE + QK_ROPE)
    out = {"w_qn": q[:, :, :QK_NOPE].reshape(d, HEADS * QK_NOPE), "w_qr": q[:, :, QK_NOPE:].reshape(d, HEADS * QK_ROPE)}
    off = hq
    out["w_ckv"] = w_in[:, off:off + KV_RANK]
    off += KV_RANK
    out["w_kr"] = jnp.pad(w_in[:, off:off + QK_ROPE], ((0, 0), (0, LANES - QK_ROPE)))
    off += QK_ROPE
    for n in ("w_lx", "w_ly", "w_gm", "w_gl"):
        out[n] = w_in[:, off:off + d]
        off += d
    out["w_main"] = jnp.concatenate([out[n] for n in ("w_qn", "w_qr", "w_lx", "w_ly", "w_gm", "w_gl")], axis=1)
    return out


def _join_w_in(G, d):
    gm = G["w_main"]
    n_qn, n_qr = HEADS * QK_NOPE, HEADS * QK_ROPE
    q = jnp.concatenate([gm[:, :n_qn].reshape(d, HEADS, QK_NOPE), gm[:, n_qn:n_qn + n_qr].reshape(d, HEADS, QK_ROPE)], axis=2)
    return jnp.concatenate([q.reshape(d, -1), G["w_ckv"], G["w_kr"][:, :QK_ROPE], gm[:, n_qn + n_qr:]], axis=1)


def kernel(x, mem, positions, g_pre_mix, g_post_mix, w_in, g_ckv, w_ukv, w_o_mla, w_conv_lru, b_conv_lru, w_rg, b_rg, w_ig, b_ig, lru_lambda, w_o_lru, w_out, g_pre_x, g_post_x, g_mem, w_cq, w_ck, w_cv, w_co, g_pre_ffn, g_post_ffn, w_up, w_fconv, b_fconv, w_down, loss_target, m_g_pre_mix, m_g_post_mix, m_w_in, m_g_ckv, m_w_ukv, m_w_o_mla, m_w_conv_lru, m_b_conv_lru, m_w_rg, m_b_rg, m_w_ig, m_b_ig, m_lru_lambda, m_w_o_lru, m_w_out, m_g_pre_x, m_g_post_x, m_g_mem, m_w_cq, m_w_ck, m_w_cv, m_w_co, m_g_pre_ffn, m_g_post_ffn, m_w_up, m_w_fconv, m_b_fconv, m_w_down, v_g_pre_mix, v_g_post_mix, v_w_in, v_g_ckv, v_w_ukv, v_w_o_mla, v_w_conv_lru, v_b_conv_lru, v_w_rg, v_b_rg, v_w_ig, v_b_ig, v_lru_lambda, v_w_o_lru, v_w_out, v_g_pre_x, v_g_post_x, v_g_mem, v_w_cq, v_w_ck, v_w_cv, v_w_co, v_g_pre_ffn, v_g_post_ffn, v_w_up, v_w_fconv, v_b_fconv, v_w_down):
    given = dict(locals())
    P = {n: given[n][0] for n in WEIGHTS}
    M = {n: given["m_" + n][0] for n in WEIGHTS}
    V = {n: given["v_" + n][0] for n in WEIGHTS}
    xs, mems, tgt = x[0], mem[0], loss_target[0]
    s, d = xs.shape
    ax, ay, ac = _place()
    dev = 4 * ax + 2 * ay + ac

    big_shapes = [P[n].shape for n, _ in BIG]
    gathered = all_gather(_pack_rows([P[n] for n, _ in BIG], 512, MD), "gather_weights")
    full, row = {}, 0
    for (n, axis), shp in zip(BIG, big_shapes):
        nrows = shp[0] * shp[1] // FLAT_COLS
        full[n] = _gathered_to_full(gathered[:, row:row + nrows].reshape(N_DEV, *shp), axis)
        row += nrows
    small_shapes = [P[n].shape for n in SMALL_SHARDED]
    taps = all_gather(_pack_vec([P[n] for n in SMALL_SHARDED], 8), "gather_taps")
    for n, blocks in zip(SMALL_SHARDED, zip(*[_unpack_vec(taps[k], small_shapes) for k in range(N_DEV)])):
        full[n] = _gathered_to_full(jnp.stack(blocks), 1)

    W = _split_w_in(full["w_in"], d)
    kv = full["w_ukv"].reshape(KV_RANK, HEADS, QK_NOPE + V_DIM)
    W["w_uk"] = kv[:, :, :QK_NOPE].reshape(KV_RANK, HEADS * QK_NOPE)
    W["w_uv"] = kv[:, :, QK_NOPE:].reshape(KV_RANK, HEADS * V_DIM)
    for n in ("w_o_mla", "w_o_lru", "w_out", "w_cq", "w_ck", "w_cv", "w_co", "w_up", "w_down"):
        W[n] = full[n]
    for n in REPLICATED:
        W[n] = P[n]
    W["w_rg"], W["w_ig"] = P["w_rg"].astype(MD), P["w_ig"].astype(MD)
    W["b_rg"], W["b_ig"] = P["b_rg"].reshape(1, -1), P["b_ig"].reshape(1, -1)
    for n in ("g_pre_mix", "g_post_mix", "g_ckv", "b_conv_lru", "lru_lambda", "g_pre_x", "g_post_x", "g_mem", "g_pre_ffn",
              "g_post_ffn", "b_fconv"):
        W[n] = P[n].reshape(1, -1)
    W["w_conv8"] = jnp.pad(full["w_conv_lru"], ((0, 8 - LRU_TAPS), (0, 0)))
    W["w_fconv8"] = jnp.pad(full["w_fconv"], ((0, 8 - FFN_TAPS), (0, 0)))

    inv_freq = ROPE_THETA ** (-jnp.arange(0, QK_ROPE, 2, dtype=F32) / QK_ROPE)
    ang = positions[0].astype(F32)[:, None] * inv_freq
    ang2 = jnp.tile(ang, (1, LANES // (QK_ROPE // 2)))

    loss, grad_x, G = _step_local(xs, mems, ang2, tgt, W)
    loss = lax.psum(loss[0, 0], ("x", "y", "c"))

    G["w_in"] = _join_w_in(G, d)
    G["w_ukv"] = jnp.concatenate([G["w_uk"].reshape(KV_RANK, HEADS, QK_NOPE), G["w_uv"].reshape(KV_RANK, HEADS, V_DIM)], axis=2).reshape(KV_RANK, -1)
    slabs = [_full_to_shards(G[n], axis).reshape(N_DEV, -1, FLAT_COLS) for n, axis in BIG]
    nrows = sum(sl.shape[1] for sl in slabs)
    r_pad = _round_up(nrows, 512)
    if r_pad > nrows:
        slabs.append(jnp.zeros((N_DEV, r_pad - nrows, FLAT_COLS), F32))
    gbuf = jnp.concatenate(slabs, axis=1).astype(MD).reshape(4, 2, r_pad, FLAT_COLS)
    idx = jnp.stack([ac, 2 * ax + ay, 2 * (1 - ax) + ay, 2 * ax + (1 - ay), 2 * (1 - ax) + (1 - ay)]).astype(jnp.int32)
    from_sibling = exchange_sibling(gbuf, "grads_to_sibling")
    chip_sums = chip_partial(idx, gbuf, from_sibling, "chip_partial")
    from_chips = exchange_chips(chip_sums, "grads_to_chips")
    pack = lambda src: _pack_rows([src[n] for n, _ in BIG], 512, F32)
    big_out = adamw_sharded(idx, gbuf, from_sibling, from_chips, pack(P), pack(M), pack(V), "adamw_matrices")
    res = {k: {} for k in ("grad", "delta", "m", "v")}
    for kind, flat in zip(("grad", "delta", "m", "v"), big_out):
        row = 0
        for (n, _), shp in zip(BIG, big_shapes):
            nr = shp[0] * shp[1] // FLAT_COLS
            res[kind][n] = flat[row:row + nr].reshape(shp)
            row += nr

    small_names = REPLICATED + SMALL_SHARDED
    g_small_shapes = [G[n].shape for n in small_names]
    partials = all_gather(_pack_vec([G[n] for n in small_names], 128), "gather_small_grads")
    summed = dict(zip(small_names, _unpack_vec(sum_devices(partials, "sum_small_grads"), g_small_shapes)))
    for n in SMALL_SHARDED:
        t, c = summed[n].shape
        summed[n] = lax.dynamic_index_in_dim(summed[n].reshape(t, N_DEV, c // N_DEV), dev, axis=1, keepdims=False)
    sg = {n: summed[n].reshape(P[n].shape) for n in small_names}
    vec = lambda src: _pack_vec([src[n] for n in small_names], 128)
    small_out = adamw_plain(vec(sg), vec(P), vec(M), vec(V), "adamw_small")
    shapes = [P[n].shape for n in small_names]
    for kind, flat in zip(("delta", "m", "v"), small_out):
        res[kind].update(zip(small_names, _unpack_vec(flat, shapes)))
    res["grad"].update(sg)

    outs = [loss, grad_x[None]]
    for kind in ("grad", "delta", "m", "v"):
        outs += [res[kind][n][None] for n in WEIGHTS]
    return tuple(outs)
```

```python
import numpy as np
import jax
import jax.numpy as jnp
from jax import lax
from jax.experimental import pallas as pl
from jax.experimental.pallas import tpu as pltpu

MD = jnp.bfloat16
F32 = jnp.float32
EPS = 1e-6
N_DEV = 8
HEADS = 16
QK_NOPE = 128
QK_ROPE = 64
V_DIM = 128
QK_PAD = 256
KV_RANK = 512
SOFTMAX_SCALE = (QK_NOPE + QK_ROPE) ** -0.5
LOG2E = 1.4426950408889634
ROPE_THETA = 10000.0
LRU_BLOCKS = 16
LRU_C = 8.0
X_HEADS = 4
X_DIM = 128
LRU_TAPS = 4
FFN_TAPS = 3
ADAM_LR, ADAM_B1, ADAM_B2, ADAM_EPS, ADAM_WD, ADAM_STEP = 0.001, 0.9, 0.999, 1e-08, 0.01, 10
LANES = 128
VMEM_LIMIT = 52 * 1024 * 1024
MESH = pl.DeviceIdType.MESH

NN = (((1,), (0,)), ((), ()))
NT = (((1,), (1,)), ((), ()))
TN = (((0,), (0,)), ((), ()))


def _tile(n, pref, align=LANES):
    if n <= pref:
        return n
    d = (pref // align) * align
    while d > align and n % d:
        d -= align
    assert n % d == 0, (n, pref, align)
    return d


def _params(sem):
    return pltpu.CompilerParams(dimension_semantics=sem, vmem_limit_bytes=VMEM_LIMIT)


def _dot(a, b, dims=NN):
    return lax.dot_general(a, b, dims, preferred_element_type=F32)


def _sigmoid(x):
    return 1.0 / (1.0 + jnp.exp(-x))


_GELU_C = 0.7978845608028654
_GELU_A = 0.044715


def _gelu(x):
    return 0.5 * x * (1.0 + jnp.tanh(x * (_GELU_C + (_GELU_C * _GELU_A) * (x * x))))


def _gelu_and_grad(x):
    x2 = x * x
    t = jnp.tanh(x * (_GELU_C + (_GELU_C * _GELU_A) * x2))
    hx = 0.5 * x
    g = hx + hx * t
    dg = 0.5 + 0.5 * t + hx * (1.0 - t * t) * (_GELU_C + (3.0 * _GELU_C * _GELU_A) * x2)
    return g, dg


def _expm1(x):
    small = x * (1.0 + x * (0.5 + x * (1.0 / 6.0 + x * (1.0 / 24.0 + x * (1.0 / 120.0)))))
    return jnp.where(jnp.abs(x) < 0.05, small, jnp.exp(x) - 1.0)


def _softplus(x):
    return jnp.maximum(x, 0.0) + jnp.log(1.0 + jnp.exp(-jnp.abs(x)))


def _rms_hat(x):
    r = lax.rsqrt(jnp.mean(x * x, axis=-1, keepdims=True) + EPS)
    return x * r, r


def _rms_bwd(x, g, dn):
    xh, r = _rms_hat(x)
    dg = jnp.sum(dn * xh, axis=0, keepdims=True)
    dxh = dn * g
    dx = r * (dxh - xh * jnp.mean(dxh * xh, axis=-1, keepdims=True))
    return dx, dg


def _acc_out(ref, val, first):
    @pl.when(first)
    def _():
        ref[...] = val

    @pl.when(jnp.logical_not(first))
    def _():
        ref[...] += val


def _lane_sums_as_row(x):
    ones = jnp.ones((8, LANES), F32)
    return lax.dot_general(ones, x, NT, precision=lax.Precision.HIGHEST, preferred_element_type=F32)[0:1, :]


def mm(pairs, mode, out_dtype, name, tm=1024, tn=1024, tk=2048):
    dims = {"nn": NN, "nt": NT, "tn": TN}[mode]
    shapes = []
    for a, b in pairs:
        if mode == "nn":
            (m, k), (k2, n) = a.shape, b.shape
        elif mode == "nt":
            (m, k), (n, k2) = a.shape, b.shape
        else:
            (k, m), (k2, n) = a.shape, b.shape
        assert k == k2, (name, a.shape, b.shape)
        shapes.append((m, n, k))
    m, n = shapes[0][0], shapes[0][1]
    assert all(s[0] == m and s[1] == n for s in shapes)
    tm, tn = _tile(m, tm), _tile(n, tn)
    tks = [_tile(s[2], tk) for s in shapes]
    nks = [s[2] // t for s, t in zip(shapes, tks)]
    starts = [sum(nks[:p]) for p in range(len(pairs))]
    nk_total = sum(nks)

    in_specs, args = [], []
    for p, (a, b) in enumerate(pairs):
        def kk(k, p=p):
            return jnp.clip(k - starts[p], 0, nks[p] - 1)
        if mode == "tn":
            in_specs.append(pl.BlockSpec((tks[p], tm), lambda i, j, k, kk=kk: (kk(k), i)))
        else:
            in_specs.append(pl.BlockSpec((tm, tks[p]), lambda i, j, k, kk=kk: (i, kk(k))))
        if mode == "nt":
            in_specs.append(pl.BlockSpec((tn, tks[p]), lambda i, j, k, kk=kk: (j, kk(k))))
        else:
            in_specs.append(pl.BlockSpec((tks[p], tn), lambda i, j, k, kk=kk: (kk(k), j)))
        args += [a, b]

    def body(*refs):
        ins, o_ref, scratch = refs[:2 * len(pairs)], refs[2 * len(pairs)], refs[2 * len(pairs) + 1:]
        if nk_total == 1:
            o_ref[...] = _dot(ins[0][...], ins[1][...], dims).astype(o_ref.dtype)
            return
        acc_ref, = scratch
        k = pl.program_id(2)
        for p in range(len(pairs)):
            @pl.when(jnp.logical_and(k >= starts[p], k < starts[p] + nks[p]))
            def _(p=p):
                prod = _dot(ins[2 * p][...], ins[2 * p + 1][...], dims)

                @pl.when(k == 0)
                def _():
                    acc_ref[...] = prod

                @pl.when(k > 0)
                def _():
                    acc_ref[...] += prod

        @pl.when(k == nk_total - 1)
        def _():
            o_ref[...] = acc_ref[...].astype(o_ref.dtype)

    return pl.pallas_call(
        body, name=name, grid=(m // tm, n // tn, nk_total),
        in_specs=in_specs, out_specs=pl.BlockSpec((tm, tn), lambda i, j, k: (i, j)),
        out_shape=jax.ShapeDtypeStruct((m, n), out_dtype),
        scratch_shapes=[] if nk_total == 1 else [pltpu.VMEM((tm, tn), F32)],
        compiler_params=_params(("parallel", "parallel", "arbitrary")),
    )(*args)


def rms_fwd(x, g, name):
    s, d = x.shape
    ts = _tile(s, 512, 16)

    def body(x_ref, g_ref, o_ref):
        xh, _ = _rms_hat(x_ref[...].astype(F32))
        o_ref[...] = (xh * g_ref[...]).astype(o_ref.dtype)

    return pl.pallas_call(
        body, name=name, grid=(s // ts,),
        in_specs=[pl.BlockSpec((ts, d), lambda i: (i, 0)), pl.BlockSpec((1, d), lambda i: (0, 0))],
        out_specs=pl.BlockSpec((ts, d), lambda i: (i, 0)),
        out_shape=jax.ShapeDtypeStruct((s, d), MD),
        compiler_params=_params(("parallel",)),
    )(x, g)


def rms_bwd(x, g, dn, name):
    s, d = x.shape
    ts = _tile(s, 256, 16)

    def body(x_ref, g_ref, dn_ref, dx_ref, dg_ref):
        dx, dg = _rms_bwd(x_ref[...].astype(F32), g_ref[...], dn_ref[...].astype(F32))
        dx_ref[...] = dx.astype(dx_ref.dtype)
        _acc_out(dg_ref, dg, pl.program_id(0) == 0)

    row = pl.BlockSpec((ts, d), lambda i: (i, 0))
    vec = pl.BlockSpec((1, d), lambda i: (0, 0))
    return pl.pallas_call(
        body, name=name, grid=(s // ts,), in_specs=[row, vec, row], out_specs=[row, vec],
        out_shape=[jax.ShapeDtypeStruct((s, d), MD), jax.ShapeDtypeStruct((1, d), F32)],
        compiler_params=_params(("arbitrary",)),
    )(x, g, dn)


def post_pre_fwd(x, y, g_post, g_pre, name):
    s, d = x.shape
    ts = _tile(s, 256, 16)

    def body(x_ref, y_ref, gp_ref, gq_ref, xn_ref, h_ref):
        yh, _ = _rms_hat(y_ref[...])
        xn = x_ref[...] + yh * gp_ref[...]
        xn_ref[...] = xn
        xh, _ = _rms_hat(xn)
        h_ref[...] = (xh * gq_ref[...]).astype(h_ref.dtype)

    row = pl.BlockSpec((ts, d), lambda i: (i, 0))
    vec = pl.BlockSpec((1, d), lambda i: (0, 0))
    return pl.pallas_call(
        body, name=name, grid=(s // ts,), in_specs=[row, row, vec, vec], out_specs=[row, row],
        out_shape=[jax.ShapeDtypeStruct((s, d), F32), jax.ShapeDtypeStruct((s, d), MD)],
        compiler_params=_params(("parallel",)),
    )(x, y, g_post, g_pre)


def loss_head(x, y, target, g_post, name):
    s, d = x.shape
    ts = _tile(s, 256, 16)

    def body(x_ref, y_ref, t_ref, g_ref, loss_ref, dx_ref, dy_ref, dg_ref):
        first = pl.program_id(0) == 0
        y = y_ref[...]
        yh, _ = _rms_hat(y)
        diff = x_ref[...] + yh * g_ref[...] - t_ref[...]
        part = 0.5 * jnp.sum(jnp.sum(diff * diff, axis=1, keepdims=True) * (1.0 / d), axis=0, keepdims=True)
        _acc_out(loss_ref, part, first)
        dx = diff * (1.0 / d)
        dx_ref[...] = dx
        dy, dg = _rms_bwd(y, g_ref[...], dx)
        dy_ref[...] = dy.astype(dy_ref.dtype)
        _acc_out(dg_ref, dg, first)

    row = pl.BlockSpec((ts, d), lambda i: (i, 0))
    vec = pl.BlockSpec((1, d), lambda i: (0, 0))
    one = pl.BlockSpec((1, 1), lambda i: (0, 0))
    return pl.pallas_call(
        body, name=name, grid=(s // ts,), in_specs=[row, row, row, vec], out_specs=[one, row, row, vec],
        out_shape=[jax.ShapeDtypeStruct((1, 1), F32), jax.ShapeDtypeStruct((s, d), F32),
                   jax.ShapeDtypeStruct((s, d), MD), jax.ShapeDtypeStruct((1, d), F32)],
        compiler_params=_params(("arbitrary",)),
    )(x, y, target, g_post)


def pre_post_bwd(x, y, dx_res, dh, g_pre, g_post, name):
    s, d = x.shape
    ts = _tile(s, 256, 16)

    def body(x_ref, y_ref, dr_ref, dh_ref, gq_ref, gp_ref, dx_ref, dy_ref, dgq_ref, dgp_ref):
        first = pl.program_id(0) == 0
        dxa, dgq = _rms_bwd(x_ref[...], gq_ref[...], dh_ref[...])
        dx = dr_ref[...] + dxa
        dx_ref[...] = dx
        dy, dgp = _rms_bwd(y_ref[...], gp_ref[...], dx)
        dy_ref[...] = dy.astype(dy_ref.dtype)
        _acc_out(dgq_ref, dgq, first)
        _acc_out(dgp_ref, dgp, first)

    row = pl.BlockSpec((ts, d), lambda i: (i, 0))
    vec = pl.BlockSpec((1, d), lambda i: (0, 0))
    return pl.pallas_call(
        body, name=name, grid=(s // ts,), in_specs=[row, row, row, row, vec, vec], out_specs=[row, row, vec, vec],
        out_shape=[jax.ShapeDtypeStruct((s, d), F32), jax.ShapeDtypeStruct((s, d), MD),
                   jax.ShapeDtypeStruct((1, d), F32), jax.ShapeDtypeStruct((1, d), F32)],
        compiler_params=_params(("arbitrary",)),
    )(x, y, dx_res, dh, g_pre, g_post)


def pre_bwd(x, dx_res, dh, g_pre, name):
    s, d = x.shape
    ts = _tile(s, 256, 16)

    def body(x_ref, dr_ref, dh_ref, g_ref, dx_ref, dg_ref):
        dxa, dg = _rms_bwd(x_ref[...], g_ref[...], dh_ref[...])
        dx_ref[...] = dr_ref[...] + dxa
        _acc_out(dg_ref, dg, pl.program_id(0) == 0)

    row = pl.BlockSpec((ts, d), lambda i: (i, 0))
    vec = pl.BlockSpec((1, d), lambda i: (0, 0))
    return pl.pallas_call(
        body, name=name, grid=(s // ts,), in_specs=[row, row, row, vec], out_specs=[row, vec],
        out_shape=[jax.ShapeDtypeStruct((s, d), F32), jax.ShapeDtypeStruct((1, d), F32)],
        compiler_params=_params(("arbitrary",)),
    )(x, dx_res, dh, g_pre)


def _swap_halves(x):
    lane = lax.broadcasted_iota(jnp.int32, x.shape, 1)
    return jnp.where((lane % QK_ROPE) < QK_ROPE // 2, pltpu.roll(x, LANES - QK_ROPE // 2, 1),
                     pltpu.roll(x, QK_ROPE // 2, 1))


def _rope_tables(ang_ref, sign):
    lane = lax.broadcasted_iota(jnp.int32, ang_ref.shape, 1)
    sgn = jnp.where((lane % QK_ROPE) < QK_ROPE // 2, -sign, sign)
    ang = ang_ref[...]
    return jnp.cos(ang), jnp.sin(ang) * sgn


def rope_fwd(proj, kr, kn, ang2, name):
    s = proj.shape[0]
    ts = _tile(s, 256, 16)
    wn, wr = HEADS * QK_NOPE, HEADS * QK_ROPE

    def body(qn_ref, qr_ref, kr_ref, kn_ref, ang_ref, qo_ref, ko_ref):
        c, sn = _rope_tables(ang_ref, 1.0)
        lane = lax.broadcasted_iota(jnp.int32, (ts, LANES), 1)
        lo = lane < QK_ROPE
        x = kr_ref[...].astype(F32)
        k_rot = jnp.where(lo, x * c + _swap_halves(x) * sn, 0.0).astype(ko_ref.dtype)
        for j in range(HEADS // 2):
            x = qr_ref[:, j * LANES:(j + 1) * LANES].astype(F32)
            r = x * c + _swap_halves(x) * sn
            pair = (jnp.where(lo, r, 0.0), jnp.where(lo, pltpu.roll(r, QK_ROPE, 1), 0.0))
            for e in range(2):
                h = 2 * j + e
                qo_ref[:, h * QK_PAD:h * QK_PAD + LANES] = qn_ref[:, h * LANES:(h + 1) * LANES]
                qo_ref[:, h * QK_PAD + LANES:(h + 1) * QK_PAD] = pair[e].astype(qo_ref.dtype)
                ko_ref[:, h * QK_PAD:h * QK_PAD + LANES] = kn_ref[:, h * LANES:(h + 1) * LANES]
                ko_ref[:, h * QK_PAD + LANES:(h + 1) * QK_PAD] = k_rot

    row = lambda w, blk: pl.BlockSpec((ts, w), lambda i: (i, blk))
    return pl.pallas_call(
        body, name=name, grid=(s // ts,),
        in_specs=[row(wn, 0), row(wr, wn // wr), row(LANES, 0), row(wn, 0), row(LANES, 0)],
        out_specs=[row(HEADS * QK_PAD, 0), row(HEADS * QK_PAD, 0)],
        out_shape=[jax.ShapeDtypeStruct((s, HEADS * QK_PAD), MD), jax.ShapeDtypeStruct((s, HEADS * QK_PAD), MD)],
        compiler_params=_params(("parallel",)),
    )(proj, proj, kr, kn, ang2)


def rope_bwd(dqf, dkf, ang2, name):
    s = dqf.shape[0]
    ts = _tile(s, 256, 16)
    wn, wr = HEADS * QK_NOPE, HEADS * QK_ROPE

    def body(dq_ref, dk_ref, ang_ref, qn_ref, qr_ref, kn_ref, kr_ref):
        c, sn = _rope_tables(ang_ref, -1.0)
        lane = lax.broadcasted_iota(jnp.int32, (ts, LANES), 1)
        lo = lane < QK_ROPE
        rot = lambda dy: dy * c + _swap_halves(dy) * sn
        dk_rope = None
        for j in range(HEADS // 2):
            halves = []
            for e in range(2):
                h = 2 * j + e
                qn_ref[:, h * LANES:(h + 1) * LANES] = dq_ref[:, h * QK_PAD:h * QK_PAD + LANES]
                kn_ref[:, h * LANES:(h + 1) * LANES] = dk_ref[:, h * QK_PAD:h * QK_PAD + LANES]
                halves.append(jnp.where(lo, dq_ref[:, h * QK_PAD + LANES:(h + 1) * QK_PAD].astype(F32), 0.0))
                dk_h = dk_ref[:, h * QK_PAD + LANES:(h + 1) * QK_PAD].astype(F32)
                dk_rope = dk_h if dk_rope is None else dk_rope + dk_h
            qr_ref[:, j * LANES:(j + 1) * LANES] = rot(halves[0] + pltpu.roll(halves[1], QK_ROPE, 1)).astype(qr_ref.dtype)
        kr_ref[...] = jnp.where(lo, rot(jnp.where(lo, dk_rope, 0.0)), 0.0).astype(kr_ref.dtype)

    row = lambda w: pl.BlockSpec((ts, w), lambda i: (i, 0))
    return pl.pallas_call(
        body, name=name, grid=(s // ts,),
        in_specs=[row(HEADS * QK_PAD), row(HEADS * QK_PAD), row(LANES)],
        out_specs=[row(wn), row(wr), row(wn), row(LANES)],
        out_shape=[jax.ShapeDtypeStruct((s, wn), MD), jax.ShapeDtypeStruct((s, wr), MD),
                   jax.ShapeDtypeStruct((s, wn), MD), jax.ShapeDtypeStruct((s, LANES), MD)],
        compiler_params=_params(("parallel",)),
    )(dqf, dkf, ang2)


_EXP2_SCALE = SOFTMAX_SCALE * LOG2E


def flash_fwd(qf, kf, v, name):
    s = qf.shape[0]
    tq = _tile(s, 512)
    tk = _tile(s, 1024)
    nq = s // tq
    last_j = lambda i: (i * tq + tq - 1) // tk
    pairs = [(i, j) for i in range(nq) for j in range(last_j(i) + 1)]
    it = jnp.asarray(np.array([p[0] for p in pairs], np.int32))
    jt = jnp.asarray(np.array([p[1] for p in pairs], np.int32))
    groups = tk // LANES

    def body(it_ref, jt_ref, q_ref, k_ref, v_ref, o_ref, lse_ref, m_s, l_s, acc_s):
        p = pl.program_id(1)
        i, j = it_ref[p], jt_ref[p]

        @pl.when(j == 0)
        def _():
            m_s[...] = jnp.full(m_s.shape, -jnp.inf, F32)
            l_s[...] = jnp.zeros(l_s.shape, F32)
            acc_s[...] = jnp.zeros(acc_s.shape, F32)

        def step(masked):
            sc = _dot(q_ref[...], k_ref[...], NT)
            if masked:
                row = lax.broadcasted_iota(jnp.int32, (tq, tk), 0) + i * tq
                col = lax.broadcasted_iota(jnp.int32, (tq, tk), 1) + j * tk
                sc = jnp.where(col <= row, sc, -jnp.inf)
            cols = [sc[:, g * LANES:(g + 1) * LANES] for g in range(groups)]
            mx = cols[0]
            for g in range(1, groups):
                mx = jnp.maximum(mx, cols[g])
            m_prev = m_s[...]
            m_new = jnp.maximum(m_prev, jnp.max(mx, axis=1, keepdims=True))
            alpha = jnp.exp2((m_prev - m_new) * _EXP2_SCALE)
            ps, psum = [], None
            for g in range(groups):
                pg = jnp.exp2((cols[g] - m_new) * _EXP2_SCALE)
                psum = pg if psum is None else psum + pg
                ps.append(pg.astype(MD))
            l_s[...] = alpha * l_s[...] + psum
            acc_s[...] = acc_s[...] * alpha + _dot(jnp.concatenate(ps, axis=1), v_ref[...])
            m_s[...] = m_new

        needs_mask = (j + 1) * tk - 1 > i * tq

        @pl.when(needs_mask)
        def _():
            step(True)

        @pl.when(jnp.logical_not(needs_mask))
        def _():
            step(False)

        @pl.when(j == (i * tq + tq - 1) // tk)
        def _():
            l = jnp.sum(l_s[...], axis=1, keepdims=True)
            o_ref[...] = (acc_s[...] / l).astype(o_ref.dtype)
            lse_rep = m_s[...] * _EXP2_SCALE + jnp.log(l) * LOG2E
            lse_ref[...] = _lane_sums_as_row(lse_rep * (1.0 / LANES))

    grid_spec = pltpu.PrefetchScalarGridSpec(
        num_scalar_prefetch=2, grid=(HEADS, len(pairs)),
        in_specs=[pl.BlockSpec((tq, QK_PAD), lambda h, p, it, jt: (it[p], h)),
                  pl.BlockSpec((tk, QK_PAD), lambda h, p, it, jt: (jt[p], h)),
                  pl.BlockSpec((tk, V_DIM), lambda h, p, it, jt: (jt[p], h))],
        out_specs=[pl.BlockSpec((tq, V_DIM), lambda h, p, it, jt: (it[p], h)),
                   pl.BlockSpec((None, 1, tq), lambda h, p, it, jt: (h, 0, it[p]))],
        scratch_shapes=[pltpu.VMEM((tq, LANES), F32), pltpu.VMEM((tq, LANES), F32), pltpu.VMEM((tq, V_DIM), F32)])
    return pl.pallas_call(
        body, name=name, grid_spec=grid_spec,
        out_shape=[jax.ShapeDtypeStruct((s, HEADS * V_DIM), MD), jax.ShapeDtypeStruct((HEADS, 1, s), F32)],
        compiler_params=_params(("parallel", "arbitrary")),
    )(it, jt, qf, kf, v)


def attn_delta(o, do, name):
    s = o.shape[0]
    t = _tile(s, 1024)

    def body(o_ref, do_ref, d_ref):
        d_ref[...] = _lane_sums_as_row(o_ref[...].astype(F32) * do_ref[...].astype(F32))

    blk = pl.BlockSpec((t, V_DIM), lambda h, i: (i, h))
    return pl.pallas_call(
        body, name=name, grid=(HEADS, s // t), in_specs=[blk, blk],
        out_specs=pl.BlockSpec((None, 1, t), lambda h, i: (h, 0, i)),
        out_shape=jax.ShapeDtypeStruct((HEADS, 1, s), F32),
        compiler_params=_params(("parallel", "parallel")),
    )(o, do)


def flash_bwd(qf, kf, v, do, lse2, delta, name):
    s = qf.shape[0]
    t = _tile(s, 512)
    nb = s // t
    pairs = [(i, j) for j in range(nb) for i in range(j, nb)]
    it = jnp.asarray(np.array([p[0] for p in pairs], np.int32))
    jt = jnp.asarray(np.array([p[1] for p in pairs], np.int32))
    n_pairs = len(pairs)

    def body(it_ref, jt_ref, q_ref, k_ref, v_ref, do_ref, lse_ref, dl_ref, dq_ref, dk_ref, dv_ref, dq_s, dk_s, dv_s):
        p = pl.program_id(1)
        i, j = it_ref[p], jt_ref[p]

        @pl.when(p == 0)
        def _():
            dq_s[...] = jnp.zeros(dq_s.shape, F32)

        @pl.when(i == j)
        def _():
            dk_s[...] = jnp.zeros(dk_s.shape, F32)
            dv_s[...] = jnp.zeros(dv_s.shape, F32)

        def step(masked):
            q, k, do_b = q_ref[...], k_ref[...], do_ref[...]
            pt = jnp.exp2(_dot(k, q, NT) * _EXP2_SCALE - lse_ref[...])
            if masked:
                krow = lax.broadcasted_iota(jnp.int32, (t, t), 0)
                qcol = lax.broadcasted_iota(jnp.int32, (t, t), 1)
                pt = jnp.where(krow <= qcol, pt, 0.0)
            dv_s[...] += _dot(pt.astype(MD), do_b)
            dst = (pt * (_dot(v_ref[...], do_b, NT) - dl_ref[...])).astype(MD)
            dk_s[...] += _dot(dst, q)
            row0 = pl.multiple_of(i * t, t)
            dq_s[pl.ds(row0, t), :] += _dot(dst, k, TN)

        @pl.when(i == j)
        def _():
            step(True)

        @pl.when(i != j)
        def _():
            step(False)

        @pl.when(i == nb - 1)
        def _():
            dk_ref[...] = (dk_s[...] * SOFTMAX_SCALE).astype(dk_ref.dtype)
            dv_ref[...] = dv_s[...].astype(dv_ref.dtype)

        @pl.when(p == n_pairs - 1)
        def _():
            dq_ref[...] = (dq_s[...] * SOFTMAX_SCALE).astype(dq_ref.dtype)

    vec = pl.BlockSpec((None, 1, t), lambda h, p, it, jt: (h, 0, it[p]))
    grid_spec = pltpu.PrefetchScalarGridSpec(
        num_scalar_prefetch=2, grid=(HEADS, n_pairs),
        in_specs=[pl.BlockSpec((t, QK_PAD), lambda h, p, it, jt: (it[p], h)),
                  pl.BlockSpec((t, QK_PAD), lambda h, p, it, jt: (jt[p], h)),
                  pl.BlockSpec((t, V_DIM), lambda h, p, it, jt: (jt[p], h)),
                  pl.BlockSpec((t, V_DIM), lambda h, p, it, jt: (it[p], h)), vec, vec],
        out_specs=[pl.BlockSpec((s, QK_PAD), lambda h, p, it, jt: (0, h)),
                   pl.BlockSpec((t, QK_PAD), lambda h, p, it, jt: (jt[p], h)),
                   pl.BlockSpec((t, V_DIM), lambda h, p, it, jt: (jt[p], h))],
        scratch_shapes=[pltpu.VMEM((s, QK_PAD), F32), pltpu.VMEM((t, QK_PAD), F32), pltpu.VMEM((t, V_DIM), F32)])
    wide = jax.ShapeDtypeStruct((s, HEADS * QK_PAD), MD)
    return pl.pallas_call(
        body, name=name, grid_spec=grid_spec,
        out_shape=[wide, wide, jax.ShapeDtypeStruct((s, HEADS * V_DIM), MD)],
        compiler_params=_params(("parallel", "arbitrary")),
    )(it, jt, qf, kf, v, do, lse2, delta)


def _halo_rows(dtype):
    return 8 if jnp.dtype(dtype).itemsize == 4 else 16


def _prev_spec(ts, tc, hr, col_off):
    return pl.BlockSpec((hr, tc), lambda j, i: (jnp.maximum(i * (ts // hr) - 1, 0), j + col_off))


def _conv_taps(x, halo, w, taps, first):
    hr, ts = halo.shape[0], x.shape[0]
    xe = jnp.concatenate([jnp.where(first, 0.0, halo), x], axis=0)
    shifted = []
    acc = None
    for k in range(taps):
        sh = taps - 1 - k
        xs = xe if sh == 0 else pltpu.roll(xe, sh, 0)
        xs = xs[hr:hr + ts, :]
        shifted.append(xs)
        term = xs * w[k:k + 1, :]
        acc = term if acc is None else acc + term
    return acc, shifted


def conv_bwd(x, dy, w8, taps, x_col_off, name, ts_pref=256):
    halves = dy.ndim == 3
    s = dy.shape[-2]
    c = dy.shape[-1] * (2 if halves else 1)
    tc = _tile(dy.shape[-1], 512)
    ts = _tile(s, ts_pref, 16)
    hx, hy = _halo_rows(x.dtype), _halo_rows(dy.dtype)
    ns = s // ts
    nh = dy.shape[-1] // tc
    xo = x_col_off // tc
    assert x_col_off % tc == 0

    def body(x_ref, xp_ref, dy_ref, dyn_ref, w_ref, dx_ref, dw_ref, db_ref):
        i = pl.program_id(1)
        first = i == 0
        xv = x_ref[...].astype(F32)
        dyv = dy_ref[...].astype(F32)
        w = w_ref[...]
        _, shifted = _conv_taps(xv, xp_ref[...].astype(F32), w, taps, first)
        dye = jnp.concatenate([dyv, jnp.where(i == ns - 1, 0.0, dyn_ref[...].astype(F32))], axis=0)
        dx = None
        for k in range(taps):
            sh = taps - 1 - k
            ds_ = dye if sh == 0 else pltpu.roll(dye, ts + hy - sh, 0)
            term = ds_[0:ts, :] * w[k:k + 1, :]
            dx = term if dx is None else dx + term
        dx_ref[...] = dx.astype(dx_ref.dtype)
        rows = [jnp.sum(dyv * shifted[k], axis=0, keepdims=True) for k in range(taps)]
        rows.append(jnp.zeros((8 - taps, tc), F32))
        _acc_out(dw_ref, jnp.concatenate(rows, axis=0), first)
        _acc_out(db_ref, jnp.sum(dyv, axis=0, keepdims=True), first)

    last = s // hy - 1
    nxt = lambda i: jnp.minimum((i + 1) * (ts // hy), last)
    if halves:
        dy_spec = pl.BlockSpec((None, ts, tc), lambda j, i: (j // nh, i, j % nh))
        dyn_spec = pl.BlockSpec((None, hy, tc), lambda j, i: (j // nh, nxt(i), j % nh))
    else:
        dy_spec = pl.BlockSpec((ts, tc), lambda j, i: (i, j))
        dyn_spec = pl.BlockSpec((hy, tc), lambda j, i: (nxt(i), j))
    return pl.pallas_call(
        body, name=name, grid=(c // tc, ns),
        in_specs=[pl.BlockSpec((ts, tc), lambda j, i: (i, j + xo)), _prev_spec(ts, tc, hx, xo), dy_spec, dyn_spec,
                  pl.BlockSpec((8, tc), lambda j, i: (0, j))],
        out_specs=[pl.BlockSpec((ts, tc), lambda j, i: (i, j)), pl.BlockSpec((8, tc), lambda j, i: (0, j)),
                   pl.BlockSpec((1, tc), lambda j, i: (0, j))],
        out_shape=[jax.ShapeDtypeStruct((s, c), MD), jax.ShapeDtypeStruct((8, c), F32), jax.ShapeDtypeStruct((1, c), F32)],
        compiler_params=_params(("parallel", "arbitrary")),
    )(x, x, dy, dy, w8)


def _lru_gates(xc, wr_ref, br, wi_ref, bi, sp, nblk):
    xb = xc.astype(MD)
    zr, zi = [], []
    for b in range(nblk):
        blk = xb[:, b * LANES:(b + 1) * LANES]
        zr.append(_dot(blk, wr_ref[b]))
        zi.append(_dot(blk, wi_ref[b]))
    r = _sigmoid(jnp.concatenate(zr, axis=1) + br)
    ig = _sigmoid(jnp.concatenate(zi, axis=1) + bi)
    log_a = -LRU_C * r * sp
    a = jnp.exp(log_a)
    mult = jnp.sqrt(-_expm1(2.0 * log_a))
    return r, ig, a, mult


def lru_fwd(proj, x_off, y_off, w8, b_conv, w_rg, b_rg, w_ig, b_ig, lam, name):
    s = proj.shape[0]
    c = lam.shape[1]
    tc, ts = _tile(c, 512), _tile(s, 256, 16)
    nblk = tc // LANES
    xo, yo = x_off // tc, y_off // tc
    hr = _halo_rows(proj.dtype)

    def body(x_ref, xp_ref, y_ref, w_ref, bc_ref, wr_ref, br_ref, wi_ref, bi_ref, lam_ref, h_ref, hg_ref, carry):
        i = pl.program_id(1)
        first = i == 0
        xc, _ = _conv_taps(x_ref[...].astype(F32), xp_ref[...].astype(F32), w_ref[...], LRU_TAPS, first)
        xc = xc + bc_ref[...]
        sp = _softplus(-lam_ref[...])
        _, ig, a, mult = _lru_gates(xc, wr_ref, br_ref[...], wi_ref, bi_ref[...], sp, nblk)
        b = mult * (ig * xc)
        row = lax.broadcasted_iota(jnp.int32, (ts, tc), 0)
        d = 1
        while d < ts:
            keep = row >= d
            a_s, b_s = pltpu.roll(a, d, 0), pltpu.roll(b, d, 0)
            b = jnp.where(keep, a * b_s + b, b)
            a = jnp.where(keep, a * a_s, a)
            d *= 2
        h0 = jnp.where(first, 0.0, carry[7:8, :])
        h = a * h0 + b
        carry[...] = h[ts - 8:ts, :]
        h_ref[...] = h
        hg_ref[...] = (h * _gelu(y_ref[...].astype(F32))).astype(hg_ref.dtype)

    tile = lambda off: pl.BlockSpec((ts, tc), lambda j, i: (i, j + off))
    vec = pl.BlockSpec((1, tc), lambda j, i: (0, j))
    wblk = pl.BlockSpec((nblk, LANES, LANES), lambda j, i: (j, 0, 0))
    out = pl.BlockSpec((ts, tc), lambda j, i: (i, j))
    return pl.pallas_call(
        body, name=name, grid=(c // tc, s // ts),
        in_specs=[tile(xo), _prev_spec(ts, tc, hr, xo), tile(yo), pl.BlockSpec((8, tc), lambda j, i: (0, j)), vec,
                  wblk, vec, wblk, vec, vec],
        out_specs=[out, out],
        out_shape=[jax.ShapeDtypeStruct((s, c), F32), jax.ShapeDtypeStruct((s, c), MD)],
        scratch_shapes=[pltpu.VMEM((8, tc), F32)],
        compiler_params=_params(("parallel", "arbitrary")),
    )(proj, proj, proj, w8, b_conv, w_rg, b_rg, w_ig, b_ig, lam)


def lru_bwd(proj, x_off, y_off, h, dhg, w8, b_conv, w_rg, b_rg, w_ig, b_ig, lam, name):
    s = proj.shape[0]
    c = lam.shape[1]
    tc, ts = _tile(c, 512), _tile(s, 256, 16)
    nblk = tc // LANES
    ns = s // ts
    xo, yo = x_off // tc, y_off // tc
    hr = _halo_rows(proj.dtype)
    rev = lambda i: ns - 1 - i

    def body(x_ref, xp_ref, y_ref, h_ref, hp_ref, dhg_ref, w_ref, bc_ref, wr_ref, br_ref, wi_ref, bi_ref, lam_ref,
             dxc_ref, dy_ref, dwr_ref, dwi_ref, dbr_ref, dbi_ref, dlam_ref, a_carry, g_carry):
        step = pl.program_id(1)
        i = rev(step)
        start = step == 0
        top = i == 0
        xc, _ = _conv_taps(x_ref[...].astype(F32), xp_ref[...].astype(F32), w_ref[...], LRU_TAPS, top)
        xc = xc + bc_ref[...]
        lam_v = lam_ref[...]
        sp = _softplus(-lam_v)
        r, ig, a, mult = _lru_gates(xc, wr_ref, br_ref[...], wi_ref, bi_ref[...], sp, nblk)
        hv = h_ref[...]
        gy, dgy = _gelu_and_grad(y_ref[...].astype(F32))
        dhg_v = dhg_ref[...]
        dy_ref[...] = (dhg_v * hv * dgy).astype(dy_ref.dtype)
        row = lax.broadcasted_iota(jnp.int32, (ts, tc), 0)
        a_next = jnp.where(row == ts - 1, jnp.where(start, 0.0, a_carry[0:1, :]), pltpu.roll(a, ts - 1, 0))
        ca, cb = a_next, dhg_v * gy
        d = 1
        while d < ts:
            keep = row < ts - d
            a_s, b_s = pltpu.roll(ca, ts - d, 0), pltpu.roll(cb, ts - d, 0)
            cb = jnp.where(keep, ca * b_s + cb, cb)
            ca = jnp.where(keep, ca * a_s, ca)
            d *= 2
        g = ca * jnp.where(start, 0.0, g_carry[0:1, :]) + cb
        a_carry[...] = a[0:8, :]
        g_carry[...] = g[0:8, :]
        h_prev = jnp.where(row == 0, jnp.where(top, 0.0, hp_ref[7:8, :]), pltpu.roll(hv, 1, 0))
        da = g * h_prev
        ixc = ig * xc
        d_ixc = g * mult
        dlog_a = da * a - g * ixc * (a * a) / mult
        dzr = dlog_a * (-LRU_C * sp) * r * (1.0 - r)
        dzi = d_ixc * xc * ig * (1.0 - ig)
        dxc = d_ixc * ig
        xb, dzr_b, dzi_b = xc.astype(MD), dzr.astype(MD), dzi.astype(MD)
        parts, dwr, dwi = [], [], []
        for b in range(nblk):
            sl = slice(b * LANES, (b + 1) * LANES)
            parts.append(_dot(dzr_b[:, sl], wr_ref[b], NT) + _dot(dzi_b[:, sl], wi_ref[b], NT))
            dwr.append(_dot(xb[:, sl], dzr_b[:, sl], TN)[None])
            dwi.append(_dot(xb[:, sl], dzi_b[:, sl], TN)[None])
        dxc_ref[...] = dxc + jnp.concatenate(parts, axis=1)
        _acc_out(dwr_ref, jnp.concatenate(dwr, axis=0), start)
        _acc_out(dwi_ref, jnp.concatenate(dwi, axis=0), start)
        _acc_out(dbr_ref, jnp.sum(dzr, axis=0, keepdims=True), start)
        _acc_out(dbi_ref, jnp.sum(dzi, axis=0, keepdims=True), start)
        dsp = jnp.sum(dlog_a * (-LRU_C) * r, axis=0, keepdims=True)
        _acc_out(dlam_ref, -dsp * _sigmoid(-lam_v), start)

    vec = pl.BlockSpec((1, tc), lambda j, i: (0, j))
    wblk = pl.BlockSpec((nblk, LANES, LANES), lambda j, i: (j, 0, 0))
    tile = lambda off: pl.BlockSpec((ts, tc), lambda j, i: (rev(i), j + off))
    x_prev = pl.BlockSpec((hr, tc), lambda j, i: (jnp.maximum(rev(i) * (ts // hr) - 1, 0), j + xo))
    h_prev = pl.BlockSpec((8, tc), lambda j, i: (jnp.maximum(rev(i) * (ts // 8) - 1, 0), j))
    return pl.pallas_call(
        body, name=name, grid=(c // tc, ns),
        in_specs=[tile(xo), x_prev, tile(yo), tile(0), h_prev, tile(0), pl.BlockSpec((8, tc), lambda j, i: (0, j)), vec,
                  wblk, vec, wblk, vec, vec],
        out_specs=[tile(0), tile(0), wblk, wblk, vec, vec, vec],
        out_shape=[jax.ShapeDtypeStruct((s, c), F32), jax.ShapeDtypeStruct((s, c), MD),
                   jax.ShapeDtypeStruct(w_rg.shape, F32), jax.ShapeDtypeStruct(w_ig.shape, F32),
                   jax.ShapeDtypeStruct((1, c), F32), jax.ShapeDtypeStruct((1, c), F32), jax.ShapeDtypeStruct((1, c), F32)],
        scratch_shapes=[pltpu.VMEM((8, tc), F32), pltpu.VMEM((8, tc), F32)],
        compiler_params=_params(("parallel", "arbitrary")),
    )(proj, proj, proj, h, h, dhg, w8, b_conv, w_rg, b_rg, w_ig, b_ig, lam)


def merge_fwd(proj, gm_off, gl_off, o_mla, o_lru, name):
    s, c = o_mla.shape
    tc, ts = _tile(c, 1024), _tile(s, 512, 16)

    def body(gm_ref, gl_ref, om_ref, ol_ref, out_ref):
        out_ref[...] = (_sigmoid(gm_ref[...].astype(F32)) * om_ref[...] + _sigmoid(gl_ref[...].astype(F32)) * ol_ref[...]).astype(out_ref.dtype)

    tile = lambda off: pl.BlockSpec((ts, tc), lambda i, j: (i, j + off // tc))
    return pl.pallas_call(
        body, name=name, grid=(s // ts, c // tc), in_specs=[tile(gm_off), tile(gl_off), tile(0), tile(0)], out_specs=tile(0),
        out_shape=jax.ShapeDtypeStruct((s, c), MD), compiler_params=_params(("parallel", "parallel")),
    )(proj, proj, o_mla, o_lru)


def merge_bwd(proj, gm_off, gl_off, o_mla, o_lru, dmerged, name):
    s, c = o_mla.shape
    tc, ts = _tile(c, 1024), _tile(s, 512, 16)

    def body(gm_ref, gl_ref, om_ref, ol_ref, dm_ref, dom_ref, dol_ref, dgm_ref, dgl_ref):
        sm, sl = _sigmoid(gm_ref[...].astype(F32)), _sigmoid(gl_ref[...].astype(F32))
        dm = dm_ref[...]
        dom_ref[...] = (dm * sm).astype(dom_ref.dtype)
        dol_ref[...] = (dm * sl).astype(dol_ref.dtype)
        dgm_ref[...] = (dm * om_ref[...] * sm * (1.0 - sm)).astype(dgm_ref.dtype)
        dgl_ref[...] = (dm * ol_ref[...] * sl * (1.0 - sl)).astype(dgl_ref.dtype)

    tile = lambda off: pl.BlockSpec((ts, tc), lambda i, j: (i, j + off // tc))
    out = jax.ShapeDtypeStruct((s, c), MD)
    return pl.pallas_call(
        body, name=name, grid=(s // ts, c // tc), in_specs=[tile(gm_off), tile(gl_off), tile(0), tile(0), tile(0)],
        out_specs=[tile(0)] * 4, out_shape=[out] * 4, compiler_params=_params(("parallel", "parallel")),
    )(proj, proj, o_mla, o_lru, dmerged)


def xattn_fwd(cq, ck, cv, name):
    s = cq.shape[0]
    m = ck.shape[0]
    ts = _tile(s, 512, 16)
    scale = X_DIM ** -0.5

    def body(q_ref, k_ref, v_ref, o_ref):
        for h in range(X_HEADS):
            sl = slice(h * X_DIM, (h + 1) * X_DIM)
            sc = _dot(q_ref[:, sl], k_ref[:, sl], NT) * scale
            e = jnp.exp(sc - jnp.max(sc, axis=1, keepdims=True))
            p = e / jnp.sum(e, axis=1, keepdims=True)
            o_ref[:, sl] = _dot(p.astype(MD), v_ref[:, sl]).astype(o_ref.dtype)

    w = X_HEADS * X_DIM
    return pl.pallas_call(
        body, name=name, grid=(s // ts,),
        in_specs=[pl.BlockSpec((ts, w), lambda i: (i, 0)), pl.BlockSpec((m, w), lambda i: (0, 0)), pl.BlockSpec((m, w), lambda i: (0, 0))],
        out_specs=pl.BlockSpec((ts, w), lambda i: (i, 0)), out_shape=jax.ShapeDtypeStruct((s, w), MD),
        compiler_params=_params(("parallel",)),
    )(cq, ck, cv)


def xattn_bwd(cq, ck, cv, dco, name):
    s = cq.shape[0]
    m = ck.shape[0]
    ts = _tile(s, 512, 16)
    scale = X_DIM ** -0.5

    def body(q_ref, k_ref, v_ref, do_ref, dq_ref, dk_ref, dv_ref):
        first = pl.program_id(0) == 0
        dks, dvs = [], []
        for h in range(X_HEADS):
            sl = slice(h * X_DIM, (h + 1) * X_DIM)
            q, k, v, do = q_ref[:, sl], k_ref[:, sl], v_ref[:, sl], do_ref[:, sl]
            sc = _dot(q, k, NT) * scale
            e = jnp.exp(sc - jnp.max(sc, axis=1, keepdims=True))
            p = e / jnp.sum(e, axis=1, keepdims=True)
            dvs.append(_dot(p.astype(MD), do, TN))
            dp = _dot(do, v, NT)
            ds = (p * (dp - jnp.sum(dp * p, axis=1, keepdims=True)) * scale).astype(MD)
            dq_ref[:, sl] = _dot(ds, k).astype(dq_ref.dtype)
            dks.append(_dot(ds, q, TN))
        _acc_out(dk_ref, jnp.concatenate(dks, axis=1), first)
        _acc_out(dv_ref, jnp.concatenate(dvs, axis=1), first)

    w = X_HEADS * X_DIM
    row = pl.BlockSpec((ts, w), lambda i: (i, 0))
    full = pl.BlockSpec((m, w), lambda i: (0, 0))
    return pl.pallas_call(
        body, name=name, grid=(s // ts,), in_specs=[row, full, full, row], out_specs=[row, full, full],
        out_shape=[jax.ShapeDtypeStruct((s, w), MD), jax.ShapeDtypeStruct((m, w), F32), jax.ShapeDtypeStruct((m, w), F32)],
        compiler_params=_params(("arbitrary",)),
    )(cq, ck, cv, dco)


def _ffn_specs(up_pre):
    s, c2 = up_pre.shape
    f = c2 // 2
    tc, ts = _tile(f, 512), _tile(s, 512, 16)
    nf = f // tc
    hr = _halo_rows(up_pre.dtype)
    tile = lambda off: pl.BlockSpec((ts, tc), lambda j, i: (i, j + off))
    w = lambda off: pl.BlockSpec((8, tc), lambda j, i: (0, j + off))
    vec = lambda off: pl.BlockSpec((1, tc), lambda j, i: (0, j + off))
    specs = [tile(0), _prev_spec(ts, tc, hr, 0), tile(nf), _prev_spec(ts, tc, hr, nf), w(0), w(nf), vec(0), vec(nf)]
    return s, f, tc, ts, nf, tile, specs


def _ffn_gate_val(g_ref, gp_ref, v_ref, vp_ref, wg_ref, wv_ref, bg_ref, bv_ref, first):
    gate, _ = _conv_taps(g_ref[...].astype(F32), gp_ref[...].astype(F32), wg_ref[...], FFN_TAPS, first)
    val, _ = _conv_taps(v_ref[...].astype(F32), vp_ref[...].astype(F32), wv_ref[...], FFN_TAPS, first)
    return gate + bg_ref[...], val + bv_ref[...]


def ffn_act_fwd(up_pre, w8, b, name):
    s, f, tc, ts, nf, tile, specs = _ffn_specs(up_pre)

    def body(*refs):
        o_ref = refs[8]
        gate, val = _ffn_gate_val(*refs[:8], pl.program_id(1) == 0)
        o_ref[...] = (_gelu(gate) * val).astype(o_ref.dtype)

    return pl.pallas_call(
        body, name=name, grid=(nf, s // ts), in_specs=specs, out_specs=tile(0),
        out_shape=jax.ShapeDtypeStruct((s, f), MD), compiler_params=_params(("parallel", "parallel")),
    )(up_pre, up_pre, up_pre, up_pre, w8, w8, b, b)


def ffn_act_bwd(up_pre, w8, b, dact, name):
    s, f, tc, ts, nf, tile, specs = _ffn_specs(up_pre)

    def body(*refs):
        da_ref, o_ref = refs[8], refs[9]
        gate, val = _ffn_gate_val(*refs[:8], pl.program_id(1) == 0)
        ge, dge = _gelu_and_grad(gate)
        da = da_ref[...].astype(F32)
        o_ref[0] = (da * val * dge).astype(o_ref.dtype)
        o_ref[1] = (da * ge).astype(o_ref.dtype)

    return pl.pallas_call(
        body, name=name, grid=(nf, s // ts), in_specs=specs + [tile(0)],
        out_specs=pl.BlockSpec((2, ts, tc), lambda j, i: (0, i, j)), out_shape=jax.ShapeDtypeStruct((2, s, f), MD),
        compiler_params=_params(("parallel", "parallel")),
    )(up_pre, up_pre, up_pre, up_pre, w8, w8, b, b, dact)


def _adam(w, g, m, v):
    m = ADAM_B1 * m + (1.0 - ADAM_B1) * g
    v = ADAM_B2 * v + (1.0 - ADAM_B2) * (g * g)
    m_hat = m / (1.0 - ADAM_B1 ** ADAM_STEP)
    v_hat = v / (1.0 - ADAM_B2 ** ADAM_STEP)
    delta = -ADAM_LR * (m_hat / (jnp.sqrt(v_hat) + ADAM_EPS) + ADAM_WD * w)
    return delta, m, v


def _rows_tile(r, c):
    return _tile(r, max(16, (1 << 18) // c // 16 * 16), 16)


def adamw_sharded(mine, sib, chips, w, m, v, name):
    r, c = w.shape
    tr = _rows_tile(r, c)

    def body(g_ref, s_ref, c_ref, w_ref, m_ref, v_ref, go_ref, d_ref, mo_ref, vo_ref):
        grad = g_ref[...].astype(F32) + s_ref[...].astype(F32)
        for j in range(3):
            grad = grad + c_ref[j].astype(F32)
        go_ref[...] = grad
        d_ref[...], mo_ref[...], vo_ref[...] = _adam(w_ref[...], grad, m_ref[...], v_ref[...])

    row = pl.BlockSpec((tr, c), lambda i: (i, 0))
    own = pl.BlockSpec((None, tr, c), lambda i: (0, i, 0))
    out = jax.ShapeDtypeStruct((r, c), F32)
    return pl.pallas_call(
        body, name=name, grid=(r // tr,),
        in_specs=[own, own, pl.BlockSpec((3, tr, c), lambda i: (0, i, 0)), row, row, row],
        out_specs=[row] * 4, out_shape=[out] * 4, compiler_params=_params(("parallel",)),
    )(mine, sib, chips, w, m, v)


def adamw_plain(g, w, m, v, name):
    r, c = w.shape
    tr = _tile(r, 512, 8)

    def body(g_ref, w_ref, m_ref, v_ref, d_ref, mo_ref, vo_ref):
        d_ref[...], mo_ref[...], vo_ref[...] = _adam(w_ref[...], g_ref[...], m_ref[...], v_ref[...])

    row = pl.BlockSpec((tr, c), lambda i: (i, 0))
    out = jax.ShapeDtypeStruct((r, c), F32)
    return pl.pallas_call(body, name=name, grid=(r // tr,), in_specs=[row] * 4, out_specs=[row] * 3, out_shape=[out] * 3,
                          compiler_params=_params(("parallel",)))(g, w, m, v)


def chip_partial(mine, sib, name):
    _, r, c = mine.shape
    tr = _rows_tile(r, c)

    def body(g_ref, s_ref, o_ref):
        o_ref[...] = (g_ref[...].astype(F32) + s_ref[...].astype(F32)).astype(o_ref.dtype)

    slab = lambda off: pl.BlockSpec((None, tr, c), lambda j, i: (j + off, i, 0))
    return pl.pallas_call(
        body, name=name, grid=(3, r // tr), in_specs=[slab(1), slab(1)], out_specs=slab(0),
        out_shape=jax.ShapeDtypeStruct((3, r, c), MD), compiler_params=_params(("parallel", "parallel")),
    )(mine, sib)


def sum_devices(parts, name):
    _, r, c = parts.shape
    tr = _tile(r, 256, 8)

    def body(p_ref, o_ref):
        acc = p_ref[0]
        for d in range(1, N_DEV):
            acc = acc + p_ref[d]
        o_ref[...] = acc

    return pl.pallas_call(
        body, name=name, grid=(r // tr,), in_specs=[pl.BlockSpec((N_DEV, tr, c), lambda i: (0, i, 0))],
        out_specs=pl.BlockSpec((tr, c), lambda i: (i, 0)), out_shape=jax.ShapeDtypeStruct((r, c), F32),
        compiler_params=_params(("parallel",)))(parts)


def _place():
    return lax.axis_index("x"), lax.axis_index("y"), lax.axis_index("c")


def _other_chips(x, y):
    return [(1 - x, y), (x, 1 - y), (1 - x, 1 - y)]


_ANY = pl.BlockSpec(memory_space=pl.ANY)


def all_gather(shards, name):
    n = len(shards)

    def body(*refs):
        x_refs, out_refs = refs[:n], refs[n:2 * n]
        send_sems, recv_sems, local_sems = refs[2 * n:]
        x, y, cc = _place()
        me, sibling = (x, y, cc), (x, y, 1 - cc)
        chips = _other_chips(x, y)

        def slot(w, px, py, pc):
            return out_refs[w].at[4 * px + 2 * py + pc]

        def copy(k, w, block, to, src=None):
            return pltpu.make_async_remote_copy(
                src_ref=slot(w, *block) if src is None else src, dst_ref=slot(w, *block),
                send_sem=send_sems.at[k, w], recv_sem=recv_sems.at[k, w], device_id=to, device_id_type=MESH)

        mine = [pltpu.make_async_copy(x_refs[w], slot(w, *me), local_sems.at[w]) for w in range(n)]
        for cp in mine:
            cp.start()
        first = [copy(0, w, me, sibling, src=x_refs[w]) for w in range(n)]
        first += [copy(1 + j, w, me, (*chip, cc), src=x_refs[w]) for j, chip in enumerate(chips) for w in range(n)]
        for cp in first:
            cp.start()
        passed = []
        for j, chip in enumerate(chips):
            for w in range(n):
                copy(1 + j, w, (*chip, cc), me).wait_recv()
                passed.append(copy(4 + j, w, (*chip, cc), sibling))
                passed[-1].start()
        for w in range(n):
            copy(0, w, sibling, me).wait_recv()
        for j, chip in enumerate(chips):
            for w in range(n):
                copy(4 + j, w, (*chip, 1 - cc), me).wait_recv()
        for cp in first + passed:
            cp.wait_send()
        for cp in mine:
            cp.wait()

    return pl.pallas_call(
        body, name=name, out_shape=[jax.ShapeDtypeStruct((N_DEV, *a.shape), a.dtype) for a in shards],
        in_specs=[_ANY] * n, out_specs=[_ANY] * n,
        scratch_shapes=[pltpu.SemaphoreType.DMA((7, n)), pltpu.SemaphoreType.DMA((7, n)), pltpu.SemaphoreType.DMA((n,))],
    )(*shards)


def exchange_sibling(gs, name):
    n = len(gs)

    def body(*refs):
        g_refs, mine_refs, sib_refs = refs[:n], refs[n:2 * n], refs[2 * n:3 * n]
        send_sems, recv_sems, local_sems = refs[3 * n:]
        x, y, cc = _place()
        chips = [(x, y)] + _other_chips(x, y)
        local, remote = [], []
        for w in range(n):
            for j, (px, py) in enumerate(chips):
                local.append(pltpu.make_async_copy(g_refs[w].at[4 * px + 2 * py + cc], mine_refs[w].at[j], local_sems.at[j, w]))
                remote.append(pltpu.make_async_remote_copy(
                    src_ref=g_refs[w].at[4 * px + 2 * py + (1 - cc)], dst_ref=sib_refs[w].at[j],
                    send_sem=send_sems.at[j, w], recv_sem=recv_sems.at[j, w], device_id=(x, y, 1 - cc), device_id_type=MESH))
        for cp in remote + local:
            cp.start()
        for cp in remote:
            cp.wait_recv()
        for cp in remote:
            cp.wait_send()
        for cp in local:
            cp.wait()

    four = [jax.ShapeDtypeStruct((4, *g.shape[1:]), g.dtype) for g in gs]
    outs = pl.pallas_call(
        body, name=name, out_shape=four + four, in_specs=[_ANY] * n, out_specs=[_ANY] * (2 * n),
        scratch_shapes=[pltpu.SemaphoreType.DMA((4, n)), pltpu.SemaphoreType.DMA((4, n)), pltpu.SemaphoreType.DMA((4, n))],
    )(*gs)
    return outs[:n], outs[n:]


def exchange_chips(ps, name):
    n = len(ps)

    def body(*refs):
        p_refs, out_refs = refs[:n], refs[n:2 * n]
        send_sems, recv_sems = refs[2 * n:]
        x, y, cc = _place()
        copies = [pltpu.make_async_remote_copy(
            src_ref=p_refs[w].at[j], dst_ref=out_refs[w].at[j], send_sem=send_sems.at[j, w], recv_sem=recv_sems.at[j, w],
            device_id=(*chip, cc), device_id_type=MESH) for j, chip in enumerate(_other_chips(x, y)) for w in range(n)]
        for cp in copies:
            cp.start()
        for cp in copies:
            cp.wait_recv()
        for cp in copies:
            cp.wait_send()

    return pl.pallas_call(
        body, name=name, out_shape=[jax.ShapeDtypeStruct(p.shape, p.dtype) for p in ps], in_specs=[_ANY] * n,
        out_specs=[_ANY] * n, scratch_shapes=[pltpu.SemaphoreType.DMA((3, n)), pltpu.SemaphoreType.DMA((3, n))],
    )(*ps)


def _step_local(x, mem, ang2, target, W):
    d = x.shape[1]
    wq = HEADS * QK_ROPE
    o_lx, o_ly, o_gm, o_gl = d + wq, 2 * d + wq, 3 * d + wq, 4 * d + wq
    one = lambda a, b, mode, dt, name, **kw: mm([(a, b)], mode, dt, name, **kw)

    h1 = rms_fwd(x, W["g_pre_mix"], "h1")
    proj = one(h1, W["w_main"], "nn", MD, "proj")
    ckv = one(h1, W["w_ckv"], "nn", MD, "ckv")
    kr = one(h1, W["w_kr"], "nn", MD, "k_rope")
    ckv_n = rms_fwd(ckv, W["g_ckv"], "ckv_n")
    kn = one(ckv_n, W["w_uk"], "nn", MD, "k_nope")
    v = one(ckv_n, W["w_uv"], "nn", MD, "v")
    qf, kf = rope_fwd(proj, kr, kn, ang2, "rope")
    o, lse2 = flash_fwd(qf, kf, v, "attn")
    o_mla = one(o, W["w_o_mla"], "nn", F32, "o_mla")
    lru_w = (W["w_conv8"], W["b_conv_lru"], W["w_rg"], W["b_rg"], W["w_ig"], W["b_ig"], W["lru_lambda"])
    hr, hg = lru_fwd(proj, o_lx, o_ly, *lru_w, "lru")
    o_lru = one(hg, W["w_o_lru"], "nn", F32, "o_lru")
    merged = merge_fwd(proj, o_gm, o_gl, o_mla, o_lru, "merge")
    y1 = one(merged, W["w_out"], "nn", F32, "y1")
    x1, h2 = post_pre_fwd(x, y1, W["g_post_mix"], W["g_pre_x"], "x1")
    mn = rms_fwd(mem, W["g_mem"], "mem_n")
    ck = one(mn, W["w_ck"], "nn", MD, "ck")
    cv = one(mn, W["w_cv"], "nn", MD, "cv")
    cq = one(h2, W["w_cq"], "nn", MD, "cq")
    co = xattn_fwd(cq, ck, cv, "xattn")
    y2 = one(co, W["w_co"], "nn", F32, "y2")
    x2, h3 = post_pre_fwd(x1, y2, W["g_post_x"], W["g_pre_ffn"], "x2")
    up_pre = one(h3, W["w_up"], "nn", MD, "up_pre")
    act = ffn_act_fwd(up_pre, W["w_fconv8"], W["b_fconv"], "act")
    y3 = one(act, W["w_down"], "nn", F32, "y3")
    G = {}
    loss, dx3, dy3, G["g_post_ffn"] = loss_head(x2, y3, target, W["g_post_ffn"], "loss")

    G["w_down"] = one(act, dy3, "tn", MD, "dw_down")
    dact = one(dy3, W["w_down"], "nt", MD, "dact")
    dup = ffn_act_bwd(up_pre, W["w_fconv8"], W["b_fconv"], dact, "dup")
    dup_pre, dwf8, G["b_fconv"] = conv_bwd(up_pre, dup, W["w_fconv8"], FFN_TAPS, 0, "ffn_conv_bwd")
    G["w_fconv"] = dwf8[:FFN_TAPS]
    G["w_up"] = one(h3, dup_pre, "tn", MD, "dw_up")
    dh3 = one(dup_pre, W["w_up"], "nt", F32, "dh3", tn=2048, tk=1024)
    dx2, dy2, G["g_pre_ffn"], G["g_post_x"] = pre_post_bwd(x2, y2, dx3, dh3, W["g_pre_ffn"], W["g_post_x"], "dx2")
    G["w_co"] = one(co, dy2, "tn", MD, "dw_co")
    dco = one(dy2, W["w_co"], "nt", MD, "dco")
    dcq, dck, dcv = xattn_bwd(cq, ck, cv, dco, "xattn_bwd")
    dck, dcv = dck.astype(MD), dcv.astype(MD)
    G["w_cq"] = one(h2, dcq, "tn", MD, "dw_cq")
    G["w_ck"] = one(mn, dck, "tn", MD, "dw_ck")
    G["w_cv"] = one(mn, dcv, "tn", MD, "dw_cv")
    dmn = mm([(dck, W["w_ck"]), (dcv, W["w_cv"])], "nt", F32, "dmem_n")
    _, G["g_mem"] = rms_bwd(mem, W["g_mem"], dmn, "dg_mem")
    dh2 = one(dcq, W["w_cq"], "nt", F32, "dh2")
    dx1, dy1, G["g_pre_x"], G["g_post_mix"] = pre_post_bwd(x1, y1, dx2, dh2, W["g_pre_x"], W["g_post_mix"], "dx1")
    G["w_out"] = one(merged, dy1, "tn", MD, "dw_out")
    dmerged = one(dy1, W["w_out"], "nt", F32, "dmerged")
    do_mla, do_lru, dgm, dgl = merge_bwd(proj, o_gm, o_gl, o_mla, o_lru, dmerged, "merge_bwd")
    G["w_o_mla"] = one(o, do_mla, "tn", MD, "dw_o_mla")
    do = one(do_mla, W["w_o_mla"], "nt", MD, "do")
    G["w_o_lru"] = one(hg, do_lru, "tn", MD, "dw_o_lru")
    dhg = one(do_lru, W["w_o_lru"], "nt", F32, "dhg")
    dxc, dly, G["w_rg"], G["w_ig"], G["b_rg"], G["b_ig"], G["lru_lambda"] = lru_bwd(proj, o_lx, o_ly, hr, dhg, *lru_w, "lru_bwd")
    dlx, dwc8, G["b_conv_lru"] = conv_bwd(proj, dxc, W["w_conv8"], LRU_TAPS, o_lx, "lru_conv_bwd")
    G["w_conv_lru"] = dwc8[:LRU_TAPS]
    delta = attn_delta(o, do, "attn_delta")
    dqf, dkf, dv = flash_bwd(qf, kf, v, do, lse2, delta, "attn_bwd")
    dqn, dqr, dkn, dkr = rope_bwd(dqf, dkf, ang2, "rope_bwd")
    G["w_uk"] = one(ckv_n, dkn, "tn", MD, "dw_uk")
    G["w_uv"] = one(ckv_n, dv, "tn", MD, "dw_uv")
    dckv_n = mm([(dkn, W["w_uk"]), (dv, W["w_uv"])], "nt", F32, "dckv_n")
    dckv, G["g_ckv"] = rms_bwd(ckv, W["g_ckv"], dckv_n, "dckv")
    dproj = jnp.concatenate([dqn, dqr, dlx, dly, dgm, dgl], axis=1)
    parts = [("w_main", dproj), ("w_ckv", dckv), ("w_kr", dkr)]
    for n, dpart in parts:
        G[n] = one(h1, dpart, "tn", MD, "d" + n)
    dh1 = mm([(dpart, W[n]) for n, dpart in parts], "nt", F32, "dh1", tk=1024)
    grad_x, G["g_pre_mix"] = pre_bwd(x, dx1, dh1, W["g_pre_mix"], "grad_x")
    return loss, grad_x, G


FLAT_COLS = 1024
BIG = (("w_in", 1), ("w_ukv", 1), ("w_o_mla", 0), ("w_o_lru", 0), ("w_out", 0), ("w_cq", 0), ("w_ck", 0), ("w_cv", 0),
       ("w_co", 1), ("w_up", 1), ("w_down", 0))
SMALL_SHARDED = ("w_conv_lru", "w_fconv")
REPLICATED = ("g_pre_mix", "g_post_mix", "g_ckv", "b_conv_lru", "w_rg", "b_rg", "w_ig", "b_ig", "lru_lambda",
              "g_pre_x", "g_post_x", "g_mem", "g_pre_ffn", "g_post_ffn", "b_fconv")
WEIGHTS = ("g_pre_mix", "g_post_mix", "w_in", "g_ckv", "w_ukv", "w_o_mla", "w_conv_lru", "b_conv_lru", "w_rg", "b_rg",
           "w_ig", "b_ig", "lru_lambda", "w_o_lru", "w_out", "g_pre_x", "g_post_x", "g_mem", "w_cq", "w_ck", "w_cv",
           "w_co", "g_pre_ffn", "g_post_ffn", "w_up", "w_fconv", "b_fconv", "w_down")


def _round_up(n, k):
    return (n + k - 1) // k * k


def _pack_vec(parts, row_mult):
    flat = jnp.concatenate([p.reshape(-1) for p in parts])
    n = flat.shape[0]
    total = _round_up(n, row_mult * FLAT_COLS)
    return jnp.pad(flat, (0, total - n)).reshape(-1, FLAT_COLS)


def _unpack_vec(flat2d, shapes):
    flat = flat2d.reshape(-1)
    out, off = [], 0
    for shp in shapes:
        n = int(np.prod(shp))
        out.append(flat[off:off + n].reshape(shp))
        off += n
    return out


def _gathered_to_full(blocks, axis):
    n, r, c = blocks.shape
    return blocks.reshape(n * r, c) if axis == 0 else blocks.transpose(1, 0, 2).reshape(r, n * c)


def _full_to_shards(full, axis):
    r, c = full.shape
    if axis == 0:
        return full.reshape(N_DEV, r // N_DEV, c)
    return full.reshape(r, N_DEV, c // N_DEV).transpose(1, 0, 2)


def _split_w_in(w_in, d):
    hq = HEADS * (QK_NOPE + QK_ROPE)
    q = w_in[:, :hq].reshape(d, HEADS, QK_NOPE + QK_ROPE)
    q_nope, q_rope = q[:, :, :QK_NOPE].reshape(d, HEADS * QK_NOPE), q[:, :, QK_NOPE:].reshape(d, HEADS * QK_ROPE)
    out = {"w_ckv": w_in[:, hq:hq + KV_RANK]}
    off = hq + KV_RANK
    out["w_kr"] = jnp.pad(w_in[:, off:off + QK_ROPE], ((0, 0), (0, LANES - QK_ROPE)))
    out["w_main"] = jnp.concatenate([q_nope, q_rope, w_in[:, off + QK_ROPE:]], axis=1)
    return out


def _join_w_in(G, d):
    gm = G["w_main"]
    n_qn, n_qr = HEADS * QK_NOPE, HEADS * QK_ROPE
    q = jnp.concatenate([gm[:, :n_qn].reshape(d, HEADS, QK_NOPE), gm[:, n_qn:n_qn + n_qr].reshape(d, HEADS, QK_ROPE)], axis=2)
    return jnp.concatenate([q.reshape(d, -1), G["w_ckv"], G["w_kr"][:, :QK_ROPE], gm[:, n_qn + n_qr:]], axis=1)


def kernel(x, mem, positions, g_pre_mix, g_post_mix, w_in, g_ckv, w_ukv, w_o_mla, w_conv_lru, b_conv_lru, w_rg, b_rg, w_ig, b_ig, lru_lambda, w_o_lru, w_out, g_pre_x, g_post_x, g_mem, w_cq, w_ck, w_cv, w_co, g_pre_ffn, g_post_ffn, w_up, w_fconv, b_fconv, w_down, loss_target, m_g_pre_mix, m_g_post_mix, m_w_in, m_g_ckv, m_w_ukv, m_w_o_mla, m_w_conv_lru, m_b_conv_lru, m_w_rg, m_b_rg, m_w_ig, m_b_ig, m_lru_lambda, m_w_o_lru, m_w_out, m_g_pre_x, m_g_post_x, m_g_mem, m_w_cq, m_w_ck, m_w_cv, m_w_co, m_g_pre_ffn, m_g_post_ffn, m_w_up, m_w_fconv, m_b_fconv, m_w_down, v_g_pre_mix, v_g_post_mix, v_w_in, v_g_ckv, v_w_ukv, v_w_o_mla, v_w_conv_lru, v_b_conv_lru, v_w_rg, v_b_rg, v_w_ig, v_b_ig, v_lru_lambda, v_w_o_lru, v_w_out, v_g_pre_x, v_g_post_x, v_g_mem, v_w_cq, v_w_ck, v_w_cv, v_w_co, v_g_pre_ffn, v_g_post_ffn, v_w_up, v_w_fconv, v_b_fconv, v_w_down):
    given = dict(locals())
    P = {n: given[n][0] for n in WEIGHTS}
    M = {n: given["m_" + n][0] for n in WEIGHTS}
    V = {n: given["v_" + n][0] for n in WEIGHTS}
    xs, mems, tgt = x[0], mem[0], loss_target[0]
    s, d = xs.shape
    ax, ay, ac = _place()
    dev = 4 * ax + 2 * ay + ac

    big_names = [n for n, _ in BIG]
    gathered = all_gather([P[n].astype(MD) for n in big_names] + [P[n] for n in SMALL_SHARDED], "gather_weights")
    full = {n: _gathered_to_full(g, axis) for (n, axis), g in zip(BIG, gathered)}
    for n, g in zip(SMALL_SHARDED, gathered[len(BIG):]):
        full[n] = _gathered_to_full(g, 1)

    W = _split_w_in(full["w_in"], d)
    kv = full["w_ukv"].reshape(KV_RANK, HEADS, QK_NOPE + V_DIM)
    W["w_uk"] = kv[:, :, :QK_NOPE].reshape(KV_RANK, HEADS * QK_NOPE)
    W["w_uv"] = kv[:, :, QK_NOPE:].reshape(KV_RANK, HEADS * V_DIM)
    for n in ("w_o_mla", "w_o_lru", "w_out", "w_cq", "w_ck", "w_cv", "w_co", "w_up", "w_down"):
        W[n] = full[n]
    W["w_rg"], W["w_ig"] = P["w_rg"].astype(MD), P["w_ig"].astype(MD)
    for n in ("g_pre_mix", "g_post_mix", "g_ckv", "b_conv_lru", "b_rg", "b_ig", "lru_lambda", "g_pre_x", "g_post_x", "g_mem",
              "g_pre_ffn", "g_post_ffn", "b_fconv"):
        W[n] = P[n].reshape(1, -1)
    W["w_conv8"] = jnp.pad(full["w_conv_lru"], ((0, 8 - LRU_TAPS), (0, 0)))
    W["w_fconv8"] = jnp.pad(full["w_fconv"], ((0, 8 - FFN_TAPS), (0, 0)))

    inv_freq = ROPE_THETA ** (-jnp.arange(0, QK_ROPE, 2, dtype=F32) / QK_ROPE)
    ang = positions[0].astype(F32)[:, None] * inv_freq
    ang2 = jnp.tile(ang, (1, LANES // (QK_ROPE // 2)))

    loss, grad_x, G = _step_local(xs, mems, ang2, tgt, W)
    loss = lax.psum(loss[0, 0], ("x", "y", "c"))

    G["w_in"] = _join_w_in(G, d)
    G["w_ukv"] = jnp.concatenate([G["w_uk"].reshape(KV_RANK, HEADS, QK_NOPE), G["w_uv"].reshape(KV_RANK, HEADS, V_DIM)], axis=2).reshape(KV_RANK, -1)
    mine, from_sibling = exchange_sibling([_full_to_shards(G[n], axis) for n, axis in BIG], "grads_to_sibling")
    chip_sums = [chip_partial(a, b, "chip_partial_" + n) for n, a, b in zip(big_names, mine, from_sibling)]
    from_chips = exchange_chips(chip_sums, "grads_to_chips")
    res = {k: {} for k in ("grad", "delta", "m", "v")}
    for n, a, b, c in zip(big_names, mine, from_sibling, from_chips):
        outs = adamw_sharded(a, b, c, P[n], M[n], V[n], "adamw_" + n)
        for kind, o in zip(("grad", "delta", "m", "v"), outs):
            res[kind][n] = o

    small_names = REPLICATED + SMALL_SHARDED
    g_small_shapes = [G[n].shape for n in small_names]
    partials, = all_gather([_pack_vec([G[n] for n in small_names], 128)], "gather_small_grads")
    summed = dict(zip(small_names, _unpack_vec(sum_devices(partials, "sum_small_grads"), g_small_shapes)))
    for n in SMALL_SHARDED:
        t, c = summed[n].shape
        summed[n] = lax.dynamic_index_in_dim(summed[n].reshape(t, N_DEV, c // N_DEV), dev, axis=1, keepdims=False)
    sg = {n: summed[n].reshape(P[n].shape) for n in small_names}
    vec = lambda src: _pack_vec([src[n] for n in small_names], 128)
    small_out = adamw_plain(vec(sg), vec(P), vec(M), vec(V), "adamw_small")
    shapes = [P[n].shape for n in small_names]
    for kind, flat in zip(("delta", "m", "v"), small_out):
        res[kind].update(zip(small_names, _unpack_vec(flat, shapes)))
    res["grad"].update(sg)

    outs = [loss, grad_x[None]]
    for kind in ("grad", "delta", "m", "v"):
        outs += [res[kind][n][None] for n in WEIGHTS]
    return tuple(outs)
```

```python
import numpy as np
import jax
import jax.numpy as jnp
from jax import lax
from jax.experimental import pallas as pl
from jax.experimental.pallas import tpu as pltpu

MD = jnp.bfloat16
F32 = jnp.float32
EPS = 1e-6
N_DEV = 8
HEADS = 16
QK_NOPE = 128
QK_ROPE = 64
V_DIM = 128
QK_PAD = 256
KV_RANK = 512
SOFTMAX_SCALE = (QK_NOPE + QK_ROPE) ** -0.5
LOG2E = 1.4426950408889634
ROPE_THETA = 10000.0
LRU_BLOCKS = 16
LRU_C = 8.0
X_HEADS = 4
X_DIM = 128
LRU_TAPS = 4
FFN_TAPS = 3
ADAM_LR, ADAM_B1, ADAM_B2, ADAM_EPS, ADAM_WD, ADAM_STEP = 0.001, 0.9, 0.999, 1e-08, 0.01, 10
LANES = 128
VMEM_LIMIT = 52 * 1024 * 1024
MESH = pl.DeviceIdType.MESH

NN = (((1,), (0,)), ((), ()))
NT = (((1,), (1,)), ((), ()))
TN = (((0,), (0,)), ((), ()))


def _tile(n, pref, align=LANES):
    if n <= pref:
        return n
    d = (pref // align) * align
    while d > align and n % d:
        d -= align
    assert n % d == 0, (n, pref, align)
    return d


def _params(sem):
    return pltpu.CompilerParams(dimension_semantics=sem, vmem_limit_bytes=VMEM_LIMIT)


def _dot(a, b, dims=NN):
    return lax.dot_general(a, b, dims, preferred_element_type=F32)


def _sigmoid(x):
    return 1.0 / (1.0 + jnp.exp(-x))


_GELU_C = 0.7978845608028654
_GELU_A = 0.044715


def _gelu(x):
    return 0.5 * x * (1.0 + jnp.tanh(x * (_GELU_C + (_GELU_C * _GELU_A) * (x * x))))


def _gelu_and_grad(x):
    x2 = x * x
    t = jnp.tanh(x * (_GELU_C + (_GELU_C * _GELU_A) * x2))
    hx = 0.5 * x
    g = hx + hx * t
    dg = 0.5 + 0.5 * t + hx * (1.0 - t * t) * (_GELU_C + (3.0 * _GELU_C * _GELU_A) * x2)
    return g, dg


def _expm1(x):
    small = x * (1.0 + x * (0.5 + x * (1.0 / 6.0 + x * (1.0 / 24.0 + x * (1.0 / 120.0)))))
    return jnp.where(jnp.abs(x) < 0.05, small, jnp.exp(x) - 1.0)


def _softplus(x):
    return jnp.maximum(x, 0.0) + jnp.log(1.0 + jnp.exp(-jnp.abs(x)))


def _rms_hat(x):
    r = lax.rsqrt(jnp.mean(x * x, axis=-1, keepdims=True) + EPS)
    return x * r, r


def _rms_bwd(x, g, dn):
    xh, r = _rms_hat(x)
    dg = jnp.sum(dn * xh, axis=0, keepdims=True)
    dxh = dn * g
    dx = r * (dxh - xh * jnp.mean(dxh * xh, axis=-1, keepdims=True))
    return dx, dg


def _acc_out(ref, val, first):
    @pl.when(first)
    def _():
        ref[...] = val

    @pl.when(jnp.logical_not(first))
    def _():
        ref[...] += val


def _lane_sums_as_row(x):
    ones = jnp.ones((8, LANES), F32)
    return lax.dot_general(ones, x, NT, precision=lax.Precision.HIGHEST, preferred_element_type=F32)[0:1, :]


def mm(pairs, mode, out_dtype, name, tm=1024, tn=1024, tk=2048):
    dims = {"nn": NN, "nt": NT, "tn": TN}[mode]
    shapes = []
    for a, b in pairs:
        if mode == "nn":
            (m, k), (k2, n) = a.shape, b.shape
        elif mode == "nt":
            (m, k), (n, k2) = a.shape, b.shape
        else:
            (k, m), (k2, n) = a.shape, b.shape
        assert k == k2, (name, a.shape, b.shape)
        shapes.append((m, n, k))
    m, n = shapes[0][0], shapes[0][1]
    assert all(s[0] == m and s[1] == n for s in shapes)
    tm, tn = _tile(m, tm), _tile(n, tn)
    tks = [_tile(s[2], tk) for s in shapes]
    nks = [s[2] // t for s, t in zip(shapes, tks)]
    starts = [sum(nks[:p]) for p in range(len(pairs))]
    nk_total = sum(nks)

    in_specs, args = [], []
    for p, (a, b) in enumerate(pairs):
        def kk(k, p=p):
            return jnp.clip(k - starts[p], 0, nks[p] - 1)
        if mode == "tn":
            in_specs.append(pl.BlockSpec((tks[p], tm), lambda i, j, k, kk=kk: (kk(k), i)))
        else:
            in_specs.append(pl.BlockSpec((tm, tks[p]), lambda i, j, k, kk=kk: (i, kk(k))))
        if mode == "nt":
            in_specs.append(pl.BlockSpec((tn, tks[p]), lambda i, j, k, kk=kk: (j, kk(k))))
        else:
            in_specs.append(pl.BlockSpec((tks[p], tn), lambda i, j, k, kk=kk: (kk(k), j)))
        args += [a, b]

    def body(*refs):
        ins, o_ref, scratch = refs[:2 * len(pairs)], refs[2 * len(pairs)], refs[2 * len(pairs) + 1:]
        if nk_total == 1:
            o_ref[...] = _dot(ins[0][...], ins[1][...], dims).astype(o_ref.dtype)
            return
        acc_ref, = scratch
        k = pl.program_id(2)
        for p in range(len(pairs)):
            @pl.when(jnp.logical_and(k >= starts[p], k < starts[p] + nks[p]))
            def _(p=p):
                prod = _dot(ins[2 * p][...], ins[2 * p + 1][...], dims)

                @pl.when(k == 0)
                def _():
                    acc_ref[...] = prod

                @pl.when(k > 0)
                def _():
                    acc_ref[...] += prod

        @pl.when(k == nk_total - 1)
        def _():
            o_ref[...] = acc_ref[...].astype(o_ref.dtype)

    return pl.pallas_call(
        body, name=name, grid=(m // tm, n // tn, nk_total),
        in_specs=in_specs, out_specs=pl.BlockSpec((tm, tn), lambda i, j, k: (i, j)),
        out_shape=jax.ShapeDtypeStruct((m, n), out_dtype),
        scratch_shapes=[] if nk_total == 1 else [pltpu.VMEM((tm, tn), F32)],
        compiler_params=_params(("parallel", "parallel", "arbitrary")),
    )(*args)


def rms_fwd(x, g, name):
    s, d = x.shape
    ts = _tile(s, 512, 16)

    def body(x_ref, g_ref, o_ref):
        xh, _ = _rms_hat(x_ref[...].astype(F32))
        o_ref[...] = (xh * g_ref[...]).astype(o_ref.dtype)

    return pl.pallas_call(
        body, name=name, grid=(s // ts,),
        in_specs=[pl.BlockSpec((ts, d), lambda i: (i, 0)), pl.BlockSpec((1, d), lambda i: (0, 0))],
        out_specs=pl.BlockSpec((ts, d), lambda i: (i, 0)),
        out_shape=jax.ShapeDtypeStruct((s, d), MD),
        compiler_params=_params(("parallel",)),
    )(x, g)


def rms_bwd(x, g, dn, name):
    s, d = x.shape
    ts = _tile(s, 256, 16)

    def body(x_ref, g_ref, dn_ref, dx_ref, dg_ref):
        dx, dg = _rms_bwd(x_ref[...].astype(F32), g_ref[...], dn_ref[...].astype(F32))
        dx_ref[...] = dx.astype(dx_ref.dtype)
        _acc_out(dg_ref, dg, pl.program_id(0) == 0)

    row = pl.BlockSpec((ts, d), lambda i: (i, 0))
    vec = pl.BlockSpec((1, d), lambda i: (0, 0))
    return pl.pallas_call(
        body, name=name, grid=(s // ts,), in_specs=[row, vec, row], out_specs=[row, vec],
        out_shape=[jax.ShapeDtypeStruct((s, d), MD), jax.ShapeDtypeStruct((1, d), F32)],
        compiler_params=_params(("arbitrary",)),
    )(x, g, dn)


def post_pre_fwd(x, y, g_post, g_pre, name):
    s, d = x.shape
    ts = _tile(s, 256, 16)

    def body(x_ref, y_ref, gp_ref, gq_ref, xn_ref, h_ref):
        yh, _ = _rms_hat(y_ref[...])
        xn = x_ref[...] + yh * gp_ref[...]
        xn_ref[...] = xn
        xh, _ = _rms_hat(xn)
        h_ref[...] = (xh * gq_ref[...]).astype(h_ref.dtype)

    row = pl.BlockSpec((ts, d), lambda i: (i, 0))
    vec = pl.BlockSpec((1, d), lambda i: (0, 0))
    return pl.pallas_call(
        body, name=name, grid=(s // ts,), in_specs=[row, row, vec, vec], out_specs=[row, row],
        out_shape=[jax.ShapeDtypeStruct((s, d), F32), jax.ShapeDtypeStruct((s, d), MD)],
        compiler_params=_params(("parallel",)),
    )(x, y, g_post, g_pre)


def loss_head(x, y, target, g_post, name):
    s, d = x.shape
    ts = _tile(s, 256, 16)

    def body(x_ref, y_ref, t_ref, g_ref, loss_ref, dx_ref, dy_ref, dg_ref):
        first = pl.program_id(0) == 0
        y = y_ref[...]
        yh, _ = _rms_hat(y)
        diff = x_ref[...] + yh * g_ref[...] - t_ref[...]
        part = 0.5 * jnp.sum(jnp.sum(diff * diff, axis=1, keepdims=True) * (1.0 / d), axis=0, keepdims=True)
        _acc_out(loss_ref, part, first)
        dx = diff * (1.0 / d)
        dx_ref[...] = dx
        dy, dg = _rms_bwd(y, g_ref[...], dx)
        dy_ref[...] = dy.astype(dy_ref.dtype)
        _acc_out(dg_ref, dg, first)

    row = pl.BlockSpec((ts, d), lambda i: (i, 0))
    vec = pl.BlockSpec((1, d), lambda i: (0, 0))
    one = pl.BlockSpec((1, 1), lambda i: (0, 0))
    return pl.pallas_call(
        body, name=name, grid=(s // ts,), in_specs=[row, row, row, vec], out_specs=[one, row, row, vec],
        out_shape=[jax.ShapeDtypeStruct((1, 1), F32), jax.ShapeDtypeStruct((s, d), F32),
                   jax.ShapeDtypeStruct((s, d), MD), jax.ShapeDtypeStruct((1, d), F32)],
        compiler_params=_params(("arbitrary",)),
    )(x, y, target, g_post)


def pre_post_bwd(x, y, dx_res, dh, g_pre, g_post, name):
    s, d = x.shape
    ts = _tile(s, 256, 16)

    def body(x_ref, y_ref, dr_ref, dh_ref, gq_ref, gp_ref, dx_ref, dy_ref, dgq_ref, dgp_ref):
        first = pl.program_id(0) == 0
        dxa, dgq = _rms_bwd(x_ref[...], gq_ref[...], dh_ref[...])
        dx = dr_ref[...] + dxa
        dx_ref[...] = dx
        dy, dgp = _rms_bwd(y_ref[...], gp_ref[...], dx)
        dy_ref[...] = dy.astype(dy_ref.dtype)
        _acc_out(dgq_ref, dgq, first)
        _acc_out(dgp_ref, dgp, first)

    row = pl.BlockSpec((ts, d), lambda i: (i, 0))
    vec = pl.BlockSpec((1, d), lambda i: (0, 0))
    return pl.pallas_call(
        body, name=name, grid=(s // ts,), in_specs=[row, row, row, row, vec, vec], out_specs=[row, row, vec, vec],
        out_shape=[jax.ShapeDtypeStruct((s, d), F32), jax.ShapeDtypeStruct((s, d), MD),
                   jax.ShapeDtypeStruct((1, d), F32), jax.ShapeDtypeStruct((1, d), F32)],
        compiler_params=_params(("arbitrary",)),
    )(x, y, dx_res, dh, g_pre, g_post)


def pre_bwd(x, dx_res, dh, g_pre, name):
    s, d = x.shape
    ts = _tile(s, 256, 16)

    def body(x_ref, dr_ref, dh_ref, g_ref, dx_ref, dg_ref):
        dxa, dg = _rms_bwd(x_ref[...], g_ref[...], dh_ref[...])
        dx_ref[...] = dr_ref[...] + dxa
        _acc_out(dg_ref, dg, pl.program_id(0) == 0)

    row = pl.BlockSpec((ts, d), lambda i: (i, 0))
    vec = pl.BlockSpec((1, d), lambda i: (0, 0))
    return pl.pallas_call(
        body, name=name, grid=(s // ts,), in_specs=[row, row, row, vec], out_specs=[row, vec],
        out_shape=[jax.ShapeDtypeStruct((s, d), F32), jax.ShapeDtypeStruct((1, d), F32)],
        compiler_params=_params(("arbitrary",)),
    )(x, dx_res, dh, g_pre)


def _swap_halves(x):
    lane = lax.broadcasted_iota(jnp.int32, x.shape, 1)
    return jnp.where((lane % QK_ROPE) < QK_ROPE // 2, pltpu.roll(x, LANES - QK_ROPE // 2, 1),
                     pltpu.roll(x, QK_ROPE // 2, 1))


def _rope_tables(ang_ref, sign):
    lane = lax.broadcasted_iota(jnp.int32, ang_ref.shape, 1)
    sgn = jnp.where((lane % QK_ROPE) < QK_ROPE // 2, -sign, sign)
    ang = ang_ref[...]
    return jnp.cos(ang), jnp.sin(ang) * sgn


def rope_fwd(proj, kr, kn, ang2, name):
    s = proj.shape[0]
    ts = _tile(s, 256, 16)
    wn, wr = HEADS * QK_NOPE, HEADS * QK_ROPE

    def body(qn_ref, qr_ref, kr_ref, kn_ref, ang_ref, qo_ref, ko_ref):
        c, sn = _rope_tables(ang_ref, 1.0)
        lane = lax.broadcasted_iota(jnp.int32, (ts, LANES), 1)
        lo = lane < QK_ROPE
        x = kr_ref[...].astype(F32)
        k_rot = jnp.where(lo, x * c + _swap_halves(x) * sn, 0.0).astype(ko_ref.dtype)
        for j in range(HEADS // 2):
            x = qr_ref[:, j * LANES:(j + 1) * LANES].astype(F32)
            r = x * c + _swap_halves(x) * sn
            pair = (jnp.where(lo, r, 0.0), jnp.where(lo, pltpu.roll(r, QK_ROPE, 1), 0.0))
            for e in range(2):
                h = 2 * j + e
                qo_ref[:, h * QK_PAD:h * QK_PAD + LANES] = qn_ref[:, h * LANES:(h + 1) * LANES]
                qo_ref[:, h * QK_PAD + LANES:(h + 1) * QK_PAD] = pair[e].astype(qo_ref.dtype)
                ko_ref[:, h * QK_PAD:h * QK_PAD + LANES] = kn_ref[:, h * LANES:(h + 1) * LANES]
                ko_ref[:, h * QK_PAD + LANES:(h + 1) * QK_PAD] = k_rot

    row = lambda w, blk: pl.BlockSpec((ts, w), lambda i: (i, blk))
    return pl.pallas_call(
        body, name=name, grid=(s // ts,),
        in_specs=[row(wn, 0), row(wr, wn // wr), row(LANES, 0), row(wn, 0), row(LANES, 0)],
        out_specs=[row(HEADS * QK_PAD, 0), row(HEADS * QK_PAD, 0)],
        out_shape=[jax.ShapeDtypeStruct((s, HEADS * QK_PAD), MD), jax.ShapeDtypeStruct((s, HEADS * QK_PAD), MD)],
        compiler_params=_params(("parallel",)),
    )(proj, proj, kr, kn, ang2)


def rope_bwd(dqf, dkf, ang2, name):
    s = dqf.shape[0]
    ts = _tile(s, 256, 16)
    wn, wr = HEADS * QK_NOPE, HEADS * QK_ROPE

    def body(dq_ref, dk_ref, ang_ref, qn_ref, qr_ref, kn_ref, kr_ref):
        c, sn = _rope_tables(ang_ref, -1.0)
        lane = lax.broadcasted_iota(jnp.int32, (ts, LANES), 1)
        lo = lane < QK_ROPE
        rot = lambda dy: dy * c + _swap_halves(dy) * sn
        dk_rope = None
        for j in range(HEADS // 2):
            halves = []
            for e in range(2):
                h = 2 * j + e
                qn_ref[:, h * LANES:(h + 1) * LANES] = dq_ref[:, h * QK_PAD:h * QK_PAD + LANES]
                kn_ref[:, h * LANES:(h + 1) * LANES] = dk_ref[:, h * QK_PAD:h * QK_PAD + LANES]
                halves.append(jnp.where(lo, dq_ref[:, h * QK_PAD + LANES:(h + 1) * QK_PAD].astype(F32), 0.0))
                dk_h = dk_ref[:, h * QK_PAD + LANES:(h + 1) * QK_PAD].astype(F32)
                dk_rope = dk_h if dk_rope is None else dk_rope + dk_h
            qr_ref[:, j * LANES:(j + 1) * LANES] = rot(halves[0] + pltpu.roll(halves[1], QK_ROPE, 1)).astype(qr_ref.dtype)
        kr_ref[...] = jnp.where(lo, rot(jnp.where(lo, dk_rope, 0.0)), 0.0).astype(kr_ref.dtype)

    row = lambda w: pl.BlockSpec((ts, w), lambda i: (i, 0))
    return pl.pallas_call(
        body, name=name, grid=(s // ts,),
        in_specs=[row(HEADS * QK_PAD), row(HEADS * QK_PAD), row(LANES)],
        out_specs=[row(wn), row(wr), row(wn), row(LANES)],
        out_shape=[jax.ShapeDtypeStruct((s, wn), MD), jax.ShapeDtypeStruct((s, wr), MD),
                   jax.ShapeDtypeStruct((s, wn), MD), jax.ShapeDtypeStruct((s, LANES), MD)],
        compiler_params=_params(("parallel",)),
    )(dqf, dkf, ang2)


_EXP2_SCALE = SOFTMAX_SCALE * LOG2E


def flash_fwd(qf, kf, v, name):
    s = qf.shape[0]
    tq = _tile(s, 512)
    tk = _tile(s, 1024)
    nq = s // tq
    last_j = lambda i: (i * tq + tq - 1) // tk
    pairs = [(i, j) for i in range(nq) for j in range(last_j(i) + 1)]
    it = jnp.asarray(np.array([p[0] for p in pairs], np.int32))
    jt = jnp.asarray(np.array([p[1] for p in pairs], np.int32))
    groups = tk // LANES

    def body(it_ref, jt_ref, q_ref, k_ref, v_ref, o_ref, lse_ref, m_s, l_s, acc_s):
        p = pl.program_id(1)
        i, j = it_ref[p], jt_ref[p]

        @pl.when(j == 0)
        def _():
            m_s[...] = jnp.full(m_s.shape, -jnp.inf, F32)
            l_s[...] = jnp.zeros(l_s.shape, F32)
            acc_s[...] = jnp.zeros(acc_s.shape, F32)

        def step(masked):
            sc = _dot(q_ref[...], k_ref[...], NT)
            if masked:
                row = lax.broadcasted_iota(jnp.int32, (tq, tk), 0) + i * tq
                col = lax.broadcasted_iota(jnp.int32, (tq, tk), 1) + j * tk
                sc = jnp.where(col <= row, sc, -jnp.inf)
            cols = [sc[:, g * LANES:(g + 1) * LANES] for g in range(groups)]
            mx = cols[0]
            for g in range(1, groups):
                mx = jnp.maximum(mx, cols[g])
            m_prev = m_s[...]
            m_new = jnp.maximum(m_prev, jnp.max(mx, axis=1, keepdims=True))
            alpha = jnp.exp2((m_prev - m_new) * _EXP2_SCALE)
            ps, psum = [], None
            for g in range(groups):
                pg = jnp.exp2((cols[g] - m_new) * _EXP2_SCALE)
                psum = pg if psum is None else psum + pg
                ps.append(pg.astype(MD))
            l_s[...] = alpha * l_s[...] + psum
            acc_s[...] = acc_s[...] * alpha + _dot(jnp.concatenate(ps, axis=1), v_ref[...])
            m_s[...] = m_new

        needs_mask = (j + 1) * tk - 1 > i * tq

        @pl.when(needs_mask)
        def _():
            step(True)

        @pl.when(jnp.logical_not(needs_mask))
        def _():
            step(False)

        @pl.when(j == (i * tq + tq - 1) // tk)
        def _():
            l = jnp.sum(l_s[...], axis=1, keepdims=True)
            o_ref[...] = (acc_s[...] / l).astype(o_ref.dtype)
            lse_rep = m_s[...] * _EXP2_SCALE + jnp.log(l) * LOG2E
            lse_ref[...] = _lane_sums_as_row(lse_rep * (1.0 / LANES))

    grid_spec = pltpu.PrefetchScalarGridSpec(
        num_scalar_prefetch=2, grid=(HEADS, len(pairs)),
        in_specs=[pl.BlockSpec((tq, QK_PAD), lambda h, p, it, jt: (it[p], h)),
                  pl.BlockSpec((tk, QK_PAD), lambda h, p, it, jt: (jt[p], h)),
                  pl.BlockSpec((tk, V_DIM), lambda h, p, it, jt: (jt[p], h))],
        out_specs=[pl.BlockSpec((tq, V_DIM), lambda h, p, it, jt: (it[p], h)),
                   pl.BlockSpec((None, 1, tq), lambda h, p, it, jt: (h, 0, it[p]))],
        scratch_shapes=[pltpu.VMEM((tq, LANES), F32), pltpu.VMEM((tq, LANES), F32), pltpu.VMEM((tq, V_DIM), F32)])
    return pl.pallas_call(
        body, name=name, grid_spec=grid_spec,
        out_shape=[jax.ShapeDtypeStruct((s, HEADS * V_DIM), MD), jax.ShapeDtypeStruct((HEADS, 1, s), F32)],
        compiler_params=_params(("parallel", "arbitrary")),
    )(it, jt, qf, kf, v)


def attn_delta(o, do, name):
    s = o.shape[0]
    t = _tile(s, 1024)

    def body(o_ref, do_ref, d_ref):
        d_ref[...] = _lane_sums_as_row(o_ref[...].astype(F32) * do_ref[...].astype(F32))

    blk = pl.BlockSpec((t, V_DIM), lambda h, i: (i, h))
    return pl.pallas_call(
        body, name=name, grid=(HEADS, s // t), in_specs=[blk, blk],
        out_specs=pl.BlockSpec((None, 1, t), lambda h, i: (h, 0, i)),
        out_shape=jax.ShapeDtypeStruct((HEADS, 1, s), F32),
        compiler_params=_params(("parallel", "parallel")),
    )(o, do)


def flash_bwd(qf, kf, v, do, lse2, delta, name):
    s = qf.shape[0]
    tq, tk = _tile(s, 1024), _tile(s, 512)
    nq, nk = s // tq, s // tk
    first_i = lambda j: (j * tk) // tq
    pairs = [(i, j) for j in range(nk) for i in range(first_i(j), nq)]
    it = jnp.asarray(np.array([p[0] for p in pairs], np.int32))
    jt = jnp.asarray(np.array([p[1] for p in pairs], np.int32))
    n_pairs = len(pairs)

    def body(it_ref, jt_ref, q_ref, k_ref, v_ref, do_ref, lse_ref, dl_ref, dq_ref, dk_ref, dv_ref, dq_s, dk_s, dv_s):
        p = pl.program_id(1)
        i, j = it_ref[p], jt_ref[p]

        @pl.when(p == 0)
        def _():
            dq_s[...] = jnp.zeros(dq_s.shape, F32)

        @pl.when(i == (j * tk) // tq)
        def _():
            dk_s[...] = jnp.zeros(dk_s.shape, F32)
            dv_s[...] = jnp.zeros(dv_s.shape, F32)

        def step(masked):
            q, k, do_b = q_ref[...], k_ref[...], do_ref[...]
            pt = jnp.exp2(_dot(k, q, NT) * _EXP2_SCALE - lse_ref[...])
            if masked:
                krow = lax.broadcasted_iota(jnp.int32, (tk, tq), 0) + j * tk
                qcol = lax.broadcasted_iota(jnp.int32, (tk, tq), 1) + i * tq
                pt = jnp.where(krow <= qcol, pt, 0.0)
            dv_s[...] += _dot(pt.astype(MD), do_b)
            dst = (pt * (_dot(v_ref[...], do_b, NT) - dl_ref[...])).astype(MD)
            dk_s[...] += _dot(dst, q)
            row0 = pl.multiple_of(i * tq, tq)
            dq_s[pl.ds(row0, tq), :] += _dot(dst, k, TN)

        needs_mask = (j + 1) * tk - 1 > i * tq

        @pl.when(needs_mask)
        def _():
            step(True)

        @pl.when(jnp.logical_not(needs_mask))
        def _():
            step(False)

        @pl.when(i == nq - 1)
        def _():
            dk_ref[...] = (dk_s[...] * SOFTMAX_SCALE).astype(dk_ref.dtype)
            dv_ref[...] = dv_s[...].astype(dv_ref.dtype)

        @pl.when(p == n_pairs - 1)
        def _():
            dq_ref[...] = (dq_s[...] * SOFTMAX_SCALE).astype(dq_ref.dtype)

    vec = pl.BlockSpec((None, 1, tq), lambda h, p, it, jt: (h, 0, it[p]))
    grid_spec = pltpu.PrefetchScalarGridSpec(
        num_scalar_prefetch=2, grid=(HEADS, n_pairs),
        in_specs=[pl.BlockSpec((tq, QK_PAD), lambda h, p, it, jt: (it[p], h)),
                  pl.BlockSpec((tk, QK_PAD), lambda h, p, it, jt: (jt[p], h)),
                  pl.BlockSpec((tk, V_DIM), lambda h, p, it, jt: (jt[p], h)),
                  pl.BlockSpec((tq, V_DIM), lambda h, p, it, jt: (it[p], h)), vec, vec],
        out_specs=[pl.BlockSpec((s, QK_PAD), lambda h, p, it, jt: (0, h)),
                   pl.BlockSpec((tk, QK_PAD), lambda h, p, it, jt: (jt[p], h)),
                   pl.BlockSpec((tk, V_DIM), lambda h, p, it, jt: (jt[p], h))],
        scratch_shapes=[pltpu.VMEM((s, QK_PAD), F32), pltpu.VMEM((tk, QK_PAD), F32), pltpu.VMEM((tk, V_DIM), F32)])
    wide = jax.ShapeDtypeStruct((s, HEADS * QK_PAD), MD)
    return pl.pallas_call(
        body, name=name, grid_spec=grid_spec,
        out_shape=[wide, wide, jax.ShapeDtypeStruct((s, HEADS * V_DIM), MD)],
        compiler_params=_params(("parallel", "arbitrary")),
    )(it, jt, qf, kf, v, do, lse2, delta)


def _halo_rows(dtype):
    return 8 if jnp.dtype(dtype).itemsize == 4 else 16


def _prev_spec(ts, tc, hr, col_off):
    return pl.BlockSpec((hr, tc), lambda j, i: (jnp.maximum(i * (ts // hr) - 1, 0), j + col_off))


def _conv_taps(x, halo, w, taps, first):
    hr, ts = halo.shape[0], x.shape[0]
    xe = jnp.concatenate([jnp.where(first, 0.0, halo), x], axis=0)
    shifted = []
    acc = None
    for k in range(taps):
        sh = taps - 1 - k
        xs = xe if sh == 0 else pltpu.roll(xe, sh, 0)
        xs = xs[hr:hr + ts, :]
        shifted.append(xs)
        term = xs * w[k:k + 1, :]
        acc = term if acc is None else acc + term
    return acc, shifted


def conv_bwd(x, dy, w8, taps, x_col_off, name, ts_pref=256):
    halves = dy.ndim == 3
    s = dy.shape[-2]
    c = dy.shape[-1] * (2 if halves else 1)
    tc = _tile(dy.shape[-1], 512)
    ts = _tile(s, ts_pref, 16)
    hx, hy = _halo_rows(x.dtype), _halo_rows(dy.dtype)
    ns = s // ts
    nh = dy.shape[-1] // tc
    xo = x_col_off // tc
    assert x_col_off % tc == 0

    def body(x_ref, xp_ref, dy_ref, dyn_ref, w_ref, dx_ref, dw_ref, db_ref):
        i = pl.program_id(1)
        first = i == 0
        xv = x_ref[...].astype(F32)
        dyv = dy_ref[...].astype(F32)
        w = w_ref[...]
        _, shifted = _conv_taps(xv, xp_ref[...].astype(F32), w, taps, first)
        dye = jnp.concatenate([dyv, jnp.where(i == ns - 1, 0.0, dyn_ref[...].astype(F32))], axis=0)
        dx = None
        for k in range(taps):
            sh = taps - 1 - k
            ds_ = dye if sh == 0 else pltpu.roll(dye, ts + hy - sh, 0)
            term = ds_[0:ts, :] * w[k:k + 1, :]
            dx = term if dx is None else dx + term
        dx_ref[...] = dx.astype(dx_ref.dtype)
        rows = [jnp.sum(dyv * shifted[k], axis=0, keepdims=True) for k in range(taps)]
        rows.append(jnp.zeros((8 - taps, tc), F32))
        _acc_out(dw_ref, jnp.concatenate(rows, axis=0), first)
        _acc_out(db_ref, jnp.sum(dyv, axis=0, keepdims=True), first)

    last = s // hy - 1
    nxt = lambda i: jnp.minimum((i + 1) * (ts // hy), last)
    if halves:
        dy_spec = pl.BlockSpec((None, ts, tc), lambda j, i: (j // nh, i, j % nh))
        dyn_spec = pl.BlockSpec((None, hy, tc), lambda j, i: (j // nh, nxt(i), j % nh))
    else:
        dy_spec = pl.BlockSpec((ts, tc), lambda j, i: (i, j))
        dyn_spec = pl.BlockSpec((hy, tc), lambda j, i: (nxt(i), j))
    return pl.pallas_call(
        body, name=name, grid=(c // tc, ns),
        in_specs=[pl.BlockSpec((ts, tc), lambda j, i: (i, j + xo)), _prev_spec(ts, tc, hx, xo), dy_spec, dyn_spec,
                  pl.BlockSpec((8, tc), lambda j, i: (0, j))],
        out_specs=[pl.BlockSpec((ts, tc), lambda j, i: (i, j)), pl.BlockSpec((8, tc), lambda j, i: (0, j)),
                   pl.BlockSpec((1, tc), lambda j, i: (0, j))],
        out_shape=[jax.ShapeDtypeStruct((s, c), MD), jax.ShapeDtypeStruct((8, c), F32), jax.ShapeDtypeStruct((1, c), F32)],
        compiler_params=_params(("parallel", "arbitrary")),
    )(x, x, dy, dy, w8)


def _lru_gates(xc, wr_ref, br, wi_ref, bi, sp, nblk):
    xb = xc.astype(MD)
    zr, zi = [], []
    for b in range(nblk):
        blk = xb[:, b * LANES:(b + 1) * LANES]
        zr.append(_dot(blk, wr_ref[b]))
        zi.append(_dot(blk, wi_ref[b]))
    r = _sigmoid(jnp.concatenate(zr, axis=1) + br)
    ig = _sigmoid(jnp.concatenate(zi, axis=1) + bi)
    log_a = -LRU_C * r * sp
    a = jnp.exp(log_a)
    mult = jnp.sqrt(-_expm1(2.0 * log_a))
    return r, ig, a, mult


def lru_fwd(proj, x_off, y_off, w8, b_conv, w_rg, b_rg, w_ig, b_ig, lam, name):
    s = proj.shape[0]
    c = lam.shape[1]
    tc, ts = _tile(c, 512), _tile(s, 256, 16)
    nblk = tc // LANES
    xo, yo = x_off // tc, y_off // tc
    hr = _halo_rows(proj.dtype)

    def body(x_ref, xp_ref, y_ref, w_ref, bc_ref, wr_ref, br_ref, wi_ref, bi_ref, lam_ref, h_ref, hg_ref, carry):
        i = pl.program_id(1)
        first = i == 0
        xc, _ = _conv_taps(x_ref[...].astype(F32), xp_ref[...].astype(F32), w_ref[...], LRU_TAPS, first)
        xc = xc + bc_ref[...]
        sp = _softplus(-lam_ref[...])
        _, ig, a, mult = _lru_gates(xc, wr_ref, br_ref[...], wi_ref, bi_ref[...], sp, nblk)
        b = mult * (ig * xc)
        row = lax.broadcasted_iota(jnp.int32, (ts, tc), 0)
        d = 1
        while d < ts:
            keep = row >= d
            a_s, b_s = pltpu.roll(a, d, 0), pltpu.roll(b, d, 0)
            b = jnp.where(keep, a * b_s + b, b)
            a = jnp.where(keep, a * a_s, a)
            d *= 2
        h0 = jnp.where(first, 0.0, carry[7:8, :])
        h = a * h0 + b
        carry[...] = h[ts - 8:ts, :]
        h_ref[...] = h
        hg_ref[...] = (h * _gelu(y_ref[...].astype(F32))).astype(hg_ref.dtype)

    tile = lambda off: pl.BlockSpec((ts, tc), lambda j, i: (i, j + off))
    vec = pl.BlockSpec((1, tc), lambda j, i: (0, j))
    wblk = pl.BlockSpec((nblk, LANES, LANES), lambda j, i: (j, 0, 0))
    out = pl.BlockSpec((ts, tc), lambda j, i: (i, j))
    return pl.pallas_call(
        body, name=name, grid=(c // tc, s // ts),
        in_specs=[tile(xo), _prev_spec(ts, tc, hr, xo), tile(yo), pl.BlockSpec((8, tc), lambda j, i: (0, j)), vec,
                  wblk, vec, wblk, vec, vec],
        out_specs=[out, out],
        out_shape=[jax.ShapeDtypeStruct((s, c), F32), jax.ShapeDtypeStruct((s, c), MD)],
        scratch_shapes=[pltpu.VMEM((8, tc), F32)],
        compiler_params=_params(("parallel", "arbitrary")),
    )(proj, proj, proj, w8, b_conv, w_rg, b_rg, w_ig, b_ig, lam)


def lru_bwd(proj, x_off, y_off, h, dhg, w8, b_conv, w_rg, b_rg, w_ig, b_ig, lam, name):
    s = proj.shape[0]
    c = lam.shape[1]
    tc, ts = _tile(c, 512), _tile(s, 256, 16)
    nblk = tc // LANES
    ns = s // ts
    xo, yo = x_off // tc, y_off // tc
    hr = _halo_rows(proj.dtype)
    rev = lambda i: ns - 1 - i

    def body(x_ref, xp_ref, y_ref, h_ref, hp_ref, dhg_ref, w_ref, bc_ref, wr_ref, br_ref, wi_ref, bi_ref, lam_ref,
             dxc_ref, dy_ref, dwr_ref, dwi_ref, dbr_ref, dbi_ref, dlam_ref, a_carry, g_carry):
        step = pl.program_id(1)
        i = rev(step)
        start = step == 0
        top = i == 0
        xc, _ = _conv_taps(x_ref[...].astype(F32), xp_ref[...].astype(F32), w_ref[...], LRU_TAPS, top)
        xc = xc + bc_ref[...]
        lam_v = lam_ref[...]
        sp = _softplus(-lam_v)
        r, ig, a, mult = _lru_gates(xc, wr_ref, br_ref[...], wi_ref, bi_ref[...], sp, nblk)
        hv = h_ref[...]
        gy, dgy = _gelu_and_grad(y_ref[...].astype(F32))
        dhg_v = dhg_ref[...]
        dy_ref[...] = (dhg_v * hv * dgy).astype(dy_ref.dtype)
        row = lax.broadcasted_iota(jnp.int32, (ts, tc), 0)
        a_next = jnp.where(row == ts - 1, jnp.where(start, 0.0, a_carry[0:1, :]), pltpu.roll(a, ts - 1, 0))
        ca, cb = a_next, dhg_v * gy
        d = 1
        while d < ts:
            keep = row < ts - d
            a_s, b_s = pltpu.roll(ca, ts - d, 0), pltpu.roll(cb, ts - d, 0)
            cb = jnp.where(keep, ca * b_s + cb, cb)
            ca = jnp.where(keep, ca * a_s, ca)
            d *= 2
        g = ca * jnp.where(start, 0.0, g_carry[0:1, :]) + cb
        a_carry[...] = a[0:8, :]
        g_carry[...] = g[0:8, :]
        h_prev = jnp.where(row == 0, jnp.where(top, 0.0, hp_ref[7:8, :]), pltpu.roll(hv, 1, 0))
        da = g * h_prev
        ixc = ig * xc
        d_ixc = g * mult
        dlog_a = da * a - g * ixc * (a * a) / mult
        dzr = dlog_a * (-LRU_C * sp) * r * (1.0 - r)
        dzi = d_ixc * xc * ig * (1.0 - ig)
        dxc = d_ixc * ig
        xb, dzr_b, dzi_b = xc.astype(MD), dzr.astype(MD), dzi.astype(MD)
        parts, dwr, dwi = [], [], []
        for b in range(nblk):
            sl = slice(b * LANES, (b + 1) * LANES)
            parts.append(_dot(dzr_b[:, sl], wr_ref[b], NT) + _dot(dzi_b[:, sl], wi_ref[b], NT))
            dwr.append(_dot(xb[:, sl], dzr_b[:, sl], TN)[None])
            dwi.append(_dot(xb[:, sl], dzi_b[:, sl], TN)[None])
        dxc_ref[...] = dxc + jnp.concatenate(parts, axis=1)
        _acc_out(dwr_ref, jnp.concatenate(dwr, axis=0), start)
        _acc_out(dwi_ref, jnp.concatenate(dwi, axis=0), start)
        _acc_out(dbr_ref, jnp.sum(dzr, axis=0, keepdims=True), start)
        _acc_out(dbi_ref, jnp.sum(dzi, axis=0, keepdims=True), start)
        dsp = jnp.sum(dlog_a * (-LRU_C) * r, axis=0, keepdims=True)
        _acc_out(dlam_ref, -dsp * _sigmoid(-lam_v), start)

    vec = pl.BlockSpec((1, tc), lambda j, i: (0, j))
    wblk = pl.BlockSpec((nblk, LANES, LANES), lambda j, i: (j, 0, 0))
    tile = lambda off: pl.BlockSpec((ts, tc), lambda j, i: (rev(i), j + off))
    x_prev = pl.BlockSpec((hr, tc), lambda j, i: (jnp.maximum(rev(i) * (ts // hr) - 1, 0), j + xo))
    h_prev = pl.BlockSpec((8, tc), lambda j, i: (jnp.maximum(rev(i) * (ts // 8) - 1, 0), j))
    return pl.pallas_call(
        body, name=name, grid=(c // tc, ns),
        in_specs=[tile(xo), x_prev, tile(yo), tile(0), h_prev, tile(0), pl.BlockSpec((8, tc), lambda j, i: (0, j)), vec,
                  wblk, vec, wblk, vec, vec],
        out_specs=[tile(0), tile(0), wblk, wblk, vec, vec, vec],
        out_shape=[jax.ShapeDtypeStruct((s, c), F32), jax.ShapeDtypeStruct((s, c), MD),
                   jax.ShapeDtypeStruct(w_rg.shape, F32), jax.ShapeDtypeStruct(w_ig.shape, F32),
                   jax.ShapeDtypeStruct((1, c), F32), jax.ShapeDtypeStruct((1, c), F32), jax.ShapeDtypeStruct((1, c), F32)],
        scratch_shapes=[pltpu.VMEM((8, tc), F32), pltpu.VMEM((8, tc), F32)],
        compiler_params=_params(("parallel", "arbitrary")),
    )(proj, proj, proj, h, h, dhg, w8, b_conv, w_rg, b_rg, w_ig, b_ig, lam)


def merge_fwd(proj, gm_off, gl_off, o_mla, o_lru, name):
    s, c = o_mla.shape
    tc, ts = _tile(c, 1024), _tile(s, 512, 16)

    def body(gm_ref, gl_ref, om_ref, ol_ref, out_ref):
        out_ref[...] = (_sigmoid(gm_ref[...].astype(F32)) * om_ref[...] + _sigmoid(gl_ref[...].astype(F32)) * ol_ref[...]).astype(out_ref.dtype)

    tile = lambda off: pl.BlockSpec((ts, tc), lambda i, j: (i, j + off // tc))
    return pl.pallas_call(
        body, name=name, grid=(s // ts, c // tc), in_specs=[tile(gm_off), tile(gl_off), tile(0), tile(0)], out_specs=tile(0),
        out_shape=jax.ShapeDtypeStruct((s, c), MD), compiler_params=_params(("parallel", "parallel")),
    )(proj, proj, o_mla, o_lru)


def merge_bwd(proj, gm_off, gl_off, o_mla, o_lru, dmerged, name):
    s, c = o_mla.shape
    tc, ts = _tile(c, 1024), _tile(s, 512, 16)

    def body(gm_ref, gl_ref, om_ref, ol_ref, dm_ref, dom_ref, dol_ref, dgm_ref, dgl_ref):
        sm, sl = _sigmoid(gm_ref[...].astype(F32)), _sigmoid(gl_ref[...].astype(F32))
        dm = dm_ref[...]
        dom_ref[...] = (dm * sm).astype(dom_ref.dtype)
        dol_ref[...] = (dm * sl).astype(dol_ref.dtype)
        dgm_ref[...] = (dm * om_ref[...] * sm * (1.0 - sm)).astype(dgm_ref.dtype)
        dgl_ref[...] = (dm * ol_ref[...] * sl * (1.0 - sl)).astype(dgl_ref.dtype)

    tile = lambda off: pl.BlockSpec((ts, tc), lambda i, j: (i, j + off // tc))
    out = jax.ShapeDtypeStruct((s, c), MD)
    return pl.pallas_call(
        body, name=name, grid=(s // ts, c // tc), in_specs=[tile(gm_off), tile(gl_off), tile(0), tile(0), tile(0)],
        out_specs=[tile(0)] * 4, out_shape=[out] * 4, compiler_params=_params(("parallel", "parallel")),
    )(proj, proj, o_mla, o_lru, dmerged)


def xattn_fwd(cq, ck, cv, name):
    s = cq.shape[0]
    m = ck.shape[0]
    ts = _tile(s, 512, 16)
    scale = X_DIM ** -0.5

    def body(q_ref, k_ref, v_ref, o_ref):
        for h in range(X_HEADS):
            sl = slice(h * X_DIM, (h + 1) * X_DIM)
            sc = _dot(q_ref[:, sl], k_ref[:, sl], NT) * scale
            e = jnp.exp(sc - jnp.max(sc, axis=1, keepdims=True))
            p = e / jnp.sum(e, axis=1, keepdims=True)
            o_ref[:, sl] = _dot(p.astype(MD), v_ref[:, sl]).astype(o_ref.dtype)

    w = X_HEADS * X_DIM
    return pl.pallas_call(
        body, name=name, grid=(s // ts,),
        in_specs=[pl.BlockSpec((ts, w), lambda i: (i, 0)), pl.BlockSpec((m, w), lambda i: (0, 0)), pl.BlockSpec((m, w), lambda i: (0, 0))],
        out_specs=pl.BlockSpec((ts, w), lambda i: (i, 0)), out_shape=jax.ShapeDtypeStruct((s, w), MD),
        compiler_params=_params(("parallel",)),
    )(cq, ck, cv)


def xattn_bwd(cq, ck, cv, dco, name):
    s = cq.shape[0]
    m = ck.shape[0]
    ts = _tile(s, 512, 16)
    scale = X_DIM ** -0.5

    def body(q_ref, k_ref, v_ref, do_ref, dq_ref, dk_ref, dv_ref):
        first = pl.program_id(0) == 0
        dks, dvs = [], []
        for h in range(X_HEADS):
            sl = slice(h * X_DIM, (h + 1) * X_DIM)
            q, k, v, do = q_ref[:, sl], k_ref[:, sl], v_ref[:, sl], do_ref[:, sl]
            sc = _dot(q, k, NT) * scale
            e = jnp.exp(sc - jnp.max(sc, axis=1, keepdims=True))
            p = e / jnp.sum(e, axis=1, keepdims=True)
            dvs.append(_dot(p.astype(MD), do, TN))
            dp = _dot(do, v, NT)
            ds = (p * (dp - jnp.sum(dp * p, axis=1, keepdims=True)) * scale).astype(MD)
            dq_ref[:, sl] = _dot(ds, k).astype(dq_ref.dtype)
            dks.append(_dot(ds, q, TN))
        _acc_out(dk_ref, jnp.concatenate(dks, axis=1), first)
        _acc_out(dv_ref, jnp.concatenate(dvs, axis=1), first)

    w = X_HEADS * X_DIM
    row = pl.BlockSpec((ts, w), lambda i: (i, 0))
    full = pl.BlockSpec((m, w), lambda i: (0, 0))
    return pl.pallas_call(
        body, name=name, grid=(s // ts,), in_specs=[row, full, full, row], out_specs=[row, full, full],
        out_shape=[jax.ShapeDtypeStruct((s, w), MD), jax.ShapeDtypeStruct((m, w), F32), jax.ShapeDtypeStruct((m, w), F32)],
        compiler_params=_params(("arbitrary",)),
    )(cq, ck, cv, dco)


def _ffn_specs(up_pre):
    s, c2 = up_pre.shape
    f = c2 // 2
    tc, ts = _tile(f, 512), _tile(s, 512, 16)
    nf = f // tc
    hr = _halo_rows(up_pre.dtype)
    tile = lambda off: pl.BlockSpec((ts, tc), lambda j, i: (i, j + off))
    w = lambda off: pl.BlockSpec((8, tc), lambda j, i: (0, j + off))
    vec = lambda off: pl.BlockSpec((1, tc), lambda j, i: (0, j + off))
    specs = [tile(0), _prev_spec(ts, tc, hr, 0), tile(nf), _prev_spec(ts, tc, hr, nf), w(0), w(nf), vec(0), vec(nf)]
    return s, f, tc, ts, nf, tile, specs


def _ffn_gate_val(g_ref, gp_ref, v_ref, vp_ref, wg_ref, wv_ref, bg_ref, bv_ref, first):
    gate, _ = _conv_taps(g_ref[...].astype(F32), gp_ref[...].astype(F32), wg_ref[...], FFN_TAPS, first)
    val, _ = _conv_taps(v_ref[...].astype(F32), vp_ref[...].astype(F32), wv_ref[...], FFN_TAPS, first)
    return gate + bg_ref[...], val + bv_ref[...]


def ffn_act_fwd(up_pre, w8, b, name):
    s, f, tc, ts, nf, tile, specs = _ffn_specs(up_pre)

    def body(*refs):
        o_ref = refs[8]
        gate, val = _ffn_gate_val(*refs[:8], pl.program_id(1) == 0)
        o_ref[...] = (_gelu(gate) * val).astype(o_ref.dtype)

    return pl.pallas_call(
        body, name=name, grid=(nf, s // ts), in_specs=specs, out_specs=tile(0),
        out_shape=jax.ShapeDtypeStruct((s, f), MD), compiler_params=_params(("parallel", "parallel")),
    )(up_pre, up_pre, up_pre, up_pre, w8, w8, b, b)


def ffn_act_bwd(up_pre, w8, b, dact, name):
    s, f, tc, ts, nf, tile, specs = _ffn_specs(up_pre)

    def body(*refs):
        da_ref, o_ref = refs[8], refs[9]
        gate, val = _ffn_gate_val(*refs[:8], pl.program_id(1) == 0)
        ge, dge = _gelu_and_grad(gate)
        da = da_ref[...].astype(F32)
        o_ref[0] = (da * val * dge).astype(o_ref.dtype)
        o_ref[1] = (da * ge).astype(o_ref.dtype)

    return pl.pallas_call(
        body, name=name, grid=(nf, s // ts), in_specs=specs + [tile(0)],
        out_specs=pl.BlockSpec((2, ts, tc), lambda j, i: (0, i, j)), out_shape=jax.ShapeDtypeStruct((2, s, f), MD),
        compiler_params=_params(("parallel", "parallel")),
    )(up_pre, up_pre, up_pre, up_pre, w8, w8, b, b, dact)


def _adam(w, g, m, v):
    m = ADAM_B1 * m + (1.0 - ADAM_B1) * g
    v = ADAM_B2 * v + (1.0 - ADAM_B2) * (g * g)
    m_hat = m / (1.0 - ADAM_B1 ** ADAM_STEP)
    v_hat = v / (1.0 - ADAM_B2 ** ADAM_STEP)
    delta = -ADAM_LR * (m_hat / (jnp.sqrt(v_hat) + ADAM_EPS) + ADAM_WD * w)
    return delta, m, v


def _rows_tile(r, c):
    return _tile(r, max(16, (1 << 18) // c // 16 * 16), 16)


def adamw_sharded(mine, sib, chips, w, m, v, name):
    r, c = w.shape
    tr = _rows_tile(r, c)

    def body(g_ref, s_ref, c_ref, w_ref, m_ref, v_ref, go_ref, d_ref, mo_ref, vo_ref):
        grad = g_ref[...].astype(F32) + s_ref[...].astype(F32)
        for j in range(3):
            grad = grad + c_ref[j].astype(F32)
        go_ref[...] = grad
        d_ref[...], mo_ref[...], vo_ref[...] = _adam(w_ref[...], grad, m_ref[...], v_ref[...])

    row = pl.BlockSpec((tr, c), lambda i: (i, 0))
    own = pl.BlockSpec((None, tr, c), lambda i: (0, i, 0))
    out = jax.ShapeDtypeStruct((r, c), F32)
    return pl.pallas_call(
        body, name=name, grid=(r // tr,),
        in_specs=[own, own, pl.BlockSpec((3, tr, c), lambda i: (0, i, 0)), row, row, row],
        out_specs=[row] * 4, out_shape=[out] * 4, compiler_params=_params(("parallel",)),
    )(mine, sib, chips, w, m, v)


def adamw_plain(g, w, m, v, name):
    r, c = w.shape
    tr = _tile(r, 512, 8)

    def body(g_ref, w_ref, m_ref, v_ref, d_ref, mo_ref, vo_ref):
        d_ref[...], mo_ref[...], vo_ref[...] = _adam(w_ref[...], g_ref[...], m_ref[...], v_ref[...])

    row = pl.BlockSpec((tr, c), lambda i: (i, 0))
    out = jax.ShapeDtypeStruct((r, c), F32)
    return pl.pallas_call(body, name=name, grid=(r // tr,), in_specs=[row] * 4, out_specs=[row] * 3, out_shape=[out] * 3,
                          compiler_params=_params(("parallel",)))(g, w, m, v)


def chip_partial(mine, sib, name):
    _, r, c = mine.shape
    tr = _rows_tile(r, c)

    def body(g_ref, s_ref, o_ref):
        o_ref[...] = (g_ref[...].astype(F32) + s_ref[...].astype(F32)).astype(o_ref.dtype)

    slab = lambda off: pl.BlockSpec((None, tr, c), lambda j, i: (j + off, i, 0))
    return pl.pallas_call(
        body, name=name, grid=(3, r // tr), in_specs=[slab(1), slab(1)], out_specs=slab(0),
        out_shape=jax.ShapeDtypeStruct((3, r, c), MD), compiler_params=_params(("parallel", "parallel")),
    )(mine, sib)


def sum_devices(parts, name):
    _, r, c = parts.shape
    tr = _tile(r, 256, 8)

    def body(p_ref, o_ref):
        acc = p_ref[0]
        for d in range(1, N_DEV):
            acc = acc + p_ref[d]
        o_ref[...] = acc

    return pl.pallas_call(
        body, name=name, grid=(r // tr,), in_specs=[pl.BlockSpec((N_DEV, tr, c), lambda i: (0, i, 0))],
        out_specs=pl.BlockSpec((tr, c), lambda i: (i, 0)), out_shape=jax.ShapeDtypeStruct((r, c), F32),
        compiler_params=_params(("parallel",)))(parts)


def _place():
    return lax.axis_index("x"), lax.axis_index("y"), lax.axis_index("c")


def _other_chips(x, y):
    return [(1 - x, y), (x, 1 - y), (1 - x, 1 - y)]


_ANY = pl.BlockSpec(memory_space=pl.ANY)


def all_gather(shards, name):
    n = len(shards)

    def body(*refs):
        x_refs, out_refs = refs[:n], refs[n:2 * n]
        send_sems, recv_sems, local_sems = refs[2 * n:]
        x, y, cc = _place()
        me, sibling = (x, y, cc), (x, y, 1 - cc)
        chips = _other_chips(x, y)

        def slot(w, px, py, pc):
            return out_refs[w].at[4 * px + 2 * py + pc]

        def copy(k, w, block, to, src=None):
            return pltpu.make_async_remote_copy(
                src_ref=slot(w, *block) if src is None else src, dst_ref=slot(w, *block),
                send_sem=send_sems.at[k, w], recv_sem=recv_sems.at[k, w], device_id=to, device_id_type=MESH)

        mine = [pltpu.make_async_copy(x_refs[w], slot(w, *me), local_sems.at[w]) for w in range(n)]
        for cp in mine:
            cp.start()
        first = [copy(0, w, me, sibling, src=x_refs[w]) for w in range(n)]
        first += [copy(1 + j, w, me, (*chip, cc), src=x_refs[w]) for j, chip in enumerate(chips) for w in range(n)]
        for cp in first:
            cp.start()
        passed = []
        for j, chip in enumerate(chips):
            for w in range(n):
                copy(1 + j, w, (*chip, cc), me).wait_recv()
                passed.append(copy(4 + j, w, (*chip, cc), sibling))
                passed[-1].start()
        for w in range(n):
            copy(0, w, sibling, me).wait_recv()
        for j, chip in enumerate(chips):
            for w in range(n):
                copy(4 + j, w, (*chip, 1 - cc), me).wait_recv()
        for cp in first + passed:
            cp.wait_send()
        for cp in mine:
            cp.wait()

    return pl.pallas_call(
        body, name=name, out_shape=[jax.ShapeDtypeStruct((N_DEV, *a.shape), a.dtype) for a in shards],
        in_specs=[_ANY] * n, out_specs=[_ANY] * n,
        scratch_shapes=[pltpu.SemaphoreType.DMA((7, n)), pltpu.SemaphoreType.DMA((7, n)), pltpu.SemaphoreType.DMA((n,))],
    )(*shards)


def exchange_sibling(gs, name):
    n = len(gs)

    def body(*refs):
        g_refs, out_refs = refs[:n], refs[n:2 * n]
        send_sems, recv_sems = refs[2 * n:]
        x, y, cc = _place()
        copies = [pltpu.make_async_remote_copy(
            src_ref=g_refs[w].at[j], dst_ref=out_refs[w].at[j], send_sem=send_sems.at[j, w], recv_sem=recv_sems.at[j, w],
            device_id=(x, y, 1 - cc), device_id_type=MESH) for w in range(n) for j in range(4)]
        for cp in copies:
            cp.start()
        for cp in copies:
            cp.wait_recv()
        for cp in copies:
            cp.wait_send()

    return pl.pallas_call(
        body, name=name, out_shape=[jax.ShapeDtypeStruct(g.shape, g.dtype) for g in gs], in_specs=[_ANY] * n,
        out_specs=[_ANY] * n, scratch_shapes=[pltpu.SemaphoreType.DMA((4, n)), pltpu.SemaphoreType.DMA((4, n))],
    )(*gs)


def exchange_chips(ps, name):
    n = len(ps)

    def body(*refs):
        p_refs, out_refs = refs[:n], refs[n:2 * n]
        send_sems, recv_sems = refs[2 * n:]
        x, y, cc = _place()
        copies = [pltpu.make_async_remote_copy(
            src_ref=p_refs[w].at[j], dst_ref=out_refs[w].at[j], send_sem=send_sems.at[j, w], recv_sem=recv_sems.at[j, w],
            device_id=(*chip, cc), device_id_type=MESH) for j, chip in enumerate(_other_chips(x, y)) for w in range(n)]
        for cp in copies:
            cp.start()
        for cp in copies:
            cp.wait_recv()
        for cp in copies:
            cp.wait_send()

    return pl.pallas_call(
        body, name=name, out_shape=[jax.ShapeDtypeStruct(p.shape, p.dtype) for p in ps], in_specs=[_ANY] * n,
        out_specs=[_ANY] * n, scratch_shapes=[pltpu.SemaphoreType.DMA((3, n)), pltpu.SemaphoreType.DMA((3, n))],
    )(*ps)


def _step_local(x, mem, ang2, target, W):
    d = x.shape[1]
    wq = HEADS * QK_ROPE
    o_lx, o_ly, o_gm, o_gl = d + wq, 2 * d + wq, 3 * d + wq, 4 * d + wq
    one = lambda a, b, mode, dt, name, **kw: mm([(a, b)], mode, dt, name, **kw)

    h1 = rms_fwd(x, W["g_pre_mix"], "h1")
    proj = one(h1, W["w_main"], "nn", MD, "proj")
    ckv = one(h1, W["w_ckv"], "nn", MD, "ckv")
    kr = one(h1, W["w_kr"], "nn", MD, "k_rope")
    ckv_n = rms_fwd(ckv, W["g_ckv"], "ckv_n")
    kn = one(ckv_n, W["w_uk"], "nn", MD, "k_nope")
    v = one(ckv_n, W["w_uv"], "nn", MD, "v")
    qf, kf = rope_fwd(proj, kr, kn, ang2, "rope")
    o, lse2 = flash_fwd(qf, kf, v, "attn")
    o_mla = one(o, W["w_o_mla"], "nn", F32, "o_mla")
    lru_w = (W["w_conv8"], W["b_conv_lru"], W["w_rg"], W["b_rg"], W["w_ig"], W["b_ig"], W["lru_lambda"])
    hr, hg = lru_fwd(proj, o_lx, o_ly, *lru_w, "lru")
    o_lru = one(hg, W["w_o_lru"], "nn", F32, "o_lru")
    merged = merge_fwd(proj, o_gm, o_gl, o_mla, o_lru, "merge")
    y1 = one(merged, W["w_out"], "nn", F32, "y1")
    x1, h2 = post_pre_fwd(x, y1, W["g_post_mix"], W["g_pre_x"], "x1")
    mn = rms_fwd(mem, W["g_mem"], "mem_n")
    ck = one(mn, W["w_ck"], "nn", MD, "ck")
    cv = one(mn, W["w_cv"], "nn", MD, "cv")
    cq = one(h2, W["w_cq"], "nn", MD, "cq")
    co = xattn_fwd(cq, ck, cv, "xattn")
    y2 = one(co, W["w_co"], "nn", F32, "y2")
    x2, h3 = post_pre_fwd(x1, y2, W["g_post_x"], W["g_pre_ffn"], "x2")
    up_pre = one(h3, W["w_up"], "nn", MD, "up_pre")
    act = ffn_act_fwd(up_pre, W["w_fconv8"], W["b_fconv"], "act")
    y3 = one(act, W["w_down"], "nn", F32, "y3")
    G = {}
    loss, dx3, dy3, G["g_post_ffn"] = loss_head(x2, y3, target, W["g_post_ffn"], "loss")

    G["w_down"] = one(act, dy3, "tn", MD, "dw_down")
    dact = one(dy3, W["w_down"], "nt", MD, "dact")
    dup = ffn_act_bwd(up_pre, W["w_fconv8"], W["b_fconv"], dact, "dup")
    dup_pre, dwf8, G["b_fconv"] = conv_bwd(up_pre, dup, W["w_fconv8"], FFN_TAPS, 0, "ffn_conv_bwd")
    G["w_fconv"] = dwf8[:FFN_TAPS]
    G["w_up"] = one(h3, dup_pre, "tn", MD, "dw_up")
    dh3 = one(dup_pre, W["w_up"], "nt", F32, "dh3", tn=2048, tk=1024)
    dx2, dy2, G["g_pre_ffn"], G["g_post_x"] = pre_post_bwd(x2, y2, dx3, dh3, W["g_pre_ffn"], W["g_post_x"], "dx2")
    G["w_co"] = one(co, dy2, "tn", MD, "dw_co")
    dco = one(dy2, W["w_co"], "nt", MD, "dco")
    dcq, dck, dcv = xattn_bwd(cq, ck, cv, dco, "xattn_bwd")
    dck, dcv = dck.astype(MD), dcv.astype(MD)
    G["w_cq"] = one(h2, dcq, "tn", MD, "dw_cq")
    G["w_ck"] = one(mn, dck, "tn", MD, "dw_ck")
    G["w_cv"] = one(mn, dcv, "tn", MD, "dw_cv")
    dmn = mm([(dck, W["w_ck"]), (dcv, W["w_cv"])], "nt", F32, "dmem_n")
    _, G["g_mem"] = rms_bwd(mem, W["g_mem"], dmn, "dg_mem")
    dh2 = one(dcq, W["w_cq"], "nt", F32, "dh2")
    dx1, dy1, G["g_pre_x"], G["g_post_mix"] = pre_post_bwd(x1, y1, dx2, dh2, W["g_pre_x"], W["g_post_mix"], "dx1")
    G["w_out"] = one(merged, dy1, "tn", MD, "dw_out")
    dmerged = one(dy1, W["w_out"], "nt", F32, "dmerged")
    do_mla, do_lru, dgm, dgl = merge_bwd(proj, o_gm, o_gl, o_mla, o_lru, dmerged, "merge_bwd")
    G["w_o_mla"] = one(o, do_mla, "tn", MD, "dw_o_mla")
    do = one(do_mla, W["w_o_mla"], "nt", MD, "do")
    G["w_o_lru"] = one(hg, do_lru, "tn", MD, "dw_o_lru")
    dhg = one(do_lru, W["w_o_lru"], "nt", F32, "dhg")
    dxc, dly, G["w_rg"], G["w_ig"], G["b_rg"], G["b_ig"], G["lru_lambda"] = lru_bwd(proj, o_lx, o_ly, hr, dhg, *lru_w, "lru_bwd")
    dlx, dwc8, G["b_conv_lru"] = conv_bwd(proj, dxc, W["w_conv8"], LRU_TAPS, o_lx, "lru_conv_bwd")
    G["w_conv_lru"] = dwc8[:LRU_TAPS]
    delta = attn_delta(o, do, "attn_delta")
    dqf, dkf, dv = flash_bwd(qf, kf, v, do, lse2, delta, "attn_bwd")
    dqn, dqr, dkn, dkr = rope_bwd(dqf, dkf, ang2, "rope_bwd")
    G["w_uk"] = one(ckv_n, dkn, "tn", MD, "dw_uk")
    G["w_uv"] = one(ckv_n, dv, "tn", MD, "dw_uv")
    dckv_n = mm([(dkn, W["w_uk"]), (dv, W["w_uv"])], "nt", F32, "dckv_n")
    dckv, G["g_ckv"] = rms_bwd(ckv, W["g_ckv"], dckv_n, "dckv")
    dproj = jnp.concatenate([dqn, dqr, dlx, dly, dgm, dgl], axis=1)
    parts = [("w_main", dproj), ("w_ckv", dckv), ("w_kr", dkr)]
    for n, dpart in parts:
        G[n] = one(h1, dpart, "tn", MD, "d" + n)
    dh1 = mm([(dpart, W[n]) for n, dpart in parts], "nt", F32, "dh1", tk=1024)
    grad_x, G["g_pre_mix"] = pre_bwd(x, dx1, dh1, W["g_pre_mix"], "grad_x")
    return loss, grad_x, G


FLAT_COLS = 1024
BIG = (("w_in", 1), ("w_ukv", 1), ("w_o_mla", 0), ("w_o_lru", 0), ("w_out", 0), ("w_cq", 0), ("w_ck", 0), ("w_cv", 0),
       ("w_co", 1), ("w_up", 1), ("w_down", 0))
SMALL_SHARDED = ("w_conv_lru", "w_fconv")
REPLICATED = ("g_pre_mix", "g_post_mix", "g_ckv", "b_conv_lru", "w_rg", "b_rg", "w_ig", "b_ig", "lru_lambda",
              "g_pre_x", "g_post_x", "g_mem", "g_pre_ffn", "g_post_ffn", "b_fconv")
WEIGHTS = ("g_pre_mix", "g_post_mix", "w_in", "g_ckv", "w_ukv", "w_o_mla", "w_conv_lru", "b_conv_lru", "w_rg", "b_rg",
           "w_ig", "b_ig", "lru_lambda", "w_o_lru", "w_out", "g_pre_x", "g_post_x", "g_mem", "w_cq", "w_ck", "w_cv",
           "w_co", "g_pre_ffn", "g_post_ffn", "w_up", "w_fconv", "b_fconv", "w_down")


def _round_up(n, k):
    return (n + k - 1) // k * k


def _pack_vec(parts, row_mult):
    flat = jnp.concatenate([p.reshape(-1) for p in parts])
    n = flat.shape[0]
    total = _round_up(n, row_mult * FLAT_COLS)
    return jnp.pad(flat, (0, total - n)).reshape(-1, FLAT_COLS)


def _unpack_vec(flat2d, shapes):
    flat = flat2d.reshape(-1)
    out, off = [], 0
    for shp in shapes:
        n = int(np.prod(shp))
        out.append(flat[off:off + n].reshape(shp))
        off += n
    return out


def _gathered_to_full(blocks, axis):
    n, r, c = blocks.shape
    return blocks.reshape(n * r, c) if axis == 0 else blocks.transpose(1, 0, 2).reshape(r, n * c)


def _slabs_for(full, axis, devices):
    r, c = full.shape
    view = full.reshape(N_DEV, r // N_DEV, c) if axis == 0 else full.reshape(r, N_DEV, c // N_DEV)
    return jnp.stack([lax.dynamic_index_in_dim(view, t, axis=axis, keepdims=False) for t in devices])


def _split_w_in(w_in, d):
    hq = HEADS * (QK_NOPE + QK_ROPE)
    q = w_in[:, :hq].reshape(d, HEADS, QK_NOPE + QK_ROPE)
    q_nope, q_rope = q[:, :, :QK_NOPE].reshape(d, HEADS * QK_NOPE), q[:, :, QK_NOPE:].reshape(d, HEADS * QK_ROPE)
    out = {"w_ckv": w_in[:, hq:hq + KV_RANK]}
    off = hq + KV_RANK
    out["w_kr"] = jnp.pad(w_in[:, off:off + QK_ROPE], ((0, 0), (0, LANES - QK_ROPE)))
    out["w_main"] = jnp.concatenate([q_nope, q_rope, w_in[:, off + QK_ROPE:]], axis=1)
    return out


def _join_w_in(G, d):
    gm = G["w_main"]
    n_qn, n_qr = HEADS * QK_NOPE, HEADS * QK_ROPE
    q = jnp.concatenate([gm[:, :n_qn].reshape(d, HEADS, QK_NOPE), gm[:, n_qn:n_qn + n_qr].reshape(d, HEADS, QK_ROPE)], axis=2)
    return jnp.concatenate([q.reshape(d, -1), G["w_ckv"], G["w_kr"][:, :QK_ROPE], gm[:, n_qn + n_qr:]], axis=1)


def kernel(x, mem, positions, g_pre_mix, g_post_mix, w_in, g_ckv, w_ukv, w_o_mla, w_conv_lru, b_conv_lru, w_rg, b_rg, w_ig, b_ig, lru_lambda, w_o_lru, w_out, g_pre_x, g_post_x, g_mem, w_cq, w_ck, w_cv, w_co, g_pre_ffn, g_post_ffn, w_up, w_fconv, b_fconv, w_down, loss_target, m_g_pre_mix, m_g_post_mix, m_w_in, m_g_ckv, m_w_ukv, m_w_o_mla, m_w_conv_lru, m_b_conv_lru, m_w_rg, m_b_rg, m_w_ig, m_b_ig, m_lru_lambda, m_w_o_lru, m_w_out, m_g_pre_x, m_g_post_x, m_g_mem, m_w_cq, m_w_ck, m_w_cv, m_w_co, m_g_pre_ffn, m_g_post_ffn, m_w_up, m_w_fconv, m_b_fconv, m_w_down, v_g_pre_mix, v_g_post_mix, v_w_in, v_g_ckv, v_w_ukv, v_w_o_mla, v_w_conv_lru, v_b_conv_lru, v_w_rg, v_b_rg, v_w_ig, v_b_ig, v_lru_lambda, v_w_o_lru, v_w_out, v_g_pre_x, v_g_post_x, v_g_mem, v_w_cq, v_w_ck, v_w_cv, v_w_co, v_g_pre_ffn, v_g_post_ffn, v_w_up, v_w_fconv, v_b_fconv, v_w_down):
    given = dict(locals())
    P = {n: given[n][0] for n in WEIGHTS}
    M = {n: given["m_" + n][0] for n in WEIGHTS}
    V = {n: given["v_" + n][0] for n in WEIGHTS}
    xs, mems, tgt = x[0], mem[0], loss_target[0]
    s, d = xs.shape
    ax, ay, ac = _place()
    dev = 4 * ax + 2 * ay + ac

    big_names = [n for n, _ in BIG]
    gathered = all_gather([P[n].astype(MD) for n in big_names] + [P[n] for n in SMALL_SHARDED], "gather_weights")
    full = {n: _gathered_to_full(g, axis) for (n, axis), g in zip(BIG, gathered)}
    for n, g in zip(SMALL_SHARDED, gathered[len(BIG):]):
        full[n] = _gathered_to_full(g, 1)

    W = _split_w_in(full["w_in"], d)
    kv = full["w_ukv"].reshape(KV_RANK, HEADS, QK_NOPE + V_DIM)
    W["w_uk"] = kv[:, :, :QK_NOPE].reshape(KV_RANK, HEADS * QK_NOPE)
    W["w_uv"] = kv[:, :, QK_NOPE:].reshape(KV_RANK, HEADS * V_DIM)
    for n in ("w_o_mla", "w_o_lru", "w_out", "w_cq", "w_ck", "w_cv", "w_co", "w_up", "w_down"):
        W[n] = full[n]
    W["w_rg"], W["w_ig"] = P["w_rg"].astype(MD), P["w_ig"].astype(MD)
    for n in ("g_pre_mix", "g_post_mix", "g_ckv", "b_conv_lru", "b_rg", "b_ig", "lru_lambda", "g_pre_x", "g_post_x", "g_mem",
              "g_pre_ffn", "g_post_ffn", "b_fconv"):
        W[n] = P[n].reshape(1, -1)
    W["w_conv8"] = jnp.pad(full["w_conv_lru"], ((0, 8 - LRU_TAPS), (0, 0)))
    W["w_fconv8"] = jnp.pad(full["w_fconv"], ((0, 8 - FFN_TAPS), (0, 0)))

    inv_freq = ROPE_THETA ** (-jnp.arange(0, QK_ROPE, 2, dtype=F32) / QK_ROPE)
    ang = positions[0].astype(F32)[:, None] * inv_freq
    ang2 = jnp.tile(ang, (1, LANES // (QK_ROPE // 2)))

    loss, grad_x, G = _step_local(xs, mems, ang2, tgt, W)
    loss = lax.psum(loss[0, 0], ("x", "y", "c"))

    G["w_in"] = _join_w_in(G, d)
    G["w_ukv"] = jnp.concatenate([G["w_uk"].reshape(KV_RANK, HEADS, QK_NOPE), G["w_uv"].reshape(KV_RANK, HEADS, V_DIM)], axis=2).reshape(KV_RANK, -1)
    chips = [(ax, ay), (1 - ax, ay), (ax, 1 - ay), (1 - ax, 1 - ay)]
    mine = [_slabs_for(G[n], axis, [4 * px + 2 * py + ac for px, py in chips]) for n, axis in BIG]
    for_sibling = [_slabs_for(G[n], axis, [4 * px + 2 * py + (1 - ac) for px, py in chips]) for n, axis in BIG]
    from_sibling = exchange_sibling(for_sibling, "grads_to_sibling")
    chip_sums = [chip_partial(a, b, "chip_partial_" + n) for n, a, b in zip(big_names, mine, from_sibling)]
    from_chips = exchange_chips(chip_sums, "grads_to_chips")
    res = {k: {} for k in ("grad", "delta", "m", "v")}
    for n, a, b, c in zip(big_names, mine, from_sibling, from_chips):
        outs = adamw_sharded(a, b, c, P[n], M[n], V[n], "adamw_" + n)
        for kind, o in zip(("grad", "delta", "m", "v"), outs):
            res[kind][n] = o

    small_names = REPLICATED + SMALL_SHARDED
    g_small_shapes = [G[n].shape for n in small_names]
    partials, = all_gather([_pack_vec([G[n] for n in small_names], 128)], "gather_small_grads")
    summed = dict(zip(small_names, _unpack_vec(sum_devices(partials, "sum_small_grads"), g_small_shapes)))
    for n in SMALL_SHARDED:
        t, c = summed[n].shape
        summed[n] = lax.dynamic_index_in_dim(summed[n].reshape(t, N_DEV, c // N_DEV), dev, axis=1, keepdims=False)
    sg = {n: summed[n].reshape(P[n].shape) for n in small_names}
    vec = lambda src: _pack_vec([src[n] for n in small_names], 128)
    small_out = adamw_plain(vec(sg), vec(P), vec(M), vec(V), "adamw_small")
    shapes = [P[n].shape for n in small_names]
    for kind, flat in zip(("delta", "m", "v"), small_out):
        res[kind].update(zip(small_names, _unpack_vec(flat, shapes)))
    res["grad"].update(sg)

    outs = [loss, grad_x[None]]
    for kind in ("grad", "delta", "m", "v"):
        outs += [res[kind][n][None] for n in WEIGHTS]
    return tuple(outs)
```

```python
import numpy as np
import jax
import jax.numpy as jnp
from jax import lax
from jax.experimental import pallas as pl
from jax.experimental.pallas import tpu as pltpu

MD = jnp.bfloat16
F32 = jnp.float32
EPS = 1e-6
N_DEV = 8
HEADS = 16
QK_NOPE = 128
QK_ROPE = 64
V_DIM = 128
QK_PAD = 256
KV_RANK = 512
SOFTMAX_SCALE = (QK_NOPE + QK_ROPE) ** -0.5
LOG2E = 1.4426950408889634
ROPE_THETA = 10000.0
LRU_BLOCKS = 16
LRU_C = 8.0
X_HEADS = 4
X_DIM = 128
LRU_TAPS = 4
FFN_TAPS = 3
ADAM_LR, ADAM_B1, ADAM_B2, ADAM_EPS, ADAM_WD, ADAM_STEP = 0.001, 0.9, 0.999, 1e-08, 0.01, 10
LANES = 128
VMEM_LIMIT = 52 * 1024 * 1024
MESH = pl.DeviceIdType.MESH

NN = (((1,), (0,)), ((), ()))
NT = (((1,), (1,)), ((), ()))
TN = (((0,), (0,)), ((), ()))


def _tile(n, pref, align=LANES):
    if n <= pref:
        return n
    d = (pref // align) * align
    while d > align and n % d:
        d -= align
    assert n % d == 0, (n, pref, align)
    return d


def _params(sem):
    return pltpu.CompilerParams(dimension_semantics=sem, vmem_limit_bytes=VMEM_LIMIT)


def _dot(a, b, dims=NN):
    return lax.dot_general(a, b, dims, preferred_element_type=F32)


def _sigmoid(x):
    return 1.0 / (1.0 + jnp.exp(-x))


_GELU_C = 0.7978845608028654
_GELU_A = 0.044715


def _gelu(x):
    return 0.5 * x * (1.0 + jnp.tanh(x * (_GELU_C + (_GELU_C * _GELU_A) * (x * x))))


def _gelu_and_grad(x):
    x2 = x * x
    t = jnp.tanh(x * (_GELU_C + (_GELU_C * _GELU_A) * x2))
    hx = 0.5 * x
    g = hx + hx * t
    dg = 0.5 + 0.5 * t + hx * (1.0 - t * t) * (_GELU_C + (3.0 * _GELU_C * _GELU_A) * x2)
    return g, dg


def _expm1(x):
    small = x * (1.0 + x * (0.5 + x * (1.0 / 6.0 + x * (1.0 / 24.0 + x * (1.0 / 120.0)))))
    return jnp.where(jnp.abs(x) < 0.05, small, jnp.exp(x) - 1.0)


def _softplus(x):
    return jnp.maximum(x, 0.0) + jnp.log(1.0 + jnp.exp(-jnp.abs(x)))


def _rms_hat(x):
    r = lax.rsqrt(jnp.mean(x * x, axis=-1, keepdims=True) + EPS)
    return x * r, r


def _rms_bwd(x, g, dn):
    xh, r = _rms_hat(x)
    dg = jnp.sum(dn * xh, axis=0, keepdims=True)
    dxh = dn * g
    dx = r * (dxh - xh * jnp.mean(dxh * xh, axis=-1, keepdims=True))
    return dx, dg


def _acc_out(ref, val, first):
    @pl.when(first)
    def _():
        ref[...] = val

    @pl.when(jnp.logical_not(first))
    def _():
        ref[...] += val


def _lane_sums_as_row(x):
    ones = jnp.ones((8, LANES), F32)
    return lax.dot_general(ones, x, NT, precision=lax.Precision.HIGHEST, preferred_element_type=F32)[0:1, :]


def mm(pairs, mode, out_dtype, name, tm=1024, tn=1024, tk=2048):
    dims = {"nn": NN, "nt": NT, "tn": TN}[mode]
    shapes = []
    for a, b in pairs:
        if mode == "nn":
            (m, k), (k2, n) = a.shape, b.shape
        elif mode == "nt":
            (m, k), (n, k2) = a.shape, b.shape
        else:
            (k, m), (k2, n) = a.shape, b.shape
        assert k == k2, (name, a.shape, b.shape)
        shapes.append((m, n, k))
    m, n = shapes[0][0], shapes[0][1]
    assert all(s[0] == m and s[1] == n for s in shapes)
    tm, tn = _tile(m, tm), _tile(n, tn)
    tks = [_tile(s[2], tk) for s in shapes]
    nks = [s[2] // t for s, t in zip(shapes, tks)]
    starts = [sum(nks[:p]) for p in range(len(pairs))]
    nk_total = sum(nks)

    in_specs, args = [], []
    for p, (a, b) in enumerate(pairs):
        def kk(k, p=p):
            return jnp.clip(k - starts[p], 0, nks[p] - 1)
        if mode == "tn":
            in_specs.append(pl.BlockSpec((tks[p], tm), lambda i, j, k, kk=kk: (kk(k), i)))
        else:
            in_specs.append(pl.BlockSpec((tm, tks[p]), lambda i, j, k, kk=kk: (i, kk(k))))
        if mode == "nt":
            in_specs.append(pl.BlockSpec((tn, tks[p]), lambda i, j, k, kk=kk: (j, kk(k))))
        else:
            in_specs.append(pl.BlockSpec((tks[p], tn), lambda i, j, k, kk=kk: (kk(k), j)))
        args += [a, b]

    def body(*refs):
        ins, o_ref, scratch = refs[:2 * len(pairs)], refs[2 * len(pairs)], refs[2 * len(pairs) + 1:]
        if nk_total == 1:
            o_ref[...] = _dot(ins[0][...], ins[1][...], dims).astype(o_ref.dtype)
            return
        acc_ref, = scratch
        k = pl.program_id(2)
        for p in range(len(pairs)):
            @pl.when(jnp.logical_and(k >= starts[p], k < starts[p] + nks[p]))
            def _(p=p):
                prod = _dot(ins[2 * p][...], ins[2 * p + 1][...], dims)

                @pl.when(k == 0)
                def _():
                    acc_ref[...] = prod

                @pl.when(k > 0)
                def _():
                    acc_ref[...] += prod

        @pl.when(k == nk_total - 1)
        def _():
            o_ref[...] = acc_ref[...].astype(o_ref.dtype)

    return pl.pallas_call(
        body, name=name, grid=(m // tm, n // tn, nk_total),
        in_specs=in_specs, out_specs=pl.BlockSpec((tm, tn), lambda i, j, k: (i, j)),
        out_shape=jax.ShapeDtypeStruct((m, n), out_dtype),
        scratch_shapes=[] if nk_total == 1 else [pltpu.VMEM((tm, tn), F32)],
        compiler_params=_params(("parallel", "parallel", "arbitrary")),
    )(*args)


def rms_fwd(x, g, name):
    s, d = x.shape
    ts = _tile(s, 512, 16)

    def body(x_ref, g_ref, o_ref):
        xh, _ = _rms_hat(x_ref[...].astype(F32))
        o_ref[...] = (xh * g_ref[...]).astype(o_ref.dtype)

    return pl.pallas_call(
        body, name=name, grid=(s // ts,),
        in_specs=[pl.BlockSpec((ts, d), lambda i: (i, 0)), pl.BlockSpec((1, d), lambda i: (0, 0))],
        out_specs=pl.BlockSpec((ts, d), lambda i: (i, 0)),
        out_shape=jax.ShapeDtypeStruct((s, d), MD),
        compiler_params=_params(("parallel",)),
    )(x, g)


def rms_bwd(x, g, dn, name):
    s, d = x.shape
    ts = _tile(s, 256, 16)

    def body(x_ref, g_ref, dn_ref, dx_ref, dg_ref):
        dx, dg = _rms_bwd(x_ref[...].astype(F32), g_ref[...], dn_ref[...].astype(F32))
        dx_ref[...] = dx.astype(dx_ref.dtype)
        _acc_out(dg_ref, dg, pl.program_id(0) == 0)

    row = pl.BlockSpec((ts, d), lambda i: (i, 0))
    vec = pl.BlockSpec((1, d), lambda i: (0, 0))
    return pl.pallas_call(
        body, name=name, grid=(s // ts,), in_specs=[row, vec, row], out_specs=[row, vec],
        out_shape=[jax.ShapeDtypeStruct((s, d), MD), jax.ShapeDtypeStruct((1, d), F32)],
        compiler_params=_params(("arbitrary",)),
    )(x, g, dn)


def post_pre_fwd(x, y, g_post, g_pre, name):
    s, d = x.shape
    ts = _tile(s, 256, 16)

    def body(x_ref, y_ref, gp_ref, gq_ref, xn_ref, h_ref):
        yh, _ = _rms_hat(y_ref[...])
        xn = x_ref[...] + yh * gp_ref[...]
        xn_ref[...] = xn
        xh, _ = _rms_hat(xn)
        h_ref[...] = (xh * gq_ref[...]).astype(h_ref.dtype)

    row = pl.BlockSpec((ts, d), lambda i: (i, 0))
    vec = pl.BlockSpec((1, d), lambda i: (0, 0))
    return pl.pallas_call(
        body, name=name, grid=(s // ts,), in_specs=[row, row, vec, vec], out_specs=[row, row],
        out_shape=[jax.ShapeDtypeStruct((s, d), F32), jax.ShapeDtypeStruct((s, d), MD)],
        compiler_params=_params(("parallel",)),
    )(x, y, g_post, g_pre)


def loss_head(x, y, target, g_post, name):
    s, d = x.shape
    ts = _tile(s, 256, 16)

    def body(x_ref, y_ref, t_ref, g_ref, loss_ref, dx_ref, dy_ref, dg_ref):
        first = pl.program_id(0) == 0
        y = y_ref[...]
        yh, _ = _rms_hat(y)
        diff = x_ref[...] + yh * g_ref[...] - t_ref[...]
        part = 0.5 * jnp.sum(jnp.sum(diff * diff, axis=1, keepdims=True) * (1.0 / d), axis=0, keepdims=True)
        _acc_out(loss_ref, part, first)
        dx = diff * (1.0 / d)
        dx_ref[...] = dx
        dy, dg = _rms_bwd(y, g_ref[...], dx)
        dy_ref[...] = dy.astype(dy_ref.dtype)
        _acc_out(dg_ref, dg, first)

    row = pl.BlockSpec((ts, d), lambda i: (i, 0))
    vec = pl.BlockSpec((1, d), lambda i: (0, 0))
    one = pl.BlockSpec((1, 1), lambda i: (0, 0))
    return pl.pallas_call(
        body, name=name, grid=(s // ts,), in_specs=[row, row, row, vec], out_specs=[one, row, row, vec],
        out_shape=[jax.ShapeDtypeStruct((1, 1), F32), jax.ShapeDtypeStruct((s, d), F32),
                   jax.ShapeDtypeStruct((s, d), MD), jax.ShapeDtypeStruct((1, d), F32)],
        compiler_params=_params(("arbitrary",)),
    )(x, y, target, g_post)


def pre_post_bwd(x, y, dx_res, dh, g_pre, g_post, name):
    s, d = x.shape
    ts = _tile(s, 256, 16)

    def body(x_ref, y_ref, dr_ref, dh_ref, gq_ref, gp_ref, dx_ref, dy_ref, dgq_ref, dgp_ref):
        first = pl.program_id(0) == 0
        dxa, dgq = _rms_bwd(x_ref[...], gq_ref[...], dh_ref[...])
        dx = dr_ref[...] + dxa
        dx_ref[...] = dx
        dy, dgp = _rms_bwd(y_ref[...], gp_ref[...], dx)
        dy_ref[...] = dy.astype(dy_ref.dtype)
        _acc_out(dgq_ref, dgq, first)
        _acc_out(dgp_ref, dgp, first)

    row = pl.BlockSpec((ts, d), lambda i: (i, 0))
    vec = pl.BlockSpec((1, d), lambda i: (0, 0))
    return pl.pallas_call(
        body, name=name, grid=(s // ts,), in_specs=[row, row, row, row, vec, vec], out_specs=[row, row, vec, vec],
        out_shape=[jax.ShapeDtypeStruct((s, d), F32), jax.ShapeDtypeStruct((s, d), MD),
                   jax.ShapeDtypeStruct((1, d), F32), jax.ShapeDtypeStruct((1, d), F32)],
        compiler_params=_params(("arbitrary",)),
    )(x, y, dx_res, dh, g_pre, g_post)


def pre_bwd(x, dx_res, dh, g_pre, name):
    s, d = x.shape
    ts = _tile(s, 256, 16)

    def body(x_ref, dr_ref, dh_ref, g_ref, dx_ref, dg_ref):
        dxa, dg = _rms_bwd(x_ref[...], g_ref[...], dh_ref[...])
        dx_ref[...] = dr_ref[...] + dxa
        _acc_out(dg_ref, dg, pl.program_id(0) == 0)

    row = pl.BlockSpec((ts, d), lambda i: (i, 0))
    vec = pl.BlockSpec((1, d), lambda i: (0, 0))
    return pl.pallas_call(
        body, name=name, grid=(s // ts,), in_specs=[row, row, row, vec], out_specs=[row, vec],
        out_shape=[jax.ShapeDtypeStruct((s, d), F32), jax.ShapeDtypeStruct((1, d), F32)],
        compiler_params=_params(("arbitrary",)),
    )(x, dx_res, dh, g_pre)


def _swap_halves(x):
    lane = lax.broadcasted_iota(jnp.int32, x.shape, 1)
    return jnp.where((lane % QK_ROPE) < QK_ROPE // 2, pltpu.roll(x, LANES - QK_ROPE // 2, 1),
                     pltpu.roll(x, QK_ROPE // 2, 1))


def _rope_tables(ang_ref, sign):
    lane = lax.broadcasted_iota(jnp.int32, ang_ref.shape, 1)
    sgn = jnp.where((lane % QK_ROPE) < QK_ROPE // 2, -sign, sign)
    ang = ang_ref[...]
    return jnp.cos(ang), jnp.sin(ang) * sgn


def rope_fwd(proj, kr, kn, ang2, name):
    s = proj.shape[0]
    ts = _tile(s, 256, 16)
    wn, wr = HEADS * QK_NOPE, HEADS * QK_ROPE

    def body(qn_ref, qr_ref, kr_ref, kn_ref, ang_ref, qo_ref, ko_ref):
        c, sn = _rope_tables(ang_ref, 1.0)
        lane = lax.broadcasted_iota(jnp.int32, (ts, LANES), 1)
        lo = lane < QK_ROPE
        x = kr_ref[...].astype(F32)
        k_rot = jnp.where(lo, x * c + _swap_halves(x) * sn, 0.0).astype(ko_ref.dtype)
        for j in range(HEADS // 2):
            x = qr_ref[:, j * LANES:(j + 1) * LANES].astype(F32)
            r = x * c + _swap_halves(x) * sn
            pair = (jnp.where(lo, r, 0.0), jnp.where(lo, pltpu.roll(r, QK_ROPE, 1), 0.0))
            for e in range(2):
                h = 2 * j + e
                qo_ref[:, h * QK_PAD:h * QK_PAD + LANES] = qn_ref[:, h * LANES:(h + 1) * LANES]
                qo_ref[:, h * QK_PAD + LANES:(h + 1) * QK_PAD] = pair[e].astype(qo_ref.dtype)
                ko_ref[:, h * QK_PAD:h * QK_PAD + LANES] = kn_ref[:, h * LANES:(h + 1) * LANES]
                ko_ref[:, h * QK_PAD + LANES:(h + 1) * QK_PAD] = k_rot

    row = lambda w, blk: pl.BlockSpec((ts, w), lambda i: (i, blk))
    return pl.pallas_call(
        body, name=name, grid=(s // ts,),
        in_specs=[row(wn, 0), row(wr, wn // wr), row(LANES, 0), row(wn, 0), row(LANES, 0)],
        out_specs=[row(HEADS * QK_PAD, 0), row(HEADS * QK_PAD, 0)],
        out_shape=[jax.ShapeDtypeStruct((s, HEADS * QK_PAD), MD), jax.ShapeDtypeStruct((s, HEADS * QK_PAD), MD)],
        compiler_params=_params(("parallel",)),
    )(proj, proj, kr, kn, ang2)


def rope_bwd(dqf, dkf, ang2, name):
    s = dqf.shape[0]
    ts = _tile(s, 256, 16)
    wn, wr = HEADS * QK_NOPE, HEADS * QK_ROPE

    def body(dq_ref, dk_ref, ang_ref, qn_ref, qr_ref, kn_ref, kr_ref):
        c, sn = _rope_tables(ang_ref, -1.0)
        lane = lax.broadcasted_iota(jnp.int32, (ts, LANES), 1)
        lo = lane < QK_ROPE
        rot = lambda dy: dy * c + _swap_halves(dy) * sn
        dk_rope = None
        for j in range(HEADS // 2):
            halves = []
            for e in range(2):
                h = 2 * j + e
                qn_ref[:, h * LANES:(h + 1) * LANES] = dq_ref[:, h * QK_PAD:h * QK_PAD + LANES]
                kn_ref[:, h * LANES:(h + 1) * LANES] = dk_ref[:, h * QK_PAD:h * QK_PAD + LANES]
                halves.append(jnp.where(lo, dq_ref[:, h * QK_PAD + LANES:(h + 1) * QK_PAD].astype(F32), 0.0))
                dk_h = dk_ref[:, h * QK_PAD + LANES:(h + 1) * QK_PAD].astype(F32)
                dk_rope = dk_h if dk_rope is None else dk_rope + dk_h
            qr_ref[:, j * LANES:(j + 1) * LANES] = rot(halves[0] + pltpu.roll(halves[1], QK_ROPE, 1)).astype(qr_ref.dtype)
        kr_ref[...] = jnp.where(lo, rot(jnp.where(lo, dk_rope, 0.0)), 0.0).astype(kr_ref.dtype)

    row = lambda w: pl.BlockSpec((ts, w), lambda i: (i, 0))
    return pl.pallas_call(
        body, name=name, grid=(s // ts,),
        in_specs=[row(HEADS * QK_PAD), row(HEADS * QK_PAD), row(LANES)],
        out_specs=[row(wn), row(wr), row(wn), row(LANES)],
        out_shape=[jax.ShapeDtypeStruct((s, wn), MD), jax.ShapeDtypeStruct((s, wr), MD),
                   jax.ShapeDtypeStruct((s, wn), MD), jax.ShapeDtypeStruct((s, LANES), MD)],
        compiler_params=_params(("parallel",)),
    )(dqf, dkf, ang2)


_EXP2_SCALE = SOFTMAX_SCALE * LOG2E


def flash_fwd(qf, kf, v, name):
    s = qf.shape[0]
    tq = _tile(s, 512)
    tk = _tile(s, 1024)
    nq = s // tq
    last_j = lambda i: (i * tq + tq - 1) // tk
    pairs = [(i, j) for i in range(nq) for j in range(last_j(i) + 1)]
    it = jnp.asarray(np.array([p[0] for p in pairs], np.int32))
    jt = jnp.asarray(np.array([p[1] for p in pairs], np.int32))
    groups = tk // LANES

    def body(it_ref, jt_ref, q_ref, k_ref, v_ref, o_ref, lse_ref, m_s, l_s, acc_s):
        p = pl.program_id(1)
        i, j = it_ref[p], jt_ref[p]

        @pl.when(j == 0)
        def _():
            m_s[...] = jnp.full(m_s.shape, -jnp.inf, F32)
            l_s[...] = jnp.zeros(l_s.shape, F32)
            acc_s[...] = jnp.zeros(acc_s.shape, F32)

        def step(masked):
            sc = _dot(q_ref[...], k_ref[...], NT)
            if masked:
                row = lax.broadcasted_iota(jnp.int32, (tq, tk), 0) + i * tq
                col = lax.broadcasted_iota(jnp.int32, (tq, tk), 1) + j * tk
                sc = jnp.where(col <= row, sc, -jnp.inf)
            cols = [sc[:, g * LANES:(g + 1) * LANES] for g in range(groups)]
            mx = cols[0]
            for g in range(1, groups):
                mx = jnp.maximum(mx, cols[g])
            m_prev = m_s[...]
            m_new = jnp.maximum(m_prev, jnp.max(mx, axis=1, keepdims=True))
            alpha = jnp.exp2((m_prev - m_new) * _EXP2_SCALE)
            ps, psum = [], None
            for g in range(groups):
                pg = jnp.exp2((cols[g] - m_new) * _EXP2_SCALE)
                psum = pg if psum is None else psum + pg
                ps.append(pg.astype(MD))
            l_s[...] = alpha * l_s[...] + psum
            acc_s[...] = acc_s[...] * alpha + _dot(jnp.concatenate(ps, axis=1), v_ref[...])
            m_s[...] = m_new

        needs_mask = (j + 1) * tk - 1 > i * tq

        @pl.when(needs_mask)
        def _():
            step(True)

        @pl.when(jnp.logical_not(needs_mask))
        def _():
            step(False)

        @pl.when(j == (i * tq + tq - 1) // tk)
        def _():
            l = jnp.sum(l_s[...], axis=1, keepdims=True)
            o_ref[...] = (acc_s[...] / l).astype(o_ref.dtype)
            lse_rep = m_s[...] * _EXP2_SCALE + jnp.log(l) * LOG2E
            lse_ref[...] = _lane_sums_as_row(lse_rep * (1.0 / LANES))

    grid_spec = pltpu.PrefetchScalarGridSpec(
        num_scalar_prefetch=2, grid=(HEADS, len(pairs)),
        in_specs=[pl.BlockSpec((tq, QK_PAD), lambda h, p, it, jt: (it[p], h)),
                  pl.BlockSpec((tk, QK_PAD), lambda h, p, it, jt: (jt[p], h)),
                  pl.BlockSpec((tk, V_DIM), lambda h, p, it, jt: (jt[p], h))],
        out_specs=[pl.BlockSpec((tq, V_DIM), lambda h, p, it, jt: (it[p], h)),
                   pl.BlockSpec((None, 1, tq), lambda h, p, it, jt: (h, 0, it[p]))],
        scratch_shapes=[pltpu.VMEM((tq, LANES), F32), pltpu.VMEM((tq, LANES), F32), pltpu.VMEM((tq, V_DIM), F32)])
    return pl.pallas_call(
        body, name=name, grid_spec=grid_spec,
        out_shape=[jax.ShapeDtypeStruct((s, HEADS * V_DIM), MD), jax.ShapeDtypeStruct((HEADS, 1, s), F32)],
        compiler_params=_params(("parallel", "arbitrary")),
    )(it, jt, qf, kf, v)


def attn_delta(o, do, name):
    s = o.shape[0]
    t = _tile(s, 1024)

    def body(o_ref, do_ref, d_ref):
        d_ref[...] = _lane_sums_as_row(o_ref[...].astype(F32) * do_ref[...].astype(F32))

    blk = pl.BlockSpec((t, V_DIM), lambda h, i: (i, h))
    return pl.pallas_call(
        body, name=name, grid=(HEADS, s // t), in_specs=[blk, blk],
        out_specs=pl.BlockSpec((None, 1, t), lambda h, i: (h, 0, i)),
        out_shape=jax.ShapeDtypeStruct((HEADS, 1, s), F32),
        compiler_params=_params(("parallel", "parallel")),
    )(o, do)


def flash_bwd(qf, kf, v, do, lse2, delta, name):
    s = qf.shape[0]
    tq, tk = _tile(s, 1024), _tile(s, 512)
    nq, nk = s // tq, s // tk
    first_i = lambda j: (j * tk) // tq
    pairs = [(i, j) for j in range(nk) for i in range(first_i(j), nq)]
    it = jnp.asarray(np.array([p[0] for p in pairs], np.int32))
    jt = jnp.asarray(np.array([p[1] for p in pairs], np.int32))
    n_pairs = len(pairs)

    def body(it_ref, jt_ref, q_ref, k_ref, v_ref, do_ref, lse_ref, dl_ref, dq_ref, dk_ref, dv_ref, dq_s, dk_s, dv_s):
        p = pl.program_id(1)
        i, j = it_ref[p], jt_ref[p]

        @pl.when(p == 0)
        def _():
            dq_s[...] = jnp.zeros(dq_s.shape, F32)

        @pl.when(i == (j * tk) // tq)
        def _():
            dk_s[...] = jnp.zeros(dk_s.shape, F32)
            dv_s[...] = jnp.zeros(dv_s.shape, F32)

        def step(masked):
            q, k, do_b = q_ref[...], k_ref[...], do_ref[...]
            pt = jnp.exp2(_dot(k, q, NT) * _EXP2_SCALE - lse_ref[...])
            if masked:
                krow = lax.broadcasted_iota(jnp.int32, (tk, tq), 0) + j * tk
                qcol = lax.broadcasted_iota(jnp.int32, (tk, tq), 1) + i * tq
                pt = jnp.where(krow <= qcol, pt, 0.0)
            dv_s[...] += _dot(pt.astype(MD), do_b)
            dst = (pt * (_dot(v_ref[...], do_b, NT) - dl_ref[...])).astype(MD)
            dk_s[...] += _dot(dst, q)
            row0 = pl.multiple_of(i * tq, tq)
            dq_s[pl.ds(row0, tq), :] += _dot(dst, k, TN)

        needs_mask = (j + 1) * tk - 1 > i * tq

        @pl.when(needs_mask)
        def _():
            step(True)

        @pl.when(jnp.logical_not(needs_mask))
        def _():
            step(False)

        @pl.when(i == nq - 1)
        def _():
            dk_ref[...] = (dk_s[...] * SOFTMAX_SCALE).astype(dk_ref.dtype)
            dv_ref[...] = dv_s[...].astype(dv_ref.dtype)

        @pl.when(p == n_pairs - 1)
        def _():
            dq_ref[...] = (dq_s[...] * SOFTMAX_SCALE).astype(dq_ref.dtype)

    vec = pl.BlockSpec((None, 1, tq), lambda h, p, it, jt: (h, 0, it[p]))
    grid_spec = pltpu.PrefetchScalarGridSpec(
        num_scalar_prefetch=2, grid=(HEADS, n_pairs),
        in_specs=[pl.BlockSpec((tq, QK_PAD), lambda h, p, it, jt: (it[p], h)),
                  pl.BlockSpec((tk, QK_PAD), lambda h, p, it, jt: (jt[p], h)),
                  pl.BlockSpec((tk, V_DIM), lambda h, p, it, jt: (jt[p], h)),
                  pl.BlockSpec((tq, V_DIM), lambda h, p, it, jt: (it[p], h)), vec, vec],
        out_specs=[pl.BlockSpec((s, QK_PAD), lambda h, p, it, jt: (0, h)),
                   pl.BlockSpec((tk, QK_PAD), lambda h, p, it, jt: (jt[p], h)),
                   pl.BlockSpec((tk, V_DIM), lambda h, p, it, jt: (jt[p], h))],
        scratch_shapes=[pltpu.VMEM((s, QK_PAD), F32), pltpu.VMEM((tk, QK_PAD), F32), pltpu.VMEM((tk, V_DIM), F32)])
    wide = jax.ShapeDtypeStruct((s, HEADS * QK_PAD), MD)
    return pl.pallas_call(
        body, name=name, grid_spec=grid_spec,
        out_shape=[wide, wide, jax.ShapeDtypeStruct((s, HEADS * V_DIM), MD)],
        compiler_params=_params(("parallel", "arbitrary")),
    )(it, jt, qf, kf, v, do, lse2, delta)


def _halo_rows(dtype):
    return 8 if jnp.dtype(dtype).itemsize == 4 else 16


def _prev_spec(ts, tc, hr, col_off):
    return pl.BlockSpec((hr, tc), lambda j, i: (jnp.maximum(i * (ts // hr) - 1, 0), j + col_off))


def _conv_taps(x, halo, w, taps, first):
    hr, ts = halo.shape[0], x.shape[0]
    xe = jnp.concatenate([jnp.where(first, 0.0, halo), x], axis=0)
    shifted = []
    acc = None
    for k in range(taps):
        sh = taps - 1 - k
        xs = xe if sh == 0 else pltpu.roll(xe, sh, 0)
        xs = xs[hr:hr + ts, :]
        shifted.append(xs)
        term = xs * w[k:k + 1, :]
        acc = term if acc is None else acc + term
    return acc, shifted


def conv_bwd(x, dy, w8, taps, x_col_off, name, ts_pref=256):
    halves = dy.ndim == 3
    s = dy.shape[-2]
    c = dy.shape[-1] * (2 if halves else 1)
    tc = _tile(dy.shape[-1], 512)
    ts = _tile(s, ts_pref, 16)
    hx, hy = _halo_rows(x.dtype), _halo_rows(dy.dtype)
    ns = s // ts
    nh = dy.shape[-1] // tc
    xo = x_col_off // tc
    assert x_col_off % tc == 0

    def body(x_ref, xp_ref, dy_ref, dyn_ref, w_ref, dx_ref, dw_ref, db_ref):
        i = pl.program_id(1)
        first = i == 0
        xv = x_ref[...].astype(F32)
        dyv = dy_ref[...].astype(F32)
        w = w_ref[...]
        _, shifted = _conv_taps(xv, xp_ref[...].astype(F32), w, taps, first)
        dye = jnp.concatenate([dyv, jnp.where(i == ns - 1, 0.0, dyn_ref[...].astype(F32))], axis=0)
        dx = None
        for k in range(taps):
            sh = taps - 1 - k
            ds_ = dye if sh == 0 else pltpu.roll(dye, ts + hy - sh, 0)
            term = ds_[0:ts, :] * w[k:k + 1, :]
            dx = term if dx is None else dx + term
        dx_ref[...] = dx.astype(dx_ref.dtype)
        rows = [jnp.sum(dyv * shifted[k], axis=0, keepdims=True) for k in range(taps)]
        rows.append(jnp.zeros((8 - taps, tc), F32))
        _acc_out(dw_ref, jnp.concatenate(rows, axis=0), first)
        _acc_out(db_ref, jnp.sum(dyv, axis=0, keepdims=True), first)

    last = s // hy - 1
    nxt = lambda i: jnp.minimum((i + 1) * (ts // hy), last)
    if halves:
        dy_spec = pl.BlockSpec((None, ts, tc), lambda j, i: (j // nh, i, j % nh))
        dyn_spec = pl.BlockSpec((None, hy, tc), lambda j, i: (j // nh, nxt(i), j % nh))
    else:
        dy_spec = pl.BlockSpec((ts, tc), lambda j, i: (i, j))
        dyn_spec = pl.BlockSpec((hy, tc), lambda j, i: (nxt(i), j))
    return pl.pallas_call(
        body, name=name, grid=(c // tc, ns),
        in_specs=[pl.BlockSpec((ts, tc), lambda j, i: (i, j + xo)), _prev_spec(ts, tc, hx, xo), dy_spec, dyn_spec,
                  pl.BlockSpec((8, tc), lambda j, i: (0, j))],
        out_specs=[pl.BlockSpec((ts, tc), lambda j, i: (i, j)), pl.BlockSpec((8, tc), lambda j, i: (0, j)),
                   pl.BlockSpec((1, tc), lambda j, i: (0, j))],
        out_shape=[jax.ShapeDtypeStruct((s, c), MD), jax.ShapeDtypeStruct((8, c), F32), jax.ShapeDtypeStruct((1, c), F32)],
        compiler_params=_params(("parallel", "arbitrary")),
    )(x, x, dy, dy, w8)


def _lru_gates(xc, wr_ref, br, wi_ref, bi, sp, nblk):
    xb = xc.astype(MD)
    zr, zi = [], []
    for b in range(nblk):
        blk = xb[:, b * LANES:(b + 1) * LANES]
        zr.append(_dot(blk, wr_ref[b]))
        zi.append(_dot(blk, wi_ref[b]))
    r = _sigmoid(jnp.concatenate(zr, axis=1) + br)
    ig = _sigmoid(jnp.concatenate(zi, axis=1) + bi)
    log_a = -LRU_C * r * sp
    a = jnp.exp(log_a)
    mult = jnp.sqrt(-_expm1(2.0 * log_a))
    return r, ig, a, mult


def lru_fwd(proj, x_off, y_off, w8, b_conv, w_rg, b_rg, w_ig, b_ig, lam, name):
    s = proj.shape[0]
    c = lam.shape[1]
    tc, ts = _tile(c, 512), _tile(s, 256, 16)
    nblk = tc // LANES
    xo, yo = x_off // tc, y_off // tc
    hr = _halo_rows(proj.dtype)

    def body(x_ref, xp_ref, y_ref, w_ref, bc_ref, wr_ref, br_ref, wi_ref, bi_ref, lam_ref, h_ref, hg_ref, carry):
        i = pl.program_id(1)
        first = i == 0
        xc, _ = _conv_taps(x_ref[...].astype(F32), xp_ref[...].astype(F32), w_ref[...], LRU_TAPS, first)
        xc = xc + bc_ref[...]
        sp = _softplus(-lam_ref[...])
        _, ig, a, mult = _lru_gates(xc, wr_ref, br_ref[...], wi_ref, bi_ref[...], sp, nblk)
        b = mult * (ig * xc)
        row = lax.broadcasted_iota(jnp.int32, (ts, tc), 0)
        d = 1
        while d < ts:
            keep = row >= d
            a_s, b_s = pltpu.roll(a, d, 0), pltpu.roll(b, d, 0)
            b = jnp.where(keep, a * b_s + b, b)
            a = jnp.where(keep, a * a_s, a)
            d *= 2
        h0 = jnp.where(first, 0.0, carry[7:8, :])
        h = a * h0 + b
        carry[...] = h[ts - 8:ts, :]
        h_ref[...] = h
        hg_ref[...] = (h * _gelu(y_ref[...].astype(F32))).astype(hg_ref.dtype)

    tile = lambda off: pl.BlockSpec((ts, tc), lambda j, i: (i, j + off))
    vec = pl.BlockSpec((1, tc), lambda j, i: (0, j))
    wblk = pl.BlockSpec((nblk, LANES, LANES), lambda j, i: (j, 0, 0))
    out = pl.BlockSpec((ts, tc), lambda j, i: (i, j))
    return pl.pallas_call(
        body, name=name, grid=(c // tc, s // ts),
        in_specs=[tile(xo), _prev_spec(ts, tc, hr, xo), tile(yo), pl.BlockSpec((8, tc), lambda j, i: (0, j)), vec,
                  wblk, vec, wblk, vec, vec],
        out_specs=[out, out],
        out_shape=[jax.ShapeDtypeStruct((s, c), F32), jax.ShapeDtypeStruct((s, c), MD)],
        scratch_shapes=[pltpu.VMEM((8, tc), F32)],
        compiler_params=_params(("parallel", "arbitrary")),
    )(proj, proj, proj, w8, b_conv, w_rg, b_rg, w_ig, b_ig, lam)


def lru_bwd(proj, x_off, y_off, h, dhg, w8, b_conv, w_rg, b_rg, w_ig, b_ig, lam, name):
    s = proj.shape[0]
    c = lam.shape[1]
    tc, ts = _tile(c, 512), _tile(s, 256, 16)
    nblk = tc // LANES
    ns = s // ts
    xo, yo = x_off // tc, y_off // tc
    hr = _halo_rows(proj.dtype)
    rev = lambda i: ns - 1 - i

    def body(x_ref, xp_ref, y_ref, h_ref, hp_ref, dhg_ref, w_ref, bc_ref, wr_ref, br_ref, wi_ref, bi_ref, lam_ref,
             dxc_ref, dy_ref, dwr_ref, dwi_ref, dbr_ref, dbi_ref, dlam_ref, a_carry, g_carry):
        step = pl.program_id(1)
        i = rev(step)
        start = step == 0
        top = i == 0
        xc, _ = _conv_taps(x_ref[...].astype(F32), xp_ref[...].astype(F32), w_ref[...], LRU_TAPS, top)
        xc = xc + bc_ref[...]
        lam_v = lam_ref[...]
        sp = _softplus(-lam_v)
        r, ig, a, mult = _lru_gates(xc, wr_ref, br_ref[...], wi_ref, bi_ref[...], sp, nblk)
        hv = h_ref[...]
        gy, dgy = _gelu_and_grad(y_ref[...].astype(F32))
        dhg_v = dhg_ref[...]
        dy_ref[...] = (dhg_v * hv * dgy).astype(dy_ref.dtype)
        row = lax.broadcasted_iota(jnp.int32, (ts, tc), 0)
        a_next = jnp.where(row == ts - 1, jnp.where(start, 0.0, a_carry[0:1, :]), pltpu.roll(a, ts - 1, 0))
        ca, cb = a_next, dhg_v * gy
        d = 1
        while d < ts:
            keep = row < ts - d
            a_s, b_s = pltpu.roll(ca, ts - d, 0), pltpu.roll(cb, ts - d, 0)
            cb = jnp.where(keep, ca * b_s + cb, cb)
            ca = jnp.where(keep, ca * a_s, ca)
            d *= 2
        g = ca * jnp.where(start, 0.0, g_carry[0:1, :]) + cb
        a_carry[...] = a[0:8, :]
        g_carry[...] = g[0:8, :]
        h_prev = jnp.where(row == 0, jnp.where(top, 0.0, hp_ref[7:8, :]), pltpu.roll(hv, 1, 0))
        da = g * h_prev
        ixc = ig * xc
        d_ixc = g * mult
        dlog_a = da * a - g * ixc * (a * a) / mult
        dzr = dlog_a * (-LRU_C * sp) * r * (1.0 - r)
        dzi = d_ixc * xc * ig * (1.0 - ig)
        dxc = d_ixc * ig
        xb, dzr_b, dzi_b = xc.astype(MD), dzr.astype(MD), dzi.astype(MD)
        parts, dwr, dwi = [], [], []
        for b in range(nblk):
            sl = slice(b * LANES, (b + 1) * LANES)
            parts.append(_dot(dzr_b[:, sl], wr_ref[b], NT) + _dot(dzi_b[:, sl], wi_ref[b], NT))
            dwr.append(_dot(xb[:, sl], dzr_b[:, sl], TN)[None])
            dwi.append(_dot(xb[:, sl], dzi_b[:, sl], TN)[None])
        dxc_ref[...] = dxc + jnp.concatenate(parts, axis=1)
        _acc_out(dwr_ref, jnp.concatenate(dwr, axis=0), start)
        _acc_out(dwi_ref, jnp.concatenate(dwi, axis=0), start)
        _acc_out(dbr_ref, jnp.sum(dzr, axis=0, keepdims=True), start)
        _acc_out(dbi_ref, jnp.sum(dzi, axis=0, keepdims=True), start)
        dsp = jnp.sum(dlog_a * (-LRU_C) * r, axis=0, keepdims=True)
        _acc_out(dlam_ref, -dsp * _sigmoid(-lam_v), start)

    vec = pl.BlockSpec((1, tc), lambda j, i: (0, j))
    wblk = pl.BlockSpec((nblk, LANES, LANES), lambda j, i: (j, 0, 0))
    tile = lambda off: pl.BlockSpec((ts, tc), lambda j, i: (rev(i), j + off))
    x_prev = pl.BlockSpec((hr, tc), lambda j, i: (jnp.maximum(rev(i) * (ts // hr) - 1, 0), j + xo))
    h_prev = pl.BlockSpec((8, tc), lambda j, i: (jnp.maximum(rev(i) * (ts // 8) - 1, 0), j))
    return pl.pallas_call(
        body, name=name, grid=(c // tc, ns),
        in_specs=[tile(xo), x_prev, tile(yo), tile(0), h_prev, tile(0), pl.BlockSpec((8, tc), lambda j, i: (0, j)), vec,
                  wblk, vec, wblk, vec, vec],
        out_specs=[tile(0), tile(0), wblk, wblk, vec, vec, vec],
        out_shape=[jax.ShapeDtypeStruct((s, c), F32), jax.ShapeDtypeStruct((s, c), MD),
                   jax.ShapeDtypeStruct(w_rg.shape, F32), jax.ShapeDtypeStruct(w_ig.shape, F32),
                   jax.ShapeDtypeStruct((1, c), F32), jax.ShapeDtypeStruct((1, c), F32), jax.ShapeDtypeStruct((1, c), F32)],
        scratch_shapes=[pltpu.VMEM((8, tc), F32), pltpu.VMEM((8, tc), F32)],
        compiler_params=_params(("parallel", "arbitrary")),
    )(proj, proj, proj, h, h, dhg, w8, b_conv, w_rg, b_rg, w_ig, b_ig, lam)


def merge_fwd(proj, gm_off, gl_off, o_mla, o_lru, name):
    s, c = o_mla.shape
    tc, ts = _tile(c, 1024), _tile(s, 512, 16)

    def body(gm_ref, gl_ref, om_ref, ol_ref, out_ref):
        out_ref[...] = (_sigmoid(gm_ref[...].astype(F32)) * om_ref[...] + _sigmoid(gl_ref[...].astype(F32)) * ol_ref[...]).astype(out_ref.dtype)

    tile = lambda off: pl.BlockSpec((ts, tc), lambda i, j: (i, j + off // tc))
    return pl.pallas_call(
        body, name=name, grid=(s // ts, c // tc), in_specs=[tile(gm_off), tile(gl_off), tile(0), tile(0)], out_specs=tile(0),
        out_shape=jax.ShapeDtypeStruct((s, c), MD), compiler_params=_params(("parallel", "parallel")),
    )(proj, proj, o_mla, o_lru)


def merge_bwd(proj, gm_off, gl_off, o_mla, o_lru, dmerged, name):
    s, c = o_mla.shape
    tc, ts = _tile(c, 1024), _tile(s, 512, 16)

    def body(gm_ref, gl_ref, om_ref, ol_ref, dm_ref, dom_ref, dol_ref, dgm_ref, dgl_ref):
        sm, sl = _sigmoid(gm_ref[...].astype(F32)), _sigmoid(gl_ref[...].astype(F32))
        dm = dm_ref[...]
        dom_ref[...] = (dm * sm).astype(dom_ref.dtype)
        dol_ref[...] = (dm * sl).astype(dol_ref.dtype)
        dgm_ref[...] = (dm * om_ref[...] * sm * (1.0 - sm)).astype(dgm_ref.dtype)
        dgl_ref[...] = (dm * ol_ref[...] * sl * (1.0 - sl)).astype(dgl_ref.dtype)

    tile = lambda off: pl.BlockSpec((ts, tc), lambda i, j: (i, j + off // tc))
    out = jax.ShapeDtypeStruct((s, c), MD)
    return pl.pallas_call(
        body, name=name, grid=(s // ts, c // tc), in_specs=[tile(gm_off), tile(gl_off), tile(0), tile(0), tile(0)],
        out_specs=[tile(0)] * 4, out_shape=[out] * 4, compiler_params=_params(("parallel", "parallel")),
    )(proj, proj, o_mla, o_lru, dmerged)


def xattn_fwd(cq, ck, cv, name):
    s = cq.shape[0]
    m = ck.shape[0]
    ts = _tile(s, 512, 16)
    scale = X_DIM ** -0.5

    def body(q_ref, k_ref, v_ref, o_ref):
        for h in range(X_HEADS):
            sl = slice(h * X_DIM, (h + 1) * X_DIM)
            sc = _dot(q_ref[:, sl], k_ref[:, sl], NT) * scale
            e = jnp.exp(sc - jnp.max(sc, axis=1, keepdims=True))
            p = e / jnp.sum(e, axis=1, keepdims=True)
            o_ref[:, sl] = _dot(p.astype(MD), v_ref[:, sl]).astype(o_ref.dtype)

    w = X_HEADS * X_DIM
    return pl.pallas_call(
        body, name=name, grid=(s // ts,),
        in_specs=[pl.BlockSpec((ts, w), lambda i: (i, 0)), pl.BlockSpec((m, w), lambda i: (0, 0)), pl.BlockSpec((m, w), lambda i: (0, 0))],
        out_specs=pl.BlockSpec((ts, w), lambda i: (i, 0)), out_shape=jax.ShapeDtypeStruct((s, w), MD),
        compiler_params=_params(("parallel",)),
    )(cq, ck, cv)


def xattn_bwd(cq, ck, cv, dco, name):
    s = cq.shape[0]
    m = ck.shape[0]
    ts = _tile(s, 512, 16)
    scale = X_DIM ** -0.5

    def body(q_ref, k_ref, v_ref, do_ref, dq_ref, dk_ref, dv_ref):
        first = pl.program_id(0) == 0
        dks, dvs = [], []
        for h in range(X_HEADS):
            sl = slice(h * X_DIM, (h + 1) * X_DIM)
            q, k, v, do = q_ref[:, sl], k_ref[:, sl], v_ref[:, sl], do_ref[:, sl]
            sc = _dot(q, k, NT) * scale
            e = jnp.exp(sc - jnp.max(sc, axis=1, keepdims=True))
            p = e / jnp.sum(e, axis=1, keepdims=True)
            dvs.append(_dot(p.astype(MD), do, TN))
            dp = _dot(do, v, NT)
            ds = (p * (dp - jnp.sum(dp * p, axis=1, keepdims=True)) * scale).astype(MD)
            dq_ref[:, sl] = _dot(ds, k).astype(dq_ref.dtype)
            dks.append(_dot(ds, q, TN))
        _acc_out(dk_ref, jnp.concatenate(dks, axis=1), first)
        _acc_out(dv_ref, jnp.concatenate(dvs, axis=1), first)

    w = X_HEADS * X_DIM
    row = pl.BlockSpec((ts, w), lambda i: (i, 0))
    full = pl.BlockSpec((m, w), lambda i: (0, 0))
    return pl.pallas_call(
        body, name=name, grid=(s // ts,), in_specs=[row, full, full, row], out_specs=[row, full, full],
        out_shape=[jax.ShapeDtypeStruct((s, w), MD), jax.ShapeDtypeStruct((m, w), F32), jax.ShapeDtypeStruct((m, w), F32)],
        compiler_params=_params(("arbitrary",)),
    )(cq, ck, cv, dco)


def _ffn_specs(up_pre):
    s, c2 = up_pre.shape
    f = c2 // 2
    tc, ts = _tile(f, 512), _tile(s, 512, 16)
    nf = f // tc
    hr = _halo_rows(up_pre.dtype)
    tile = lambda off: pl.BlockSpec((ts, tc), lambda j, i: (i, j + off))
    w = lambda off: pl.BlockSpec((8, tc), lambda j, i: (0, j + off))
    vec = lambda off: pl.BlockSpec((1, tc), lambda j, i: (0, j + off))
    specs = [tile(0), _prev_spec(ts, tc, hr, 0), tile(nf), _prev_spec(ts, tc, hr, nf), w(0), w(nf), vec(0), vec(nf)]
    return s, f, tc, ts, nf, tile, specs


def _ffn_gate_val(g_ref, gp_ref, v_ref, vp_ref, wg_ref, wv_ref, bg_ref, bv_ref, first):
    gate, _ = _conv_taps(g_ref[...].astype(F32), gp_ref[...].astype(F32), wg_ref[...], FFN_TAPS, first)
    val, _ = _conv_taps(v_ref[...].astype(F32), vp_ref[...].astype(F32), wv_ref[...], FFN_TAPS, first)
    return gate + bg_ref[...], val + bv_ref[...]


def ffn_act_fwd(up_pre, w8, b, name):
    s, f, tc, ts, nf, tile, specs = _ffn_specs(up_pre)

    def body(*refs):
        o_ref = refs[8]
        gate, val = _ffn_gate_val(*refs[:8], pl.program_id(1) == 0)
        o_ref[...] = (_gelu(gate) * val).astype(o_ref.dtype)

    return pl.pallas_call(
        body, name=name, grid=(nf, s // ts), in_specs=specs, out_specs=tile(0),
        out_shape=jax.ShapeDtypeStruct((s, f), MD), compiler_params=_params(("parallel", "parallel")),
    )(up_pre, up_pre, up_pre, up_pre, w8, w8, b, b)


def ffn_act_bwd(up_pre, w8, b, dact, name):
    s, f, tc, ts, nf, tile, specs = _ffn_specs(up_pre)

    def body(*refs):
        da_ref, o_ref = refs[8], refs[9]
        gate, val = _ffn_gate_val(*refs[:8], pl.program_id(1) == 0)
        ge, dge = _gelu_and_grad(gate)
        da = da_ref[...].astype(F32)
        o_ref[0] = (da * val * dge).astype(o_ref.dtype)
        o_ref[1] = (da * ge).astype(o_ref.dtype)

    return pl.pallas_call(
        body, name=name, grid=(nf, s // ts), in_specs=specs + [tile(0)],
        out_specs=pl.BlockSpec((2, ts, tc), lambda j, i: (0, i, j)), out_shape=jax.ShapeDtypeStruct((2, s, f), MD),
        compiler_params=_params(("parallel", "parallel")),
    )(up_pre, up_pre, up_pre, up_pre, w8, w8, b, b, dact)


def _adam(w, g, m, v):
    m = ADAM_B1 * m + (1.0 - ADAM_B1) * g
    v = ADAM_B2 * v + (1.0 - ADAM_B2) * (g * g)
    m_hat = m / (1.0 - ADAM_B1 ** ADAM_STEP)
    v_hat = v / (1.0 - ADAM_B2 ** ADAM_STEP)
    delta = -ADAM_LR * (m_hat / (jnp.sqrt(v_hat) + ADAM_EPS) + ADAM_WD * w)
    return delta, m, v


def _rows_tile(r, c):
    return _tile(r, max(16, (1 << 18) // c // 16 * 16), 16)


def adamw_sharded(mine, peers, w, m, v, name):
    r, c = w.shape
    tr = _rows_tile(r, c)

    def body(g_ref, p_ref, w_ref, m_ref, v_ref, go_ref, d_ref, mo_ref, vo_ref):
        grad = g_ref[...].astype(F32)
        for k in range(N_DEV - 1):
            grad = grad + p_ref[k].astype(F32)
        go_ref[...] = grad
        d_ref[...], mo_ref[...], vo_ref[...] = _adam(w_ref[...], grad, m_ref[...], v_ref[...])

    row = pl.BlockSpec((tr, c), lambda i: (i, 0))
    out = jax.ShapeDtypeStruct((r, c), F32)
    return pl.pallas_call(
        body, name=name, grid=(r // tr,),
        in_specs=[row, pl.BlockSpec((N_DEV - 1, tr, c), lambda i: (0, i, 0)), row, row, row],
        out_specs=[row] * 4, out_shape=[out] * 4, compiler_params=_params(("parallel",)),
    )(mine, peers, w, m, v)


def adamw_plain(g, w, m, v, name):
    r, c = w.shape
    tr = _tile(r, 512, 8)

    def body(g_ref, w_ref, m_ref, v_ref, d_ref, mo_ref, vo_ref):
        d_ref[...], mo_ref[...], vo_ref[...] = _adam(w_ref[...], g_ref[...], m_ref[...], v_ref[...])

    row = pl.BlockSpec((tr, c), lambda i: (i, 0))
    out = jax.ShapeDtypeStruct((r, c), F32)
    return pl.pallas_call(body, name=name, grid=(r // tr,), in_specs=[row] * 4, out_specs=[row] * 3, out_shape=[out] * 3,
                          compiler_params=_params(("parallel",)))(g, w, m, v)


def sum_devices(parts, name):
    _, r, c = parts.shape
    tr = _tile(r, 256, 8)

    def body(p_ref, o_ref):
        acc = p_ref[0]
        for d in range(1, N_DEV):
            acc = acc + p_ref[d]
        o_ref[...] = acc

    return pl.pallas_call(
        body, name=name, grid=(r // tr,), in_specs=[pl.BlockSpec((N_DEV, tr, c), lambda i: (0, i, 0))],
        out_specs=pl.BlockSpec((tr, c), lambda i: (i, 0)), out_shape=jax.ShapeDtypeStruct((r, c), F32),
        compiler_params=_params(("parallel",)))(parts)


def _place():
    return lax.axis_index("x"), lax.axis_index("y"), lax.axis_index("c")


def _other_chips(x, y):
    return [(1 - x, y), (x, 1 - y), (1 - x, 1 - y)]


_ANY = pl.BlockSpec(memory_space=pl.ANY)


def all_gather(shards, name):
    n = len(shards)

    def body(*refs):
        x_refs, out_refs = refs[:n], refs[n:2 * n]
        send_sems, recv_sems, local_sems = refs[2 * n:]
        x, y, cc = _place()
        me, sibling = (x, y, cc), (x, y, 1 - cc)
        chips = _other_chips(x, y)

        def slot(w, px, py, pc):
            return out_refs[w].at[4 * px + 2 * py + pc]

        def copy(k, w, block, to, src=None):
            return pltpu.make_async_remote_copy(
                src_ref=slot(w, *block) if src is None else src, dst_ref=slot(w, *block),
                send_sem=send_sems.at[k, w], recv_sem=recv_sems.at[k, w], device_id=to, device_id_type=MESH)

        mine = [pltpu.make_async_copy(x_refs[w], slot(w, *me), local_sems.at[w]) for w in range(n)]
        for cp in mine:
            cp.start()
        first = [copy(0, w, me, sibling, src=x_refs[w]) for w in range(n)]
        first += [copy(1 + j, w, me, (*chip, cc), src=x_refs[w]) for j, chip in enumerate(chips) for w in range(n)]
        for cp in first:
            cp.start()
        passed = []
        for j, chip in enumerate(chips):
            for w in range(n):
                copy(1 + j, w, (*chip, cc), me).wait_recv()
                passed.append(copy(4 + j, w, (*chip, cc), sibling))
                passed[-1].start()
        for w in range(n):
            copy(0, w, sibling, me).wait_recv()
        for j, chip in enumerate(chips):
            for w in range(n):
                copy(4 + j, w, (*chip, 1 - cc), me).wait_recv()
        for cp in first + passed:
            cp.wait_send()
        for cp in mine:
            cp.wait()

    return pl.pallas_call(
        body, name=name, out_shape=[jax.ShapeDtypeStruct((N_DEV, *a.shape), a.dtype) for a in shards],
        in_specs=[_ANY] * n, out_specs=[_ANY] * n,
        scratch_shapes=[pltpu.SemaphoreType.DMA((7, n)), pltpu.SemaphoreType.DMA((7, n)), pltpu.SemaphoreType.DMA((n,))],
    )(*shards)


_HBM = pl.BlockSpec(memory_space=pltpu.HBM)
_SEM = pl.BlockSpec(memory_space=pltpu.SEMAPHORE)
_DATAFLOW = pltpu.SideEffectType.DATAFLOW_SIDE_EFFECTING
_PEER_FLIPS = ((0, 0, 1), (1, 0, 0), (1, 0, 1), (0, 1, 0), (0, 1, 1), (1, 1, 0), (1, 1, 1))
N_PEERS = len(_PEER_FLIPS)


def _peer_copies(kind, src_refs, land_refs, send_sems, recv_sems):
    x, y, c = _place()
    n = len(src_refs)
    copies = []
    for w in range(n):
        for k, (fx, fy, fc) in enumerate(_PEER_FLIPS):
            px, py, pc = (1 - x if fx else x), (1 - y if fy else y), (1 - c if fc else c)
            if kind == "scatter":
                src, dst = src_refs[w].at[4 * px + 2 * py + pc], land_refs[w].at[k]
            else:
                src, dst = src_refs[w], land_refs[w].at[4 * x + 2 * y + c]
            copies.append(pltpu.make_async_remote_copy(
                src_ref=src, dst_ref=dst, send_sem=send_sems.at[k * n + w], recv_sem=recv_sems.at[k * n + w],
                device_id=(px, py, pc), device_id_type=MESH))
    return copies


def send_start(kind, srcs, name):
    n = len(srcs)
    if kind == "scatter":
        lands = [lax.empty((N_PEERS, *a.shape[1:]), a.dtype) for a in srcs]
    else:
        lands = [lax.empty((N_DEV, *a.shape), a.dtype) for a in srcs]

    def body(*refs):
        src_refs, land_refs = refs[:n], refs[n:2 * n]
        send_sems, recv_sems = refs[2 * n], refs[2 * n + 1]
        token = refs[-1]
        for cp in _peer_copies(kind, src_refs, land_refs, send_sems, recv_sems):
            cp.start()
        token[...] = jnp.zeros_like(token)

    hbm = lambda a: pltpu.with_memory_space_constraint(a, pltpu.HBM)
    outs = pl.pallas_call(
        body, name=name,
        out_shape=(pltpu.SemaphoreType.DMA((N_PEERS * n,)), pltpu.SemaphoreType.DMA((N_PEERS * n,)),
                   *[pltpu.HBM(a.shape, a.dtype) for a in srcs + lands], jax.ShapeDtypeStruct((8, LANES), F32)),
        in_specs=[_HBM] * (2 * n),
        out_specs=(_SEM, _SEM, *[_HBM] * (2 * n), pl.BlockSpec(memory_space=pltpu.VMEM)),
        input_output_aliases={i: 2 + i for i in range(2 * n)},
        compiler_params=pltpu.CompilerParams(has_side_effects=_DATAFLOW),
    )(*[hbm(a) for a in srcs + lands])
    return (kind, outs[0], outs[1], list(outs[2:2 + n]), list(outs[2 + n:2 + 2 * n])), outs[-1]


def send_wait(handle, after, name):
    kind, send_sems, recv_sems, srcs, lands = handle
    n = len(srcs)

    def body(*refs):
        src_refs, land_refs = refs[:n], refs[n:2 * n]
        for cp in _peer_copies(kind, src_refs, land_refs, refs[2 * n], refs[2 * n + 1]):
            cp.wait_send()
            cp.wait_recv()

    outs = pl.pallas_call(
        body, name=name, out_shape=tuple(pltpu.HBM(a.shape, a.dtype) for a in srcs + lands),
        in_specs=[_HBM] * (2 * n) + [_SEM, _SEM, _ANY], out_specs=tuple([_HBM] * (2 * n)),
        input_output_aliases={i: i for i in range(2 * n)},
        compiler_params=pltpu.CompilerParams(has_side_effects=_DATAFLOW),
    )(*srcs, *lands, send_sems, recv_sems, after)
    return list(outs[:n]), list(outs[n:])


class _NoExchange:
    def kv_weights(self, after):
        return {}

    def other_weights(self, after):
        return {}

    def send_grads(self, G, names):
        return None


def _after(a, token):
    return a if token is None else a + token[0:1, 0:1].astype(a.dtype)


def _step_local(x, mem, ang2, target, W, hooks=None):
    hooks = hooks or _NoExchange()
    W = dict(W)
    d = x.shape[1]
    wq = HEADS * QK_ROPE
    o_lx, o_ly, o_gm, o_gl = d + wq, 2 * d + wq, 3 * d + wq, 4 * d + wq
    one = lambda a, b, mode, dt, name, **kw: mm([(a, b)], mode, dt, name, **kw)

    h1 = rms_fwd(x, W["g_pre_mix"], "h1")
    proj = one(h1, W["w_main"], "nn", MD, "proj")
    ckv = one(h1, W["w_ckv"], "nn", MD, "ckv")
    kr = one(h1, W["w_kr"], "nn", MD, "k_rope")
    ckv_n = rms_fwd(ckv, W["g_ckv"], "ckv_n")
    W.update(hooks.kv_weights(ckv_n))
    kn = one(ckv_n, W["w_uk"], "nn", MD, "k_nope")
    v = one(ckv_n, W["w_uv"], "nn", MD, "v")
    qf, kf = rope_fwd(proj, kr, kn, ang2, "rope")
    o, lse2 = flash_fwd(qf, kf, v, "attn")
    W.update(hooks.other_weights(o))
    o_mla = one(o, W["w_o_mla"], "nn", F32, "o_mla")
    lru_w = (W["w_conv8"], W["b_conv_lru"], W["w_rg"], W["b_rg"], W["w_ig"], W["b_ig"], W["lru_lambda"])
    hr, hg = lru_fwd(proj, o_lx, o_ly, *lru_w, "lru")
    o_lru = one(hg, W["w_o_lru"], "nn", F32, "o_lru")
    merged = merge_fwd(proj, o_gm, o_gl, o_mla, o_lru, "merge")
    y1 = one(merged, W["w_out"], "nn", F32, "y1")
    x1, h2 = post_pre_fwd(x, y1, W["g_post_mix"], W["g_pre_x"], "x1")
    mn = rms_fwd(mem, W["g_mem"], "mem_n")
    ck = one(mn, W["w_ck"], "nn", MD, "ck")
    cv = one(mn, W["w_cv"], "nn", MD, "cv")
    cq = one(h2, W["w_cq"], "nn", MD, "cq")
    co = xattn_fwd(cq, ck, cv, "xattn")
    y2 = one(co, W["w_co"], "nn", F32, "y2")
    x2, h3 = post_pre_fwd(x1, y2, W["g_post_x"], W["g_pre_ffn"], "x2")
    up_pre = one(h3, W["w_up"], "nn", MD, "up_pre")
    act = ffn_act_fwd(up_pre, W["w_fconv8"], W["b_fconv"], "act")
    y3 = one(act, W["w_down"], "nn", F32, "y3")
    G = {}
    loss, dx3, dy3, G["g_post_ffn"] = loss_head(x2, y3, target, W["g_post_ffn"], "loss")

    G["w_down"] = one(act, dy3, "tn", MD, "dw_down")
    sent = hooks.send_grads(G, ("w_down",))
    dact = one(dy3, W["w_down"], "nt", MD, "dact")
    dup = ffn_act_bwd(up_pre, W["w_fconv8"], _after(W["b_fconv"], sent), dact, "dup")
    dup_pre, dwf8, G["b_fconv"] = conv_bwd(up_pre, dup, W["w_fconv8"], FFN_TAPS, 0, "ffn_conv_bwd")
    G["w_fconv"] = dwf8[:FFN_TAPS]
    G["w_up"] = one(h3, dup_pre, "tn", MD, "dw_up")
    sent = hooks.send_grads(G, ("w_up",))
    dh3 = one(dup_pre, W["w_up"], "nt", F32, "dh3", tn=2048, tk=1024)
    dx2, dy2, G["g_pre_ffn"], G["g_post_x"] = pre_post_bwd(x2, y2, dx3, dh3, _after(W["g_pre_ffn"], sent), W["g_post_x"], "dx2")
    G["w_co"] = one(co, dy2, "tn", MD, "dw_co")
    dco = one(dy2, W["w_co"], "nt", MD, "dco")
    dcq, dck, dcv = xattn_bwd(cq, ck, cv, dco, "xattn_bwd")
    dck, dcv = dck.astype(MD), dcv.astype(MD)
    G["w_cq"] = one(h2, dcq, "tn", MD, "dw_cq")
    G["w_ck"] = one(mn, dck, "tn", MD, "dw_ck")
    G["w_cv"] = one(mn, dcv, "tn", MD, "dw_cv")
    sent = hooks.send_grads(G, ("w_co", "w_cq", "w_ck", "w_cv"))
    dmn = mm([(dck, W["w_ck"]), (dcv, W["w_cv"])], "nt", F32, "dmem_n")
    _, G["g_mem"] = rms_bwd(mem, W["g_mem"], dmn, "dg_mem")
    dh2 = one(dcq, W["w_cq"], "nt", F32, "dh2")
    dx1, dy1, G["g_pre_x"], G["g_post_mix"] = pre_post_bwd(x1, y1, dx2, dh2, _after(W["g_pre_x"], sent), W["g_post_mix"], "dx1")
    G["w_out"] = one(merged, dy1, "tn", MD, "dw_out")
    dmerged = one(dy1, W["w_out"], "nt", F32, "dmerged")
    do_mla, do_lru, dgm, dgl = merge_bwd(proj, o_gm, o_gl, o_mla, o_lru, dmerged, "merge_bwd")
    G["w_o_mla"] = one(o, do_mla, "tn", MD, "dw_o_mla")
    do = one(do_mla, W["w_o_mla"], "nt", MD, "do")
    G["w_o_lru"] = one(hg, do_lru, "tn", MD, "dw_o_lru")
    sent = hooks.send_grads(G, ("w_out", "w_o_mla", "w_o_lru"))
    dhg = one(do_lru, W["w_o_lru"], "nt", F32, "dhg")
    lru_w = lru_w[:-1] + (_after(lru_w[-1], sent),)
    dxc, dly, G["w_rg"], G["w_ig"], G["b_rg"], G["b_ig"], G["lru_lambda"] = lru_bwd(proj, o_lx, o_ly, hr, dhg, *lru_w, "lru_bwd")
    dlx, dwc8, G["b_conv_lru"] = conv_bwd(proj, dxc, W["w_conv8"], LRU_TAPS, o_lx, "lru_conv_bwd")
    G["w_conv_lru"] = dwc8[:LRU_TAPS]
    delta = attn_delta(o, do, "attn_delta")
    dqf, dkf, dv = flash_bwd(qf, kf, v, do, lse2, delta, "attn_bwd")
    dqn, dqr, dkn, dkr = rope_bwd(dqf, dkf, ang2, "rope_bwd")
    G["w_uk"] = one(ckv_n, dkn, "tn", MD, "dw_uk")
    G["w_uv"] = one(ckv_n, dv, "tn", MD, "dw_uv")
    dckv_n = mm([(dkn, W["w_uk"]), (dv, W["w_uv"])], "nt", F32, "dckv_n")
    dckv, G["g_ckv"] = rms_bwd(ckv, W["g_ckv"], dckv_n, "dckv")
    dproj = jnp.concatenate([dqn, dqr, dlx, dly, dgm, dgl], axis=1)
    parts = [("w_main", dproj), ("w_ckv", dckv), ("w_kr", dkr)]
    for n, dpart in parts:
        G[n] = one(h1, dpart, "tn", MD, "d" + n)
    sent = hooks.send_grads(G, ("w_ukv", "w_in"))
    parts[-1] = ("w_kr", _after(dkr, sent))
    dh1 = mm([(dpart, W[n]) for n, dpart in parts], "nt", F32, "dh1", tk=1024)
    grad_x, G["g_pre_mix"] = pre_bwd(x, dx1, dh1, W["g_pre_mix"], "grad_x")
    return loss, grad_x, G


FLAT_COLS = 1024
BIG = (("w_in", 1), ("w_ukv", 1), ("w_o_mla", 0), ("w_o_lru", 0), ("w_out", 0), ("w_cq", 0), ("w_ck", 0), ("w_cv", 0),
       ("w_co", 1), ("w_up", 1), ("w_down", 0))
SMALL_SHARDED = ("w_conv_lru", "w_fconv")
REPLICATED = ("g_pre_mix", "g_post_mix", "g_ckv", "b_conv_lru", "w_rg", "b_rg", "w_ig", "b_ig", "lru_lambda",
              "g_pre_x", "g_post_x", "g_mem", "g_pre_ffn", "g_post_ffn", "b_fconv")
WEIGHTS = ("g_pre_mix", "g_post_mix", "w_in", "g_ckv", "w_ukv", "w_o_mla", "w_conv_lru", "b_conv_lru", "w_rg", "b_rg",
           "w_ig", "b_ig", "lru_lambda", "w_o_lru", "w_out", "g_pre_x", "g_post_x", "g_mem", "w_cq", "w_ck", "w_cv",
           "w_co", "g_pre_ffn", "g_post_ffn", "w_up", "w_fconv", "b_fconv", "w_down")


def _round_up(n, k):
    return (n + k - 1) // k * k


def _pack_vec(parts, row_mult):
    flat = jnp.concatenate([p.reshape(-1) for p in parts])
    n = flat.shape[0]
    total = _round_up(n, row_mult * FLAT_COLS)
    return jnp.pad(flat, (0, total - n)).reshape(-1, FLAT_COLS)


def _unpack_vec(flat2d, shapes):
    flat = flat2d.reshape(-1)
    out, off = [], 0
    for shp in shapes:
        n = int(np.prod(shp))
        out.append(flat[off:off + n].reshape(shp))
        off += n
    return out


def _gathered_to_full(blocks, axis):
    n, r, c = blocks.shape
    return blocks.reshape(n * r, c) if axis == 0 else blocks.transpose(1, 0, 2).reshape(r, n * c)


def _full_to_shards(full, axis):
    r, c = full.shape
    if axis == 0:
        return full.reshape(N_DEV, r // N_DEV, c)
    return full.reshape(r, N_DEV, c // N_DEV).transpose(1, 0, 2)


def _split_w_in(w_in, d):
    hq = HEADS * (QK_NOPE + QK_ROPE)
    q = w_in[:, :hq].reshape(d, HEADS, QK_NOPE + QK_ROPE)
    q_nope, q_rope = q[:, :, :QK_NOPE].reshape(d, HEADS * QK_NOPE), q[:, :, QK_NOPE:].reshape(d, HEADS * QK_ROPE)
    out = {"w_ckv": w_in[:, hq:hq + KV_RANK]}
    off = hq + KV_RANK
    out["w_kr"] = jnp.pad(w_in[:, off:off + QK_ROPE], ((0, 0), (0, LANES - QK_ROPE)))
    out["w_main"] = jnp.concatenate([q_nope, q_rope, w_in[:, off + QK_ROPE:]], axis=1)
    return out


def _join_w_in(G, d):
    gm = G["w_main"]
    n_qn, n_qr = HEADS * QK_NOPE, HEADS * QK_ROPE
    q = jnp.concatenate([gm[:, :n_qn].reshape(d, HEADS, QK_NOPE), gm[:, n_qn:n_qn + n_qr].reshape(d, HEADS, QK_ROPE)], axis=2)
    return jnp.concatenate([q.reshape(d, -1), G["w_ckv"], G["w_kr"][:, :QK_ROPE], gm[:, n_qn + n_qr:]], axis=1)


def kernel(x, mem, positions, g_pre_mix, g_post_mix, w_in, g_ckv, w_ukv, w_o_mla, w_conv_lru, b_conv_lru, w_rg, b_rg, w_ig, b_ig, lru_lambda, w_o_lru, w_out, g_pre_x, g_post_x, g_mem, w_cq, w_ck, w_cv, w_co, g_pre_ffn, g_post_ffn, w_up, w_fconv, b_fconv, w_down, loss_target, m_g_pre_mix, m_g_post_mix, m_w_in, m_g_ckv, m_w_ukv, m_w_o_mla, m_w_conv_lru, m_b_conv_lru, m_w_rg, m_b_rg, m_w_ig, m_b_ig, m_lru_lambda, m_w_o_lru, m_w_out, m_g_pre_x, m_g_post_x, m_g_mem, m_w_cq, m_w_ck, m_w_cv, m_w_co, m_g_pre_ffn, m_g_post_ffn, m_w_up, m_w_fconv, m_b_fconv, m_w_down, v_g_pre_mix, v_g_post_mix, v_w_in, v_g_ckv, v_w_ukv, v_w_o_mla, v_w_conv_lru, v_b_conv_lru, v_w_rg, v_b_rg, v_w_ig, v_b_ig, v_lru_lambda, v_w_o_lru, v_w_out, v_g_pre_x, v_g_post_x, v_g_mem, v_w_cq, v_w_ck, v_w_cv, v_w_co, v_g_pre_ffn, v_g_post_ffn, v_w_up, v_w_fconv, v_b_fconv, v_w_down):
    given = dict(locals())
    P = {n: given[n][0] for n in WEIGHTS}
    M = {n: given["m_" + n][0] for n in WEIGHTS}
    V = {n: given["v_" + n][0] for n in WEIGHTS}
    xs, mems, tgt = x[0], mem[0], loss_target[0]
    s, d = xs.shape
    ax, ay, ac = _place()
    dev = 4 * ax + 2 * ay + ac

    axis_of = dict(BIG)
    axis_of.update({n: 1 for n in SMALL_SHARDED})
    wire = lambda n: P[n] if n in SMALL_SHARDED else P[n].astype(MD)
    gathered_in, = all_gather([wire("w_in")], "gather_w_in")
    kv_names = ("w_ukv",)
    later_names = ("w_o_mla", "w_conv_lru", "w_o_lru", "w_out", "w_ck", "w_cv", "w_cq", "w_co", "w_up", "w_fconv", "w_down")
    gathered_in, kv_srcs, later_srcs = lax.optimization_barrier(
        (gathered_in, [wire(n) for n in kv_names], [wire(n) for n in later_names]))
    kv_handle, started_kv = send_start("gather", kv_srcs, "start_gather_kv")
    later_handle, started_later = send_start("gather", later_srcs, "start_gather_rest")

    def landed(names, handle, after, name):
        srcs, lands = send_wait(handle, after, name)
        full = {}
        for n, src, land in zip(names, srcs, lands):
            full[n] = _gathered_to_full(lax.dynamic_update_slice(land, src[None], (dev, 0, 0)), axis_of[n])
        return full

    W = _split_w_in(_gathered_to_full(gathered_in, 1), d)
    W["w_rg"], W["w_ig"] = P["w_rg"].astype(MD), P["w_ig"].astype(MD)
    for n in ("g_pre_mix", "g_post_mix", "g_ckv", "b_conv_lru", "b_rg", "b_ig", "lru_lambda", "g_pre_x", "g_post_x", "g_mem",
              "g_pre_ffn", "g_post_ffn", "b_fconv"):
        W[n] = P[n].reshape(1, -1)
    W["g_pre_mix"] = _after(_after(W["g_pre_mix"], started_kv), started_later)

    inv_freq = ROPE_THETA ** (-jnp.arange(0, QK_ROPE, 2, dtype=F32) / QK_ROPE)
    ang = positions[0].astype(F32)[:, None] * inv_freq
    ang2 = jnp.tile(ang, (1, LANES // (QK_ROPE // 2)))

    pending = []

    class Exchange:
        def kv_weights(self, after):
            kv = landed(kv_names, kv_handle, after, "wait_gather_kv")["w_ukv"].reshape(KV_RANK, HEADS, QK_NOPE + V_DIM)
            return {"w_uk": kv[:, :, :QK_NOPE].reshape(KV_RANK, HEADS * QK_NOPE),
                    "w_uv": kv[:, :, QK_NOPE:].reshape(KV_RANK, HEADS * V_DIM)}

        def other_weights(self, after):
            full = landed(later_names, later_handle, after, "wait_gather_rest")
            full["w_conv8"] = jnp.pad(full.pop("w_conv_lru"), ((0, 8 - LRU_TAPS), (0, 0)))
            full["w_fconv8"] = jnp.pad(full.pop("w_fconv"), ((0, 8 - FFN_TAPS), (0, 0)))
            return full

        def send_grads(self, G, names):
            if "w_in" in names:
                G["w_in"] = _join_w_in(G, d)
                G["w_ukv"] = jnp.concatenate([G["w_uk"].reshape(KV_RANK, HEADS, QK_NOPE),
                                              G["w_uv"].reshape(KV_RANK, HEADS, V_DIM)], axis=2).reshape(KV_RANK, -1)
            handle, started = send_start("scatter", [_full_to_shards(G[n], axis_of[n]) for n in names], "start_grads_" + names[0])
            pending.append((names, handle))
            return started

    loss, grad_x, G = _step_local(xs, mems, ang2, tgt, W, Exchange())
    loss = lax.psum(loss[0, 0], ("x", "y", "c"))

    res = {k: {} for k in ("grad", "delta", "m", "v")}
    for names, handle in pending:
        slabs, peers = send_wait(handle, grad_x, "wait_grads_" + names[0])
        for n, slab, peer in zip(names, slabs, peers):
            mine = lax.dynamic_index_in_dim(slab, dev, axis=0, keepdims=False)
            outs = adamw_sharded(mine, peer, P[n], M[n], V[n], "adamw_" + n)
            for kind, o in zip(("grad", "delta", "m", "v"), outs):
                res[kind][n] = o

    small_names = REPLICATED + SMALL_SHARDED
    g_small_shapes = [G[n].shape for n in small_names]
    partials, = all_gather([_pack_vec([G[n] for n in small_names], 128)], "gather_small_grads")
    summed = dict(zip(small_names, _unpack_vec(sum_devices(partials, "sum_small_grads"), g_small_shapes)))
    for n in SMALL_SHARDED:
        t, c = summed[n].shape
        summed[n] = lax.dynamic_index_in_dim(summed[n].reshape(t, N_DEV, c // N_DEV), dev, axis=1, keepdims=False)
    sg = {n: summed[n].reshape(P[n].shape) for n in small_names}
    vec = lambda src: _pack_vec([src[n] for n in small_names], 128)
    small_out = adamw_plain(vec(sg), vec(P), vec(M), vec(V), "adamw_small")
    shapes = [P[n].shape for n in small_names]
    for kind, flat in zip(("delta", "m", "v"), small_out):
        res[kind].update(zip(small_names, _unpack_vec(flat, shapes)))
    res["grad"].update(sg)

    outs = [loss, grad_x[None]]
    for kind in ("grad", "delta", "m", "v"):
        outs += [res[kind][n][None] for n in WEIGHTS]
    return tuple(outs)
```

```python
import numpy as np
import jax
import jax.numpy as jnp
from jax import lax
from jax.experimental import pallas as pl
from jax.experimental.pallas import tpu as pltpu

MD = jnp.bfloat16
F32 = jnp.float32
EPS = 1e-6
N_DEV = 8
HEADS = 16
QK_NOPE = 128
QK_ROPE = 64
V_DIM = 128
QK_PAD = 256
KV_RANK = 512
SOFTMAX_SCALE = (QK_NOPE + QK_ROPE) ** -0.5
LOG2E = 1.4426950408889634
ROPE_THETA = 10000.0
LRU_BLOCKS = 16
LRU_C = 8.0
X_HEADS = 4
X_DIM = 128
LRU_TAPS = 4
FFN_TAPS = 3
ADAM_LR, ADAM_B1, ADAM_B2, ADAM_EPS, ADAM_WD, ADAM_STEP = 0.001, 0.9, 0.999, 1e-08, 0.01, 10
LANES = 128
VMEM_LIMIT = 52 * 1024 * 1024
MESH = pl.DeviceIdType.MESH

NN = (((1,), (0,)), ((), ()))
NT = (((1,), (1,)), ((), ()))
TN = (((0,), (0,)), ((), ()))


def _tile(n, pref, align=LANES):
    if n <= pref:
        return n
    d = (pref // align) * align
    while d > align and n % d:
        d -= align
    assert n % d == 0, (n, pref, align)
    return d


def _params(sem):
    return pltpu.CompilerParams(dimension_semantics=sem, vmem_limit_bytes=VMEM_LIMIT)


def _dot(a, b, dims=NN):
    return lax.dot_general(a, b, dims, preferred_element_type=F32)


def _sigmoid(x):
    return 1.0 / (1.0 + jnp.exp(-x))


_GELU_C = 0.7978845608028654
_GELU_A = 0.044715


def _gelu(x):
    return 0.5 * x * (1.0 + jnp.tanh(x * (_GELU_C + (_GELU_C * _GELU_A) * (x * x))))


def _gelu_and_grad(x):
    x2 = x * x
    t = jnp.tanh(x * (_GELU_C + (_GELU_C * _GELU_A) * x2))
    hx = 0.5 * x
    g = hx + hx * t
    dg = 0.5 + 0.5 * t + hx * (1.0 - t * t) * (_GELU_C + (3.0 * _GELU_C * _GELU_A) * x2)
    return g, dg


def _expm1(x):
    small = x * (1.0 + x * (0.5 + x * (1.0 / 6.0 + x * (1.0 / 24.0 + x * (1.0 / 120.0)))))
    return jnp.where(jnp.abs(x) < 0.05, small, jnp.exp(x) - 1.0)


def _softplus(x):
    return jnp.maximum(x, 0.0) + jnp.log(1.0 + jnp.exp(-jnp.abs(x)))


def _rms_hat(x):
    r = lax.rsqrt(jnp.mean(x * x, axis=-1, keepdims=True) + EPS)
    return x * r, r


def _rms_bwd(x, g, dn):
    xh, r = _rms_hat(x)
    dg = jnp.sum(dn * xh, axis=0, keepdims=True)
    dxh = dn * g
    dx = r * (dxh - xh * jnp.mean(dxh * xh, axis=-1, keepdims=True))
    return dx, dg


def _acc_out(ref, val, first):
    @pl.when(first)
    def _():
        ref[...] = val

    @pl.when(jnp.logical_not(first))
    def _():
        ref[...] += val


def _lane_sums_as_row(x):
    ones = jnp.ones((8, LANES), F32)
    return lax.dot_general(ones, x, NT, precision=lax.Precision.HIGHEST, preferred_element_type=F32)[0:1, :]


def mm(pairs, mode, out_dtype, name, tm=1024, tn=1024, tk=2048):
    dims = {"nn": NN, "nt": NT, "tn": TN}[mode]
    shapes = []
    for a, b in pairs:
        if mode == "nn":
            (m, k), (k2, n) = a.shape, b.shape
        elif mode == "nt":
            (m, k), (n, k2) = a.shape, b.shape
        else:
            (k, m), (k2, n) = a.shape, b.shape
        assert k == k2, (name, a.shape, b.shape)
        shapes.append((m, n, k))
    m, n = shapes[0][0], shapes[0][1]
    assert all(s[0] == m and s[1] == n for s in shapes)
    tm, tn = _tile(m, tm), _tile(n, tn)
    tks = [_tile(s[2], tk) for s in shapes]
    nks = [s[2] // t for s, t in zip(shapes, tks)]
    starts = [sum(nks[:p]) for p in range(len(pairs))]
    nk_total = sum(nks)

    in_specs, args = [], []
    for p, (a, b) in enumerate(pairs):
        def kk(k, p=p):
            return jnp.clip(k - starts[p], 0, nks[p] - 1)
        if mode == "tn":
            in_specs.append(pl.BlockSpec((tks[p], tm), lambda i, j, k, kk=kk: (kk(k), i)))
        else:
            in_specs.append(pl.BlockSpec((tm, tks[p]), lambda i, j, k, kk=kk: (i, kk(k))))
        if mode == "nt":
            in_specs.append(pl.BlockSpec((tn, tks[p]), lambda i, j, k, kk=kk: (j, kk(k))))
        else:
            in_specs.append(pl.BlockSpec((tks[p], tn), lambda i, j, k, kk=kk: (kk(k), j)))
        args += [a, b]

    def body(*refs):
        ins, o_ref, scratch = refs[:2 * len(pairs)], refs[2 * len(pairs)], refs[2 * len(pairs) + 1:]
        if nk_total == 1:
            o_ref[...] = _dot(ins[0][...], ins[1][...], dims).astype(o_ref.dtype)
            return
        acc_ref, = scratch
        k = pl.program_id(2)
        for p in range(len(pairs)):
            @pl.when(jnp.logical_and(k >= starts[p], k < starts[p] + nks[p]))
            def _(p=p):
                prod = _dot(ins[2 * p][...], ins[2 * p + 1][...], dims)

                @pl.when(k == 0)
                def _():
                    acc_ref[...] = prod

                @pl.when(k > 0)
                def _():
                    acc_ref[...] += prod

        @pl.when(k == nk_total - 1)
        def _():
            o_ref[...] = acc_ref[...].astype(o_ref.dtype)

    return pl.pallas_call(
        body, name=name, grid=(m // tm, n // tn, nk_total),
        in_specs=in_specs, out_specs=pl.BlockSpec((tm, tn), lambda i, j, k: (i, j)),
        out_shape=jax.ShapeDtypeStruct((m, n), out_dtype),
        scratch_shapes=[] if nk_total == 1 else [pltpu.VMEM((tm, tn), F32)],
        compiler_params=_params(("parallel", "parallel", "arbitrary")),
    )(*args)


def rms_fwd(x, g, name):
    s, d = x.shape
    ts = _tile(s, 512, 16)

    def body(x_ref, g_ref, o_ref):
        xh, _ = _rms_hat(x_ref[...].astype(F32))
        o_ref[...] = (xh * g_ref[...]).astype(o_ref.dtype)

    return pl.pallas_call(
        body, name=name, grid=(s // ts,),
        in_specs=[pl.BlockSpec((ts, d), lambda i: (i, 0)), pl.BlockSpec((1, d), lambda i: (0, 0))],
        out_specs=pl.BlockSpec((ts, d), lambda i: (i, 0)),
        out_shape=jax.ShapeDtypeStruct((s, d), MD),
        compiler_params=_params(("parallel",)),
    )(x, g)


def rms_bwd(x, g, dn, name):
    s, d = x.shape
    ts = _tile(s, 256, 16)

    def body(x_ref, g_ref, dn_ref, dx_ref, dg_ref):
        dx, dg = _rms_bwd(x_ref[...].astype(F32), g_ref[...], dn_ref[...].astype(F32))
        dx_ref[...] = dx.astype(dx_ref.dtype)
        _acc_out(dg_ref, dg, pl.program_id(0) == 0)

    row = pl.BlockSpec((ts, d), lambda i: (i, 0))
    vec = pl.BlockSpec((1, d), lambda i: (0, 0))
    return pl.pallas_call(
        body, name=name, grid=(s // ts,), in_specs=[row, vec, row], out_specs=[row, vec],
        out_shape=[jax.ShapeDtypeStruct((s, d), MD), jax.ShapeDtypeStruct((1, d), F32)],
        compiler_params=_params(("arbitrary",)),
    )(x, g, dn)


def post_pre_fwd(x, y, g_post, g_pre, name):
    s, d = x.shape
    ts = _tile(s, 256, 16)

    def body(x_ref, y_ref, gp_ref, gq_ref, xn_ref, h_ref):
        yh, _ = _rms_hat(y_ref[...])
        xn = x_ref[...] + yh * gp_ref[...]
        xn_ref[...] = xn
        xh, _ = _rms_hat(xn)
        h_ref[...] = (xh * gq_ref[...]).astype(h_ref.dtype)

    row = pl.BlockSpec((ts, d), lambda i: (i, 0))
    vec = pl.BlockSpec((1, d), lambda i: (0, 0))
    return pl.pallas_call(
        body, name=name, grid=(s // ts,), in_specs=[row, row, vec, vec], out_specs=[row, row],
        out_shape=[jax.ShapeDtypeStruct((s, d), F32), jax.ShapeDtypeStruct((s, d), MD)],
        compiler_params=_params(("parallel",)),
    )(x, y, g_post, g_pre)


def loss_head(x, y, target, g_post, name):
    s, d = x.shape
    ts = _tile(s, 256, 16)

    def body(x_ref, y_ref, t_ref, g_ref, loss_ref, dx_ref, dy_ref, dg_ref):
        first = pl.program_id(0) == 0
        y = y_ref[...]
        yh, _ = _rms_hat(y)
        diff = x_ref[...] + yh * g_ref[...] - t_ref[...]
        part = 0.5 * jnp.sum(jnp.sum(diff * diff, axis=1, keepdims=True) * (1.0 / d), axis=0, keepdims=True)
        _acc_out(loss_ref, part, first)
        dx = diff * (1.0 / d)
        dx_ref[...] = dx
        dy, dg = _rms_bwd(y, g_ref[...], dx)
        dy_ref[...] = dy.astype(dy_ref.dtype)
        _acc_out(dg_ref, dg, first)

    row = pl.BlockSpec((ts, d), lambda i: (i, 0))
    vec = pl.BlockSpec((1, d), lambda i: (0, 0))
    one = pl.BlockSpec((1, 1), lambda i: (0, 0))
    return pl.pallas_call(
        body, name=name, grid=(s // ts,), in_specs=[row, row, row, vec], out_specs=[one, row, row, vec],
        out_shape=[jax.ShapeDtypeStruct((1, 1), F32), jax.ShapeDtypeStruct((s, d), F32),
                   jax.ShapeDtypeStruct((s, d), MD), jax.ShapeDtypeStruct((1, d), F32)],
        compiler_params=_params(("arbitrary",)),
    )(x, y, target, g_post)


def pre_post_bwd(x, y, dx_res, dh, g_pre, g_post, name):
    s, d = x.shape
    ts = _tile(s, 256, 16)

    def body(x_ref, y_ref, dr_ref, dh_ref, gq_ref, gp_ref, dx_ref, dy_ref, dgq_ref, dgp_ref):
        first = pl.program_id(0) == 0
        dxa, dgq = _rms_bwd(x_ref[...], gq_ref[...], dh_ref[...])
        dx = dr_ref[...] + dxa
        dx_ref[...] = dx
        dy, dgp = _rms_bwd(y_ref[...], gp_ref[...], dx)
        dy_ref[...] = dy.astype(dy_ref.dtype)
        _acc_out(dgq_ref, dgq, first)
        _acc_out(dgp_ref, dgp, first)

    row = pl.BlockSpec((ts, d), lambda i: (i, 0))
    vec = pl.BlockSpec((1, d), lambda i: (0, 0))
    return pl.pallas_call(
        body, name=name, grid=(s // ts,), in_specs=[row, row, row, row, vec, vec], out_specs=[row, row, vec, vec],
        out_shape=[jax.ShapeDtypeStruct((s, d), F32), jax.ShapeDtypeStruct((s, d), MD),
                   jax.ShapeDtypeStruct((1, d), F32), jax.ShapeDtypeStruct((1, d), F32)],
        compiler_params=_params(("arbitrary",)),
    )(x, y, dx_res, dh, g_pre, g_post)


def pre_bwd(x, dx_res, dh, g_pre, name):
    s, d = x.shape
    ts = _tile(s, 256, 16)

    def body(x_ref, dr_ref, dh_ref, g_ref, dx_ref, dg_ref):
        dxa, dg = _rms_bwd(x_ref[...], g_ref[...], dh_ref[...])
        dx_ref[...] = dr_ref[...] + dxa
        _acc_out(dg_ref, dg, pl.program_id(0) == 0)

    row = pl.BlockSpec((ts, d), lambda i: (i, 0))
    vec = pl.BlockSpec((1, d), lambda i: (0, 0))
    return pl.pallas_call(
        body, name=name, grid=(s // ts,), in_specs=[row, row, row, vec], out_specs=[row, vec],
        out_shape=[jax.ShapeDtypeStruct((s, d), F32), jax.ShapeDtypeStruct((1, d), F32)],
        compiler_params=_params(("arbitrary",)),
    )(x, dx_res, dh, g_pre)


def _swap_halves(x):
    lane = lax.broadcasted_iota(jnp.int32, x.shape, 1)
    return jnp.where((lane % QK_ROPE) < QK_ROPE // 2, pltpu.roll(x, LANES - QK_ROPE // 2, 1),
                     pltpu.roll(x, QK_ROPE // 2, 1))


def _rope_tables(ang_ref, sign):
    lane = lax.broadcasted_iota(jnp.int32, ang_ref.shape, 1)
    sgn = jnp.where((lane % QK_ROPE) < QK_ROPE // 2, -sign, sign)
    ang = ang_ref[...]
    return jnp.cos(ang), jnp.sin(ang) * sgn


def rope_fwd(proj, kr, kn, ang2, name):
    s = proj.shape[0]
    ts = _tile(s, 256, 16)
    wn, wr = HEADS * QK_NOPE, HEADS * QK_ROPE

    def body(qn_ref, qr_ref, kr_ref, kn_ref, ang_ref, qo_ref, ko_ref):
        c, sn = _rope_tables(ang_ref, 1.0)
        lane = lax.broadcasted_iota(jnp.int32, (ts, LANES), 1)
        lo = lane < QK_ROPE
        x = kr_ref[...].astype(F32)
        k_rot = jnp.where(lo, x * c + _swap_halves(x) * sn, 0.0).astype(ko_ref.dtype)
        for j in range(HEADS // 2):
            x = qr_ref[:, j * LANES:(j + 1) * LANES].astype(F32)
            r = x * c + _swap_halves(x) * sn
            pair = (jnp.where(lo, r, 0.0), jnp.where(lo, pltpu.roll(r, QK_ROPE, 1), 0.0))
            for e in range(2):
                h = 2 * j + e
                qo_ref[:, h * QK_PAD:h * QK_PAD + LANES] = qn_ref[:, h * LANES:(h + 1) * LANES]
                qo_ref[:, h * QK_PAD + LANES:(h + 1) * QK_PAD] = pair[e].astype(qo_ref.dtype)
                ko_ref[:, h * QK_PAD:h * QK_PAD + LANES] = kn_ref[:, h * LANES:(h + 1) * LANES]
                ko_ref[:, h * QK_PAD + LANES:(h + 1) * QK_PAD] = k_rot

    row = lambda w, blk: pl.BlockSpec((ts, w), lambda i: (i, blk))
    return pl.pallas_call(
        body, name=name, grid=(s // ts,),
        in_specs=[row(wn, 0), row(wr, wn // wr), row(LANES, 0), row(wn, 0), row(LANES, 0)],
        out_specs=[row(HEADS * QK_PAD, 0), row(HEADS * QK_PAD, 0)],
        out_shape=[jax.ShapeDtypeStruct((s, HEADS * QK_PAD), MD), jax.ShapeDtypeStruct((s, HEADS * QK_PAD), MD)],
        compiler_params=_params(("parallel",)),
    )(proj, proj, kr, kn, ang2)


def rope_bwd(dqf, dkf, ang2, name):
    s = dqf.shape[0]
    ts = _tile(s, 256, 16)
    wn, wr = HEADS * QK_NOPE, HEADS * QK_ROPE

    def body(dq_ref, dk_ref, ang_ref, qn_ref, qr_ref, kn_ref, kr_ref):
        c, sn = _rope_tables(ang_ref, -1.0)
        lane = lax.broadcasted_iota(jnp.int32, (ts, LANES), 1)
        lo = lane < QK_ROPE
        rot = lambda dy: dy * c + _swap_halves(dy) * sn
        dk_rope = None
        for j in range(HEADS // 2):
            halves = []
            for e in range(2):
                h = 2 * j + e
                qn_ref[:, h * LANES:(h + 1) * LANES] = dq_ref[:, h * QK_PAD:h * QK_PAD + LANES]
                kn_ref[:, h * LANES:(h + 1) * LANES] = dk_ref[:, h * QK_PAD:h * QK_PAD + LANES]
                halves.append(jnp.where(lo, dq_ref[:, h * QK_PAD + LANES:(h + 1) * QK_PAD].astype(F32), 0.0))
                dk_h = dk_ref[:, h * QK_PAD + LANES:(h + 1) * QK_PAD].astype(F32)
                dk_rope = dk_h if dk_rope is None else dk_rope + dk_h
            qr_ref[:, j * LANES:(j + 1) * LANES] = rot(halves[0] + pltpu.roll(halves[1], QK_ROPE, 1)).astype(qr_ref.dtype)
        kr_ref[...] = jnp.where(lo, rot(jnp.where(lo, dk_rope, 0.0)), 0.0).astype(kr_ref.dtype)

    row = lambda w: pl.BlockSpec((ts, w), lambda i: (i, 0))
    return pl.pallas_call(
        body, name=name, grid=(s // ts,),
        in_specs=[row(HEADS * QK_PAD), row(HEADS * QK_PAD), row(LANES)],
        out_specs=[row(wn), row(wr), row(wn), row(LANES)],
        out_shape=[jax.ShapeDtypeStruct((s, wn), MD), jax.ShapeDtypeStruct((s, wr), MD),
                   jax.ShapeDtypeStruct((s, wn), MD), jax.ShapeDtypeStruct((s, LANES), MD)],
        compiler_params=_params(("parallel",)),
    )(dqf, dkf, ang2)


_EXP2_SCALE = SOFTMAX_SCALE * LOG2E


def flash_fwd(qf, kf, v, name):
    s = qf.shape[0]
    tq = _tile(s, 1024)
    tk = _tile(s, 1024)
    nq = s // tq
    last_j = lambda i: (i * tq + tq - 1) // tk
    pairs = [(i, j) for i in range(nq) for j in range(last_j(i) + 1)]
    it = jnp.asarray(np.array([p[0] for p in pairs], np.int32))
    jt = jnp.asarray(np.array([p[1] for p in pairs], np.int32))
    groups = tk // LANES

    def body(it_ref, jt_ref, q_ref, k_ref, v_ref, o_ref, lse_ref, m_s, l_s, acc_s):
        p = pl.program_id(1)
        i, j = it_ref[p], jt_ref[p]

        @pl.when(j == 0)
        def _():
            m_s[...] = jnp.full(m_s.shape, -jnp.inf, F32)
            l_s[...] = jnp.zeros(l_s.shape, F32)
            acc_s[...] = jnp.zeros(acc_s.shape, F32)

        def step(masked):
            sc = _dot(q_ref[...], k_ref[...], NT)
            if masked:
                row = lax.broadcasted_iota(jnp.int32, (tq, tk), 0) + i * tq
                col = lax.broadcasted_iota(jnp.int32, (tq, tk), 1) + j * tk
                sc = jnp.where(col <= row, sc, -jnp.inf)
            cols = [sc[:, g * LANES:(g + 1) * LANES] for g in range(groups)]
            mx = cols[0]
            for g in range(1, groups):
                mx = jnp.maximum(mx, cols[g])
            m_prev = m_s[...]
            m_new = jnp.maximum(m_prev, jnp.max(mx, axis=1, keepdims=True))
            alpha = jnp.exp2((m_prev - m_new) * _EXP2_SCALE)
            ps, psum = [], None
            for g in range(groups):
                pg = jnp.exp2((cols[g] - m_new) * _EXP2_SCALE)
                psum = pg if psum is None else psum + pg
                ps.append(pg.astype(MD))
            l_s[...] = alpha * l_s[...] + psum
            acc_s[...] = acc_s[...] * alpha + _dot(jnp.concatenate(ps, axis=1), v_ref[...])
            m_s[...] = m_new

        needs_mask = (j + 1) * tk - 1 > i * tq

        @pl.when(needs_mask)
        def _():
            step(True)

        @pl.when(jnp.logical_not(needs_mask))
        def _():
            step(False)

        @pl.when(j == (i * tq + tq - 1) // tk)
        def _():
            l = jnp.sum(l_s[...], axis=1, keepdims=True)
            o_ref[...] = (acc_s[...] / l).astype(o_ref.dtype)
            lse_rep = m_s[...] * _EXP2_SCALE + jnp.log(l) * LOG2E
            lse_ref[...] = _lane_sums_as_row(lse_rep * (1.0 / LANES))

    grid_spec = pltpu.PrefetchScalarGridSpec(
        num_scalar_prefetch=2, grid=(HEADS, len(pairs)),
        in_specs=[pl.BlockSpec((tq, QK_PAD), lambda h, p, it, jt: (it[p], h)),
                  pl.BlockSpec((tk, QK_PAD), lambda h, p, it, jt: (jt[p], h)),
                  pl.BlockSpec((tk, V_DIM), lambda h, p, it, jt: (jt[p], h))],
        out_specs=[pl.BlockSpec((tq, V_DIM), lambda h, p, it, jt: (it[p], h)),
                   pl.BlockSpec((None, 1, tq), lambda h, p, it, jt: (h, 0, it[p]))],
        scratch_shapes=[pltpu.VMEM((tq, LANES), F32), pltpu.VMEM((tq, LANES), F32), pltpu.VMEM((tq, V_DIM), F32)])
    return pl.pallas_call(
        body, name=name, grid_spec=grid_spec,
        out_shape=[jax.ShapeDtypeStruct((s, HEADS * V_DIM), MD), jax.ShapeDtypeStruct((HEADS, 1, s), F32)],
        compiler_params=_params(("parallel", "arbitrary")),
    )(it, jt, qf, kf, v)


def attn_delta(o, do, name):
    s = o.shape[0]
    t = _tile(s, 1024)

    def body(o_ref, do_ref, d_ref):
        d_ref[...] = _lane_sums_as_row(o_ref[...].astype(F32) * do_ref[...].astype(F32))

    blk = pl.BlockSpec((t, V_DIM), lambda h, i: (i, h))
    return pl.pallas_call(
        body, name=name, grid=(HEADS, s // t), in_specs=[blk, blk],
        out_specs=pl.BlockSpec((None, 1, t), lambda h, i: (h, 0, i)),
        out_shape=jax.ShapeDtypeStruct((HEADS, 1, s), F32),
        compiler_params=_params(("parallel", "parallel")),
    )(o, do)


def flash_bwd(qf, kf, v, do, lse2, delta, name):
    s = qf.shape[0]
    tq, tk = _tile(s, 1024), _tile(s, 1024)
    nq, nk = s // tq, s // tk
    first_i = lambda j: (j * tk) // tq
    pairs = [(i, j) for j in range(nk) for i in range(first_i(j), nq)]
    it = jnp.asarray(np.array([p[0] for p in pairs], np.int32))
    jt = jnp.asarray(np.array([p[1] for p in pairs], np.int32))
    n_pairs = len(pairs)

    def body(it_ref, jt_ref, q_ref, k_ref, v_ref, do_ref, lse_ref, dl_ref, dq_ref, dk_ref, dv_ref, dq_s, dk_s, dv_s):
        p = pl.program_id(1)
        i, j = it_ref[p], jt_ref[p]

        @pl.when(p == 0)
        def _():
            dq_s[...] = jnp.zeros(dq_s.shape, F32)

        @pl.when(i == (j * tk) // tq)
        def _():
            dk_s[...] = jnp.zeros(dk_s.shape, F32)
            dv_s[...] = jnp.zeros(dv_s.shape, F32)

        def step(masked):
            q, k, do_b = q_ref[...], k_ref[...], do_ref[...]
            pt = jnp.exp2(_dot(k, q, NT) * _EXP2_SCALE - lse_ref[...])
            if masked:
                krow = lax.broadcasted_iota(jnp.int32, (tk, tq), 0) + j * tk
                qcol = lax.broadcasted_iota(jnp.int32, (tk, tq), 1) + i * tq
                pt = jnp.where(krow <= qcol, pt, 0.0)
            dv_s[...] += _dot(pt.astype(MD), do_b)
            dst = (pt * (_dot(v_ref[...], do_b, NT) - dl_ref[...])).astype(MD)
            dk_s[...] += _dot(dst, q)
            row0 = pl.multiple_of(i * tq, tq)
            dq_s[pl.ds(row0, tq), :] += _dot(dst, k, TN)

        needs_mask = (j + 1) * tk - 1 > i * tq

        @pl.when(needs_mask)
        def _():
            step(True)

        @pl.when(jnp.logical_not(needs_mask))
        def _():
            step(False)

        @pl.when(i == nq - 1)
        def _():
            dk_ref[...] = (dk_s[...] * SOFTMAX_SCALE).astype(dk_ref.dtype)
            dv_ref[...] = dv_s[...].astype(dv_ref.dtype)

        @pl.when(p == n_pairs - 1)
        def _():
            dq_ref[...] = (dq_s[...] * SOFTMAX_SCALE).astype(dq_ref.dtype)

    vec = pl.BlockSpec((None, 1, tq), lambda h, p, it, jt: (h, 0, it[p]))
    grid_spec = pltpu.PrefetchScalarGridSpec(
        num_scalar_prefetch=2, grid=(HEADS, n_pairs),
        in_specs=[pl.BlockSpec((tq, QK_PAD), lambda h, p, it, jt: (it[p], h)),
                  pl.BlockSpec((tk, QK_PAD), lambda h, p, it, jt: (jt[p], h)),
                  pl.BlockSpec((tk, V_DIM), lambda h, p, it, jt: (jt[p], h)),
                  pl.BlockSpec((tq, V_DIM), lambda h, p, it, jt: (it[p], h)), vec, vec],
        out_specs=[pl.BlockSpec((s, QK_PAD), lambda h, p, it, jt: (0, h)),
                   pl.BlockSpec((tk, QK_PAD), lambda h, p, it, jt: (jt[p], h)),
                   pl.BlockSpec((tk, V_DIM), lambda h, p, it, jt: (jt[p], h))],
        scratch_shapes=[pltpu.VMEM((s, QK_PAD), F32), pltpu.VMEM((tk, QK_PAD), F32), pltpu.VMEM((tk, V_DIM), F32)])
    wide = jax.ShapeDtypeStruct((s, HEADS * QK_PAD), MD)
    return pl.pallas_call(
        body, name=name, grid_spec=grid_spec,
        out_shape=[wide, wide, jax.ShapeDtypeStruct((s, HEADS * V_DIM), MD)],
        compiler_params=_params(("parallel", "arbitrary")),
    )(it, jt, qf, kf, v, do, lse2, delta)


def _halo_rows(dtype):
    return 8 if jnp.dtype(dtype).itemsize == 4 else 16


def _prev_spec(ts, tc, hr, col_off):
    return pl.BlockSpec((hr, tc), lambda j, i: (jnp.maximum(i * (ts // hr) - 1, 0), j + col_off))


def _conv_taps(x, halo, w, taps, first):
    hr, ts = halo.shape[0], x.shape[0]
    xe = jnp.concatenate([jnp.where(first, 0.0, halo), x], axis=0)
    shifted = []
    acc = None
    for k in range(taps):
        sh = taps - 1 - k
        xs = xe if sh == 0 else pltpu.roll(xe, sh, 0)
        xs = xs[hr:hr + ts, :]
        shifted.append(xs)
        term = xs * w[k:k + 1, :]
        acc = term if acc is None else acc + term
    return acc, shifted


def conv_bwd(x, dy, w8, taps, x_col_off, name, ts_pref=1024):
    halves = dy.ndim == 3
    s = dy.shape[-2]
    c = dy.shape[-1] * (2 if halves else 1)
    tc = _tile(dy.shape[-1], 512)
    ts = _tile(s, ts_pref, 16)
    hx, hy = _halo_rows(x.dtype), _halo_rows(dy.dtype)
    ns = s // ts
    nh = dy.shape[-1] // tc
    xo = x_col_off // tc
    assert x_col_off % tc == 0

    def body(x_ref, xp_ref, dy_ref, dyn_ref, w_ref, dx_ref, dw_ref, db_ref):
        i = pl.program_id(1)
        first = i == 0
        xv = x_ref[...].astype(F32)
        dyv = dy_ref[...].astype(F32)
        w = w_ref[...]
        _, shifted = _conv_taps(xv, xp_ref[...].astype(F32), w, taps, first)
        dye = jnp.concatenate([dyv, jnp.where(i == ns - 1, 0.0, dyn_ref[...].astype(F32))], axis=0)
        dx = None
        for k in range(taps):
            sh = taps - 1 - k
            ds_ = dye if sh == 0 else pltpu.roll(dye, ts + hy - sh, 0)
            term = ds_[0:ts, :] * w[k:k + 1, :]
            dx = term if dx is None else dx + term
        dx_ref[...] = dx.astype(dx_ref.dtype)
        rows = [jnp.sum(dyv * shifted[k], axis=0, keepdims=True) for k in range(taps)]
        rows.append(jnp.zeros((8 - taps, tc), F32))
        _acc_out(dw_ref, jnp.concatenate(rows, axis=0), first)
        _acc_out(db_ref, jnp.sum(dyv, axis=0, keepdims=True), first)

    last = s // hy - 1
    nxt = lambda i: jnp.minimum((i + 1) * (ts // hy), last)
    if halves:
        dy_spec = pl.BlockSpec((None, ts, tc), lambda j, i: (j // nh, i, j % nh))
        dyn_spec = pl.BlockSpec((None, hy, tc), lambda j, i: (j // nh, nxt(i), j % nh))
    else:
        dy_spec = pl.BlockSpec((ts, tc), lambda j, i: (i, j))
        dyn_spec = pl.BlockSpec((hy, tc), lambda j, i: (nxt(i), j))
    return pl.pallas_call(
        body, name=name, grid=(c // tc, ns),
        in_specs=[pl.BlockSpec((ts, tc), lambda j, i: (i, j + xo)), _prev_spec(ts, tc, hx, xo), dy_spec, dyn_spec,
                  pl.BlockSpec((8, tc), lambda j, i: (0, j))],
        out_specs=[pl.BlockSpec((ts, tc), lambda j, i: (i, j)), pl.BlockSpec((8, tc), lambda j, i: (0, j)),
                   pl.BlockSpec((1, tc), lambda j, i: (0, j))],
        out_shape=[jax.ShapeDtypeStruct((s, c), MD), jax.ShapeDtypeStruct((8, c), F32), jax.ShapeDtypeStruct((1, c), F32)],
        compiler_params=_params(("parallel", "arbitrary")),
    )(x, x, dy, dy, w8)


def _lru_gates(xc, wr_ref, br, wi_ref, bi, sp, nblk):
    xb = xc.astype(MD)
    zr, zi = [], []
    for b in range(nblk):
        blk = xb[:, b * LANES:(b + 1) * LANES]
        zr.append(_dot(blk, wr_ref[b]))
        zi.append(_dot(blk, wi_ref[b]))
    r = _sigmoid(jnp.concatenate(zr, axis=1) + br)
    ig = _sigmoid(jnp.concatenate(zi, axis=1) + bi)
    log_a = -LRU_C * r * sp
    a = jnp.exp(log_a)
    mult = jnp.sqrt(-_expm1(2.0 * log_a))
    return r, ig, a, mult


def lru_fwd(proj, x_off, y_off, w8, b_conv, w_rg, b_rg, w_ig, b_ig, lam, name):
    s = proj.shape[0]
    c = lam.shape[1]
    tc, ts = _tile(c, 512), _tile(s, 256, 16)
    nblk = tc // LANES
    xo, yo = x_off // tc, y_off // tc
    hr = _halo_rows(proj.dtype)

    def body(x_ref, xp_ref, y_ref, w_ref, bc_ref, wr_ref, br_ref, wi_ref, bi_ref, lam_ref, h_ref, hg_ref, carry):
        i = pl.program_id(1)
        first = i == 0
        xc, _ = _conv_taps(x_ref[...].astype(F32), xp_ref[...].astype(F32), w_ref[...], LRU_TAPS, first)
        xc = xc + bc_ref[...]
        sp = _softplus(-lam_ref[...])
        _, ig, a, mult = _lru_gates(xc, wr_ref, br_ref[...], wi_ref, bi_ref[...], sp, nblk)
        b = mult * (ig * xc)
        row = lax.broadcasted_iota(jnp.int32, (ts, tc), 0)
        d = 1
        while d < ts:
            keep = row >= d
            a_s, b_s = pltpu.roll(a, d, 0), pltpu.roll(b, d, 0)
            b = jnp.where(keep, a * b_s + b, b)
            a = jnp.where(keep, a * a_s, a)
            d *= 2
        h0 = jnp.where(first, 0.0, carry[7:8, :])
        h = a * h0 + b
        carry[...] = h[ts - 8:ts, :]
        h_ref[...] = h
        hg_ref[...] = (h * _gelu(y_ref[...].astype(F32))).astype(hg_ref.dtype)

    tile = lambda off: pl.BlockSpec((ts, tc), lambda j, i: (i, j + off))
    vec = pl.BlockSpec((1, tc), lambda j, i: (0, j))
    wblk = pl.BlockSpec((nblk, LANES, LANES), lambda j, i: (j, 0, 0))
    out = pl.BlockSpec((ts, tc), lambda j, i: (i, j))
    return pl.pallas_call(
        body, name=name, grid=(c // tc, s // ts),
        in_specs=[tile(xo), _prev_spec(ts, tc, hr, xo), tile(yo), pl.BlockSpec((8, tc), lambda j, i: (0, j)), vec,
                  wblk, vec, wblk, vec, vec],
        out_specs=[out, out],
        out_shape=[jax.ShapeDtypeStruct((s, c), F32), jax.ShapeDtypeStruct((s, c), MD)],
        scratch_shapes=[pltpu.VMEM((8, tc), F32)],
        compiler_params=_params(("parallel", "arbitrary")),
    )(proj, proj, proj, w8, b_conv, w_rg, b_rg, w_ig, b_ig, lam)


def lru_bwd(proj, x_off, y_off, h, dhg, w8, b_conv, w_rg, b_rg, w_ig, b_ig, lam, name):
    s = proj.shape[0]
    c = lam.shape[1]
    tc, ts = _tile(c, 512), _tile(s, 256, 16)
    nblk = tc // LANES
    ns = s // ts
    xo, yo = x_off // tc, y_off // tc
    hr = _halo_rows(proj.dtype)
    rev = lambda i: ns - 1 - i

    def body(x_ref, xp_ref, y_ref, h_ref, hp_ref, dhg_ref, w_ref, bc_ref, wr_ref, br_ref, wi_ref, bi_ref, lam_ref,
             dxc_ref, dy_ref, dwr_ref, dwi_ref, dbr_ref, dbi_ref, dlam_ref, a_carry, g_carry):
        step = pl.program_id(1)
        i = rev(step)
        start = step == 0
        top = i == 0
        xc, _ = _conv_taps(x_ref[...].astype(F32), xp_ref[...].astype(F32), w_ref[...], LRU_TAPS, top)
        xc = xc + bc_ref[...]
        lam_v = lam_ref[...]
        sp = _softplus(-lam_v)
        r, ig, a, mult = _lru_gates(xc, wr_ref, br_ref[...], wi_ref, bi_ref[...], sp, nblk)
        hv = h_ref[...]
        gy, dgy = _gelu_and_grad(y_ref[...].astype(F32))
        dhg_v = dhg_ref[...]
        dy_ref[...] = (dhg_v * hv * dgy).astype(dy_ref.dtype)
        row = lax.broadcasted_iota(jnp.int32, (ts, tc), 0)
        a_next = jnp.where(row == ts - 1, jnp.where(start, 0.0, a_carry[0:1, :]), pltpu.roll(a, ts - 1, 0))
        ca, cb = a_next, dhg_v * gy
        d = 1
        while d < ts:
            keep = row < ts - d
            a_s, b_s = pltpu.roll(ca, ts - d, 0), pltpu.roll(cb, ts - d, 0)
            cb = jnp.where(keep, ca * b_s + cb, cb)
            ca = jnp.where(keep, ca * a_s, ca)
            d *= 2
        g = ca * jnp.where(start, 0.0, g_carry[0:1, :]) + cb
        a_carry[...] = a[0:8, :]
        g_carry[...] = g[0:8, :]
        h_prev = jnp.where(row == 0, jnp.where(top, 0.0, hp_ref[7:8, :]), pltpu.roll(hv, 1, 0))
        da = g * h_prev
        ixc = ig * xc
        d_ixc = g * mult
        dlog_a = da * a - g * ixc * (a * a) / mult
        dzr = dlog_a * (-LRU_C * sp) * r * (1.0 - r)
        dzi = d_ixc * xc * ig * (1.0 - ig)
        dxc = d_ixc * ig
        xb, dzr_b, dzi_b = xc.astype(MD), dzr.astype(MD), dzi.astype(MD)
        parts, dwr, dwi = [], [], []
        for b in range(nblk):
            sl = slice(b * LANES, (b + 1) * LANES)
            parts.append(_dot(dzr_b[:, sl], wr_ref[b], NT) + _dot(dzi_b[:, sl], wi_ref[b], NT))
            dwr.append(_dot(xb[:, sl], dzr_b[:, sl], TN)[None])
            dwi.append(_dot(xb[:, sl], dzi_b[:, sl], TN)[None])
        dxc_ref[...] = dxc + jnp.concatenate(parts, axis=1)
        _acc_out(dwr_ref, jnp.concatenate(dwr, axis=0), start)
        _acc_out(dwi_ref, jnp.concatenate(dwi, axis=0), start)
        _acc_out(dbr_ref, jnp.sum(dzr, axis=0, keepdims=True), start)
        _acc_out(dbi_ref, jnp.sum(dzi, axis=0, keepdims=True), start)
        dsp = jnp.sum(dlog_a * (-LRU_C) * r, axis=0, keepdims=True)
        _acc_out(dlam_ref, -dsp * _sigmoid(-lam_v), start)

    vec = pl.BlockSpec((1, tc), lambda j, i: (0, j))
    wblk = pl.BlockSpec((nblk, LANES, LANES), lambda j, i: (j, 0, 0))
    tile = lambda off: pl.BlockSpec((ts, tc), lambda j, i: (rev(i), j + off))
    x_prev = pl.BlockSpec((hr, tc), lambda j, i: (jnp.maximum(rev(i) * (ts // hr) - 1, 0), j + xo))
    h_prev = pl.BlockSpec((8, tc), lambda j, i: (jnp.maximum(rev(i) * (ts // 8) - 1, 0), j))
    return pl.pallas_call(
        body, name=name, grid=(c // tc, ns),
        in_specs=[tile(xo), x_prev, tile(yo), tile(0), h_prev, tile(0), pl.BlockSpec((8, tc), lambda j, i: (0, j)), vec,
                  wblk, vec, wblk, vec, vec],
        out_specs=[tile(0), tile(0), wblk, wblk, vec, vec, vec],
        out_shape=[jax.ShapeDtypeStruct((s, c), F32), jax.ShapeDtypeStruct((s, c), MD),
                   jax.ShapeDtypeStruct(w_rg.shape, F32), jax.ShapeDtypeStruct(w_ig.shape, F32),
                   jax.ShapeDtypeStruct((1, c), F32), jax.ShapeDtypeStruct((1, c), F32), jax.ShapeDtypeStruct((1, c), F32)],
        scratch_shapes=[pltpu.VMEM((8, tc), F32), pltpu.VMEM((8, tc), F32)],
        compiler_params=_params(("parallel", "arbitrary")),
    )(proj, proj, proj, h, h, dhg, w8, b_conv, w_rg, b_rg, w_ig, b_ig, lam)


def merge_fwd(proj, gm_off, gl_off, o_mla, o_lru, name):
    s, c = o_mla.shape
    tc, ts = _tile(c, 1024), _tile(s, 512, 16)

    def body(gm_ref, gl_ref, om_ref, ol_ref, out_ref):
        out_ref[...] = (_sigmoid(gm_ref[...].astype(F32)) * om_ref[...] + _sigmoid(gl_ref[...].astype(F32)) * ol_ref[...]).astype(out_ref.dtype)

    tile = lambda off: pl.BlockSpec((ts, tc), lambda i, j: (i, j + off // tc))
    return pl.pallas_call(
        body, name=name, grid=(s // ts, c // tc), in_specs=[tile(gm_off), tile(gl_off), tile(0), tile(0)], out_specs=tile(0),
        out_shape=jax.ShapeDtypeStruct((s, c), MD), compiler_params=_params(("parallel", "parallel")),
    )(proj, proj, o_mla, o_lru)


def merge_bwd(proj, gm_off, gl_off, o_mla, o_lru, dmerged, name):
    s, c = o_mla.shape
    tc, ts = _tile(c, 1024), _tile(s, 512, 16)

    def body(gm_ref, gl_ref, om_ref, ol_ref, dm_ref, dom_ref, dol_ref, dgm_ref, dgl_ref):
        sm, sl = _sigmoid(gm_ref[...].astype(F32)), _sigmoid(gl_ref[...].astype(F32))
        dm = dm_ref[...]
        dom_ref[...] = (dm * sm).astype(dom_ref.dtype)
        dol_ref[...] = (dm * sl).astype(dol_ref.dtype)
        dgm_ref[...] = (dm * om_ref[...] * sm * (1.0 - sm)).astype(dgm_ref.dtype)
        dgl_ref[...] = (dm * ol_ref[...] * sl * (1.0 - sl)).astype(dgl_ref.dtype)

    tile = lambda off: pl.BlockSpec((ts, tc), lambda i, j: (i, j + off // tc))
    out = jax.ShapeDtypeStruct((s, c), MD)
    return pl.pallas_call(
        body, name=name, grid=(s // ts, c // tc), in_specs=[tile(gm_off), tile(gl_off), tile(0), tile(0), tile(0)],
        out_specs=[tile(0)] * 4, out_shape=[out] * 4, compiler_params=_params(("parallel", "parallel")),
    )(proj, proj, o_mla, o_lru, dmerged)


def xattn_fwd(cq, ck, cv, name):
    s = cq.shape[0]
    m = ck.shape[0]
    ts = _tile(s, 512, 16)
    scale = X_DIM ** -0.5

    def body(q_ref, k_ref, v_ref, o_ref):
        for h in range(X_HEADS):
            sl = slice(h * X_DIM, (h + 1) * X_DIM)
            sc = _dot(q_ref[:, sl], k_ref[:, sl], NT) * scale
            e = jnp.exp(sc - jnp.max(sc, axis=1, keepdims=True))
            p = e / jnp.sum(e, axis=1, keepdims=True)
            o_ref[:, sl] = _dot(p.astype(MD), v_ref[:, sl]).astype(o_ref.dtype)

    w = X_HEADS * X_DIM
    return pl.pallas_call(
        body, name=name, grid=(s // ts,),
        in_specs=[pl.BlockSpec((ts, w), lambda i: (i, 0)), pl.BlockSpec((m, w), lambda i: (0, 0)), pl.BlockSpec((m, w), lambda i: (0, 0))],
        out_specs=pl.BlockSpec((ts, w), lambda i: (i, 0)), out_shape=jax.ShapeDtypeStruct((s, w), MD),
        compiler_params=_params(("parallel",)),
    )(cq, ck, cv)


def xattn_bwd(cq, ck, cv, dco, name):
    s = cq.shape[0]
    m = ck.shape[0]
    ts = _tile(s, 512, 16)
    scale = X_DIM ** -0.5

    def body(q_ref, k_ref, v_ref, do_ref, dq_ref, dk_ref, dv_ref):
        first = pl.program_id(0) == 0
        dks, dvs = [], []
        for h in range(X_HEADS):
            sl = slice(h * X_DIM, (h + 1) * X_DIM)
            q, k, v, do = q_ref[:, sl], k_ref[:, sl], v_ref[:, sl], do_ref[:, sl]
            sc = _dot(q, k, NT) * scale
            e = jnp.exp(sc - jnp.max(sc, axis=1, keepdims=True))
            p = e / jnp.sum(e, axis=1, keepdims=True)
            dvs.append(_dot(p.astype(MD), do, TN))
            dp = _dot(do, v, NT)
            ds = (p * (dp - jnp.sum(dp * p, axis=1, keepdims=True)) * scale).astype(MD)
            dq_ref[:, sl] = _dot(ds, k).astype(dq_ref.dtype)
            dks.append(_dot(ds, q, TN))
        _acc_out(dk_ref, jnp.concatenate(dks, axis=1), first)
        _acc_out(dv_ref, jnp.concatenate(dvs, axis=1), first)

    w = X_HEADS * X_DIM
    row = pl.BlockSpec((ts, w), lambda i: (i, 0))
    full = pl.BlockSpec((m, w), lambda i: (0, 0))
    return pl.pallas_call(
        body, name=name, grid=(s // ts,), in_specs=[row, full, full, row], out_specs=[row, full, full],
        out_shape=[jax.ShapeDtypeStruct((s, w), MD), jax.ShapeDtypeStruct((m, w), F32), jax.ShapeDtypeStruct((m, w), F32)],
        compiler_params=_params(("arbitrary",)),
    )(cq, ck, cv, dco)


def _ffn_specs(up_pre):
    s, c2 = up_pre.shape
    f = c2 // 2
    tc, ts = _tile(f, 512), _tile(s, 1024, 16)
    nf = f // tc
    hr = _halo_rows(up_pre.dtype)
    tile = lambda off: pl.BlockSpec((ts, tc), lambda j, i: (i, j + off))
    w = lambda off: pl.BlockSpec((8, tc), lambda j, i: (0, j + off))
    vec = lambda off: pl.BlockSpec((1, tc), lambda j, i: (0, j + off))
    specs = [tile(0), _prev_spec(ts, tc, hr, 0), tile(nf), _prev_spec(ts, tc, hr, nf), w(0), w(nf), vec(0), vec(nf)]
    return s, f, tc, ts, nf, tile, specs


def _ffn_gate_val(g_ref, gp_ref, v_ref, vp_ref, wg_ref, wv_ref, bg_ref, bv_ref, first):
    gate, _ = _conv_taps(g_ref[...].astype(F32), gp_ref[...].astype(F32), wg_ref[...], FFN_TAPS, first)
    val, _ = _conv_taps(v_ref[...].astype(F32), vp_ref[...].astype(F32), wv_ref[...], FFN_TAPS, first)
    return gate + bg_ref[...], val + bv_ref[...]


def ffn_act_fwd(up_pre, w8, b, name):
    s, f, tc, ts, nf, tile, specs = _ffn_specs(up_pre)

    def body(*refs):
        o_ref = refs[8]
        gate, val = _ffn_gate_val(*refs[:8], pl.program_id(1) == 0)
        o_ref[...] = (_gelu(gate) * val).astype(o_ref.dtype)

    return pl.pallas_call(
        body, name=name, grid=(nf, s // ts), in_specs=specs, out_specs=tile(0),
        out_shape=jax.ShapeDtypeStruct((s, f), MD), compiler_params=_params(("parallel", "parallel")),
    )(up_pre, up_pre, up_pre, up_pre, w8, w8, b, b)


def ffn_act_bwd(up_pre, w8, b, dact, name):
    s, f, tc, ts, nf, tile, specs = _ffn_specs(up_pre)

    def body(*refs):
        da_ref, o_ref = refs[8], refs[9]
        gate, val = _ffn_gate_val(*refs[:8], pl.program_id(1) == 0)
        ge, dge = _gelu_and_grad(gate)
        da = da_ref[...].astype(F32)
        o_ref[0] = (da * val * dge).astype(o_ref.dtype)
        o_ref[1] = (da * ge).astype(o_ref.dtype)

    return pl.pallas_call(
        body, name=name, grid=(nf, s // ts), in_specs=specs + [tile(0)],
        out_specs=pl.BlockSpec((2, ts, tc), lambda j, i: (0, i, j)), out_shape=jax.ShapeDtypeStruct((2, s, f), MD),
        compiler_params=_params(("parallel", "parallel")),
    )(up_pre, up_pre, up_pre, up_pre, w8, w8, b, b, dact)


def _adam(w, g, m, v):
    m = ADAM_B1 * m + (1.0 - ADAM_B1) * g
    v = ADAM_B2 * v + (1.0 - ADAM_B2) * (g * g)
    m_hat = m / (1.0 - ADAM_B1 ** ADAM_STEP)
    v_hat = v / (1.0 - ADAM_B2 ** ADAM_STEP)
    delta = -ADAM_LR * (m_hat / (jnp.sqrt(v_hat) + ADAM_EPS) + ADAM_WD * w)
    return delta, m, v


def _rows_tile(r, c):
    return _tile(r, max(16, (1 << 18) // c // 16 * 16), 16)


def adamw_sharded(mine, peers, w, m, v, name):
    r, c = w.shape
    tr = _rows_tile(r, c)

    def body(g_ref, p_ref, w_ref, m_ref, v_ref, go_ref, d_ref, mo_ref, vo_ref):
        grad = g_ref[...].astype(F32)
        for k in range(N_DEV - 1):
            grad = grad + p_ref[k].astype(F32)
        go_ref[...] = grad
        d_ref[...], mo_ref[...], vo_ref[...] = _adam(w_ref[...], grad, m_ref[...], v_ref[...])

    row = pl.BlockSpec((tr, c), lambda i: (i, 0))
    out = jax.ShapeDtypeStruct((r, c), F32)
    return pl.pallas_call(
        body, name=name, grid=(r // tr,),
        in_specs=[row, pl.BlockSpec((N_DEV - 1, tr, c), lambda i: (0, i, 0)), row, row, row],
        out_specs=[row] * 4, out_shape=[out] * 4, compiler_params=_params(("parallel",)),
    )(mine, peers, w, m, v)


def adamw_plain(g, w, m, v, name):
    r, c = w.shape
    tr = _tile(r, 512, 8)

    def body(g_ref, w_ref, m_ref, v_ref, d_ref, mo_ref, vo_ref):
        d_ref[...], mo_ref[...], vo_ref[...] = _adam(w_ref[...], g_ref[...], m_ref[...], v_ref[...])

    row = pl.BlockSpec((tr, c), lambda i: (i, 0))
    out = jax.ShapeDtypeStruct((r, c), F32)
    return pl.pallas_call(body, name=name, grid=(r // tr,), in_specs=[row] * 4, out_specs=[row] * 3, out_shape=[out] * 3,
                          compiler_params=_params(("parallel",)))(g, w, m, v)


def sum_devices(parts, name):
    _, r, c = parts.shape
    tr = _tile(r, 256, 8)

    def body(p_ref, o_ref):
        acc = p_ref[0]
        for d in range(1, N_DEV):
            acc = acc + p_ref[d]
        o_ref[...] = acc

    return pl.pallas_call(
        body, name=name, grid=(r // tr,), in_specs=[pl.BlockSpec((N_DEV, tr, c), lambda i: (0, i, 0))],
        out_specs=pl.BlockSpec((tr, c), lambda i: (i, 0)), out_shape=jax.ShapeDtypeStruct((r, c), F32),
        compiler_params=_params(("parallel",)))(parts)


def _place():
    return lax.axis_index("x"), lax.axis_index("y"), lax.axis_index("c")


def _other_chips(x, y):
    return [(1 - x, y), (x, 1 - y), (1 - x, 1 - y)]


_ANY = pl.BlockSpec(memory_space=pl.ANY)


def all_gather(shards, name):
    n = len(shards)

    def body(*refs):
        x_refs, out_refs = refs[:n], refs[n:2 * n]
        send_sems, recv_sems, local_sems = refs[2 * n:]
        x, y, cc = _place()
        me, sibling = (x, y, cc), (x, y, 1 - cc)
        chips = _other_chips(x, y)

        def slot(w, px, py, pc):
            return out_refs[w].at[4 * px + 2 * py + pc]

        def copy(k, w, block, to, src=None):
            return pltpu.make_async_remote_copy(
                src_ref=slot(w, *block) if src is None else src, dst_ref=slot(w, *block),
                send_sem=send_sems.at[k, w], recv_sem=recv_sems.at[k, w], device_id=to, device_id_type=MESH)

        mine = [pltpu.make_async_copy(x_refs[w], slot(w, *me), local_sems.at[w]) for w in range(n)]
        for cp in mine:
            cp.start()
        first = [copy(0, w, me, sibling, src=x_refs[w]) for w in range(n)]
        first += [copy(1 + j, w, me, (*chip, cc), src=x_refs[w]) for j, chip in enumerate(chips) for w in range(n)]
        for cp in first:
            cp.start()
        passed = []
        for j, chip in enumerate(chips):
            for w in range(n):
                copy(1 + j, w, (*chip, cc), me).wait_recv()
                passed.append(copy(4 + j, w, (*chip, cc), sibling))
                passed[-1].start()
        for w in range(n):
            copy(0, w, sibling, me).wait_recv()
        for j, chip in enumerate(chips):
            for w in range(n):
                copy(4 + j, w, (*chip, 1 - cc), me).wait_recv()
        for cp in first + passed:
            cp.wait_send()
        for cp in mine:
            cp.wait()

    return pl.pallas_call(
        body, name=name, out_shape=[jax.ShapeDtypeStruct((N_DEV, *a.shape), a.dtype) for a in shards],
        in_specs=[_ANY] * n, out_specs=[_ANY] * n,
        scratch_shapes=[pltpu.SemaphoreType.DMA((7, n)), pltpu.SemaphoreType.DMA((7, n)), pltpu.SemaphoreType.DMA((n,))],
    )(*shards)


_HBM = pl.BlockSpec(memory_space=pltpu.HBM)
_SEM = pl.BlockSpec(memory_space=pltpu.SEMAPHORE)
_DATAFLOW = pltpu.SideEffectType.DATAFLOW_SIDE_EFFECTING
_PEER_FLIPS = ((0, 0, 1), (1, 0, 0), (1, 0, 1), (0, 1, 0), (0, 1, 1), (1, 1, 0), (1, 1, 1))
N_PEERS = len(_PEER_FLIPS)


def _peer_copies(kind, src_refs, land_refs, send_sems, recv_sems):
    x, y, c = _place()
    n = len(src_refs)
    copies = []
    for w in range(n):
        for k, (fx, fy, fc) in enumerate(_PEER_FLIPS):
            px, py, pc = (1 - x if fx else x), (1 - y if fy else y), (1 - c if fc else c)
            if kind == "scatter":
                src, dst = src_refs[w].at[4 * px + 2 * py + pc], land_refs[w].at[k]
            else:
                src, dst = src_refs[w], land_refs[w].at[4 * x + 2 * y + c]
            copies.append(pltpu.make_async_remote_copy(
                src_ref=src, dst_ref=dst, send_sem=send_sems.at[k * n + w], recv_sem=recv_sems.at[k * n + w],
                device_id=(px, py, pc), device_id_type=MESH))
    return copies


def send_start(kind, srcs, name):
    n = len(srcs)
    if kind == "scatter":
        lands = [lax.empty((N_PEERS, *a.shape[1:]), a.dtype) for a in srcs]
    else:
        lands = [lax.empty((N_DEV, *a.shape), a.dtype) for a in srcs]

    def body(*refs):
        src_refs, land_refs = refs[:n], refs[n:2 * n]
        send_sems, recv_sems = refs[2 * n], refs[2 * n + 1]
        token = refs[-1]
        for cp in _peer_copies(kind, src_refs, land_refs, send_sems, recv_sems):
            cp.start()
        token[...] = jnp.zeros_like(token)

    hbm = lambda a: pltpu.with_memory_space_constraint(a, pltpu.HBM)
    outs = pl.pallas_call(
        body, name=name,
        out_shape=(pltpu.SemaphoreType.DMA((N_PEERS * n,)), pltpu.SemaphoreType.DMA((N_PEERS * n,)),
                   *[pltpu.HBM(a.shape, a.dtype) for a in srcs + lands], jax.ShapeDtypeStruct((8, LANES), F32)),
        in_specs=[_HBM] * (2 * n),
        out_specs=(_SEM, _SEM, *[_HBM] * (2 * n), pl.BlockSpec(memory_space=pltpu.VMEM)),
        input_output_aliases={i: 2 + i for i in range(2 * n)},
        compiler_params=pltpu.CompilerParams(has_side_effects=_DATAFLOW),
    )(*[hbm(a) for a in srcs + lands])
    return (kind, outs[0], outs[1], list(outs[2:2 + n]), list(outs[2 + n:2 + 2 * n])), outs[-1]


def send_wait(handle, after, name):
    kind, send_sems, recv_sems, srcs, lands = handle
    n = len(srcs)

    def body(*refs):
        src_refs, land_refs = refs[:n], refs[n:2 * n]
        for cp in _peer_copies(kind, src_refs, land_refs, refs[2 * n], refs[2 * n + 1]):
            cp.wait_send()
            cp.wait_recv()

    outs = pl.pallas_call(
        body, name=name, out_shape=tuple(pltpu.HBM(a.shape, a.dtype) for a in srcs + lands),
        in_specs=[_HBM] * (2 * n) + [_SEM, _SEM, _ANY], out_specs=tuple([_HBM] * (2 * n)),
        input_output_aliases={i: i for i in range(2 * n)},
        compiler_params=pltpu.CompilerParams(has_side_effects=_DATAFLOW),
    )(*srcs, *lands, send_sems, recv_sems, after)
    return list(outs[:n]), list(outs[n:])


class _NoExchange:
    def other_weights(self, after):
        return {}

    def send_grads(self, G, names):
        return None


def _after(a, token):
    return a if token is None else a + token[0:1, 0:1].astype(a.dtype)


def _step_local(x, mem, ang2, target, W, hooks=None):
    hooks = hooks or _NoExchange()
    W = dict(W)
    d = x.shape[1]
    wq = HEADS * QK_ROPE
    o_lx, o_ly, o_gm, o_gl = d + wq, 2 * d + wq, 3 * d + wq, 4 * d + wq
    one = lambda a, b, mode, dt, name, **kw: mm([(a, b)], mode, dt, name, **kw)

    h1 = rms_fwd(x, W["g_pre_mix"], "h1")
    proj = one(h1, W["w_main"], "nn", MD, "proj")
    ckv = one(h1, W["w_ckv"], "nn", MD, "ckv")
    kr = one(h1, W["w_kr"], "nn", MD, "k_rope")
    ckv_n = rms_fwd(ckv, W["g_ckv"], "ckv_n")
    kn =one(ckv_n, W["w_uk"], "nn", MD, "k_nope")
    v = one(ckv_n, W["w_uv"], "nn", MD, "v")
    qf, kf = rope_fwd(proj, kr, kn, ang2, "rope")
    o, lse2 = flash_fwd(qf, kf, v, "attn")
    W.update(hooks.other_weights(o))
    o_mla = one(o, W["w_o_mla"], "nn", F32, "o_mla")
    lru_w = (W["w_conv8"], W["b_conv_lru"], W["w_rg"], W["b_rg"], W["w_ig"], W["b_ig"], W["lru_lambda"])
    hr, hg = lru_fwd(proj, o_lx, o_ly, *lru_w, "lru")
    o_lru = one(hg, W["w_o_lru"], "nn", F32, "o_lru")
    merged = merge_fwd(proj, o_gm, o_gl, o_mla, o_lru, "merge")
    y1 = one(merged, W["w_out"], "nn", F32, "y1")
    x1, h2 = post_pre_fwd(x, y1, W["g_post_mix"], W["g_pre_x"], "x1")
    mn = rms_fwd(mem, W["g_mem"], "mem_n")
    ck = one(mn, W["w_ck"], "nn", MD, "ck")
    cv = one(mn, W["w_cv"], "nn", MD, "cv")
    cq = one(h2, W["w_cq"], "nn", MD, "cq")
    co = xattn_fwd(cq, ck, cv, "xattn")
    y2 = one(co, W["w_co"], "nn", F32, "y2")
    x2, h3 = post_pre_fwd(x1, y2, W["g_post_x"], W["g_pre_ffn"], "x2")
    up_pre = one(h3, W["w_up"], "nn", MD, "up_pre")
    act = ffn_act_fwd(up_pre, W["w_fconv8"], W["b_fconv"], "act")
    y3 = one(act, W["w_down"], "nn", F32, "y3")
    G = {}
    loss, dx3, dy3, G["g_post_ffn"] = loss_head(x2, y3, target, W["g_post_ffn"], "loss")

    G["w_down"] = one(act, dy3, "tn", MD, "dw_down")
    sent = hooks.send_grads(G, ("w_down",))
    dact = one(dy3, W["w_down"], "nt", MD, "dact")
    dup = ffn_act_bwd(up_pre, W["w_fconv8"], _after(W["b_fconv"], sent), dact, "dup")
    dup_pre, dwf8, G["b_fconv"] = conv_bwd(up_pre, dup, W["w_fconv8"], FFN_TAPS, 0, "ffn_conv_bwd")
    G["w_fconv"] = dwf8[:FFN_TAPS]
    G["w_up"] = one(h3, dup_pre, "tn", MD, "dw_up")
    sent = hooks.send_grads(G, ("w_up",))
    dh3 = one(dup_pre, W["w_up"], "nt", F32, "dh3", tn=2048, tk=1024)
    dx2, dy2, G["g_pre_ffn"], G["g_post_x"] = pre_post_bwd(x2, y2, dx3, dh3, _after(W["g_pre_ffn"], sent), W["g_post_x"], "dx2")
    G["w_co"] = one(co, dy2, "tn", MD, "dw_co")
    dco = one(dy2, W["w_co"], "nt", MD, "dco")
    dcq, dck, dcv = xattn_bwd(cq, ck, cv, dco, "xattn_bwd")
    dck, dcv = dck.astype(MD), dcv.astype(MD)
    G["w_cq"] = one(h2, dcq, "tn", MD, "dw_cq")
    G["w_ck"] = one(mn, dck, "tn", MD, "dw_ck")
    G["w_cv"] = one(mn, dcv, "tn", MD, "dw_cv")
    sent = hooks.send_grads(G, ("w_co", "w_cq", "w_ck", "w_cv"))
    dmn = mm([(dck, W["w_ck"]), (dcv, W["w_cv"])], "nt", F32, "dmem_n")
    _, G["g_mem"] = rms_bwd(mem, W["g_mem"], dmn, "dg_mem")
    dh2 = one(dcq, W["w_cq"], "nt", F32, "dh2")
    dx1, dy1, G["g_pre_x"], G["g_post_mix"] = pre_post_bwd(x1, y1, dx2, dh2, _after(W["g_pre_x"], sent), W["g_post_mix"], "dx1")
    G["w_out"] = one(merged, dy1, "tn", MD, "dw_out")
    dmerged = one(dy1, W["w_out"], "nt", F32, "dmerged")
    do_mla, do_lru, dgm, dgl = merge_bwd(proj, o_gm, o_gl, o_mla, o_lru, dmerged, "merge_bwd")
    G["w_o_mla"] = one(o, do_mla, "tn", MD, "dw_o_mla")
    do = one(do_mla, W["w_o_mla"], "nt", MD, "do")
    G["w_o_lru"] = one(hg, do_lru, "tn", MD, "dw_o_lru")
    sent = hooks.send_grads(G, ("w_out", "w_o_mla", "w_o_lru"))
    dhg = one(do_lru, W["w_o_lru"], "nt", F32, "dhg")
    lru_w = lru_w[:-1] + (_after(lru_w[-1], sent),)
    dxc, dly, G["w_rg"], G["w_ig"], G["b_rg"], G["b_ig"], G["lru_lambda"] = lru_bwd(proj, o_lx, o_ly, hr, dhg, *lru_w, "lru_bwd")
    dlx, dwc8, G["b_conv_lru"] = conv_bwd(proj, dxc, W["w_conv8"], LRU_TAPS, o_lx, "lru_conv_bwd")
    G["w_conv_lru"] = dwc8[:LRU_TAPS]
    delta = attn_delta(o, do, "attn_delta")
    dqf, dkf, dv = flash_bwd(qf, kf, v, do, lse2, delta, "attn_bwd")
    dqn, dqr, dkn, dkr = rope_bwd(dqf, dkf, ang2, "rope_bwd")
    G["w_uk"] = one(ckv_n, dkn, "tn", MD, "dw_uk")
    G["w_uv"] = one(ckv_n, dv, "tn", MD, "dw_uv")
    dckv_n = mm([(dkn, W["w_uk"]), (dv, W["w_uv"])], "nt", F32, "dckv_n")
    dckv, G["g_ckv"] = rms_bwd(ckv, W["g_ckv"], dckv_n, "dckv")
    dproj = jnp.concatenate([dqn, dqr, dlx, dly, dgm, dgl], axis=1)
    parts = [("w_main", dproj), ("w_ckv", dckv), ("w_kr", dkr)]
    for n, dpart in parts:
        G[n] = one(h1, dpart, "tn", MD, "d" + n)
    sent = hooks.send_grads(G, ("w_ukv", "w_in"))
    parts[-1] = ("w_kr", _after(dkr, sent))
    dh1 = mm([(dpart, W[n]) for n, dpart in parts], "nt", F32, "dh1", tk=1024)
    grad_x, G["g_pre_mix"] = pre_bwd(x, dx1, dh1, W["g_pre_mix"], "grad_x")
    return loss, grad_x, G


FLAT_COLS = 1024
BIG = (("w_in", 1), ("w_ukv", 1), ("w_o_mla", 0), ("w_o_lru", 0), ("w_out", 0), ("w_cq", 0), ("w_ck", 0), ("w_cv", 0),
       ("w_co", 1), ("w_up", 1), ("w_down", 0))
SMALL_SHARDED = ("w_conv_lru", "w_fconv")
REPLICATED = ("g_pre_mix", "g_post_mix", "g_ckv", "b_conv_lru", "w_rg", "b_rg", "w_ig", "b_ig", "lru_lambda",
              "g_pre_x", "g_post_x", "g_mem", "g_pre_ffn", "g_post_ffn", "b_fconv")
WEIGHTS = ("g_pre_mix", "g_post_mix", "w_in", "g_ckv", "w_ukv", "w_o_mla", "w_conv_lru", "b_conv_lru", "w_rg", "b_rg",
           "w_ig", "b_ig", "lru_lambda", "w_o_lru", "w_out", "g_pre_x", "g_post_x", "g_mem", "w_cq", "w_ck", "w_cv",
           "w_co", "g_pre_ffn", "g_post_ffn", "w_up", "w_fconv", "b_fconv", "w_down")


def _round_up(n, k):
    return (n + k - 1) // k * k


def _pack_vec(parts, row_mult):
    flat = jnp.concatenate([p.reshape(-1) for p in parts])
    n = flat.shape[0]
    total = _round_up(n, row_mult * FLAT_COLS)
    return jnp.pad(flat, (0, total - n)).reshape(-1, FLAT_COLS)


def _unpack_vec(flat2d, shapes):
    flat = flat2d.reshape(-1)
    out, off = [], 0
    for shp in shapes:
        n = int(np.prod(shp))
        out.append(flat[off:off + n].reshape(shp))
        off += n
    return out


def _gathered_to_full(blocks, axis):
    n, r, c = blocks.shape
    return blocks.reshape(n * r, c) if axis == 0 else blocks.transpose(1, 0, 2).reshape(r, n * c)


def _full_to_shards(full, axis):
    r, c = full.shape
    if axis == 0:
        return full.reshape(N_DEV, r // N_DEV, c)
    return full.reshape(r, N_DEV, c // N_DEV).transpose(1, 0, 2)


def _split_w_in(w_in, d):
    hq = HEADS * (QK_NOPE + QK_ROPE)
    q = w_in[:, :hq].reshape(d, HEADS, QK_NOPE + QK_ROPE)
    q_nope, q_rope = q[:, :, :QK_NOPE].reshape(d, HEADS * QK_NOPE), q[:, :, QK_NOPE:].reshape(d, HEADS * QK_ROPE)
    out = {"w_ckv": w_in[:, hq:hq + KV_RANK]}
    off = hq + KV_RANK
    out["w_kr"] = jnp.pad(w_in[:, off:off + QK_ROPE], ((0, 0), (0, LANES - QK_ROPE)))
    out["w_main"] = jnp.concatenate([q_nope, q_rope, w_in[:, off + QK_ROPE:]], axis=1)
    return out


def _join_w_in(G, d):
    gm = G["w_main"]
    n_qn, n_qr = HEADS * QK_NOPE, HEADS * QK_ROPE
    q = jnp.concatenate([gm[:, :n_qn].reshape(d, HEADS, QK_NOPE), gm[:, n_qn:n_qn + n_qr].reshape(d, HEADS, QK_ROPE)], axis=2)
    return jnp.concatenate([q.reshape(d, -1), G["w_ckv"], G["w_kr"][:, :QK_ROPE], gm[:, n_qn + n_qr:]], axis=1)


def kernel(x, mem, positions, g_pre_mix, g_post_mix, w_in, g_ckv, w_ukv, w_o_mla, w_conv_lru, b_conv_lru, w_rg, b_rg, w_ig, b_ig, lru_lambda, w_o_lru, w_out, g_pre_x, g_post_x, g_mem, w_cq, w_ck, w_cv, w_co, g_pre_ffn, g_post_ffn, w_up, w_fconv, b_fconv, w_down, loss_target, m_g_pre_mix, m_g_post_mix, m_w_in, m_g_ckv, m_w_ukv, m_w_o_mla, m_w_conv_lru, m_b_conv_lru, m_w_rg, m_b_rg, m_w_ig, m_b_ig, m_lru_lambda, m_w_o_lru, m_w_out, m_g_pre_x, m_g_post_x, m_g_mem, m_w_cq, m_w_ck, m_w_cv, m_w_co, m_g_pre_ffn, m_g_post_ffn, m_w_up, m_w_fconv, m_b_fconv, m_w_down, v_g_pre_mix, v_g_post_mix, v_w_in, v_g_ckv, v_w_ukv, v_w_o_mla, v_w_conv_lru, v_b_conv_lru, v_w_rg, v_b_rg, v_w_ig, v_b_ig, v_lru_lambda, v_w_o_lru, v_w_out, v_g_pre_x, v_g_post_x, v_g_mem, v_w_cq, v_w_ck, v_w_cv, v_w_co, v_g_pre_ffn, v_g_post_ffn, v_w_up, v_w_fconv, v_b_fconv, v_w_down):
    given = dict(locals())
    P = {n: given[n][0] for n in WEIGHTS}
    M = {n: given["m_" + n][0] for n in WEIGHTS}
    V = {n: given["v_" + n][0] for n in WEIGHTS}
    xs, mems, tgt = x[0], mem[0], loss_target[0]
    s, d = xs.shape
    ax, ay, ac = _place()
    dev = 4 * ax + 2 * ay + ac

    axis_of = dict(BIG)
    axis_of.update({n: 1 for n in SMALL_SHARDED})
    wire = lambda n: P[n] if n in SMALL_SHARDED else P[n].astype(MD)
    gathered_in, gathered_kv = all_gather([wire("w_in"), wire("w_ukv")], "gather_first")
    later_names = ("w_o_mla", "w_conv_lru", "w_o_lru", "w_out", "w_ck", "w_cv", "w_cq", "w_co", "w_up", "w_fconv", "w_down")
    gathered_in, later_srcs = lax.optimization_barrier((gathered_in, [wire(n) for n in later_names]))
    later_handle, started_later = send_start("gather", later_srcs, "start_gather_rest")

    def landed(names, handle, after, name):
        srcs, lands = send_wait(handle, after, name)
        full = {}
        for n, src, land in zip(names, srcs, lands):
            full[n] = _gathered_to_full(lax.dynamic_update_slice(land, src[None], (dev, 0, 0)), axis_of[n])
        return full

    W = _split_w_in(_gathered_to_full(gathered_in, 1), d)
    W["w_rg"], W["w_ig"] = P["w_rg"].astype(MD), P["w_ig"].astype(MD)
    for n in ("g_pre_mix", "g_post_mix", "g_ckv", "b_conv_lru", "b_rg", "b_ig", "lru_lambda", "g_pre_x", "g_post_x", "g_mem",
              "g_pre_ffn", "g_post_ffn", "b_fconv"):
        W[n] = P[n].reshape(1, -1)
    kv = _gathered_to_full(gathered_kv, 1).reshape(KV_RANK, HEADS, QK_NOPE + V_DIM)
    W["w_uk"] = kv[:, :, :QK_NOPE].reshape(KV_RANK, HEADS * QK_NOPE)
    W["w_uv"] = kv[:, :, QK_NOPE:].reshape(KV_RANK, HEADS * V_DIM)
    W["g_pre_mix"] = _after(W["g_pre_mix"], started_later)

    inv_freq = ROPE_THETA ** (-jnp.arange(0, QK_ROPE, 2, dtype=F32) / QK_ROPE)
    ang = positions[0].astype(F32)[:, None] * inv_freq
    ang2 = jnp.tile(ang, (1, LANES // (QK_ROPE // 2)))

    pending = []

    class Exchange:
        def other_weights(self, after):
            full = landed(later_names, later_handle, after, "wait_gather_rest")
            full["w_conv8"] = jnp.pad(full.pop("w_conv_lru"), ((0, 8 - LRU_TAPS), (0, 0)))
            full["w_fconv8"] = jnp.pad(full.pop("w_fconv"), ((0, 8 - FFN_TAPS), (0, 0)))
            return full

        def send_grads(self, G, names):
            if "w_in" in names:
                G["w_in"] = _join_w_in(G, d)
                G["w_ukv"] = jnp.concatenate([G["w_uk"].reshape(KV_RANK, HEADS, QK_NOPE),
                                              G["w_uv"].reshape(KV_RANK, HEADS, V_DIM)], axis=2).reshape(KV_RANK, -1)
            handle, started = send_start("scatter", [_full_to_shards(G[n], axis_of[n]) for n in names], "start_grads_" + names[0])
            pending.append((names, handle))
            return started

    loss, grad_x, G = _step_local(xs, mems, ang2, tgt, W, Exchange())
    loss = lax.psum(loss[0, 0], ("x", "y", "c"))

    res = {k: {} for k in ("grad", "delta", "m", "v")}
    for names, handle in pending:
        slabs, peers = send_wait(handle, grad_x, "wait_grads_" + names[0])
        for n, slab, peer in zip(names, slabs, peers):
            mine = lax.dynamic_index_in_dim(slab, dev, axis=0, keepdims=False)
            outs = adamw_sharded(mine, peer, P[n], M[n], V[n], "adamw_" + n)
            for kind, o in zip(("grad", "delta", "m", "v"), outs):
                res[kind][n] = o

    small_names = REPLICATED + SMALL_SHARDED
    g_small_shapes = [G[n].shape for n in small_names]
    partials, = all_gather([_pack_vec([G[n] for n in small_names], 128)], "gather_small_grads")
    summed = dict(zip(small_names, _unpack_vec(sum_devices(partials, "sum_small_grads"), g_small_shapes)))
    for n in SMALL_SHARDED:
        t, c = summed[n].shape
        summed[n] = lax.dynamic_index_in_dim(summed[n].reshape(t, N_DEV, c // N_DEV), dev, axis=1, keepdims=False)
    sg = {n: summed[n].reshape(P[n].shape) for n in small_names}
    vec = lambda src: _pack_vec([src[n] for n in small_names], 128)
    small_out = adamw_plain(vec(sg), vec(P), vec(M), vec(V), "adamw_small")
    shapes = [P[n].shape for n in small_names]
    for kind, flat in zip(("delta", "m", "v"), small_out):
        res[kind].update(zip(small_names, _unpack_vec(flat, shapes)))
    res["grad"].update(sg)

    outs = [loss, grad_x[None]]
    for kind in ("grad", "delta", "m", "v"):
        outs += [res[kind][n][None] for n in WEIGHTS]
    return tuple(outs)
```

```python
import numpy as np
import jax
import jax.numpy as jnp
from jax import lax
from jax.experimental import pallas as pl
from jax.experimental.pallas import tpu as pltpu

MD = jnp.bfloat16
F32 = jnp.float32
EPS = 1e-6
N_DEV = 8
HEADS = 16
QK_NOPE = 128
QK_ROPE = 64
V_DIM = 128
QK_PAD = 256
KV_RANK = 512
SOFTMAX_SCALE = (QK_NOPE + QK_ROPE) ** -0.5
LOG2E = 1.4426950408889634
ROPE_THETA = 10000.0
LRU_BLOCKS = 16
LRU_C = 8.0
X_HEADS = 4
X_DIM = 128
LRU_TAPS = 4
FFN_TAPS = 3
ADAM_LR, ADAM_B1, ADAM_B2, ADAM_EPS, ADAM_WD, ADAM_STEP = 0.001, 0.9, 0.999, 1e-08, 0.01, 10
LANES = 128
VMEM_LIMIT = 52 * 1024 * 1024
MESH = pl.DeviceIdType.MESH

NN = (((1,), (0,)), ((), ()))
NT = (((1,), (1,)), ((), ()))
TN = (((0,), (0,)), ((), ()))


def _tile(n, pref, align=LANES):
    if n <= pref:
        return n
    d = (pref // align) * align
    while d > align and n % d:
        d -= align
    assert n % d == 0, (n, pref, align)
    return d


def _params(sem):
    return pltpu.CompilerParams(dimension_semantics=sem, vmem_limit_bytes=VMEM_LIMIT)


def _dot(a, b, dims=NN):
    return lax.dot_general(a, b, dims, preferred_element_type=F32)


def _sigmoid(x):
    return 1.0 / (1.0 + jnp.exp(-x))


_GELU_C = 0.7978845608028654
_GELU_A = 0.044715


def _gelu(x):
    return 0.5 * x * (1.0 + jnp.tanh(x * (_GELU_C + (_GELU_C * _GELU_A) * (x * x))))


def _gelu_and_grad(x):
    x2 = x * x
    t = jnp.tanh(x * (_GELU_C + (_GELU_C * _GELU_A) * x2))
    hx = 0.5 * x
    g = hx + hx * t
    dg = 0.5 + 0.5 * t + hx * (1.0 - t * t) * (_GELU_C + (3.0 * _GELU_C * _GELU_A) * x2)
    return g, dg


def _expm1(x):
    small = x * (1.0 + x * (0.5 + x * (1.0 / 6.0 + x * (1.0 / 24.0 + x * (1.0 / 120.0)))))
    return jnp.where(jnp.abs(x) < 0.05, small, jnp.exp(x) - 1.0)


def _softplus(x):
    return jnp.maximum(x, 0.0) + jnp.log(1.0 + jnp.exp(-jnp.abs(x)))


def _rms_hat(x):
    r = lax.rsqrt(jnp.mean(x * x, axis=-1, keepdims=True) + EPS)
    return x * r, r


def _rms_bwd(x, g, dn):
    xh, r = _rms_hat(x)
    dg = jnp.sum(dn * xh, axis=0, keepdims=True)
    dxh = dn * g
    dx = r * (dxh - xh * jnp.mean(dxh * xh, axis=-1, keepdims=True))
    return dx, dg


def _acc_out(ref, val, first):
    @pl.when(first)
    def _():
        ref[...] = val

    @pl.when(jnp.logical_not(first))
    def _():
        ref[...] += val


def _lane_sums_as_row(x):
    ones = jnp.ones((8, LANES), F32)
    return lax.dot_general(ones, x, NT, precision=lax.Precision.HIGHEST, preferred_element_type=F32)[0:1, :]


def mm(pairs, mode, out_dtype, name, tm=1024, tn=1024, tk=2048):
    dims = {"nn": NN, "nt": NT, "tn": TN}[mode]
    shapes = []
    for a, b in pairs:
        if mode == "nn":
            (m, k), (k2, n) = a.shape, b.shape
        elif mode == "nt":
            (m, k), (n, k2) = a.shape, b.shape
        else:
            (k, m), (k2, n) = a.shape, b.shape
        assert k == k2, (name, a.shape, b.shape)
        shapes.append((m, n, k))
    m, n = shapes[0][0], shapes[0][1]
    assert all(s[0] == m and s[1] == n for s in shapes)
    tm, tn = _tile(m, tm), _tile(n, tn)
    tks = [_tile(s[2], tk) for s in shapes]
    nks = [s[2] // t for s, t in zip(shapes, tks)]
    starts = [sum(nks[:p]) for p in range(len(pairs))]
    nk_total = sum(nks)

    in_specs, args = [], []
    for p, (a, b) in enumerate(pairs):
        def kk(k, p=p):
            return jnp.clip(k - starts[p], 0, nks[p] - 1)
        if mode == "tn":
            in_specs.append(pl.BlockSpec((tks[p], tm), lambda i, j, k, kk=kk: (kk(k), i)))
        else:
            in_specs.append(pl.BlockSpec((tm, tks[p]), lambda i, j, k, kk=kk: (i, kk(k))))
        if mode == "nt":
            in_specs.append(pl.BlockSpec((tn, tks[p]), lambda i, j, k, kk=kk: (j, kk(k))))
        else:
            in_specs.append(pl.BlockSpec((tks[p], tn), lambda i, j, k, kk=kk: (kk(k), j)))
        args += [a, b]

    def body(*refs):
        ins, o_ref, scratch = refs[:2 * len(pairs)], refs[2 * len(pairs)], refs[2 * len(pairs) + 1:]
        if nk_total == 1:
            o_ref[...] = _dot(ins[0][...], ins[1][...], dims).astype(o_ref.dtype)
            return
        acc_ref, = scratch
        k = pl.program_id(2)
        for p in range(len(pairs)):
            @pl.when(jnp.logical_and(k >= starts[p], k < starts[p] + nks[p]))
            def _(p=p):
                prod = _dot(ins[2 * p][...], ins[2 * p + 1][...], dims)

                @pl.when(k == 0)
                def _():
                    acc_ref[...] = prod

                @pl.when(k > 0)
                def _():
                    acc_ref[...] += prod

        @pl.when(k == nk_total - 1)
        def _():
            o_ref[...] = acc_ref[...].astype(o_ref.dtype)

    return pl.pallas_call(
        body, name=name, grid=(m // tm, n // tn, nk_total),
        in_specs=in_specs, out_specs=pl.BlockSpec((tm, tn), lambda i, j, k: (i, j)),
        out_shape=jax.ShapeDtypeStruct((m, n), out_dtype),
        scratch_shapes=[] if nk_total == 1 else [pltpu.VMEM((tm, tn), F32)],
        compiler_params=_params(("parallel", "parallel", "arbitrary")),
    )(*args)


def rms_fwd(x, g, name):
    s, d = x.shape
    ts = _tile(s, 512, 16)

    def body(x_ref, g_ref, o_ref):
        xh, _ = _rms_hat(x_ref[...].astype(F32))
        o_ref[...] = (xh * g_ref[...]).astype(o_ref.dtype)

    return pl.pallas_call(
        body, name=name, grid=(s // ts,),
        in_specs=[pl.BlockSpec((ts, d), lambda i: (i, 0)), pl.BlockSpec((1, d), lambda i: (0, 0))],
        out_specs=pl.BlockSpec((ts, d), lambda i: (i, 0)),
        out_shape=jax.ShapeDtypeStruct((s, d), MD),
        compiler_params=_params(("parallel",)),
    )(x, g)


def rms_bwd(x, g, dn, name):
    s, d = x.shape
    ts = _tile(s, 256, 16)

    def body(x_ref, g_ref, dn_ref, dx_ref, dg_ref):
        dx, dg = _rms_bwd(x_ref[...].astype(F32), g_ref[...], dn_ref[...].astype(F32))
        dx_ref[...] = dx.astype(dx_ref.dtype)
        _acc_out(dg_ref, dg, pl.program_id(0) == 0)

    row = pl.BlockSpec((ts, d), lambda i: (i, 0))
    vec = pl.BlockSpec((1, d), lambda i: (0, 0))
    return pl.pallas_call(
        body, name=name, grid=(s // ts,), in_specs=[row, vec, row], out_specs=[row, vec],
        out_shape=[jax.ShapeDtypeStruct((s, d), MD), jax.ShapeDtypeStruct((1, d), F32)],
        compiler_params=_params(("arbitrary",)),
    )(x, g, dn)


def post_pre_fwd(x, y, g_post, g_pre, name):
    s, d = x.shape
    ts = _tile(s, 256, 16)

    def body(x_ref, y_ref, gp_ref, gq_ref, xn_ref, h_ref):
        yh, _ = _rms_hat(y_ref[...])
        xn = x_ref[...] + yh * gp_ref[...]
        xn_ref[...] = xn
        xh, _ = _rms_hat(xn)
        h_ref[...] = (xh * gq_ref[...]).astype(h_ref.dtype)

    row = pl.BlockSpec((ts, d), lambda i: (i, 0))
    vec = pl.BlockSpec((1, d), lambda i: (0, 0))
    return pl.pallas_call(
        body, name=name, grid=(s // ts,), in_specs=[row, row, vec, vec], out_specs=[row, row],
        out_shape=[jax.ShapeDtypeStruct((s, d), F32), jax.ShapeDtypeStruct((s, d), MD)],
        compiler_params=_params(("parallel",)),
    )(x, y, g_post, g_pre)


def loss_head(x, y, target, g_post, name):
    s, d = x.shape
    ts = _tile(s, 256, 16)

    def body(x_ref, y_ref, t_ref, g_ref, loss_ref, dx_ref, dy_ref, dg_ref):
        first = pl.program_id(0) == 0
        y = y_ref[...]
        yh, _ = _rms_hat(y)
        diff = x_ref[...] + yh * g_ref[...] - t_ref[...]
        part = 0.5 * jnp.sum(jnp.sum(diff * diff, axis=1, keepdims=True) * (1.0 / d), axis=0, keepdims=True)
        _acc_out(loss_ref, part, first)
        dx = diff * (1.0 / d)
        dx_ref[...] = dx
        dy, dg = _rms_bwd(y, g_ref[...], dx)
        dy_ref[...] = dy.astype(dy_ref.dtype)
        _acc_out(dg_ref, dg, first)

    row = pl.BlockSpec((ts, d), lambda i: (i, 0))
    vec = pl.BlockSpec((1, d), lambda i: (0, 0))
    one = pl.BlockSpec((1, 1), lambda i: (0, 0))
    return pl.pallas_call(
        body, name=name, grid=(s // ts,), in_specs=[row, row, row, vec], out_specs=[one, row, row, vec],
        out_shape=[jax.ShapeDtypeStruct((1, 1), F32), jax.ShapeDtypeStruct((s, d), F32),
                   jax.ShapeDtypeStruct((s, d), MD), jax.ShapeDtypeStruct((1, d), F32)],
        compiler_params=_params(("arbitrary",)),
    )(x, y, target, g_post)


def pre_post_bwd(x, y, dx_res, dh, g_pre, g_post, name):
    s, d = x.shape
    ts = _tile(s, 256, 16)

    def body(x_ref, y_ref, dr_ref, dh_ref, gq_ref, gp_ref, dx_ref, dy_ref, dgq_ref, dgp_ref):
        first = pl.program_id(0) == 0
        dxa, dgq = _rms_bwd(x_ref[...], gq_ref[...], dh_ref[...])
        dx = dr_ref[...] + dxa
        dx_ref[...] = dx
        dy, dgp = _rms_bwd(y_ref[...], gp_ref[...], dx)
        dy_ref[...] = dy.astype(dy_ref.dtype)
        _acc_out(dgq_ref, dgq, first)
        _acc_out(dgp_ref, dgp, first)

    row = pl.BlockSpec((ts, d), lambda i: (i, 0))
    vec = pl.BlockSpec((1, d), lambda i: (0, 0))
    return pl.pallas_call(
        body, name=name, grid=(s // ts,), in_specs=[row, row, row, row, vec, vec], out_specs=[row, row, vec, vec],
        out_shape=[jax.ShapeDtypeStruct((s, d), F32), jax.ShapeDtypeStruct((s, d), MD),
                   jax.ShapeDtypeStruct((1, d), F32), jax.ShapeDtypeStruct((1, d), F32)],
        compiler_params=_params(("arbitrary",)),
    )(x, y, dx_res, dh, g_pre, g_post)


def pre_bwd(x, dx_res, dh, g_pre, name):
    s, d = x.shape
    ts = _tile(s, 256, 16)

    def body(x_ref, dr_ref, dh_ref, g_ref, dx_ref, dg_ref):
        dxa, dg = _rms_bwd(x_ref[...], g_ref[...], dh_ref[...])
        dx_ref[...] = dr_ref[...] + dxa
        _acc_out(dg_ref, dg, pl.program_id(0) == 0)

    row = pl.BlockSpec((ts, d), lambda i: (i, 0))
    vec = pl.BlockSpec((1, d), lambda i: (0, 0))
    return pl.pallas_call(
        body, name=name, grid=(s // ts,), in_specs=[row, row, row, vec], out_specs=[row, vec],
        out_shape=[jax.ShapeDtypeStruct((s, d), F32), jax.ShapeDtypeStruct((1, d), F32)],
        compiler_params=_params(("arbitrary",)),
    )(x, dx_res, dh, g_pre)


def _swap_halves(x):
    lane = lax.broadcasted_iota(jnp.int32, x.shape, 1)
    return jnp.where((lane % QK_ROPE) < QK_ROPE // 2, pltpu.roll(x, LANES - QK_ROPE // 2, 1),
                     pltpu.roll(x, QK_ROPE // 2, 1))


def _rope_tables(ang_ref, sign):
    lane = lax.broadcasted_iota(jnp.int32, ang_ref.shape, 1)
    sgn = jnp.where((lane % QK_ROPE) < QK_ROPE // 2, -sign, sign)
    ang = ang_ref[...]
    return jnp.cos(ang), jnp.sin(ang) * sgn


def rope_fwd(proj, kr, kn, ang2, name):
    s = proj.shape[0]
    ts = _tile(s, 256, 16)
    wn, wr = HEADS * QK_NOPE, HEADS * QK_ROPE

    def body(qn_ref, qr_ref, kr_ref, kn_ref, ang_ref, qo_ref, ko_ref):
        c, sn = _rope_tables(ang_ref, 1.0)
        lane = lax.broadcasted_iota(jnp.int32, (ts, LANES), 1)
        lo = lane < QK_ROPE
        x = kr_ref[...].astype(F32)
        k_rot = jnp.where(lo, x * c + _swap_halves(x) * sn, 0.0).astype(ko_ref.dtype)
        for j in range(HEADS // 2):
            x = qr_ref[:, j * LANES:(j + 1) * LANES].astype(F32)
            r = x * c + _swap_halves(x) * sn
            pair = (jnp.where(lo, r, 0.0), jnp.where(lo, pltpu.roll(r, QK_ROPE, 1), 0.0))
            for e in range(2):
                h = 2 * j + e
                qo_ref[:, h * QK_PAD:h * QK_PAD + LANES] = qn_ref[:, h * LANES:(h + 1) * LANES]
                qo_ref[:, h * QK_PAD + LANES:(h + 1) * QK_PAD] = pair[e].astype(qo_ref.dtype)
                ko_ref[:, h * QK_PAD:h * QK_PAD + LANES] = kn_ref[:, h * LANES:(h + 1) * LANES]
                ko_ref[:, h * QK_PAD + LANES:(h + 1) * QK_PAD] = k_rot

    row = lambda w, blk: pl.BlockSpec((ts, w), lambda i: (i, blk))
    return pl.pallas_call(
        body, name=name, grid=(s // ts,),
        in_specs=[row(wn, 0), row(wr, wn // wr), row(LANES, 0), row(wn, 0), row(LANES, 0)],
        out_specs=[row(HEADS * QK_PAD, 0), row(HEADS * QK_PAD, 0)],
        out_shape=[jax.ShapeDtypeStruct((s, HEADS * QK_PAD), MD), jax.ShapeDtypeStruct((s, HEADS * QK_PAD), MD)],
        compiler_params=_params(("parallel",)),
    )(proj, proj, kr, kn, ang2)


def rope_bwd(dqf, dkf, ang2, name):
    s = dqf.shape[0]
    ts = _tile(s, 256, 16)
    wn, wr = HEADS * QK_NOPE, HEADS * QK_ROPE

    def body(dq_ref, dk_ref, ang_ref, qn_ref, qr_ref, kn_ref, kr_ref):
        c, sn = _rope_tables(ang_ref, -1.0)
        lane = lax.broadcasted_iota(jnp.int32, (ts, LANES), 1)
        lo = lane < QK_ROPE
        rot = lambda dy: dy * c + _swap_halves(dy) * sn
        dk_rope = None
        for j in range(HEADS // 2):
            halves = []
            for e in range(2):
                h = 2 * j + e
                qn_ref[:, h * LANES:(h + 1) * LANES] = dq_ref[:, h * QK_PAD:h * QK_PAD + LANES]
                kn_ref[:, h * LANES:(h + 1) * LANES] = dk_ref[:, h * QK_PAD:h * QK_PAD + LANES]
                halves.append(jnp.where(lo, dq_ref[:, h * QK_PAD + LANES:(h + 1) * QK_PAD].astype(F32), 0.0))
                dk_h = dk_ref[:, h * QK_PAD + LANES:(h + 1) * QK_PAD].astype(F32)
                dk_rope = dk_h if dk_rope is None else dk_rope + dk_h
            qr_ref[:, j * LANES:(j + 1) * LANES] = rot(halves[0] + pltpu.roll(halves[1], QK_ROPE, 1)).astype(qr_ref.dtype)
        kr_ref[...] = jnp.where(lo, rot(jnp.where(lo, dk_rope, 0.0)), 0.0).astype(kr_ref.dtype)

    row = lambda w: pl.BlockSpec((ts, w), lambda i: (i, 0))
    return pl.pallas_call(
        body, name=name, grid=(s // ts,),
        in_specs=[row(HEADS * QK_PAD), row(HEADS * QK_PAD), row(LANES)],
        out_specs=[row(wn), row(wr), row(wn), row(LANES)],
        out_shape=[jax.ShapeDtypeStruct((s, wn), MD), jax.ShapeDtypeStruct((s, wr), MD),
                   jax.ShapeDtypeStruct((s, wn), MD), jax.ShapeDtypeStruct((s, LANES), MD)],
        compiler_params=_params(("parallel",)),
    )(dqf, dkf, ang2)


_EXP2_SCALE = SOFTMAX_SCALE * LOG2E


def flash_fwd(qf, kf, v, name):
    s = qf.shape[0]
    tq = _tile(s, 1024)
    tk = _tile(s, 1024)
    nq = s // tq
    last_j = lambda i: (i * tq + tq - 1) // tk
    pairs = [(i, j) for i in range(nq) for j in range(last_j(i) + 1)]
    it = jnp.asarray(np.array([p[0] for p in pairs], np.int32))
    jt = jnp.asarray(np.array([p[1] for p in pairs], np.int32))
    groups = tk // LANES

    def body(it_ref, jt_ref, q_ref, k_ref, v_ref, o_ref, lse_ref, m_s, l_s, acc_s):
        p = pl.program_id(1)
        i, j = it_ref[p], jt_ref[p]

        @pl.when(j == 0)
        def _():
            m_s[...] = jnp.full(m_s.shape, -jnp.inf, F32)
            l_s[...] = jnp.zeros(l_s.shape, F32)
            acc_s[...] = jnp.zeros(acc_s.shape, F32)

        def step(masked):
            sc = _dot(q_ref[...], k_ref[...], NT)
            if masked:
                row = lax.broadcasted_iota(jnp.int32, (tq, tk), 0) + i * tq
                col = lax.broadcasted_iota(jnp.int32, (tq, tk), 1) + j * tk
                sc = jnp.where(col <= row, sc, -jnp.inf)
            cols = [sc[:, g * LANES:(g + 1) * LANES] for g in range(groups)]
            mx = cols[0]
            for g in range(1, groups):
                mx = jnp.maximum(mx, cols[g])
            m_prev = m_s[...]
            m_new = jnp.maximum(m_prev, jnp.max(mx, axis=1, keepdims=True))
            alpha = jnp.exp2((m_prev - m_new) * _EXP2_SCALE)
            ps, psum = [], None
            for g in range(groups):
                pg = jnp.exp2((cols[g] - m_new) * _EXP2_SCALE)
                psum = pg if psum is None else psum + pg
                ps.append(pg.astype(MD))
            l_s[...] = alpha * l_s[...] + psum
            acc_s[...] = acc_s[...] * alpha + _dot(jnp.concatenate(ps, axis=1), v_ref[...])
            m_s[...] = m_new

        needs_mask = (j + 1) * tk - 1 > i * tq

        @pl.when(needs_mask)
        def _():
            step(True)

        @pl.when(jnp.logical_not(needs_mask))
        def _():
            step(False)

        @pl.when(j == (i * tq + tq - 1) // tk)
        def _():
            l = jnp.sum(l_s[...], axis=1, keepdims=True)
            o_ref[...] = (acc_s[...] / l).astype(o_ref.dtype)
            lse_rep = m_s[...] * _EXP2_SCALE + jnp.log(l) * LOG2E
            lse_ref[...] = _lane_sums_as_row(lse_rep * (1.0 / LANES))

    grid_spec = pltpu.PrefetchScalarGridSpec(
        num_scalar_prefetch=2, grid=(HEADS, len(pairs)),
        in_specs=[pl.BlockSpec((tq, QK_PAD), lambda h, p, it, jt: (it[p], h)),
                  pl.BlockSpec((tk, QK_PAD), lambda h, p, it, jt: (jt[p], h)),
                  pl.BlockSpec((tk, V_DIM), lambda h, p, it, jt: (jt[p], h))],
        out_specs=[pl.BlockSpec((tq, V_DIM), lambda h, p, it, jt: (it[p], h)),
                   pl.BlockSpec((None, 1, tq), lambda h, p, it, jt: (h, 0, it[p]))],
        scratch_shapes=[pltpu.VMEM((tq, LANES), F32), pltpu.VMEM((tq, LANES), F32), pltpu.VMEM((tq, V_DIM), F32)])
    return pl.pallas_call(
        body, name=name, grid_spec=grid_spec,
        out_shape=[jax.ShapeDtypeStruct((s, HEADS * V_DIM), MD), jax.ShapeDtypeStruct((HEADS, 1, s), F32)],
        compiler_params=_params(("parallel", "arbitrary")),
    )(it, jt, qf, kf, v)


def attn_delta(o, do, name):
    s = o.shape[0]
    t = _tile(s, 1024)

    def body(o_ref, do_ref, d_ref):
        d_ref[...] = _lane_sums_as_row(o_ref[...].astype(F32) * do_ref[...].astype(F32))

    blk = pl.BlockSpec((t, V_DIM), lambda h, i: (i, h))
    return pl.pallas_call(
        body, name=name, grid=(HEADS, s // t), in_specs=[blk, blk],
        out_specs=pl.BlockSpec((None, 1, t), lambda h, i: (h, 0, i)),
        out_shape=jax.ShapeDtypeStruct((HEADS, 1, s), F32),
        compiler_params=_params(("parallel", "parallel")),
    )(o, do)


def flash_bwd(qf, kf, v, do, lse2, delta, name):
    s = qf.shape[0]
    tq, tk = _tile(s, 1024), _tile(s, 1024)
    nq, nk = s // tq, s // tk
    first_i = lambda j: (j * tk) // tq
    pairs = [(i, j) for j in range(nk) for i in range(first_i(j), nq)]
    it = jnp.asarray(np.array([p[0] for p in pairs], np.int32))
    jt = jnp.asarray(np.array([p[1] for p in pairs], np.int32))
    n_pairs = len(pairs)

    def body(it_ref, jt_ref, q_ref, k_ref, v_ref, do_ref, lse_ref, dl_ref, dq_ref, dk_ref, dv_ref, dq_s, dk_s, dv_s):
        p = pl.program_id(1)
        i, j = it_ref[p], jt_ref[p]

        @pl.when(p == 0)
        def _():
            dq_s[...] = jnp.zeros(dq_s.shape, F32)

        @pl.when(i == (j * tk) // tq)
        def _():
            dk_s[...] = jnp.zeros(dk_s.shape, F32)
            dv_s[...] = jnp.zeros(dv_s.shape, F32)

        def step(masked):
            q, k, do_b = q_ref[...], k_ref[...], do_ref[...]
            pt = jnp.exp2(_dot(k, q, NT) * _EXP2_SCALE - lse_ref[...])
            if masked:
                krow = lax.broadcasted_iota(jnp.int32, (tk, tq), 0) + j * tk
                qcol = lax.broadcasted_iota(jnp.int32, (tk, tq), 1) + i * tq
                pt = jnp.where(krow <= qcol, pt, 0.0)
            dv_s[...] += _dot(pt.astype(MD), do_b)
            dst = (pt * (_dot(v_ref[...], do_b, NT) - dl_ref[...])).astype(MD)
            dk_s[...] += _dot(dst, q)
            row0 = pl.multiple_of(i * tq, tq)
            dq_s[pl.ds(row0, tq), :] += _dot(dst, k, TN)

        needs_mask = (j + 1) * tk - 1 > i * tq

        @pl.when(needs_mask)
        def _():
            step(True)

        @pl.when(jnp.logical_not(needs_mask))
        def _():
            step(False)

        @pl.when(i == nq - 1)
        def _():
            dk_ref[...] = (dk_s[...] * SOFTMAX_SCALE).astype(dk_ref.dtype)
            dv_ref[...] = dv_s[...].astype(dv_ref.dtype)

        @pl.when(p == n_pairs - 1)
        def _():
            dq_ref[...] = (dq_s[...] * SOFTMAX_SCALE).astype(dq_ref.dtype)

    vec = pl.BlockSpec((None, 1, tq), lambda h, p, it, jt: (h, 0, it[p]))
    grid_spec = pltpu.PrefetchScalarGridSpec(
        num_scalar_prefetch=2, grid=(HEADS, n_pairs),
        in_specs=[pl.BlockSpec((tq, QK_PAD), lambda h, p, it, jt: (it[p], h)),
                  pl.BlockSpec((tk, QK_PAD), lambda h, p, it, jt: (jt[p], h)),
                  pl.BlockSpec((tk, V_DIM), lambda h, p, it, jt: (jt[p], h)),
                  pl.BlockSpec((tq, V_DIM), lambda h, p, it, jt: (it[p], h)), vec, vec],
        out_specs=[pl.BlockSpec((s, QK_PAD), lambda h, p, it, jt: (0, h)),
                   pl.BlockSpec((tk, QK_PAD), lambda h, p, it, jt: (jt[p], h)),
                   pl.BlockSpec((tk, V_DIM), lambda h, p, it, jt: (jt[p], h))],
        scratch_shapes=[pltpu.VMEM((s, QK_PAD), F32), pltpu.VMEM((tk, QK_PAD), F32), pltpu.VMEM((tk, V_DIM), F32)])
    wide = jax.ShapeDtypeStruct((s, HEADS * QK_PAD), MD)
    return pl.pallas_call(
        body, name=name, grid_spec=grid_spec,
        out_shape=[wide, wide, jax.ShapeDtypeStruct((s, HEADS * V_DIM), MD)],
        compiler_params=_params(("parallel", "arbitrary")),
    )(it, jt, qf, kf, v, do, lse2, delta)


def _halo_rows(dtype):
    return 8 if jnp.dtype(dtype).itemsize == 4 else 16


def _prev_spec(ts, tc, hr, col_off):
    return pl.BlockSpec((hr, tc), lambda j, i: (jnp.maximum(i * (ts // hr) - 1, 0), j + col_off))


def _conv_taps(x, halo, w, taps, first):
    hr, ts = halo.shape[0], x.shape[0]
    xe = jnp.concatenate([jnp.where(first, 0.0, halo), x], axis=0)
    shifted = []
    acc = None
    for k in range(taps):
        sh = taps - 1 - k
        xs = xe if sh == 0 else pltpu.roll(xe, sh, 0)
        xs = xs[hr:hr + ts, :]
        shifted.append(xs)
        term = xs * w[k:k + 1, :]
        acc = term if acc is None else acc + term
    return acc, shifted


def conv_bwd(x, dy, w8, taps, x_col_off, name, ts_pref=1024):
    halves = dy.ndim == 3
    s = dy.shape[-2]
    c = dy.shape[-1] * (2 if halves else 1)
    tc = _tile(dy.shape[-1], 512)
    ts = _tile(s, ts_pref, 16)
    hx, hy = _halo_rows(x.dtype), _halo_rows(dy.dtype)
    ns = s // ts
    nh = dy.shape[-1] // tc
    xo = x_col_off // tc
    assert x_col_off % tc == 0

    def body(x_ref, xp_ref, dy_ref, dyn_ref, w_ref, dx_ref, dw_ref, db_ref):
        i = pl.program_id(1)
        first = i == 0
        xv = x_ref[...].astype(F32)
        dyv = dy_ref[...].astype(F32)
        w = w_ref[...]
        _, shifted = _conv_taps(xv, xp_ref[...].astype(F32), w, taps, first)
        dye = jnp.concatenate([dyv, jnp.where(i == ns - 1, 0.0, dyn_ref[...].astype(F32))], axis=0)
        dx = None
        for k in range(taps):
            sh = taps - 1 - k
            ds_ = dye if sh == 0 else pltpu.roll(dye, ts + hy - sh, 0)
            term = ds_[0:ts, :] * w[k:k + 1, :]
            dx = term if dx is None else dx + term
        dx_ref[...] = dx.astype(dx_ref.dtype)
        rows = [jnp.sum(dyv * shifted[k], axis=0, keepdims=True) for k in range(taps)]
        rows.append(jnp.zeros((8 - taps, tc), F32))
        _acc_out(dw_ref, jnp.concatenate(rows, axis=0), first)
        _acc_out(db_ref, jnp.sum(dyv, axis=0, keepdims=True), first)

    last = s // hy - 1
    nxt = lambda i: jnp.minimum((i + 1) * (ts // hy), last)
    if halves:
        dy_spec = pl.BlockSpec((None, ts, tc), lambda j, i: (j // nh, i, j % nh))
        dyn_spec = pl.BlockSpec((None, hy, tc), lambda j, i: (j // nh, nxt(i), j % nh))
    else:
        dy_spec = pl.BlockSpec((ts, tc), lambda j, i: (i, j))
        dyn_spec = pl.BlockSpec((hy, tc), lambda j, i: (nxt(i), j))
    return pl.pallas_call(
        body, name=name, grid=(c // tc, ns),
        in_specs=[pl.BlockSpec((ts, tc), lambda j, i: (i, j + xo)), _prev_spec(ts, tc, hx, xo), dy_spec, dyn_spec,
                  pl.BlockSpec((8, tc), lambda j, i: (0, j))],
        out_specs=[pl.BlockSpec((ts, tc), lambda j, i: (i, j)), pl.BlockSpec((8, tc), lambda j, i: (0, j)),
                   pl.BlockSpec((1, tc), lambda j, i: (0, j))],
        out_shape=[jax.ShapeDtypeStruct((s, c), MD), jax.ShapeDtypeStruct((8, c), F32), jax.ShapeDtypeStruct((1, c), F32)],
        compiler_params=_params(("parallel", "arbitrary")),
    )(x, x, dy, dy, w8)


def _lru_gates(xc, wr_ref, br, wi_ref, bi, sp, nblk):
    xb = xc.astype(MD)
    zr, zi = [], []
    for b in range(nblk):
        blk = xb[:, b * LANES:(b + 1) * LANES]
        zr.append(_dot(blk, wr_ref[b]))
        zi.append(_dot(blk, wi_ref[b]))
    r = _sigmoid(jnp.concatenate(zr, axis=1) + br)
    ig = _sigmoid(jnp.concatenate(zi, axis=1) + bi)
    log_a = -LRU_C * r * sp
    a = jnp.exp(log_a)
    mult = jnp.sqrt(-_expm1(2.0 * log_a))
    return r, ig, a, mult


def lru_fwd(proj, x_off, y_off, w8, b_conv, w_rg, b_rg, w_ig, b_ig, lam, name):
    s = proj.shape[0]
    c = lam.shape[1]
    tc, ts = _tile(c, 512), _tile(s, 256, 16)
    nblk = tc // LANES
    xo, yo = x_off // tc, y_off // tc
    hr = _halo_rows(proj.dtype)

    def body(x_ref, xp_ref, y_ref, w_ref, bc_ref, wr_ref, br_ref, wi_ref, bi_ref, lam_ref, h_ref, hg_ref, carry):
        i = pl.program_id(1)
        first = i == 0
        xc, _ = _conv_taps(x_ref[...].astype(F32), xp_ref[...].astype(F32), w_ref[...], LRU_TAPS, first)
        xc = xc + bc_ref[...]
        sp = _softplus(-lam_ref[...])
        _, ig, a, mult = _lru_gates(xc, wr_ref, br_ref[...], wi_ref, bi_ref[...], sp, nblk)
        b = mult * (ig * xc)
        row = lax.broadcasted_iota(jnp.int32, (ts, tc), 0)
        d = 1
        while d < ts:
            keep = row >= d
            a_s, b_s = pltpu.roll(a, d, 0), pltpu.roll(b, d, 0)
            b = jnp.where(keep, a * b_s + b, b)
            a = jnp.where(keep, a * a_s, a)
            d *= 2
        h0 = jnp.where(first, 0.0, carry[7:8, :])
        h = a * h0 + b
        carry[...] = h[ts - 8:ts, :]
        h_ref[...] = h
        hg_ref[...] = (h * _gelu(y_ref[...].astype(F32))).astype(hg_ref.dtype)

    tile = lambda off: pl.BlockSpec((ts, tc), lambda j, i: (i, j + off))
    vec = pl.BlockSpec((1, tc), lambda j, i: (0, j))
    wblk = pl.BlockSpec((nblk, LANES, LANES), lambda j, i: (j, 0, 0))
    out = pl.BlockSpec((ts, tc), lambda j, i: (i, j))
    return pl.pallas_call(
        body, name=name, grid=(c // tc, s // ts),
        in_specs=[tile(xo), _prev_spec(ts, tc, hr, xo), tile(yo), pl.BlockSpec((8, tc), lambda j, i: (0, j)), vec,
                  wblk, vec, wblk, vec, vec],
        out_specs=[out, out],
        out_shape=[jax.ShapeDtypeStruct((s, c), F32), jax.ShapeDtypeStruct((s, c), MD)],
        scratch_shapes=[pltpu.VMEM((8, tc), F32)],
        compiler_params=_params(("parallel", "arbitrary")),
    )(proj, proj, proj, w8, b_conv, w_rg, b_rg, w_ig, b_ig, lam)


def lru_bwd(proj, x_off, y_off, h, dhg, w8, b_conv, w_rg, b_rg, w_ig, b_ig, lam, name):
    s = proj.shape[0]
    c = lam.shape[1]
    tc, ts = _tile(c, 512), _tile(s, 256, 16)
    nblk = tc // LANES
    ns = s // ts
    xo, yo = x_off // tc, y_off // tc
    hr = _halo_rows(proj.dtype)
    rev = lambda i: ns - 1 - i

    def body(x_ref, xp_ref, y_ref, h_ref, hp_ref, dhg_ref, w_ref, bc_ref, wr_ref, br_ref, wi_ref, bi_ref, lam_ref,
             dxc_ref, dy_ref, dwr_ref, dwi_ref, dbr_ref, dbi_ref, dlam_ref, a_carry, g_carry):
        step = pl.program_id(1)
        i = rev(step)
        start = step == 0
        top = i == 0
        xc, _ = _conv_taps(x_ref[...].astype(F32), xp_ref[...].astype(F32), w_ref[...], LRU_TAPS, top)
        xc = xc + bc_ref[...]
        lam_v = lam_ref[...]
        sp = _softplus(-lam_v)
        r, ig, a, mult = _lru_gates(xc, wr_ref, br_ref[...], wi_ref, bi_ref[...], sp, nblk)
        hv = h_ref[...]
        gy, dgy = _gelu_and_grad(y_ref[...].astype(F32))
        dhg_v = dhg_ref[...]
        dy_ref[...] = (dhg_v * hv * dgy).astype(dy_ref.dtype)
        row = lax.broadcasted_iota(jnp.int32, (ts, tc), 0)
        a_next = jnp.where(row == ts - 1, jnp.where(start, 0.0, a_carry[0:1, :]), pltpu.roll(a, ts - 1, 0))
        ca, cb = a_next, dhg_v * gy
        d = 1
        while d < ts:
            keep = row < ts - d
            a_s, b_s = pltpu.roll(ca, ts - d, 0), pltpu.roll(cb, ts - d, 0)
            cb = jnp.where(keep, ca * b_s + cb, cb)
            ca = jnp.where(keep, ca * a_s, ca)
            d *= 2
        g = ca * jnp.where(start, 0.0, g_carry[0:1, :]) + cb
        a_carry[...] = a[0:8, :]
        g_carry[...] = g[0:8, :]
        h_prev = jnp.where(row == 0, jnp.where(top, 0.0, hp_ref[7:8, :]), pltpu.roll(hv, 1, 0))
        da = g * h_prev
        ixc = ig * xc
        d_ixc = g * mult
        dlog_a = da * a - g * ixc * (a * a) / mult
        dzr = dlog_a * (-LRU_C * sp) * r * (1.0 - r)
        dzi = d_ixc * xc * ig * (1.0 - ig)
        dxc = d_ixc * ig
        xb, dzr_b, dzi_b = xc.astype(MD), dzr.astype(MD), dzi.astype(MD)
        parts, dwr, dwi = [], [], []
        for b in range(nblk):
            sl = slice(b * LANES, (b + 1) * LANES)
            parts.append(_dot(dzr_b[:, sl], wr_ref[b], NT) + _dot(dzi_b[:, sl], wi_ref[b], NT))
            dwr.append(_dot(xb[:, sl], dzr_b[:, sl], TN)[None])
            dwi.append(_dot(xb[:, sl], dzi_b[:, sl], TN)[None])
        dxc_ref[...] = dxc + jnp.concatenate(parts, axis=1)
        _acc_out(dwr_ref, jnp.concatenate(dwr, axis=0), start)
        _acc_out(dwi_ref, jnp.concatenate(dwi, axis=0), start)
        _acc_out(dbr_ref, jnp.sum(dzr, axis=0, keepdims=True), start)
        _acc_out(dbi_ref, jnp.sum(dzi, axis=0, keepdims=True), start)
        dsp = jnp.sum(dlog_a * (-LRU_C) * r, axis=0, keepdims=True)
        _acc_out(dlam_ref, -dsp * _sigmoid(-lam_v), start)

    vec = pl.BlockSpec((1, tc), lambda j, i: (0, j))
    wblk = pl.BlockSpec((nblk, LANES, LANES), lambda j, i: (j, 0, 0))
    tile = lambda off: pl.BlockSpec((ts, tc), lambda j, i: (rev(i), j + off))
    x_prev = pl.BlockSpec((hr, tc), lambda j, i: (jnp.maximum(rev(i) * (ts // hr) - 1, 0), j + xo))
    h_prev = pl.BlockSpec((8, tc), lambda j, i: (jnp.maximum(rev(i) * (ts // 8) - 1, 0), j))
    return pl.pallas_call(
        body, name=name, grid=(c // tc, ns),
        in_specs=[tile(xo), x_prev, tile(yo), tile(0), h_prev, tile(0), pl.BlockSpec((8, tc), lambda j, i: (0, j)), vec,
                  wblk, vec, wblk, vec, vec],
        out_specs=[tile(0), tile(0), wblk, wblk, vec, vec, vec],
        out_shape=[jax.ShapeDtypeStruct((s, c), F32), jax.ShapeDtypeStruct((s, c), MD),
                   jax.ShapeDtypeStruct(w_rg.shape, F32), jax.ShapeDtypeStruct(w_ig.shape, F32),
                   jax.ShapeDtypeStruct((1, c), F32), jax.ShapeDtypeStruct((1, c), F32), jax.ShapeDtypeStruct((1, c), F32)],
        scratch_shapes=[pltpu.VMEM((8, tc), F32), pltpu.VMEM((8, tc), F32)],
        compiler_params=_params(("parallel", "arbitrary")),
    )(proj, proj, proj, h, h, dhg, w8, b_conv, w_rg, b_rg, w_ig, b_ig, lam)


def merge_fwd(proj, gm_off, gl_off, o_mla, o_lru, name):
    s, c = o_mla.shape
    tc, ts = _tile(c, 1024), _tile(s, 512, 16)

    def body(gm_ref, gl_ref, om_ref, ol_ref, out_ref):
        out_ref[...] = (_sigmoid(gm_ref[...].astype(F32)) * om_ref[...] + _sigmoid(gl_ref[...].astype(F32)) * ol_ref[...]).astype(out_ref.dtype)

    tile = lambda off: pl.BlockSpec((ts, tc), lambda i, j: (i, j + off // tc))
    return pl.pallas_call(
        body, name=name, grid=(s // ts, c // tc), in_specs=[tile(gm_off), tile(gl_off), tile(0), tile(0)], out_specs=tile(0),
        out_shape=jax.ShapeDtypeStruct((s, c), MD), compiler_params=_params(("parallel", "parallel")),
    )(proj, proj, o_mla, o_lru)


def merge_bwd(proj, gm_off, gl_off, o_mla, o_lru, dmerged, name):
    s, c = o_mla.shape
    tc, ts = _tile(c, 1024), _tile(s, 512, 16)

    def body(gm_ref, gl_ref, om_ref, ol_ref, dm_ref, dom_ref, dol_ref, dgm_ref, dgl_ref):
        sm, sl = _sigmoid(gm_ref[...].astype(F32)), _sigmoid(gl_ref[...].astype(F32))
        dm = dm_ref[...]
        dom_ref[...] = (dm * sm).astype(dom_ref.dtype)
        dol_ref[...] = (dm * sl).astype(dol_ref.dtype)
        dgm_ref[...] = (dm * om_ref[...] * sm * (1.0 - sm)).astype(dgm_ref.dtype)
        dgl_ref[...] = (dm * ol_ref[...] * sl * (1.0 - sl)).astype(dgl_ref.dtype)

    tile = lambda off: pl.BlockSpec((ts, tc), lambda i, j: (i, j + off // tc))
    out = jax.ShapeDtypeStruct((s, c), MD)
    return pl.pallas_call(
        body, name=name, grid=(s // ts, c // tc), in_specs=[tile(gm_off), tile(gl_off), tile(0), tile(0), tile(0)],
        out_specs=[tile(0)] * 4, out_shape=[out] * 4, compiler_params=_params(("parallel", "parallel")),
    )(proj, proj, o_mla, o_lru, dmerged)


def xattn_fwd(cq, ck, cv, name):
    s = cq.shape[0]
    m = ck.shape[0]
    ts = _tile(s, 512, 16)
    scale = X_DIM ** -0.5

    def body(q_ref, k_ref, v_ref, o_ref):
        for h in range(X_HEADS):
            sl = slice(h * X_DIM, (h + 1) * X_DIM)
            sc = _dot(q_ref[:, sl], k_ref[:, sl], NT) * scale
            e = jnp.exp(sc - jnp.max(sc, axis=1, keepdims=True))
            p = e / jnp.sum(e, axis=1, keepdims=True)
            o_ref[:, sl] = _dot(p.astype(MD), v_ref[:, sl]).astype(o_ref.dtype)

    w = X_HEADS * X_DIM
    return pl.pallas_call(
        body, name=name, grid=(s // ts,),
        in_specs=[pl.BlockSpec((ts, w), lambda i: (i, 0)), pl.BlockSpec((m, w), lambda i: (0, 0)), pl.BlockSpec((m, w), lambda i: (0, 0))],
        out_specs=pl.BlockSpec((ts, w), lambda i: (i, 0)), out_shape=jax.ShapeDtypeStruct((s, w), MD),
        compiler_params=_params(("parallel",)),
    )(cq, ck, cv)


def xattn_bwd(cq, ck, cv, dco, name):
    s = cq.shape[0]
    m = ck.shape[0]
    ts = _tile(s, 512, 16)
    scale = X_DIM ** -0.5

    def body(q_ref, k_ref, v_ref, do_ref, dq_ref, dk_ref, dv_ref):
        first = pl.program_id(0) == 0
        dks, dvs = [], []
        for h in range(X_HEADS):
            sl = slice(h * X_DIM, (h + 1) * X_DIM)
            q, k, v, do = q_ref[:, sl], k_ref[:, sl], v_ref[:, sl], do_ref[:, sl]
            sc = _dot(q, k, NT) * scale
            e = jnp.exp(sc - jnp.max(sc, axis=1, keepdims=True))
            p = e / jnp.sum(e, axis=1, keepdims=True)
            dvs.append(_dot(p.astype(MD), do, TN))
            dp = _dot(do, v, NT)
            ds = (p * (dp - jnp.sum(dp * p, axis=1, keepdims=True)) * scale).astype(MD)
            dq_ref[:, sl] = _dot(ds, k).astype(dq_ref.dtype)
            dks.append(_dot(ds, q, TN))
        _acc_out(dk_ref, jnp.concatenate(dks, axis=1), first)
        _acc_out(dv_ref, jnp.concatenate(dvs, axis=1), first)

    w = X_HEADS * X_DIM
    row = pl.BlockSpec((ts, w), lambda i: (i, 0))
    full = pl.BlockSpec((m, w), lambda i: (0, 0))
    return pl.pallas_call(
        body, name=name, grid=(s // ts,), in_specs=[row, full, full, row], out_specs=[row, full, full],
        out_shape=[jax.ShapeDtypeStruct((s, w), MD), jax.ShapeDtypeStruct((m, w), F32), jax.ShapeDtypeStruct((m, w), F32)],
        compiler_params=_params(("arbitrary",)),
    )(cq, ck, cv, dco)


def _ffn_specs(up_pre):
    s, c2 = up_pre.shape
    f = c2 // 2
    tc, ts = _tile(f, 512), _tile(s, 1024, 16)
    nf = f // tc
    hr = _halo_rows(up_pre.dtype)
    tile = lambda off: pl.BlockSpec((ts, tc), lambda j, i: (i, j + off))
    w = lambda off: pl.BlockSpec((8, tc), lambda j, i: (0, j + off))
    vec = lambda off: pl.BlockSpec((1, tc), lambda j, i: (0, j + off))
    specs = [tile(0), _prev_spec(ts, tc, hr, 0), tile(nf), _prev_spec(ts, tc, hr, nf), w(0), w(nf), vec(0), vec(nf)]
    return s, f, tc, ts, nf, tile, specs


def _ffn_gate_val(g_ref, gp_ref, v_ref, vp_ref, wg_ref, wv_ref, bg_ref, bv_ref, first):
    gate, _ = _conv_taps(g_ref[...].astype(F32), gp_ref[...].astype(F32), wg_ref[...], FFN_TAPS, first)
    val, _ = _conv_taps(v_ref[...].astype(F32), vp_ref[...].astype(F32), wv_ref[...], FFN_TAPS, first)
    return gate + bg_ref[...], val + bv_ref[...]


def ffn_act_fwd(up_pre, w8, b, name):
    s, f, tc, ts, nf, tile, specs = _ffn_specs(up_pre)

    def body(*refs):
        o_ref = refs[8]
        gate, val = _ffn_gate_val(*refs[:8], pl.program_id(1) == 0)
        o_ref[...] = (_gelu(gate) * val).astype(o_ref.dtype)

    return pl.pallas_call(
        body, name=name, grid=(nf, s // ts), in_specs=specs, out_specs=tile(0),
        out_shape=jax.ShapeDtypeStruct((s, f), MD), compiler_params=_params(("parallel", "parallel")),
    )(up_pre, up_pre, up_pre, up_pre, w8, w8, b, b)


def ffn_act_bwd(up_pre, w8, b, dact, name):
    s, f, tc, ts, nf, tile, specs = _ffn_specs(up_pre)

    def body(*refs):
        da_ref, o_ref = refs[8], refs[9]
        gate, val = _ffn_gate_val(*refs[:8], pl.program_id(1) == 0)
        ge, dge = _gelu_and_grad(gate)
        da = da_ref[...].astype(F32)
        o_ref[0] = (da * val * dge).astype(o_ref.dtype)
        o_ref[1] = (da * ge).astype(o_ref.dtype)

    return pl.pallas_call(
        body, name=name, grid=(nf, s // ts), in_specs=specs + [tile(0)],
        out_specs=pl.BlockSpec((2, ts, tc), lambda j, i: (0, i, j)), out_shape=jax.ShapeDtypeStruct((2, s, f), MD),
        compiler_params=_params(("parallel", "parallel")),
    )(up_pre, up_pre, up_pre, up_pre, w8, w8, b, b, dact)


def _adam(w, g, m, v):
    m = ADAM_B1 * m + (1.0 - ADAM_B1) * g
    v = ADAM_B2 * v + (1.0 - ADAM_B2) * (g * g)
    m_hat = m / (1.0 - ADAM_B1 ** ADAM_STEP)
    v_hat = v / (1.0 - ADAM_B2 ** ADAM_STEP)
    delta = -ADAM_LR * (m_hat / (jnp.sqrt(v_hat) + ADAM_EPS) + ADAM_WD * w)
    return delta, m, v


def _rows_tile(r, c):
    return _tile(r, max(16, (1 << 18) // c // 16 * 16), 16)


def adamw_sharded(mine, peers, w, m, v, name):
    r, c = w.shape
    tr = _rows_tile(r, c)

    def body(g_ref, p_ref, w_ref, m_ref, v_ref, go_ref, d_ref, mo_ref, vo_ref):
        grad = g_ref[...].astype(F32)
        for k in range(N_DEV - 1):
            grad = grad + p_ref[k].astype(F32)
        go_ref[...] = grad
        d_ref[...], mo_ref[...], vo_ref[...] = _adam(w_ref[...], grad, m_ref[...], v_ref[...])

    row = pl.BlockSpec((tr, c), lambda i: (i, 0))
    out = jax.ShapeDtypeStruct((r, c), F32)
    return pl.pallas_call(
        body, name=name, grid=(r // tr,),
        in_specs=[row, pl.BlockSpec((N_DEV - 1, tr, c), lambda i: (0, i, 0)), row, row, row],
        out_specs=[row] * 4, out_shape=[out] * 4, compiler_params=_params(("parallel",)),
    )(mine, peers, w, m, v)


def adamw_plain(g, w, m, v, name):
    r, c = w.shape
    tr = _tile(r, 512, 8)

    def body(g_ref, w_ref, m_ref, v_ref, d_ref, mo_ref, vo_ref):
        d_ref[...], mo_ref[...], vo_ref[...] = _adam(w_ref[...], g_ref[...], m_ref[...], v_ref[...])

    row = pl.BlockSpec((tr, c), lambda i: (i, 0))
    out = jax.ShapeDtypeStruct((r, c), F32)
    return pl.pallas_call(body, name=name, grid=(r // tr,), in_specs=[row] * 4, out_specs=[row] * 3, out_shape=[out] * 3,
                          compiler_params=_params(("parallel",)))(g, w, m, v)


def sum_devices(parts, name):
    _, r, c = parts.shape
    tr = _tile(r, 256, 8)

    def body(p_ref, o_ref):
        acc = p_ref[0]
        for d in range(1, N_DEV):
            acc = acc + p_ref[d]
        o_ref[...] = acc

    return pl.pallas_call(
        body, name=name, grid=(r // tr,), in_specs=[pl.BlockSpec((N_DEV, tr, c), lambda i: (0, i, 0))],
        out_specs=pl.BlockSpec((tr, c), lambda i: (i, 0)), out_shape=jax.ShapeDtypeStruct((r, c), F32),
        compiler_params=_params(("parallel",)))(parts)


def _place():
    return lax.axis_index("x"), lax.axis_index("y"), lax.axis_index("c")


def _other_chips(x, y):
    return [(1 - x, y), (x, 1 - y), (1 - x, 1 - y)]


_ANY = pl.BlockSpec(memory_space=pl.ANY)


def all_gather(shards, name):
    n = len(shards)

    def body(*refs):
        x_refs, out_refs = refs[:n], refs[n:2 * n]
        send_sems, recv_sems, local_sems = refs[2 * n:]
        x, y, cc = _place()
        me, sibling = (x, y, cc), (x, y, 1 - cc)
        chips = _other_chips(x, y)

        def slot(w, px, py, pc):
            return out_refs[w].at[4 * px + 2 * py + pc]

        def copy(k, w, block, to, src=None):
            return pltpu.make_async_remote_copy(
                src_ref=slot(w, *block) if src is None else src, dst_ref=slot(w, *block),
                send_sem=send_sems.at[k, w], recv_sem=recv_sems.at[k, w], device_id=to, device_id_type=MESH)

        mine = [pltpu.make_async_copy(x_refs[w], slot(w, *me), local_sems.at[w]) for w in range(n)]
        for cp in mine:
            cp.start()
        first = [copy(0, w, me, sibling, src=x_refs[w]) for w in range(n)]
        first += [copy(1 + j, w, me, (*chip, cc), src=x_refs[w]) for j, chip in enumerate(chips) for w in range(n)]
        for cp in first:
            cp.start()
        passed = []
        for j, chip in enumerate(chips):
            for w in range(n):
                copy(1 + j, w, (*chip, cc), me).wait_recv()
                passed.append(copy(4 + j, w, (*chip, cc), sibling))
                passed[-1].start()
        for w in range(n):
            copy(0, w, sibling, me).wait_recv()
        for j, chip in enumerate(chips):
            for w in range(n):
                copy(4 + j, w, (*chip, 1 - cc), me).wait_recv()
        for cp in first + passed:
            cp.wait_send()
        for cp in mine:
            cp.wait()

    return pl.pallas_call(
        body, name=name, out_shape=[jax.ShapeDtypeStruct((N_DEV, *a.shape), a.dtype) for a in shards],
        in_specs=[_ANY] * n, out_specs=[_ANY] * n,
        scratch_shapes=[pltpu.SemaphoreType.DMA((7, n)), pltpu.SemaphoreType.DMA((7, n)), pltpu.SemaphoreType.DMA((n,))],
    )(*shards)


_HBM = pl.BlockSpec(memory_space=pltpu.HBM)
_SEM = pl.BlockSpec(memory_space=pltpu.SEMAPHORE)
_DATAFLOW = pltpu.SideEffectType.DATAFLOW_SIDE_EFFECTING
_PEER_FLIPS = ((0, 0, 1), (1, 0, 0), (1, 0, 1), (0, 1, 0), (0, 1, 1), (1, 1, 0), (1, 1, 1))
N_PEERS = len(_PEER_FLIPS)


def _peer_copies(kind, src_refs, land_refs, send_sems, recv_sems):
    x, y, c = _place()
    n = len(src_refs)
    copies = []
    for w in range(n):
        for k, (fx, fy, fc) in enumerate(_PEER_FLIPS):
            px, py, pc = (1 - x if fx else x), (1 - y if fy else y), (1 - c if fc else c)
            if kind == "scatter":
                src, dst = src_refs[w].at[4 * px + 2 * py + pc], land_refs[w].at[k]
            else:
                src, dst = src_refs[w], land_refs[w].at[4 * x + 2 * y + c]
            copies.append(pltpu.make_async_remote_copy(
                src_ref=src, dst_ref=dst, send_sem=send_sems.at[k * n + w], recv_sem=recv_sems.at[k * n + w],
                device_id=(px, py, pc), device_id_type=MESH))
    return copies


def send_start(kind, srcs, name):
    n = len(srcs)
    if kind == "scatter":
        lands = [lax.empty((N_PEERS, *a.shape[1:]), a.dtype) for a in srcs]
    else:
        lands = [lax.empty((N_DEV, *a.shape), a.dtype) for a in srcs]

    def body(*refs):
        src_refs, land_refs = refs[:n], refs[n:2 * n]
        send_sems, recv_sems = refs[2 * n], refs[2 * n + 1]
        token = refs[-1]
        for cp in _peer_copies(kind, src_refs, land_refs, send_sems, recv_sems):
            cp.start()
        token[...] = jnp.zeros_like(token)

    hbm = lambda a: pltpu.with_memory_space_constraint(a, pltpu.HBM)
    outs = pl.pallas_call(
        body, name=name,
        out_shape=(pltpu.SemaphoreType.DMA((N_PEERS * n,)), pltpu.SemaphoreType.DMA((N_PEERS * n,)),
                   *[pltpu.HBM(a.shape, a.dtype) for a in srcs + lands], jax.ShapeDtypeStruct((8, LANES), F32)),
        in_specs=[_HBM] * (2 * n),
        out_specs=(_SEM, _SEM, *[_HBM] * (2 * n), pl.BlockSpec(memory_space=pltpu.VMEM)),
        input_output_aliases={i: 2 + i for i in range(2 * n)},
        compiler_params=pltpu.CompilerParams(has_side_effects=_DATAFLOW),
    )(*[hbm(a) for a in srcs + lands])
    return (kind, outs[0], outs[1], list(outs[2:2 + n]), list(outs[2 + n:2 + 2 * n])), outs[-1]


def send_wait(handle, after, name):
    kind, send_sems, recv_sems, srcs, lands = handle
    n = len(srcs)

    def body(*refs):
        src_refs, land_refs = refs[:n], refs[n:2 * n]
        for cp in _peer_copies(kind, src_refs, land_refs, refs[2 * n], refs[2 * n + 1]):
            cp.wait_send()
            cp.wait_recv()

    outs = pl.pallas_call(
        body, name=name, out_shape=tuple(pltpu.HBM(a.shape, a.dtype) for a in srcs + lands),
        in_specs=[_HBM] * (2 * n) + [_SEM, _SEM, _ANY], out_specs=tuple([_HBM] * (2 * n)),
        input_output_aliases={i: i for i in range(2 * n)},
        compiler_params=pltpu.CompilerParams(has_side_effects=_DATAFLOW),
    )(*srcs, *lands, send_sems, recv_sems, after)
    return list(outs[:n]), list(outs[n:])


class _NoExchange:
    def other_weights(self, after):
        return {}

    def send_grads(self, G, names):
        return None

    def send_small_grads(self, G):
        return None


def _after(a, token):
    return a if token is None else a + token[0:1, 0:1].astype(a.dtype)


def _step_local(x, mem, ang2, target, W, hooks=None):
    hooks = hooks or _NoExchange()
    W = dict(W)
    d = x.shape[1]
    wq = HEADS * QK_ROPE
    o_lx, o_ly, o_gm, o_gl = d + wq, 2 * d + wq, 3 * d + wq, 4 * d + wq
    one = lambda a, b, mode, dt, name, **kw: mm([(a, b)], mode, dt, name, **kw)

    h1 = rms_fwd(x, W["g_pre_mix"], "h1")
    proj = one(h1, W["w_main"], "nn", MD, "proj")
    ckv = one(h1, W["w_ckv"], "nn", MD, "ckv")
    kr = one(h1, W["w_kr"], "nn", MD, "k_rope")
    ckv_n = rms_fwd(ckv, W["g_ckv"], "ckv_n")
    kn =one(ckv_n, W["w_uk"], "nn", MD, "k_nope")
    v = one(ckv_n, W["w_uv"], "nn", MD, "v")
    qf, kf = rope_fwd(proj, kr, kn, ang2, "rope")
    o, lse2 = flash_fwd(qf, kf, v, "attn")
    W.update(hooks.other_weights(o))
    o_mla = one(o, W["w_o_mla"], "nn", F32, "o_mla")
    lru_w = (W["w_conv8"], W["b_conv_lru"], W["w_rg"], W["b_rg"], W["w_ig"], W["b_ig"], W["lru_lambda"])
    hr, hg = lru_fwd(proj, o_lx, o_ly, *lru_w, "lru")
    o_lru = one(hg, W["w_o_lru"], "nn", F32, "o_lru")
    merged = merge_fwd(proj, o_gm, o_gl, o_mla, o_lru, "merge")
    y1 = one(merged, W["w_out"], "nn", F32, "y1")
    x1, h2 = post_pre_fwd(x, y1, W["g_post_mix"], W["g_pre_x"], "x1")
    mn = rms_fwd(mem, W["g_mem"], "mem_n")
    ck = one(mn, W["w_ck"], "nn", MD, "ck")
    cv = one(mn, W["w_cv"], "nn", MD, "cv")
    cq = one(h2, W["w_cq"], "nn", MD, "cq")
    co = xattn_fwd(cq, ck, cv, "xattn")
    y2 = one(co, W["w_co"], "nn", F32, "y2")
    x2, h3 = post_pre_fwd(x1, y2, W["g_post_x"], W["g_pre_ffn"], "x2")
    up_pre = one(h3, W["w_up"], "nn", MD, "up_pre")
    act = ffn_act_fwd(up_pre, W["w_fconv8"], W["b_fconv"], "act")
    y3 = one(act, W["w_down"], "nn", F32, "y3", tk=2816)
    G = {}
    loss, dx3, dy3, G["g_post_ffn"] = loss_head(x2, y3, target, W["g_post_ffn"], "loss")

    G["w_down"] = one(act, dy3, "tn", MD, "dw_down")
    sent = hooks.send_grads(G, ("w_down",))
    dact = one(dy3, W["w_down"], "nt", MD, "dact")
    dup = ffn_act_bwd(up_pre, W["w_fconv8"], _after(W["b_fconv"], sent), dact, "dup")
    dup_pre, dwf8, G["b_fconv"] = conv_bwd(up_pre, dup, W["w_fconv8"], FFN_TAPS, 0, "ffn_conv_bwd")
    G["w_fconv"] = dwf8[:FFN_TAPS]
    G["w_up"] = one(h3, dup_pre, "tn", MD, "dw_up")
    sent = hooks.send_grads(G, ("w_up",))
    dh3 = one(dup_pre, W["w_up"], "nt", F32, "dh3", tn=2048, tk=1024)
    dx2, dy2, G["g_pre_ffn"], G["g_post_x"] = pre_post_bwd(x2, y2, dx3, dh3, _after(W["g_pre_ffn"], sent), W["g_post_x"], "dx2")
    G["w_co"] = one(co, dy2, "tn", MD, "dw_co")
    dco = one(dy2, W["w_co"], "nt", MD, "dco")
    dcq, dck, dcv = xattn_bwd(cq, ck, cv, dco, "xattn_bwd")
    dck, dcv = dck.astype(MD), dcv.astype(MD)
    G["w_cq"] = one(h2, dcq, "tn", MD, "dw_cq")
    G["w_ck"] = one(mn, dck, "tn", MD, "dw_ck")
    G["w_cv"] = one(mn, dcv, "tn", MD, "dw_cv")
    sent = hooks.send_grads(G, ("w_co", "w_cq", "w_ck", "w_cv"))
    dmn = mm([(dck, W["w_ck"]), (dcv, W["w_cv"])], "nt", F32, "dmem_n")
    _, G["g_mem"] = rms_bwd(mem, W["g_mem"], dmn, "dg_mem")
    dh2 = one(dcq, W["w_cq"], "nt", F32, "dh2")
    dx1, dy1, G["g_pre_x"], G["g_post_mix"] = pre_post_bwd(x1, y1, dx2, dh2, _after(W["g_pre_x"], sent), W["g_post_mix"], "dx1")
    G["w_out"] = one(merged, dy1, "tn", MD, "dw_out")
    dmerged = one(dy1, W["w_out"], "nt", F32, "dmerged")
    do_mla, do_lru, dgm, dgl = merge_bwd(proj, o_gm, o_gl, o_mla, o_lru, dmerged, "merge_bwd")
    G["w_o_mla"] = one(o, do_mla, "tn", MD, "dw_o_mla")
    do = one(do_mla, W["w_o_mla"], "nt", MD, "do")
    G["w_o_lru"] = one(hg, do_lru, "tn", MD, "dw_o_lru")
    sent = hooks.send_grads(G, ("w_out", "w_o_mla", "w_o_lru"))
    dhg = one(do_lru, W["w_o_lru"], "nt", F32, "dhg")
    lru_w = lru_w[:-1] + (_after(lru_w[-1], sent),)
    dxc, dly, G["w_rg"], G["w_ig"], G["b_rg"], G["b_ig"], G["lru_lambda"] = lru_bwd(proj, o_lx, o_ly, hr, dhg, *lru_w, "lru_bwd")
    dlx, dwc8, G["b_conv_lru"] = conv_bwd(proj, dxc, W["w_conv8"], LRU_TAPS, o_lx, "lru_conv_bwd")
    G["w_conv_lru"] = dwc8[:LRU_TAPS]
    delta = attn_delta(o, do, "attn_delta")
    dqf, dkf, dv = flash_bwd(qf, kf, v, do, lse2, delta, "attn_bwd")
    dqn, dqr, dkn, dkr = rope_bwd(dqf, dkf, ang2, "rope_bwd")
    G["w_uk"] = one(ckv_n, dkn, "tn", MD, "dw_uk")
    G["w_uv"] = one(ckv_n, dv, "tn", MD, "dw_uv")
    dckv_n = mm([(dkn, W["w_uk"]), (dv, W["w_uv"])], "nt", F32, "dckv_n")
    dckv, G["g_ckv"] = rms_bwd(ckv, W["g_ckv"], dckv_n, "dckv")
    dckv = _after(dckv, hooks.send_small_grads(G))
    dproj = jnp.concatenate([dqn, dqr, dlx, dly, dgm, dgl], axis=1)
    parts = [("w_main", dproj), ("w_ckv", dckv), ("w_kr", dkr)]
    for n, dpart in parts:
        G[n] = one(h1, dpart, "tn", MD, "d" + n)
    sent = hooks.send_grads(G, ("w_ukv", "w_in"))
    parts[-1] = ("w_kr", _after(dkr, sent))
    dh1 = mm([(dpart, W[n]) for n, dpart in parts], "nt", F32, "dh1", tm=512, tn=2048, tk=1024)
    grad_x, G["g_pre_mix"] = pre_bwd(x, dx1, dh1, W["g_pre_mix"], "grad_x")
    return loss, grad_x, G


FLAT_COLS = 1024
BIG = (("w_in", 1), ("w_ukv", 1), ("w_o_mla", 0), ("w_o_lru", 0), ("w_out", 0), ("w_cq", 0), ("w_ck", 0), ("w_cv", 0),
       ("w_co", 1), ("w_up", 1), ("w_down", 0))
SMALL_SHARDED = ("w_conv_lru", "w_fconv")
REPLICATED = ("g_pre_mix", "g_post_mix", "g_ckv", "b_conv_lru", "w_rg", "b_rg", "w_ig", "b_ig", "lru_lambda",
              "g_pre_x", "g_post_x", "g_mem", "g_pre_ffn", "g_post_ffn", "b_fconv")
WEIGHTS = ("g_pre_mix", "g_post_mix", "w_in", "g_ckv", "w_ukv", "w_o_mla", "w_conv_lru", "b_conv_lru", "w_rg", "b_rg",
           "w_ig", "b_ig", "lru_lambda", "w_o_lru", "w_out", "g_pre_x", "g_post_x", "g_mem", "w_cq", "w_ck", "w_cv",
           "w_co", "g_pre_ffn", "g_post_ffn", "w_up", "w_fconv", "b_fconv", "w_down")


def _round_up(n, k):
    return (n + k - 1) // k * k


def _pack_vec(parts, row_mult):
    flat = jnp.concatenate([p.reshape(-1) for p in parts])
    n = flat.shape[0]
    total = _round_up(n, row_mult * FLAT_COLS)
    return jnp.pad(flat, (0, total - n)).reshape(-1, FLAT_COLS)


def _unpack_vec(flat2d, shapes):
    flat = flat2d.reshape(-1)
    out, off = [], 0
    for shp in shapes:
        n = int(np.prod(shp))
        out.append(flat[off:off + n].reshape(shp))
        off += n
    return out


def _gathered_to_full(blocks, axis):
    n, r, c = blocks.shape
    return blocks.reshape(n * r, c) if axis == 0 else blocks.transpose(1, 0, 2).reshape(r, n * c)


def _full_to_shards(full, axis):
    r, c = full.shape
    if axis == 0:
        return full.reshape(N_DEV, r // N_DEV, c)
    return full.reshape(r, N_DEV, c // N_DEV).transpose(1, 0, 2)


def _split_w_in(w_in, d):
    hq = HEADS * (QK_NOPE + QK_ROPE)
    q = w_in[:, :hq].reshape(d, HEADS, QK_NOPE + QK_ROPE)
    q_nope, q_rope = q[:, :, :QK_NOPE].reshape(d, HEADS * QK_NOPE), q[:, :, QK_NOPE:].reshape(d, HEADS * QK_ROPE)
    out = {"w_ckv": w_in[:, hq:hq + KV_RANK]}
    off = hq + KV_RANK
    out["w_kr"] = jnp.pad(w_in[:, off:off + QK_ROPE], ((0, 0), (0, LANES - QK_ROPE)))
    out["w_main"] = jnp.concatenate([q_nope, q_rope, w_in[:, off + QK_ROPE:]], axis=1)
    return out


def _join_w_in(G, d):
    gm = G["w_main"]
    n_qn, n_qr = HEADS * QK_NOPE, HEADS * QK_ROPE
    q = jnp.concatenate([gm[:, :n_qn].reshape(d, HEADS, QK_NOPE), gm[:, n_qn:n_qn + n_qr].reshape(d, HEADS, QK_ROPE)], axis=2)
    return jnp.concatenate([q.reshape(d, -1), G["w_ckv"], G["w_kr"][:, :QK_ROPE], gm[:, n_qn + n_qr:]], axis=1)


def kernel(x, mem, positions, g_pre_mix, g_post_mix, w_in, g_ckv, w_ukv, w_o_mla, w_conv_lru, b_conv_lru, w_rg, b_rg, w_ig, b_ig, lru_lambda, w_o_lru, w_out, g_pre_x, g_post_x, g_mem, w_cq, w_ck, w_cv, w_co, g_pre_ffn, g_post_ffn, w_up, w_fconv, b_fconv, w_down, loss_target, m_g_pre_mix, m_g_post_mix, m_w_in, m_g_ckv, m_w_ukv, m_w_o_mla, m_w_conv_lru, m_b_conv_lru, m_w_rg, m_b_rg, m_w_ig, m_b_ig, m_lru_lambda, m_w_o_lru, m_w_out, m_g_pre_x, m_g_post_x, m_g_mem, m_w_cq, m_w_ck, m_w_cv, m_w_co, m_g_pre_ffn, m_g_post_ffn, m_w_up, m_w_fconv, m_b_fconv, m_w_down, v_g_pre_mix, v_g_post_mix, v_w_in, v_g_ckv, v_w_ukv, v_w_o_mla, v_w_conv_lru, v_b_conv_lru, v_w_rg, v_b_rg, v_w_ig, v_b_ig, v_lru_lambda, v_w_o_lru, v_w_out, v_g_pre_x, v_g_post_x, v_g_mem, v_w_cq, v_w_ck, v_w_cv, v_w_co, v_g_pre_ffn, v_g_post_ffn, v_w_up, v_w_fconv, v_b_fconv, v_w_down):
    given = dict(locals())
    P = {n: given[n][0] for n in WEIGHTS}
    M = {n: given["m_" + n][0] for n in WEIGHTS}
    V = {n: given["v_" + n][0] for n in WEIGHTS}
    xs, mems, tgt = x[0], mem[0], loss_target[0]
    s, d = xs.shape
    ax, ay, ac = _place()
    dev = 4 * ax + 2 * ay + ac

    axis_of = dict(BIG)
    axis_of.update({n: 1 for n in SMALL_SHARDED})
    wire = lambda n: P[n] if n in SMALL_SHARDED else P[n].astype(MD)
    gathered_in, gathered_kv = all_gather([wire("w_in"), wire("w_ukv")], "gather_first")
    later_names = ("w_o_mla", "w_conv_lru", "w_o_lru", "w_out", "w_ck", "w_cv", "w_cq", "w_co", "w_up", "w_fconv", "w_down")
    gathered_in, later_srcs = lax.optimization_barrier((gathered_in, [wire(n) for n in later_names]))
    later_handle, started_later = send_start("gather", later_srcs, "start_gather_rest")

    def landed(names, handle, after, name):
        srcs, lands = send_wait(handle, after, name)
        full = {}
        for n, src, land in zip(names, srcs, lands):
            full[n] = _gathered_to_full(lax.dynamic_update_slice(land, src[None], (dev, 0, 0)), axis_of[n])
        return full

    W = _split_w_in(_gathered_to_full(gathered_in, 1), d)
    W["w_rg"], W["w_ig"] = P["w_rg"].astype(MD), P["w_ig"].astype(MD)
    for n in ("g_pre_mix", "g_post_mix", "g_ckv", "b_conv_lru", "b_rg", "b_ig", "lru_lambda", "g_pre_x", "g_post_x", "g_mem",
              "g_pre_ffn", "g_post_ffn", "b_fconv"):
        W[n] = P[n].reshape(1, -1)
    kv = _gathered_to_full(gathered_kv, 1).reshape(KV_RANK, HEADS, QK_NOPE + V_DIM)
    W["w_uk"] = kv[:, :, :QK_NOPE].reshape(KV_RANK, HEADS * QK_NOPE)
    W["w_uv"] = kv[:, :, QK_NOPE:].reshape(KV_RANK, HEADS * V_DIM)
    W["g_pre_mix"] = _after(W["g_pre_mix"], started_later)

    inv_freq = ROPE_THETA ** (-jnp.arange(0, QK_ROPE, 2, dtype=F32) / QK_ROPE)
    ang = positions[0].astype(F32)[:, None] * inv_freq
    ang2 = jnp.tile(ang, (1, LANES // (QK_ROPE // 2)))

    pending = []
    small_pending = []
    small_names = REPLICATED + SMALL_SHARDED
    early_small = tuple(n for n in small_names if n != "g_pre_mix")

    class Exchange:
        def other_weights(self, after):
            full = landed(later_names, later_handle, after, "wait_gather_rest")
            full["w_conv8"] = jnp.pad(full.pop("w_conv_lru"), ((0, 8 - LRU_TAPS), (0, 0)))
            full["w_fconv8"] = jnp.pad(full.pop("w_fconv"), ((0, 8 - FFN_TAPS), (0, 0)))
            return full

        def send_grads(self, G, names):
            if "w_in" in names:
                G["w_in"] = _join_w_in(G, d)
                G["w_ukv"] = jnp.concatenate([G["w_uk"].reshape(KV_RANK, HEADS, QK_NOPE),
                                              G["w_uv"].reshape(KV_RANK, HEADS, V_DIM)], axis=2).reshape(KV_RANK, -1)
            handle, started = send_start("scatter", [_full_to_shards(G[n], axis_of[n]) for n in names], "start_grads_" + names[0])
            pending.append((names, handle))
            return started

        def send_small_grads(self, G):
            handle, started = send_start("gather", [_pack_vec([G[n] for n in early_small], 128)], "start_small_grads")
            small_pending.append(handle)
            return started

    loss, grad_x, G = _step_local(xs, mems, ang2, tgt, W, Exchange())
    loss = lax.psum(loss[0, 0], ("x", "y", "c"))

    res = {k: {} for k in ("grad", "delta", "m", "v")}
    for names, handle in pending:
        slabs, peers = send_wait(handle, grad_x, "wait_grads_" + names[0])
        for n, slab, peer in zip(names, slabs, peers):
            mine = lax.dynamic_index_in_dim(slab, dev, axis=0, keepdims=False)
            outs = adamw_sharded(mine, peer, P[n], M[n], V[n], "adamw_" + n)
            for kind, o in zip(("grad", "delta", "m", "v"), outs):
                res[kind][n] = o

    (own,), (land,) = send_wait(small_pending[0], grad_x, "wait_small_grads")
    partials = lax.dynamic_update_slice(land, own[None], (dev, 0, 0))
    summed = dict(zip(early_small, _unpack_vec(sum_devices(partials, "sum_small_grads"), [G[n].shape for n in early_small])))
    last, = all_gather([_pack_vec([G["g_pre_mix"]], 8)], "gather_last_grad")
    summed["g_pre_mix"], = _unpack_vec(sum_devices(last, "sum_last_grad"), [G["g_pre_mix"].shape])
    for n in SMALL_SHARDED:
        t, c = summed[n].shape
        summed[n] = lax.dynamic_index_in_dim(summed[n].reshape(t, N_DEV, c // N_DEV), dev, axis=1, keepdims=False)
    sg = {n: summed[n].reshape(P[n].shape) for n in small_names}
    vec = lambda src: _pack_vec([src[n] for n in small_names], 128)
    small_out = adamw_plain(vec(sg), vec(P), vec(M), vec(V), "adamw_small")
    shapes = [P[n].shape for n in small_names]
    for kind, flat in zip(("delta", "m", "v"), small_out):
        res[kind].update(zip(small_names, _unpack_vec(flat, shapes)))
    res["grad"].update(sg)

    outs = [loss, grad_x[None]]
    for kind in ("grad", "delta", "m", "v"):
        outs += [res[kind][n][None] for n in WEIGHTS]
    return tuple(outs)
```

```python
import numpy as np
import jax
import jax.numpy as jnp
from jax import lax
from jax.experimental import pallas as pl
from jax.experimental.pallas import tpu as pltpu

MD = jnp.bfloat16
F32 = jnp.float32
EPS = 1e-6
N_DEV = 8
HEADS = 16
QK_NOPE = 128
QK_ROPE = 64
V_DIM = 128
QK_PAD = 256
KV_RANK = 512
SOFTMAX_SCALE = (QK_NOPE + QK_ROPE) ** -0.5
LOG2E = 1.4426950408889634
ROPE_THETA = 10000.0
LRU_BLOCKS = 16
LRU_C = 8.0
X_HEADS = 4
X_DIM = 128
LRU_TAPS = 4
FFN_TAPS = 3
ADAM_LR, ADAM_B1, ADAM_B2, ADAM_EPS, ADAM_WD, ADAM_STEP = 0.001, 0.9, 0.999, 1e-08, 0.01, 10
LANES = 128
VMEM_LIMIT = 52 * 1024 * 1024
MESH = pl.DeviceIdType.MESH

NN = (((1,), (0,)), ((), ()))
NT = (((1,), (1,)), ((), ()))
TN = (((0,), (0,)), ((), ()))


def _tile(n, pref, align=LANES):
    if n <= pref:
        return n
    d = (pref // align) * align
    while d > align and n % d:
        d -= align
    assert n % d == 0, (n, pref, align)
    return d


def _params(sem):
    return pltpu.CompilerParams(dimension_semantics=sem, vmem_limit_bytes=VMEM_LIMIT)


def _dot(a, b, dims=NN):
    return lax.dot_general(a, b, dims, preferred_element_type=F32)


def _sigmoid(x):
    return 1.0 / (1.0 + jnp.exp(-x))


_GELU_C = 0.7978845608028654
_GELU_A = 0.044715


def _gelu(x):
    return 0.5 * x * (1.0 + jnp.tanh(x * (_GELU_C + (_GELU_C * _GELU_A) * (x * x))))


def _gelu_and_grad(x):
    x2 = x * x
    t = jnp.tanh(x * (_GELU_C + (_GELU_C * _GELU_A) * x2))
    hx = 0.5 * x
    g = hx + hx * t
    dg = 0.5 + 0.5 * t + hx * (1.0 - t * t) * (_GELU_C + (3.0 * _GELU_C * _GELU_A) * x2)
    return g, dg


def _expm1(x):
    small = x * (1.0 + x * (0.5 + x * (1.0 / 6.0 + x * (1.0 / 24.0 + x * (1.0 / 120.0)))))
    return jnp.where(jnp.abs(x) < 0.05, small, jnp.exp(x) - 1.0)


def _softplus(x):
    return jnp.maximum(x, 0.0) + jnp.log(1.0 + jnp.exp(-jnp.abs(x)))


def _rms_hat(x):
    r = lax.rsqrt(jnp.mean(x * x, axis=-1, keepdims=True) + EPS)
    return x * r, r


def _rms_bwd(x, g, dn):
    xh, r = _rms_hat(x)
    dg = jnp.sum(dn * xh, axis=0, keepdims=True)
    dxh = dn * g
    dx = r * (dxh - xh * jnp.mean(dxh * xh, axis=-1, keepdims=True))
    return dx, dg


def _acc_out(ref, val, first):
    @pl.when(first)
    def _():
        ref[...] = val

    @pl.when(jnp.logical_not(first))
    def _():
        ref[...] += val


def _lane_sums_as_row(x):
    ones = jnp.ones((8, LANES), F32)
    return lax.dot_general(ones, x, NT, precision=lax.Precision.HIGHEST, preferred_element_type=F32)[0:1, :]


def mm(pairs, mode, out_dtype, name, tm=1024, tn=1024, tk=2048):
    dims = {"nn": NN, "nt": NT, "tn": TN}[mode]
    shapes = []
    for a, b in pairs:
        if mode == "nn":
            (m, k), (k2, n) = a.shape, b.shape
        elif mode == "nt":
            (m, k), (n, k2) = a.shape, b.shape
        else:
            (k, m), (k2, n) = a.shape, b.shape
        assert k == k2, (name, a.shape, b.shape)
        shapes.append((m, n, k))
    m, n = shapes[0][0], shapes[0][1]
    assert all(s[0] == m and s[1] == n for s in shapes)
    tm, tn = _tile(m, tm), _tile(n, tn)
    tks = [_tile(s[2], tk) for s in shapes]
    nks = [s[2] // t for s, t in zip(shapes, tks)]
    starts = [sum(nks[:p]) for p in range(len(pairs))]
    nk_total = sum(nks)

    in_specs, args = [], []
    for p, (a, b) in enumerate(pairs):
        def kk(k, p=p):
            return jnp.clip(k - starts[p], 0, nks[p] - 1)
        if mode == "tn":
            in_specs.append(pl.BlockSpec((tks[p], tm), lambda i, j, k, kk=kk: (kk(k), i)))
        else:
            in_specs.append(pl.BlockSpec((tm, tks[p]), lambda i, j, k, kk=kk: (i, kk(k))))
        if mode == "nt":
            in_specs.append(pl.BlockSpec((tn, tks[p]), lambda i, j, k, kk=kk: (j, kk(k))))
        else:
            in_specs.append(pl.BlockSpec((tks[p], tn), lambda i, j, k, kk=kk: (kk(k), j)))
        args += [a, b]

    def body(*refs):
        ins, o_ref, scratch = refs[:2 * len(pairs)], refs[2 * len(pairs)], refs[2 * len(pairs) + 1:]
        if nk_total == 1:
            o_ref[...] = _dot(ins[0][...], ins[1][...], dims).astype(o_ref.dtype)
            return
        acc_ref, = scratch
        k = pl.program_id(2)
        for p in range(len(pairs)):
            @pl.when(jnp.logical_and(k >= starts[p], k < starts[p] + nks[p]))
            def _(p=p):
                prod = _dot(ins[2 * p][...], ins[2 * p + 1][...], dims)

                @pl.when(k == 0)
                def _():
                    acc_ref[...] = prod

                @pl.when(k > 0)
                def _():
                    acc_ref[...] += prod

        @pl.when(k == nk_total - 1)
        def _():
            o_ref[...] = acc_ref[...].astype(o_ref.dtype)

    return pl.pallas_call(
        body, name=name, grid=(m // tm, n // tn, nk_total),
        in_specs=in_specs, out_specs=pl.BlockSpec((tm, tn), lambda i, j, k: (i, j)),
        out_shape=jax.ShapeDtypeStruct((m, n), out_dtype),
        scratch_shapes=[] if nk_total == 1 else [pltpu.VMEM((tm, tn), F32)],
        compiler_params=_params(("parallel", "parallel", "arbitrary")),
    )(*args)


def rms_fwd(x, g, name):
    s, d = x.shape
    ts = _tile(s, 512, 16)

    def body(x_ref, g_ref, o_ref):
        xh, _ = _rms_hat(x_ref[...].astype(F32))
        o_ref[...] = (xh * g_ref[...]).astype(o_ref.dtype)

    return pl.pallas_call(
        body, name=name, grid=(s // ts,),
        in_specs=[pl.BlockSpec((ts, d), lambda i: (i, 0)), pl.BlockSpec((1, d), lambda i: (0, 0))],
        out_specs=pl.BlockSpec((ts, d), lambda i: (i, 0)),
        out_shape=jax.ShapeDtypeStruct((s, d), MD),
        compiler_params=_params(("parallel",)),
    )(x, g)


def rms_bwd(x, g, dn, name):
    s, d = x.shape
    ts = _tile(s, 256, 16)

    def body(x_ref, g_ref, dn_ref, dx_ref, dg_ref):
        dx, dg = _rms_bwd(x_ref[...].astype(F32), g_ref[...], dn_ref[...].astype(F32))
        dx_ref[...] = dx.astype(dx_ref.dtype)
        _acc_out(dg_ref, dg, pl.program_id(0) == 0)

    row = pl.BlockSpec((ts, d), lambda i: (i, 0))
    vec = pl.BlockSpec((1, d), lambda i: (0, 0))
    return pl.pallas_call(
        body, name=name, grid=(s // ts,), in_specs=[row, vec, row], out_specs=[row, vec],
        out_shape=[jax.ShapeDtypeStruct((s, d), MD), jax.ShapeDtypeStruct((1, d), F32)],
        compiler_params=_params(("arbitrary",)),
    )(x, g, dn)


def post_pre_fwd(x, y, g_post, g_pre, name):
    s, d = x.shape
    ts = _tile(s, 256, 16)

    def body(x_ref, y_ref, gp_ref, gq_ref, xn_ref, h_ref):
        yh, _ = _rms_hat(y_ref[...])
        xn = x_ref[...] + yh * gp_ref[...]
        xn_ref[...] = xn
        xh, _ = _rms_hat(xn)
        h_ref[...] = (xh * gq_ref[...]).astype(h_ref.dtype)

    row = pl.BlockSpec((ts, d), lambda i: (i, 0))
    vec = pl.BlockSpec((1, d), lambda i: (0, 0))
    return pl.pallas_call(
        body, name=name, grid=(s // ts,), in_specs=[row, row, vec, vec], out_specs=[row, row],
        out_shape=[jax.ShapeDtypeStruct((s, d), F32), jax.ShapeDtypeStruct((s, d), MD)],
        compiler_params=_params(("parallel",)),
    )(x, y, g_post, g_pre)


def loss_head(x, y, target, g_post, name):
    s, d = x.shape
    ts = _tile(s, 256, 16)

    def body(x_ref, y_ref, t_ref, g_ref, loss_ref, dx_ref, dy_ref, dg_ref):
        first = pl.program_id(0) == 0
        y = y_ref[...]
        yh, _ = _rms_hat(y)
        diff = x_ref[...] + yh * g_ref[...] - t_ref[...]
        part = 0.5 * jnp.sum(jnp.sum(diff * diff, axis=1, keepdims=True) * (1.0 / d), axis=0, keepdims=True)
        _acc_out(loss_ref, part, first)
        dx = diff * (1.0 / d)
        dx_ref[...] = dx
        dy, dg = _rms_bwd(y, g_ref[...], dx)
        dy_ref[...] = dy.astype(dy_ref.dtype)
        _acc_out(dg_ref, dg, first)

    row = pl.BlockSpec((ts, d), lambda i: (i, 0))
    vec = pl.BlockSpec((1, d), lambda i: (0, 0))
    one = pl.BlockSpec((1, 1), lambda i: (0, 0))
    return pl.pallas_call(
        body, name=name, grid=(s // ts,), in_specs=[row, row, row, vec], out_specs=[one, row, row, vec],
        out_shape=[jax.ShapeDtypeStruct((1, 1), F32), jax.ShapeDtypeStruct((s, d), F32),
                   jax.ShapeDtypeStruct((s, d), MD), jax.ShapeDtypeStruct((1, d), F32)],
        compiler_params=_params(("arbitrary",)),
    )(x, y, target, g_post)


def pre_post_bwd(x, y, dx_res, dh, g_pre, g_post, name):
    s, d = x.shape
    ts = _tile(s, 256, 16)

    def body(x_ref, y_ref, dr_ref, dh_ref, gq_ref, gp_ref, dx_ref, dy_ref, dgq_ref, dgp_ref):
        first = pl.program_id(0) == 0
        dxa, dgq = _rms_bwd(x_ref[...], gq_ref[...], dh_ref[...])
        dx = dr_ref[...] + dxa
        dx_ref[...] = dx
        dy, dgp = _rms_bwd(y_ref[...], gp_ref[...], dx)
        dy_ref[...] = dy.astype(dy_ref.dtype)
        _acc_out(dgq_ref, dgq, first)
        _acc_out(dgp_ref, dgp, first)

    row = pl.BlockSpec((ts, d), lambda i: (i, 0))
    vec = pl.BlockSpec((1, d), lambda i: (0, 0))
    return pl.pallas_call(
        body, name=name, grid=(s // ts,), in_specs=[row, row, row, row, vec, vec], out_specs=[row, row, vec, vec],
        out_shape=[jax.ShapeDtypeStruct((s, d), F32), jax.ShapeDtypeStruct((s, d), MD),
                   jax.ShapeDtypeStruct((1, d), F32), jax.ShapeDtypeStruct((1, d), F32)],
        compiler_params=_params(("arbitrary",)),
    )(x, y, dx_res, dh, g_pre, g_post)


def pre_bwd(x, dx_res, dh, g_pre, name):
    s, d = x.shape
    ts = _tile(s, 256, 16)

    def body(x_ref, dr_ref, dh_ref, g_ref, dx_ref, dg_ref):
        dxa, dg = _rms_bwd(x_ref[...], g_ref[...], dh_ref[...])
        dx_ref[...] = dr_ref[...] + dxa
        _acc_out(dg_ref, dg, pl.program_id(0) == 0)

    row = pl.BlockSpec((ts, d), lambda i: (i, 0))
    vec = pl.BlockSpec((1, d), lambda i: (0, 0))
    return pl.pallas_call(
        body, name=name, grid=(s // ts,), in_specs=[row, row, row, vec], out_specs=[row, vec],
        out_shape=[jax.ShapeDtypeStruct((s, d), F32), jax.ShapeDtypeStruct((1, d), F32)],
        compiler_params=_params(("arbitrary",)),
    )(x, dx_res, dh, g_pre)


def _swap_halves(x):
    lane = lax.broadcasted_iota(jnp.int32, x.shape, 1)
    return jnp.where((lane % QK_ROPE) < QK_ROPE // 2, pltpu.roll(x, LANES - QK_ROPE // 2, 1),
                     pltpu.roll(x, QK_ROPE // 2, 1))


def _rope_tables(ang_ref, sign):
    lane = lax.broadcasted_iota(jnp.int32, ang_ref.shape, 1)
    sgn = jnp.where((lane % QK_ROPE) < QK_ROPE // 2, -sign, sign)
    ang = ang_ref[...]
    return jnp.cos(ang), jnp.sin(ang) * sgn


def rope_fwd(proj, kr, kn, ang2, name):
    s = proj.shape[0]
    ts = _tile(s, 256, 16)
    wn, wr = HEADS * QK_NOPE, HEADS * QK_ROPE

    def body(qn_ref, qr_ref, kr_ref, kn_ref, ang_ref, qo_ref, ko_ref):
        c, sn = _rope_tables(ang_ref, 1.0)
        lane = lax.broadcasted_iota(jnp.int32, (ts, LANES), 1)
        lo = lane < QK_ROPE
        x = kr_ref[...].astype(F32)
        k_rot = jnp.where(lo, x * c + _swap_halves(x) * sn, 0.0).astype(ko_ref.dtype)
        for j in range(HEADS // 2):
            x = qr_ref[:, j * LANES:(j + 1) * LANES].astype(F32)
            r = x * c + _swap_halves(x) * sn
            pair = (jnp.where(lo, r, 0.0), jnp.where(lo, pltpu.roll(r, QK_ROPE, 1), 0.0))
            for e in range(2):
                h = 2 * j + e
                qo_ref[:, h * QK_PAD:h * QK_PAD + LANES] = qn_ref[:, h * LANES:(h + 1) * LANES]
                qo_ref[:, h * QK_PAD + LANES:(h + 1) * QK_PAD] = pair[e].astype(qo_ref.dtype)
                ko_ref[:, h * QK_PAD:h * QK_PAD + LANES] = kn_ref[:, h * LANES:(h + 1) * LANES]
                ko_ref[:, h * QK_PAD + LANES:(h + 1) * QK_PAD] = k_rot

    row = lambda w, blk: pl.BlockSpec((ts, w), lambda i: (i, blk))
    return pl.pallas_call(
        body, name=name, grid=(s // ts,),
        in_specs=[row(wn, 0), row(wr, wn // wr), row(LANES, 0), row(wn, 0), row(LANES, 0)],
        out_specs=[row(HEADS * QK_PAD, 0), row(HEADS * QK_PAD, 0)],
        out_shape=[jax.ShapeDtypeStruct((s, HEADS * QK_PAD), MD), jax.ShapeDtypeStruct((s, HEADS * QK_PAD), MD)],
        compiler_params=_params(("parallel",)),
    )(proj, proj, kr, kn, ang2)


def rope_bwd(dqf, dkf, ang2, name):
    s = dqf.shape[0]
    ts = _tile(s, 256, 16)
    wn, wr = HEADS * QK_NOPE, HEADS * QK_ROPE

    def body(dq_ref, dk_ref, ang_ref, qn_ref, qr_ref, kn_ref, kr_ref):
        c, sn = _rope_tables(ang_ref, -1.0)
        lane = lax.broadcasted_iota(jnp.int32, (ts, LANES), 1)
        lo = lane < QK_ROPE
        rot = lambda dy: dy * c + _swap_halves(dy) * sn
        dk_rope = None
        for j in range(HEADS // 2):
            halves = []
            for e in range(2):
                h = 2 * j + e
                qn_ref[:, h * LANES:(h + 1) * LANES] = dq_ref[:, h * QK_PAD:h * QK_PAD + LANES]
                kn_ref[:, h * LANES:(h + 1) * LANES] = dk_ref[:, h * QK_PAD:h * QK_PAD + LANES]
                halves.append(jnp.where(lo, dq_ref[:, h * QK_PAD + LANES:(h + 1) * QK_PAD].astype(F32), 0.0))
                dk_h = dk_ref[:, h * QK_PAD + LANES:(h + 1) * QK_PAD].astype(F32)
                dk_rope = dk_h if dk_rope is None else dk_rope + dk_h
            qr_ref[:, j * LANES:(j + 1) * LANES] = rot(halves[0] + pltpu.roll(halves[1], QK_ROPE, 1)).astype(qr_ref.dtype)
        kr_ref[...] = jnp.where(lo, rot(jnp.where(lo, dk_rope, 0.0)), 0.0).astype(kr_ref.dtype)

    row = lambda w: pl.BlockSpec((ts, w), lambda i: (i, 0))
    return pl.pallas_call(
        body, name=name, grid=(s // ts,),
        in_specs=[row(HEADS * QK_PAD), row(HEADS * QK_PAD), row(LANES)],
        out_specs=[row(wn), row(wr), row(wn), row(LANES)],
        out_shape=[jax.ShapeDtypeStruct((s, wn), MD), jax.ShapeDtypeStruct((s, wr), MD),
                   jax.ShapeDtypeStruct((s, wn), MD), jax.ShapeDtypeStruct((s, LANES), MD)],
        compiler_params=_params(("parallel",)),
    )(dqf, dkf, ang2)


_EXP2_SCALE = SOFTMAX_SCALE * LOG2E


def flash_fwd(qf, kf, v, name):
    s = qf.shape[0]
    t = _tile(s, 1024)
    nq = s // t
    half = t // 2
    pairs = [(i, j) for i in range(nq) for j in range(i + 1)]
    it = jnp.asarray(np.array([p[0] for p in pairs], np.int32))
    jt = jnp.asarray(np.array([p[1] for p in pairs], np.int32))
    tq = tk = t

    def body(it_ref, jt_ref, q_ref, k_ref, v_ref, o_ref, lse_ref, m_s, l_s, acc_s):
        p = pl.program_id(1)
        i, j = it_ref[p], jt_ref[p]

        @pl.when(j == 0)
        def _():
            m_s[...] = jnp.full(m_s.shape, -jnp.inf, F32)
            l_s[...] = jnp.zeros(l_s.shape, F32)
            acc_s[...] = jnp.zeros(acc_s.shape, F32)

        def step(q0, nq_rows, nk_rows, masked):
            rows, keys = pl.ds(q0, nq_rows), pl.ds(0, nk_rows)
            sc = _dot(q_ref[rows, :], k_ref[keys, :], NT)
            if masked:
                row = lax.broadcasted_iota(jnp.int32, (nq_rows, nk_rows), 0) + q0
                col = lax.broadcasted_iota(jnp.int32, (nq_rows, nk_rows), 1)
                sc = jnp.where(col <= row, sc, -jnp.inf)
            groups = nk_rows // LANES
            cols = [sc[:, g * LANES:(g + 1) * LANES] for g in range(groups)]
            mx = cols[0]
            for g in range(1, groups):
                mx = jnp.maximum(mx, cols[g])
            m_prev = m_s[rows, :]
            m_new = jnp.maximum(m_prev, jnp.max(mx, axis=1, keepdims=True))
            alpha = jnp.exp2((m_prev - m_new) * _EXP2_SCALE)
            ps, psum = [], None
            for g in range(groups):
                pg = jnp.exp2((cols[g] - m_new) * _EXP2_SCALE)
                psum = pg if psum is None else psum + pg
                ps.append(pg.astype(MD))
            l_s[rows, :] = alpha * l_s[rows, :] + psum
            acc_s[rows, :] = acc_s[rows, :] * alpha + _dot(jnp.concatenate(ps, axis=1), v_ref[keys, :])
            m_s[rows, :] = m_new

        @pl.when(j == i)
        def _():
            step(0, half, half, True)
            step(half, half, t, True)

        @pl.when(j != i)
        def _():
            step(0, t, t, False)

        @pl.when(j == i)
        def _():
            l = jnp.sum(l_s[...], axis=1, keepdims=True)
            o_ref[...] = (acc_s[...] / l).astype(o_ref.dtype)
            lse_rep = m_s[...] * _EXP2_SCALE + jnp.log(l) * LOG2E
            lse_ref[...] = _lane_sums_as_row(lse_rep * (1.0 / LANES))

    grid_spec = pltpu.PrefetchScalarGridSpec(
        num_scalar_prefetch=2, grid=(HEADS, len(pairs)),
        in_specs=[pl.BlockSpec((tq, QK_PAD), lambda h, p, it, jt: (it[p], h)),
                  pl.BlockSpec((tk, QK_PAD), lambda h, p, it, jt: (jt[p], h)),
                  pl.BlockSpec((tk, V_DIM), lambda h, p, it, jt: (jt[p], h))],
        out_specs=[pl.BlockSpec((tq, V_DIM), lambda h, p, it, jt: (it[p], h)),
                   pl.BlockSpec((None, 1, tq), lambda h, p, it, jt: (h, 0, it[p]))],
        scratch_shapes=[pltpu.VMEM((tq, LANES), F32), pltpu.VMEM((tq, LANES), F32), pltpu.VMEM((tq, V_DIM), F32)])
    return pl.pallas_call(
        body, name=name, grid_spec=grid_spec,
        out_shape=[jax.ShapeDtypeStruct((s, HEADS * V_DIM), MD), jax.ShapeDtypeStruct((HEADS, 1, s), F32)],
        compiler_params=_params(("parallel", "arbitrary")),
    )(it, jt, qf, kf, v)


def flash_bwd(qf, kf, v, o, do, lse2, name):
    s = qf.shape[0]
    t = _tile(s, 1024)
    tq = tk = t
    nq = s // t
    half = t // 2
    pairs = [(i, j) for j in range(nq) for i in range(j, nq)]
    it = jnp.asarray(np.array([p[0] for p in pairs], np.int32))
    jt = jnp.asarray(np.array([p[1] for p in pairs], np.int32))
    n_pairs = len(pairs)

    def body(it_ref, jt_ref, q_ref, k_ref, v_ref, o_ref, do_ref, lse_ref, dq_ref, dk_ref, dv_ref, dq_s, dk_s, dv_s, dl_s):
        p = pl.program_id(1)
        i, j = it_ref[p], jt_ref[p]

        @pl.when(p == 0)
        def _():
            dq_s[...] = jnp.zeros(dq_s.shape, F32)

        @pl.when(j == 0)
        def _():
            dl_s[i] = _lane_sums_as_row(o_ref[...].astype(F32) * do_ref[...].astype(F32))

        @pl.when(i == j)
        def _():
            dk_s[...] = jnp.zeros(dk_s.shape, F32)
            dv_s[...] = jnp.zeros(dv_s.shape, F32)

        def step(k0, nk_rows, q0, nq_rows, masked):
            keys, rows = pl.ds(k0, nk_rows), pl.ds(q0, nq_rows)
            q, k, do_b = q_ref[rows, :], k_ref[keys, :], do_ref[rows, :]
            pt = jnp.exp2(_dot(k, q, NT) * _EXP2_SCALE - lse_ref[:, rows])
            if masked:
                krow = lax.broadcasted_iota(jnp.int32, (nk_rows, nq_rows), 0) + k0
                qcol = lax.broadcasted_iota(jnp.int32, (nk_rows, nq_rows), 1) + q0
                pt = jnp.where(krow <= qcol, pt, 0.0)
            dv_s[keys, :] += _dot(pt.astype(MD), do_b)
            dst = (pt * (_dot(v_ref[keys, :], do_b, NT) - dl_s[i, :, rows])).astype(MD)
            dk_s[keys, :] += _dot(dst, q)
            row0 = pl.multiple_of(i * t + q0, half)
            dq_s[pl.ds(row0, nq_rows), :] += _dot(dst, k, TN)

        @pl.when(i == j)
        def _():
            step(0, half, 0, t, True)
            step(half, half, half, half, True)

        @pl.when(i != j)
        def _():
            step(0, t, 0, t, False)

        @pl.when(i == nq - 1)
        def _():
            dk_ref[...] = (dk_s[...] * SOFTMAX_SCALE).astype(dk_ref.dtype)
            dv_ref[...] = dv_s[...].astype(dv_ref.dtype)

        @pl.when(p == n_pairs - 1)
        def _():
            dq_ref[...] = (dq_s[...] * SOFTMAX_SCALE).astype(dq_ref.dtype)

    vec = pl.BlockSpec((None, 1, tq), lambda h, p, it, jt: (h, 0, it[p]))
    grid_spec = pltpu.PrefetchScalarGridSpec(
        num_scalar_prefetch=2, grid=(HEADS, n_pairs),
        in_specs=[pl.BlockSpec((tq, QK_PAD), lambda h, p, it, jt: (it[p], h)),
                  pl.BlockSpec((tk, QK_PAD), lambda h, p, it, jt: (jt[p], h)),
                  pl.BlockSpec((tk, V_DIM), lambda h, p, it, jt: (jt[p], h)),
                  pl.BlockSpec((tq, V_DIM), lambda h, p, it, jt: (it[p], h)),
                  pl.BlockSpec((tq, V_DIM), lambda h, p, it, jt: (it[p], h)), vec],
        out_specs=[pl.BlockSpec((s, QK_PAD), lambda h, p, it, jt: (0, h)),
                   pl.BlockSpec((tk, QK_PAD), lambda h, p, it, jt: (jt[p], h)),
                   pl.BlockSpec((tk, V_DIM), lambda h, p, it, jt: (jt[p], h))],
        scratch_shapes=[pltpu.VMEM((s, QK_PAD), F32), pltpu.VMEM((tk, QK_PAD), F32), pltpu.VMEM((tk, V_DIM), F32),
                        pltpu.VMEM((nq, 1, t), F32)])
    wide = jax.ShapeDtypeStruct((s, HEADS * QK_PAD), MD)
    return pl.pallas_call(
        body, name=name, grid_spec=grid_spec,
        out_shape=[wide, wide, jax.ShapeDtypeStruct((s, HEADS * V_DIM), MD)],
        compiler_params=_params(("parallel", "arbitrary")),
    )(it, jt, qf, kf, v, o, do, lse2)


def _halo_rows(dtype):
    return 8 if jnp.dtype(dtype).itemsize == 4 else 16


def _prev_spec(ts, tc, hr, col_off):
    return pl.BlockSpec((hr, tc), lambda j, i: (jnp.maximum(i * (ts // hr) - 1, 0), j + col_off))


def _conv_taps(x, halo, w, taps, first):
    hr, ts = halo.shape[0], x.shape[0]
    xe = jnp.concatenate([jnp.where(first, 0.0, halo), x], axis=0)
    shifted = []
    acc = None
    for k in range(taps):
        sh = taps - 1 - k
        xs = xe if sh == 0 else pltpu.roll(xe, sh, 0)
        xs = xs[hr:hr + ts, :]
        shifted.append(xs)
        term = xs * w[k:k + 1, :]
        acc = term if acc is None else acc + term
    return acc, shifted


def conv_bwd(x, dy, w8, taps, x_col_off, name, ts_pref=1024):
    halves = dy.ndim == 3
    s = dy.shape[-2]
    c = dy.shape[-1] * (2 if halves else 1)
    tc = _tile(dy.shape[-1], 512)
    ts = _tile(s, ts_pref, 16)
    hx, hy = _halo_rows(x.dtype), _halo_rows(dy.dtype)
    ns = s // ts
    nh = dy.shape[-1] // tc
    xo = x_col_off // tc
    assert x_col_off % tc == 0

    def body(x_ref, xp_ref, dy_ref, dyn_ref, w_ref, dx_ref, dw_ref, db_ref):
        i = pl.program_id(1)
        first = i == 0
        xv = x_ref[...].astype(F32)
        dyv = dy_ref[...].astype(F32)
        w = w_ref[...]
        _, shifted = _conv_taps(xv, xp_ref[...].astype(F32), w, taps, first)
        dye = jnp.concatenate([dyv, jnp.where(i == ns - 1, 0.0, dyn_ref[...].astype(F32))], axis=0)
        dx = None
        for k in range(taps):
            sh = taps - 1 - k
            ds_ = dye if sh == 0 else pltpu.roll(dye, ts + hy - sh, 0)
            term = ds_[0:ts, :] * w[k:k + 1, :]
            dx = term if dx is None else dx + term
        dx_ref[...] = dx.astype(dx_ref.dtype)
        rows = [jnp.sum(dyv * shifted[k], axis=0, keepdims=True) for k in range(taps)]
        rows.append(jnp.zeros((8 - taps, tc), F32))
        _acc_out(dw_ref, jnp.concatenate(rows, axis=0), first)
        _acc_out(db_ref, jnp.sum(dyv, axis=0, keepdims=True), first)

    last = s // hy - 1
    nxt = lambda i: jnp.minimum((i + 1) * (ts // hy), last)
    if halves:
        dy_spec = pl.BlockSpec((None, ts, tc), lambda j, i: (j // nh, i, j % nh))
        dyn_spec = pl.BlockSpec((None, hy, tc), lambda j, i: (j // nh, nxt(i), j % nh))
    else:
        dy_spec = pl.BlockSpec((ts, tc), lambda j, i: (i, j))
        dyn_spec = pl.BlockSpec((hy, tc), lambda j, i: (nxt(i), j))
    return pl.pallas_call(
        body, name=name, grid=(c // tc, ns),
        in_specs=[pl.BlockSpec((ts, tc), lambda j, i: (i, j + xo)), _prev_spec(ts, tc, hx, xo), dy_spec, dyn_spec,
                  pl.BlockSpec((8, tc), lambda j, i: (0, j))],
        out_specs=[pl.BlockSpec((ts, tc), lambda j, i: (i, j)), pl.BlockSpec((8, tc), lambda j, i: (0, j)),
                   pl.BlockSpec((1, tc), lambda j, i: (0, j))],
        out_shape=[jax.ShapeDtypeStruct((s, c), MD), jax.ShapeDtypeStruct((8, c), F32), jax.ShapeDtypeStruct((1, c), F32)],
        compiler_params=_params(("parallel", "arbitrary")),
    )(x, x, dy, dy, w8)


def _lru_gates(xc, wr_ref, br, wi_ref, bi, sp, nblk):
    xb = xc.astype(MD)
    zr, zi = [], []
    for b in range(nblk):
        blk = xb[:, b * LANES:(b + 1) * LANES]
        zr.append(_dot(blk, wr_ref[b]))
        zi.append(_dot(blk, wi_ref[b]))
    r = _sigmoid(jnp.concatenate(zr, axis=1) + br)
    ig = _sigmoid(jnp.concatenate(zi, axis=1) + bi)
    log_a = -LRU_C * r * sp
    a = jnp.exp(log_a)
    mult = jnp.sqrt(-_expm1(2.0 * log_a))
    return r, ig, a, mult


def lru_fwd(proj, x_off, y_off, w8, b_conv, w_rg, b_rg, w_ig, b_ig, lam, name):
    s = proj.shape[0]
    c = lam.shape[1]
    tc, ts = _tile(c, 512), _tile(s, 256, 16)
    nblk = tc // LANES
    xo, yo = x_off // tc, y_off // tc
    hr = _halo_rows(proj.dtype)

    def body(x_ref, xp_ref, y_ref, w_ref, bc_ref, wr_ref, br_ref, wi_ref, bi_ref, lam_ref, h_ref, hg_ref, carry):
        i = pl.program_id(1)
        first = i == 0
        xc, _ = _conv_taps(x_ref[...].astype(F32), xp_ref[...].astype(F32), w_ref[...], LRU_TAPS, first)
        xc = xc + bc_ref[...]
        sp = _softplus(-lam_ref[...])
        _, ig, a, mult = _lru_gates(xc, wr_ref, br_ref[...], wi_ref, bi_ref[...], sp, nblk)
        b = mult * (ig * xc)
        row = lax.broadcasted_iota(jnp.int32, (ts, tc), 0)
        d = 1
        while d < ts:
            keep = row >= d
            a_s, b_s = pltpu.roll(a, d, 0), pltpu.roll(b, d, 0)
            b = jnp.where(keep, a * b_s + b, b)
            a = jnp.where(keep, a * a_s, a)
            d *= 2
        h0 = jnp.where(first, 0.0, carry[7:8, :])
        h = a * h0 + b
        carry[...] = h[ts - 8:ts, :]
        h_ref[...] = h
        hg_ref[...] = (h * _gelu(y_ref[...].astype(F32))).astype(hg_ref.dtype)

    tile = lambda off: pl.BlockSpec((ts, tc), lambda j, i: (i, j + off))
    vec = pl.BlockSpec((1, tc), lambda j, i: (0, j))
    wblk = pl.BlockSpec((nblk, LANES, LANES), lambda j, i: (j, 0, 0))
    out = pl.BlockSpec((ts, tc), lambda j, i: (i, j))
    return pl.pallas_call(
        body, name=name, grid=(c // tc, s // ts),
        in_specs=[tile(xo), _prev_spec(ts, tc, hr, xo), tile(yo), pl.BlockSpec((8, tc), lambda j, i: (0, j)), vec,
                  wblk, vec, wblk, vec, vec],
        out_specs=[out, out],
        out_shape=[jax.ShapeDtypeStruct((s, c), F32), jax.ShapeDtypeStruct((s, c), MD)],
        scratch_shapes=[pltpu.VMEM((8, tc), F32)],
        compiler_params=_params(("parallel", "arbitrary")),
    )(proj, proj, proj, w8, b_conv, w_rg, b_rg, w_ig, b_ig, lam)


def lru_bwd(proj, x_off, y_off, h, dhg, w8, b_conv, w_rg, b_rg, w_ig, b_ig, lam, name):
    s = proj.shape[0]
    c = lam.shape[1]
    tc, ts = _tile(c, 512), _tile(s, 256, 16)
    nblk = tc // LANES
    ns = s // ts
    xo, yo = x_off // tc, y_off // tc
    hr = _halo_rows(proj.dtype)
    rev = lambda i: ns - 1 - i

    def body(x_ref, xp_ref, y_ref, h_ref, hp_ref, dhg_ref, w_ref, bc_ref, wr_ref, br_ref, wi_ref, bi_ref, lam_ref,
             dxc_ref, dy_ref, dwr_ref, dwi_ref, dbr_ref, dbi_ref, dlam_ref, a_carry, g_carry):
        step = pl.program_id(1)
        i = rev(step)
        start = step == 0
        top = i == 0
        xc, _ = _conv_taps(x_ref[...].astype(F32), xp_ref[...].astype(F32), w_ref[...], LRU_TAPS, top)
        xc = xc + bc_ref[...]
        lam_v = lam_ref[...]
        sp = _softplus(-lam_v)
        r, ig, a, mult = _lru_gates(xc, wr_ref, br_ref[...], wi_ref, bi_ref[...], sp, nblk)
        hv = h_ref[...]
        gy, dgy = _gelu_and_grad(y_ref[...].astype(F32))
        dhg_v = dhg_ref[...]
        dy_ref[...] = (dhg_v * hv * dgy).astype(dy_ref.dtype)
        row = lax.broadcasted_iota(jnp.int32, (ts, tc), 0)
        a_next = jnp.where(row == ts - 1, jnp.where(start, 0.0, a_carry[0:1, :]), pltpu.roll(a, ts - 1, 0))
        ca, cb = a_next, dhg_v * gy
        d = 1
        while d < ts:
            keep = row < ts - d
            a_s, b_s = pltpu.roll(ca, ts - d, 0), pltpu.roll(cb, ts - d, 0)
            cb = jnp.where(keep, ca * b_s + cb, cb)
            ca = jnp.where(keep, ca * a_s, ca)
            d *= 2
        g = ca * jnp.where(start, 0.0, g_carry[0:1, :]) + cb
        a_carry[...] = a[0:8, :]
        g_carry[...] = g[0:8, :]
        h_prev = jnp.where(row == 0, jnp.where(top, 0.0, hp_ref[7:8, :]), pltpu.roll(hv, 1, 0))
        da = g * h_prev
        ixc = ig * xc
        d_ixc = g * mult
        dlog_a = da * a - g * ixc * (a * a) / mult
        dzr = dlog_a * (-LRU_C * sp) * r * (1.0 - r)
        dzi = d_ixc * xc * ig * (1.0 - ig)
        dxc = d_ixc * ig
        xb, dzr_b, dzi_b = xc.astype(MD), dzr.astype(MD), dzi.astype(MD)
        parts, dwr, dwi = [], [], []
        for b in range(nblk):
            sl = slice(b * LANES, (b + 1) * LANES)
            parts.append(_dot(dzr_b[:, sl], wr_ref[b], NT) + _dot(dzi_b[:, sl], wi_ref[b], NT))
            dwr.append(_dot(xb[:, sl], dzr_b[:, sl], TN)[None])
            dwi.append(_dot(xb[:, sl], dzi_b[:, sl], TN)[None])
        dxc_ref[...] = dxc + jnp.concatenate(parts, axis=1)
        _acc_out(dwr_ref, jnp.concatenate(dwr, axis=0), start)
        _acc_out(dwi_ref, jnp.concatenate(dwi, axis=0), start)
        _acc_out(dbr_ref, jnp.sum(dzr, axis=0, keepdims=True), start)
        _acc_out(dbi_ref, jnp.sum(dzi, axis=0, keepdims=True), start)
        dsp = jnp.sum(dlog_a * (-LRU_C) * r, axis=0, keepdims=True)
        _acc_out(dlam_ref, -dsp * _sigmoid(-lam_v), start)

    vec = pl.BlockSpec((1, tc), lambda j, i: (0, j))
    wblk = pl.BlockSpec((nblk, LANES, LANES), lambda j, i: (j, 0, 0))
    tile = lambda off: pl.BlockSpec((ts, tc), lambda j, i: (rev(i), j + off))
    x_prev = pl.BlockSpec((hr, tc), lambda j, i: (jnp.maximum(rev(i) * (ts // hr) - 1, 0), j + xo))
    h_prev = pl.BlockSpec((8, tc), lambda j, i: (jnp.maximum(rev(i) * (ts // 8) - 1, 0), j))
    return pl.pallas_call(
        body, name=name, grid=(c // tc, ns),
        in_specs=[tile(xo), x_prev, tile(yo), tile(0), h_prev, tile(0), pl.BlockSpec((8, tc), lambda j, i: (0, j)), vec,
                  wblk, vec, wblk, vec, vec],
        out_specs=[tile(0), tile(0), wblk, wblk, vec, vec, vec],
        out_shape=[jax.ShapeDtypeStruct((s, c), F32), jax.ShapeDtypeStruct((s, c), MD),
                   jax.ShapeDtypeStruct(w_rg.shape, F32), jax.ShapeDtypeStruct(w_ig.shape, F32),
                   jax.ShapeDtypeStruct((1, c), F32), jax.ShapeDtypeStruct((1, c), F32), jax.ShapeDtypeStruct((1, c), F32)],
        scratch_shapes=[pltpu.VMEM((8, tc), F32), pltpu.VMEM((8, tc), F32)],
        compiler_params=_params(("parallel", "arbitrary")),
    )(proj, proj, proj, h, h, dhg, w8, b_conv, w_rg, b_rg, w_ig, b_ig, lam)


def merge_fwd(proj, gm_off, gl_off, o_mla, o_lru, name):
    s, c = o_mla.shape
    tc, ts = _tile(c, 1024), _tile(s, 512, 16)

    def body(gm_ref, gl_ref, om_ref, ol_ref, out_ref):
        out_ref[...] = (_sigmoid(gm_ref[...].astype(F32)) * om_ref[...] + _sigmoid(gl_ref[...].astype(F32)) * ol_ref[...]).astype(out_ref.dtype)

    tile = lambda off: pl.BlockSpec((ts, tc), lambda i, j: (i, j + off // tc))
    return pl.pallas_call(
        body, name=name, grid=(s // ts, c // tc), in_specs=[tile(gm_off), tile(gl_off), tile(0), tile(0)], out_specs=tile(0),
        out_shape=jax.ShapeDtypeStruct((s, c), MD), compiler_params=_params(("parallel", "parallel")),
    )(proj, proj, o_mla, o_lru)


def merge_bwd(proj, gm_off, gl_off, o_mla, o_lru, dmerged, name):
    s, c = o_mla.shape
    tc, ts = _tile(c, 1024), _tile(s, 512, 16)

    def body(gm_ref, gl_ref, om_ref, ol_ref, dm_ref, dom_ref, dol_ref, dgm_ref, dgl_ref):
        sm, sl = _sigmoid(gm_ref[...].astype(F32)), _sigmoid(gl_ref[...].astype(F32))
        dm = dm_ref[...]
        dom_ref[...] = (dm * sm).astype(dom_ref.dtype)
        dol_ref[...] = (dm * sl).astype(dol_ref.dtype)
        dgm_ref[...] = (dm * om_ref[...] * sm * (1.0 - sm)).astype(dgm_ref.dtype)
        dgl_ref[...] = (dm * ol_ref[...] * sl * (1.0 - sl)).astype(dgl_ref.dtype)

    tile = lambda off: pl.BlockSpec((ts, tc), lambda i, j: (i, j + off // tc))
    out = jax.ShapeDtypeStruct((s, c), MD)
    return pl.pallas_call(
        body, name=name, grid=(s // ts, c // tc), in_specs=[tile(gm_off), tile(gl_off), tile(0), tile(0), tile(0)],
        out_specs=[tile(0)] * 4, out_shape=[out] * 4, compiler_params=_params(("parallel", "parallel")),
    )(proj, proj, o_mla, o_lru, dmerged)


def xattn_fwd(cq, ck, cv, name):
    s = cq.shape[0]
    m = ck.shape[0]
    ts = _tile(s, 512, 16)
    scale = X_DIM ** -0.5

    def body(q_ref, k_ref, v_ref, o_ref):
        for h in range(X_HEADS):
            sl = slice(h * X_DIM, (h + 1) * X_DIM)
            sc = _dot(q_ref[:, sl], k_ref[:, sl], NT) * scale
            e = jnp.exp(sc - jnp.max(sc, axis=1, keepdims=True))
            p = e / jnp.sum(e, axis=1, keepdims=True)
            o_ref[:, sl] = _dot(p.astype(MD), v_ref[:, sl]).astype(o_ref.dtype)

    w = X_HEADS * X_DIM
    return pl.pallas_call(
        body, name=name, grid=(s // ts,),
        in_specs=[pl.BlockSpec((ts, w), lambda i: (i, 0)), pl.BlockSpec((m, w), lambda i: (0, 0)), pl.BlockSpec((m, w), lambda i: (0, 0))],
        out_specs=pl.BlockSpec((ts, w), lambda i: (i, 0)), out_shape=jax.ShapeDtypeStruct((s, w), MD),
        compiler_params=_params(("parallel",)),
    )(cq, ck, cv)


def xattn_bwd(cq, ck, cv, dco, name):
    s = cq.shape[0]
    m = ck.shape[0]
    ts = _tile(s, 512, 16)
    scale = X_DIM ** -0.5

    def body(q_ref, k_ref, v_ref, do_ref, dq_ref, dk_ref, dv_ref):
        first = pl.program_id(0) == 0
        dks, dvs = [], []
        for h in range(X_HEADS):
            sl = slice(h * X_DIM, (h + 1) * X_DIM)
            q, k, v, do = q_ref[:, sl], k_ref[:, sl], v_ref[:, sl], do_ref[:, sl]
            sc = _dot(q, k, NT) * scale
            e = jnp.exp(sc - jnp.max(sc, axis=1, keepdims=True))
            p = e / jnp.sum(e, axis=1, keepdims=True)
            dvs.append(_dot(p.astype(MD), do, TN))
            dp = _dot(do, v, NT)
            ds = (p * (dp - jnp.sum(dp * p, axis=1, keepdims=True)) * scale).astype(MD)
            dq_ref[:, sl] = _dot(ds, k).astype(dq_ref.dtype)
            dks.append(_dot(ds, q, TN))
        _acc_out(dk_ref, jnp.concatenate(dks, axis=1), first)
        _acc_out(dv_ref, jnp.concatenate(dvs, axis=1), first)

    w = X_HEADS * X_DIM
    row = pl.BlockSpec((ts, w), lambda i: (i, 0))
    full = pl.BlockSpec((m, w), lambda i: (0, 0))
    return pl.pallas_call(
        body, name=name, grid=(s // ts,), in_specs=[row, full, full, row], out_specs=[row, full, full],
        out_shape=[jax.ShapeDtypeStruct((s, w), MD), jax.ShapeDtypeStruct((m, w), F32), jax.ShapeDtypeStruct((m, w), F32)],
        compiler_params=_params(("arbitrary",)),
    )(cq, ck, cv, dco)


def _ffn_specs(up_pre):
    s, c2 = up_pre.shape
    f = c2 // 2
    tc, ts = _tile(f, 512), _tile(s, 1024, 16)
    nf = f // tc
    hr = _halo_rows(up_pre.dtype)
    tile = lambda off: pl.BlockSpec((ts, tc), lambda j, i: (i, j + off))
    w = lambda off: pl.BlockSpec((8, tc), lambda j, i: (0, j + off))
    vec = lambda off: pl.BlockSpec((1, tc), lambda j, i: (0, j + off))
    specs = [tile(0), _prev_spec(ts, tc, hr, 0), tile(nf), _prev_spec(ts, tc, hr, nf), w(0), w(nf), vec(0), vec(nf)]
    return s, f, tc, ts, nf, tile, specs


def _ffn_gate_val(g_ref, gp_ref, v_ref, vp_ref, wg_ref, wv_ref, bg_ref, bv_ref, first):
    gate, _ = _conv_taps(g_ref[...].astype(F32), gp_ref[...].astype(F32), wg_ref[...], FFN_TAPS, first)
    val, _ = _conv_taps(v_ref[...].astype(F32), vp_ref[...].astype(F32), wv_ref[...], FFN_TAPS, first)
    return gate + bg_ref[...], val + bv_ref[...]


def ffn_act_fwd(up_pre, w8, b, name):
    s, f, tc, ts, nf, tile, specs = _ffn_specs(up_pre)

    def body(*refs):
        o_ref = refs[8]
        gate, val = _ffn_gate_val(*refs[:8], pl.program_id(1) == 0)
        o_ref[...] = (_gelu(gate) * val).astype(o_ref.dtype)

    return pl.pallas_call(
        body, name=name, grid=(nf, s // ts), in_specs=specs, out_specs=tile(0),
        out_shape=jax.ShapeDtypeStruct((s, f), MD), compiler_params=_params(("parallel", "parallel")),
    )(up_pre, up_pre, up_pre, up_pre, w8, w8, b, b)


def ffn_act_bwd(up_pre, w8, b, dact, name):
    s, f, tc, ts, nf, tile, specs = _ffn_specs(up_pre)

    def body(*refs):
        da_ref, o_ref = refs[8], refs[9]
        gate, val = _ffn_gate_val(*refs[:8], pl.program_id(1) == 0)
        ge, dge = _gelu_and_grad(gate)
        da = da_ref[...].astype(F32)
        o_ref[0] = (da * val * dge).astype(o_ref.dtype)
        o_ref[1] = (da * ge).astype(o_ref.dtype)

    return pl.pallas_call(
        body, name=name, grid=(nf, s // ts), in_specs=specs + [tile(0)],
        out_specs=pl.BlockSpec((2, ts, tc), lambda j, i: (0, i, j)), out_shape=jax.ShapeDtypeStruct((2, s, f), MD),
        compiler_params=_params(("parallel", "parallel")),
    )(up_pre, up_pre, up_pre, up_pre, w8, w8, b, b, dact)


def _adam(w, g, m, v):
    m = ADAM_B1 * m + (1.0 - ADAM_B1) * g
    v = ADAM_B2 * v + (1.0 - ADAM_B2) * (g * g)
    m_hat = m / (1.0 - ADAM_B1 ** ADAM_STEP)
    v_hat = v / (1.0 - ADAM_B2 ** ADAM_STEP)
    delta = -ADAM_LR * (m_hat / (jnp.sqrt(v_hat) + ADAM_EPS) + ADAM_WD * w)
    return delta, m, v


def _rows_tile(r, c):
    return _tile(r, max(16, (1 << 18) // c // 16 * 16), 16)


def adamw_sharded(mine, peers, w, m, v, name):
    r, c = w.shape
    tr = _rows_tile(r, c)

    def body(g_ref, p_ref, w_ref, m_ref, v_ref, go_ref, d_ref, mo_ref, vo_ref):
        grad = g_ref[...].astype(F32)
        for k in range(N_DEV - 1):
            grad = grad + p_ref[k].astype(F32)
        go_ref[...] = grad
        d_ref[...], mo_ref[...], vo_ref[...] = _adam(w_ref[...], grad, m_ref[...], v_ref[...])

    row = pl.BlockSpec((tr, c), lambda i: (i, 0))
    out = jax.ShapeDtypeStruct((r, c), F32)
    return pl.pallas_call(
        body, name=name, grid=(r // tr,),
        in_specs=[row, pl.BlockSpec((N_DEV - 1, tr, c), lambda i: (0, i, 0)), row, row, row],
        out_specs=[row] * 4, out_shape=[out] * 4, compiler_params=_params(("parallel",)),
    )(mine, peers, w, m, v)


def adamw_plain(g, w, m, v, name):
    r, c = w.shape
    tr = _tile(r, 512, 8)

    def body(g_ref, w_ref, m_ref, v_ref, d_ref, mo_ref, vo_ref):
        d_ref[...], mo_ref[...], vo_ref[...] = _adam(w_ref[...], g_ref[...], m_ref[...], v_ref[...])

    row = pl.BlockSpec((tr, c), lambda i: (i, 0))
    out = jax.ShapeDtypeStruct((r, c), F32)
    return pl.pallas_call(body, name=name, grid=(r // tr,), in_specs=[row] * 4, out_specs=[row] * 3, out_shape=[out] * 3,
                          compiler_params=_params(("parallel",)))(g, w, m, v)


def sum_devices(parts, name):
    _, r, c = parts.shape
    tr = _tile(r, 256, 8)

    def body(p_ref, o_ref):
        acc = p_ref[0]
        for d in range(1, N_DEV):
            acc = acc + p_ref[d]
        o_ref[...] = acc

    return pl.pallas_call(
        body, name=name, grid=(r // tr,), in_specs=[pl.BlockSpec((N_DEV, tr, c), lambda i: (0, i, 0))],
        out_specs=pl.BlockSpec((tr, c), lambda i: (i, 0)), out_shape=jax.ShapeDtypeStruct((r, c), F32),
        compiler_params=_params(("parallel",)))(parts)


def _place():
    return lax.axis_index("x"), lax.axis_index("y"), lax.axis_index("c")


def _other_chips(x, y):
    return [(1 - x, y), (x, 1 - y), (1 - x, 1 - y)]


_ANY = pl.BlockSpec(memory_space=pl.ANY)


def all_gather(shards, name):
    n = len(shards)

    def body(*refs):
        x_refs, out_refs = refs[:n], refs[n:2 * n]
        send_sems, recv_sems, local_sems = refs[2 * n:]
        x, y, cc = _place()
        me, sibling = (x, y, cc), (x, y, 1 - cc)
        chips = _other_chips(x, y)

        def slot(w, px, py, pc):
            return out_refs[w].at[4 * px + 2 * py + pc]

        def copy(k, w, block, to, src=None):
            return pltpu.make_async_remote_copy(
                src_ref=slot(w, *block) if src is None else src, dst_ref=slot(w, *block),
                send_sem=send_sems.at[k, w], recv_sem=recv_sems.at[k, w], device_id=to, device_id_type=MESH)

        mine = [pltpu.make_async_copy(x_refs[w], slot(w, *me), local_sems.at[w]) for w in range(n)]
        for cp in mine:
            cp.start()
        first = [copy(0, w, me, sibling, src=x_refs[w]) for w in range(n)]
        first += [copy(1 + j, w, me, (*chip, cc), src=x_refs[w]) for j, chip in enumerate(chips) for w in range(n)]
        for cp in first:
            cp.start()
        passed = []
        for j, chip in enumerate(chips):
            for w in range(n):
                copy(1 + j, w, (*chip, cc), me).wait_recv()
                passed.append(copy(4 + j, w, (*chip, cc), sibling))
                passed[-1].start()
        for w in range(n):
            copy(0, w, sibling, me).wait_recv()
        for j, chip in enumerate(chips):
            for w in range(n):
                copy(4 + j, w, (*chip, 1 - cc), me).wait_recv()
        for cp in first + passed:
            cp.wait_send()
        for cp in mine:
            cp.wait()

    return pl.pallas_call(
        body, name=name, out_shape=[jax.ShapeDtypeStruct((N_DEV, *a.shape), a.dtype) for a in shards],
        in_specs=[_ANY] * n, out_specs=[_ANY] * n,
        scratch_shapes=[pltpu.SemaphoreType.DMA((7, n)), pltpu.SemaphoreType.DMA((7, n)), pltpu.SemaphoreType.DMA((n,))],
    )(*shards)


_HBM = pl.BlockSpec(memory_space=pltpu.HBM)
_SEM = pl.BlockSpec(memory_space=pltpu.SEMAPHORE)
_DATAFLOW = pltpu.SideEffectType.DATAFLOW_SIDE_EFFECTING
_PEER_FLIPS = ((0, 0, 1), (1, 0, 0), (1, 0, 1), (0, 1, 0), (0, 1, 1), (1, 1, 0), (1, 1, 1))
N_PEERS = len(_PEER_FLIPS)


def _peer_copies(kind, src_refs, land_refs, send_sems, recv_sems):
    x, y, c = _place()
    n = len(src_refs)
    copies = []
    for w in range(n):
        for k, (fx, fy, fc) in enumerate(_PEER_FLIPS):
            px, py, pc = (1 - x if fx else x), (1 - y if fy else y), (1 - c if fc else c)
            if kind == "scatter":
                src, dst = src_refs[w].at[4 * px + 2 * py + pc], land_refs[w].at[k]
            else:
                src, dst = src_refs[w], land_refs[w].at[4 * x + 2 * y + c]
            copies.append(pltpu.make_async_remote_copy(
                src_ref=src, dst_ref=dst, send_sem=send_sems.at[k * n + w], recv_sem=recv_sems.at[k * n + w],
                device_id=(px, py, pc), device_id_type=MESH))
    return copies


def send_start(kind, srcs, name):
    n = len(srcs)
    if kind == "scatter":
        lands = [lax.empty((N_PEERS, *a.shape[1:]), a.dtype) for a in srcs]
    else:
        lands = [lax.empty((N_DEV, *a.shape), a.dtype) for a in srcs]

    def body(*refs):
        src_refs, land_refs = refs[:n], refs[n:2 * n]
        send_sems, recv_sems = refs[2 * n], refs[2 * n + 1]
        token = refs[-1]
        for cp in _peer_copies(kind, src_refs, land_refs, send_sems, recv_sems):
            cp.start()
        token[...] = jnp.zeros_like(token)

    hbm = lambda a: pltpu.with_memory_space_constraint(a, pltpu.HBM)
    outs = pl.pallas_call(
        body, name=name,
        out_shape=(pltpu.SemaphoreType.DMA((N_PEERS * n,)), pltpu.SemaphoreType.DMA((N_PEERS * n,)),
                   *[pltpu.HBM(a.shape, a.dtype) for a in srcs + lands], jax.ShapeDtypeStruct((8, LANES), F32)),
        in_specs=[_HBM] * (2 * n),
        out_specs=(_SEM, _SEM, *[_HBM] * (2 * n), pl.BlockSpec(memory_space=pltpu.VMEM)),
        input_output_aliases={i: 2 + i for i in range(2 * n)},
        compiler_params=pltpu.CompilerParams(has_side_effects=_DATAFLOW),
    )(*[hbm(a) for a in srcs + lands])
    return (kind, outs[0], outs[1], list(outs[2:2 + n]), list(outs[2 + n:2 + 2 * n])), outs[-1]


def send_wait(handle, after, name):
    kind, send_sems, recv_sems, srcs, lands = handle
    n = len(srcs)

    def body(*refs):
        src_refs, land_refs = refs[:n], refs[n:2 * n]
        for cp in _peer_copies(kind, src_refs, land_refs, refs[2 * n], refs[2 * n + 1]):
            cp.wait_send()
            cp.wait_recv()

    outs = pl.pallas_call(
        body, name=name, out_shape=tuple(pltpu.HBM(a.shape, a.dtype) for a in srcs + lands),
        in_specs=[_HBM] * (2 * n) + [_SEM, _SEM, _ANY], out_specs=tuple([_HBM] * (2 * n)),
        input_output_aliases={i: i for i in range(2 * n)},
        compiler_params=pltpu.CompilerParams(has_side_effects=_DATAFLOW),
    )(*srcs, *lands, send_sems, recv_sems, after)
    return list(outs[:n]), list(outs[n:])


class _NoExchange:
    def other_weights(self, after):
        return {}

    def send_grads(self, G, names):
        return None

    def send_small_grads(self, G):
        return None


def _after(a, token):
    return a if token is None else a + token[0:1, 0:1].astype(a.dtype)


def _step_local(x, mem, ang2, target, W, hooks=None):
    hooks = hooks or _NoExchange()
    W = dict(W)
    d = x.shape[1]
    wq = HEADS * QK_ROPE
    o_lx, o_ly, o_gm, o_gl = d + wq, 2 * d + wq, 3 * d + wq, 4 * d + wq
    one = lambda a, b, mode, dt, name, **kw: mm([(a, b)], mode, dt, name, **kw)

    h1 = rms_fwd(x, W["g_pre_mix"], "h1")
    proj = one(h1, W["w_main"], "nn", MD, "proj")
    ckv = one(h1, W["w_ckv"], "nn", MD, "ckv")
    kr = one(h1, W["w_kr"], "nn", MD, "k_rope")
    ckv_n = rms_fwd(ckv, W["g_ckv"], "ckv_n")
    kn =one(ckv_n, W["w_uk"], "nn", MD, "k_nope")
    v = one(ckv_n, W["w_uv"], "nn", MD, "v")
    qf, kf = rope_fwd(proj, kr, kn, ang2, "rope")
    o, lse2 = flash_fwd(qf, kf, v, "attn")
    W.update(hooks.other_weights(o))
    o_mla = one(o, W["w_o_mla"], "nn", F32, "o_mla")
    lru_w = (W["w_conv8"], W["b_conv_lru"], W["w_rg"], W["b_rg"], W["w_ig"], W["b_ig"], W["lru_lambda"])
    hr, hg = lru_fwd(proj, o_lx, o_ly, *lru_w, "lru")
    o_lru = one(hg, W["w_o_lru"], "nn", F32, "o_lru")
    merged = merge_fwd(proj, o_gm, o_gl, o_mla, o_lru, "merge")
    y1 = one(merged, W["w_out"], "nn", F32, "y1")
    x1, h2 = post_pre_fwd(x, y1, W["g_post_mix"], W["g_pre_x"], "x1")
    mn = rms_fwd(mem, W["g_mem"], "mem_n")
    ck = one(mn, W["w_ck"], "nn", MD, "ck")
    cv = one(mn, W["w_cv"], "nn", MD, "cv")
    cq = one(h2, W["w_cq"], "nn", MD, "cq")
    co = xattn_fwd(cq, ck, cv, "xattn")
    y2 = one(co, W["w_co"], "nn", F32, "y2")
    x2, h3 = post_pre_fwd(x1, y2, W["g_post_x"], W["g_pre_ffn"], "x2")
    up_pre = one(h3, W["w_up"], "nn", MD, "up_pre")
    act = ffn_act_fwd(up_pre, W["w_fconv8"], W["b_fconv"], "act")
    y3 = one(act, W["w_down"], "nn", F32, "y3", tk=2816)
    G = {}
    loss, dx3, dy3, G["g_post_ffn"] = loss_head(x2, y3, target, W["g_post_ffn"], "loss")

    G["w_down"] = one(act, dy3, "tn", MD, "dw_down")
    sent = hooks.send_grads(G, ("w_down",))
    dact = one(dy3, W["w_down"], "nt", MD, "dact")
    dup = ffn_act_bwd(up_pre, W["w_fconv8"], _after(W["b_fconv"], sent), dact, "dup")
    dup_pre, dwf8, G["b_fconv"] = conv_bwd(up_pre, dup, W["w_fconv8"], FFN_TAPS, 0, "ffn_conv_bwd")
    G["w_fconv"] = dwf8[:FFN_TAPS]
    G["w_up"] = one(h3, dup_pre, "tn", MD, "dw_up")
    sent = hooks.send_grads(G, ("w_up",))
    dh3 = one(dup_pre, W["w_up"], "nt", F32, "dh3", tn=2048, tk=1024)
    dx2, dy2, G["g_pre_ffn"], G["g_post_x"] = pre_post_bwd(x2, y2, dx3, dh3, _after(W["g_pre_ffn"], sent), W["g_post_x"], "dx2")
    G["w_co"] = one(co, dy2, "tn", MD, "dw_co")
    dco = one(dy2, W["w_co"], "nt", MD, "dco")
    dcq, dck, dcv = xattn_bwd(cq, ck, cv, dco, "xattn_bwd")
    dck, dcv = dck.astype(MD), dcv.astype(MD)
    G["w_cq"] = one(h2, dcq, "tn", MD, "dw_cq")
    G["w_ck"] = one(mn, dck, "tn", MD, "dw_ck")
    G["w_cv"] = one(mn, dcv, "tn", MD, "dw_cv")
    sent = hooks.send_grads(G, ("w_co", "w_cq", "w_ck", "w_cv"))
    dmn = mm([(dck, W["w_ck"]), (dcv, W["w_cv"])], "nt", F32, "dmem_n")
    _, G["g_mem"] = rms_bwd(mem, W["g_mem"], dmn, "dg_mem")
    dh2 = one(dcq, W["w_cq"], "nt", F32, "dh2")
    dx1, dy1, G["g_pre_x"], G["g_post_mix"] = pre_post_bwd(x1, y1, dx2, dh2, _after(W["g_pre_x"], sent), W["g_post_mix"], "dx1")
    G["w_out"] = one(merged, dy1, "tn", MD, "dw_out")
    dmerged = one(dy1, W["w_out"], "nt", F32, "dmerged")
    do_mla, do_lru, dgm, dgl = merge_bwd(proj, o_gm, o_gl, o_mla, o_lru, dmerged, "merge_bwd")
    G["w_o_mla"] = one(o, do_mla, "tn", MD, "dw_o_mla")
    do = one(do_mla, W["w_o_mla"], "nt", MD, "do")
    G["w_o_lru"] = one(hg, do_lru, "tn", MD, "dw_o_lru")
    sent = hooks.send_grads(G, ("w_out", "w_o_mla", "w_o_lru"))
    dhg = one(do_lru, W["w_o_lru"], "nt", F32, "dhg")
    lru_w = lru_w[:-1] + (_after(lru_w[-1], sent),)
    dxc, dly, G["w_rg"], G["w_ig"], G["b_rg"], G["b_ig"], G["lru_lambda"] = lru_bwd(proj, o_lx, o_ly, hr, dhg, *lru_w, "lru_bwd")
    dlx, dwc8, G["b_conv_lru"] = conv_bwd(proj, dxc, W["w_conv8"], LRU_TAPS, o_lx, "lru_conv_bwd")
    G["w_conv_lru"] = dwc8[:LRU_TAPS]
    dqf, dkf, dv = flash_bwd(qf, kf, v, o, do, lse2, "attn_bwd")
    dqn, dqr, dkn, dkr = rope_bwd(dqf, dkf, ang2, "rope_bwd")
    G["w_uk"] = one(ckv_n, dkn, "tn", MD, "dw_uk")
    G["w_uv"] = one(ckv_n, dv, "tn", MD, "dw_uv")
    dckv_n = mm([(dkn, W["w_uk"]), (dv, W["w_uv"])], "nt", F32, "dckv_n")
    dckv, G["g_ckv"] = rms_bwd(ckv, W["g_ckv"], dckv_n, "dckv")
    dckv = _after(dckv, hooks.send_small_grads(G))
    dproj = jnp.concatenate([dqn, dqr, dlx, dly, dgm, dgl], axis=1)
    parts = [("w_main", dproj), ("w_ckv", dckv), ("w_kr", dkr)]
    for n, dpart in parts:
        G[n] = one(h1, dpart, "tn", MD, "d" + n)
    sent = hooks.send_grads(G, ("w_ukv", "w_in"))
    parts[-1] = ("w_kr", _after(dkr, sent))
    dh1 = mm([(dpart, W[n]) for n, dpart in parts], "nt", F32, "dh1", tm=512, tn=2048, tk=1024)
    grad_x, G["g_pre_mix"] = pre_bwd(x, dx1, dh1, W["g_pre_mix"], "grad_x")
    return loss, grad_x, G


FLAT_COLS = 1024
BIG = (("w_in", 1), ("w_ukv", 1), ("w_o_mla", 0), ("w_o_lru", 0), ("w_out", 0), ("w_cq", 0), ("w_ck", 0), ("w_cv", 0),
       ("w_co", 1), ("w_up", 1), ("w_down", 0))
SMALL_SHARDED = ("w_conv_lru", "w_fconv")
REPLICATED = ("g_pre_mix", "g_post_mix", "g_ckv", "b_conv_lru", "w_rg", "b_rg", "w_ig", "b_ig", "lru_lambda",
              "g_pre_x", "g_post_x", "g_mem", "g_pre_ffn", "g_post_ffn", "b_fconv")
WEIGHTS = ("g_pre_mix", "g_post_mix", "w_in", "g_ckv", "w_ukv", "w_o_mla", "w_conv_lru", "b_conv_lru", "w_rg", "b_rg",
           "w_ig", "b_ig", "lru_lambda", "w_o_lru", "w_out", "g_pre_x", "g_post_x", "g_mem", "w_cq", "w_ck", "w_cv",
           "w_co", "g_pre_ffn", "g_post_ffn", "w_up", "w_fconv", "b_fconv", "w_down")


def _round_up(n, k):
    return (n + k - 1) // k * k


def _pack_vec(parts, row_mult):
    flat = jnp.concatenate([p.reshape(-1) for p in parts])
    n = flat.shape[0]
    total = _round_up(n, row_mult * FLAT_COLS)
    return jnp.pad(flat, (0, total - n)).reshape(-1, FLAT_COLS)


def _unpack_vec(flat2d, shapes):
    flat = flat2d.reshape(-1)
    out, off = [], 0
    for shp in shapes:
        n = int(np.prod(shp))
        out.append(flat[off:off + n].reshape(shp))
        off += n
    return out


def _gathered_to_full(blocks, axis):
    n, r, c = blocks.shape
    return blocks.reshape(n * r, c) if axis == 0 else blocks.transpose(1, 0, 2).reshape(r, n * c)


def _full_to_shards(full, axis):
    r, c = full.shape
    if axis == 0:
        return full.reshape(N_DEV, r // N_DEV, c)
    return full.reshape(r, N_DEV, c // N_DEV).transpose(1, 0, 2)


def _split_w_in(w_in, d):
    hq = HEADS * (QK_NOPE + QK_ROPE)
    q = w_in[:, :hq].reshape(d, HEADS, QK_NOPE + QK_ROPE)
    q_nope, q_rope = q[:, :, :QK_NOPE].reshape(d, HEADS * QK_NOPE), q[:, :, QK_NOPE:].reshape(d, HEADS * QK_ROPE)
    out = {"w_ckv": w_in[:, hq:hq + KV_RANK]}
    off = hq + KV_RANK
    out["w_kr"] = jnp.pad(w_in[:, off:off + QK_ROPE], ((0, 0), (0, LANES - QK_ROPE)))
    out["w_main"] = jnp.concatenate([q_nope, q_rope, w_in[:, off + QK_ROPE:]], axis=1)
    return out


def _join_w_in(G, d):
    gm = G["w_main"]
    n_qn, n_qr = HEADS * QK_NOPE, HEADS * QK_ROPE
    q = jnp.concatenate([gm[:, :n_qn].reshape(d, HEADS, QK_NOPE), gm[:, n_qn:n_qn + n_qr].reshape(d, HEADS, QK_ROPE)], axis=2)
    return jnp.concatenate([q.reshape(d, -1), G["w_ckv"], G["w_kr"][:, :QK_ROPE], gm[:, n_qn + n_qr:]], axis=1)


def kernel(x, mem, positions, g_pre_mix, g_post_mix, w_in, g_ckv, w_ukv, w_o_mla, w_conv_lru, b_conv_lru, w_rg, b_rg, w_ig, b_ig, lru_lambda, w_o_lru, w_out, g_pre_x, g_post_x, g_mem, w_cq, w_ck, w_cv, w_co, g_pre_ffn, g_post_ffn, w_up, w_fconv, b_fconv, w_down, loss_target, m_g_pre_mix, m_g_post_mix, m_w_in, m_g_ckv, m_w_ukv, m_w_o_mla, m_w_conv_lru, m_b_conv_lru, m_w_rg, m_b_rg, m_w_ig, m_b_ig, m_lru_lambda, m_w_o_lru, m_w_out, m_g_pre_x, m_g_post_x, m_g_mem, m_w_cq, m_w_ck, m_w_cv, m_w_co, m_g_pre_ffn, m_g_post_ffn, m_w_up, m_w_fconv, m_b_fconv, m_w_down, v_g_pre_mix, v_g_post_mix, v_w_in, v_g_ckv, v_w_ukv, v_w_o_mla, v_w_conv_lru, v_b_conv_lru, v_w_rg, v_b_rg, v_w_ig, v_b_ig, v_lru_lambda, v_w_o_lru, v_w_out, v_g_pre_x, v_g_post_x, v_g_mem, v_w_cq, v_w_ck, v_w_cv, v_w_co, v_g_pre_ffn, v_g_post_ffn, v_w_up, v_w_fconv, v_b_fconv, v_w_down):
    given = dict(locals())
    P = {n: given[n][0] for n in WEIGHTS}
    M = {n: given["m_" + n][0] for n in WEIGHTS}
    V = {n: given["v_" + n][0] for n in WEIGHTS}
    xs, mems, tgt = x[0], mem[0], loss_target[0]
    s, d = xs.shape
    ax, ay, ac = _place()
    dev = 4 * ax + 2 * ay + ac

    axis_of = dict(BIG)
    axis_of.update({n: 1 for n in SMALL_SHARDED})
    wire = lambda n: P[n] if n in SMALL_SHARDED else P[n].astype(MD)
    gathered_in, gathered_kv = all_gather([wire("w_in"), wire("w_ukv")], "gather_first")
    later_names = ("w_o_mla", "w_conv_lru", "w_o_lru", "w_out", "w_ck", "w_cv", "w_cq", "w_co", "w_up", "w_fconv", "w_down")
    gathered_in, later_srcs = lax.optimization_barrier((gathered_in, [wire(n) for n in later_names]))
    later_handle, started_later = send_start("gather", later_srcs, "start_gather_rest")

    def landed(names, handle, after, name):
        srcs, lands = send_wait(handle, after, name)
        full = {}
        for n, src, land in zip(names, srcs, lands):
            full[n] = _gathered_to_full(lax.dynamic_update_slice(land, src[None], (dev, 0, 0)), axis_of[n])
        return full

    W = _split_w_in(_gathered_to_full(gathered_in, 1), d)
    W["w_rg"], W["w_ig"] = P["w_rg"].astype(MD), P["w_ig"].astype(MD)
    for n in ("g_pre_mix", "g_post_mix", "g_ckv", "b_conv_lru", "b_rg", "b_ig", "lru_lambda", "g_pre_x", "g_post_x", "g_mem",
              "g_pre_ffn", "g_post_ffn", "b_fconv"):
        W[n] = P[n].reshape(1, -1)
    kv = _gathered_to_full(gathered_kv, 1).reshape(KV_RANK, HEADS, QK_NOPE + V_DIM)
    W["w_uk"] = kv[:, :, :QK_NOPE].reshape(KV_RANK, HEADS * QK_NOPE)
    W["w_uv"] = kv[:, :, QK_NOPE:].reshape(KV_RANK, HEADS * V_DIM)
    W["g_pre_mix"] = _after(W["g_pre_mix"], started_later)

    inv_freq = ROPE_THETA ** (-jnp.arange(0, QK_ROPE, 2, dtype=F32) / QK_ROPE)
    ang = positions[0].astype(F32)[:, None] * inv_freq
    ang2 = jnp.tile(ang, (1, LANES // (QK_ROPE // 2)))

    pending = []
    small_pending = []
    small_names = REPLICATED + SMALL_SHARDED
    early_small = tuple(n for n in small_names if n != "g_pre_mix")

    class Exchange:
        def other_weights(self, after):
            full = landed(later_names, later_handle, after, "wait_gather_rest")
            full["w_conv8"] = jnp.pad(full.pop("w_conv_lru"), ((0, 8 - LRU_TAPS), (0, 0)))
            full["w_fconv8"] = jnp.pad(full.pop("w_fconv"), ((0, 8 - FFN_TAPS), (0, 0)))
            return full

        def send_grads(self, G, names):
            if "w_in" in names:
                G["w_in"] = _join_w_in(G, d)
                G["w_ukv"] = jnp.concatenate([G["w_uk"].reshape(KV_RANK, HEADS, QK_NOPE),
                                              G["w_uv"].reshape(KV_RANK, HEADS, V_DIM)], axis=2).reshape(KV_RANK, -1)
            handle, started = send_start("scatter", [_full_to_shards(G[n], axis_of[n]) for n in names], "start_grads_" + names[0])
            pending.append((names, handle))
            return started

        def send_small_grads(self, G):
            handle, started = send_start("gather", [_pack_vec([G[n] for n in early_small], 128)], "start_small_grads")
            small_pending.append(handle)
            return started

    loss, grad_x, G = _step_local(xs, mems, ang2, tgt, W, Exchange())
    loss = lax.psum(loss[0, 0], ("x", "y", "c"))

    res = {k: {} for k in ("grad", "delta", "m", "v")}
    for names, handle in pending:
        slabs, peers = send_wait(handle, grad_x, "wait_grads_" + names[0])
        for n, slab, peer in zip(names, slabs, peers):
            mine = lax.dynamic_index_in_dim(slab, dev, axis=0, keepdims=False)
            outs = adamw_sharded(mine, peer, P[n], M[n], V[n], "adamw_" + n)
            for kind, o in zip(("grad", "delta", "m", "v"), outs):
                res[kind][n] = o

    (own,), (land,) = send_wait(small_pending[0], grad_x, "wait_small_grads")
    partials = lax.dynamic_update_slice(land, own[None], (dev, 0, 0))
    summed = dict(zip(early_small, _unpack_vec(sum_devices(partials, "sum_small_grads"), [G[n].shape for n in early_small])))
    last, = all_gather([_pack_vec([G["g_pre_mix"]], 8)], "gather_last_grad")
    summed["g_pre_mix"], = _unpack_vec(sum_devices(last, "sum_last_grad"), [G["g_pre_mix"].shape])
    for n in SMALL_SHARDED:
        t, c = summed[n].shape
        summed[n] = lax.dynamic_index_in_dim(summed[n].reshape(t, N_DEV, c // N_DEV), dev, axis=1, keepdims=False)
    sg = {n: summed[n].reshape(P[n].shape) for n in small_names}
    vec = lambda src: _pack_vec([src[n] for n in small_names], 128)
    small_out = adamw_plain(vec(sg), vec(P), vec(M), vec(V), "adamw_small")
    shapes = [P[n].shape for n in small_names]
    for kind, flat in zip(("delta", "m", "v"), small_out):
        res[kind].update(zip(small_names, _unpack_vec(flat, shapes)))
    res["grad"].update(sg)

    outs = [loss, grad_x[None]]
    for kind in ("grad", "delta", "m", "v"):
        outs += [res[kind][n][None] for n in WEIGHTS]
    return tuple(outs)
```

```python
import numpy as np
import jax
import jax.numpy as jnp
from jax import lax
from jax.experimental import pallas as pl
from jax.experimental.pallas import tpu as pltpu

MD = jnp.bfloat16
F32 = jnp.float32
EPS = 1e-6
N_DEV = 8
HEADS = 16
QK_NOPE = 128
QK_ROPE = 64
V_DIM = 128
QK_PAD = 256
KV_RANK = 512
SOFTMAX_SCALE = (QK_NOPE + QK_ROPE) ** -0.5
LOG2E = 1.4426950408889634
ROPE_THETA = 10000.0
LRU_BLOCKS = 16
LRU_C = 8.0
X_HEADS = 4
X_DIM = 128
LRU_TAPS = 4
FFN_TAPS = 3
ADAM_LR, ADAM_B1, ADAM_B2, ADAM_EPS, ADAM_WD, ADAM_STEP = 0.001, 0.9, 0.999, 1e-08, 0.01, 10
LANES = 128
VMEM_LIMIT = 52 * 1024 * 1024
MESH = pl.DeviceIdType.MESH

NN = (((1,), (0,)), ((), ()))
NT = (((1,), (1,)), ((), ()))
TN = (((0,), (0,)), ((), ()))


def _tile(n, pref, align=LANES):
    if n <= pref:
        return n
    d = (pref // align) * align
    while d > align and n % d:
        d -= align
    assert n % d == 0, (n, pref, align)
    return d


def _params(sem):
    return pltpu.CompilerParams(dimension_semantics=sem, vmem_limit_bytes=VMEM_LIMIT)


def _dot(a, b, dims=NN):
    return lax.dot_general(a, b, dims, preferred_element_type=F32)


def _sigmoid(x):
    return 1.0 / (1.0 + jnp.exp(-x))


_GELU_C = 0.7978845608028654
_GELU_A = 0.044715


def _gelu(x):
    return 0.5 * x * (1.0 + jnp.tanh(x * (_GELU_C + (_GELU_C * _GELU_A) * (x * x))))


def _gelu_and_grad(x):
    x2 = x * x
    t = jnp.tanh(x * (_GELU_C + (_GELU_C * _GELU_A) * x2))
    hx = 0.5 * x
    g = hx + hx * t
    dg = 0.5 + 0.5 * t + hx * (1.0 - t * t) * (_GELU_C + (3.0 * _GELU_C * _GELU_A) * x2)
    return g, dg


def _expm1(x):
    small = x * (1.0 + x * (0.5 + x * (1.0 / 6.0 + x * (1.0 / 24.0 + x * (1.0 / 120.0)))))
    return jnp.where(jnp.abs(x) < 0.05, small, jnp.exp(x) - 1.0)


def _softplus(x):
    return jnp.maximum(x, 0.0) + jnp.log(1.0 + jnp.exp(-jnp.abs(x)))


def _rms_hat(x):
    r = lax.rsqrt(jnp.mean(x * x, axis=-1, keepdims=True) + EPS)
    return x * r, r


def _rms_bwd(x, g, dn):
    xh, r = _rms_hat(x)
    dg = jnp.sum(dn * xh, axis=0, keepdims=True)
    dxh = dn * g
    dx = r * (dxh - xh * jnp.mean(dxh * xh, axis=-1, keepdims=True))
    return dx, dg


def _acc_out(ref, val, first):
    @pl.when(first)
    def _():
        ref[...] = val

    @pl.when(jnp.logical_not(first))
    def _():
        ref[...] += val


def _lane_sums_as_row(x):
    ones = jnp.ones((8, LANES), F32)
    return lax.dot_general(ones, x, NT, precision=lax.Precision.HIGHEST, preferred_element_type=F32)[0:1, :]


def mm(pairs, mode, out_dtype, name, tm=1024, tn=1024, tk=2048):
    dims = {"nn": NN, "nt": NT, "tn": TN}[mode]
    shapes = []
    for a, b in pairs:
        if mode == "nn":
            (m, k), (k2, n) = a.shape, b.shape
        elif mode == "nt":
            (m, k), (n, k2) = a.shape, b.shape
        else:
            (k, m), (k2, n) = a.shape, b.shape
        assert k == k2, (name, a.shape, b.shape)
        shapes.append((m, n, k))
    m, n = shapes[0][0], shapes[0][1]
    assert all(s[0] == m and s[1] == n for s in shapes)
    tm, tn = _tile(m, tm), _tile(n, tn)
    tks = [_tile(s[2], tk) for s in shapes]
    nks = [s[2] // t for s, t in zip(shapes, tks)]
    starts = [sum(nks[:p]) for p in range(len(pairs))]
    nk_total = sum(nks)

    in_specs, args = [], []
    for p, (a, b) in enumerate(pairs):
        def kk(k, p=p):
            return jnp.clip(k - starts[p], 0, nks[p] - 1)
        if mode == "tn":
            in_specs.append(pl.BlockSpec((tks[p], tm), lambda i, j, k, kk=kk: (kk(k), i)))
        else:
            in_specs.append(pl.BlockSpec((tm, tks[p]), lambda i, j, k, kk=kk: (i, kk(k))))
        if mode == "nt":
            in_specs.append(pl.BlockSpec((tn, tks[p]), lambda i, j, k, kk=kk: (j, kk(k))))
        else:
            in_specs.append(pl.BlockSpec((tks[p], tn), lambda i, j, k, kk=kk: (kk(k), j)))
        args += [a, b]

    def body(*refs):
        ins, o_ref, scratch = refs[:2 * len(pairs)], refs[2 * len(pairs)], refs[2 * len(pairs) + 1:]
        if nk_total == 1:
            o_ref[...] = _dot(ins[0][...], ins[1][...], dims).astype(o_ref.dtype)
            return
        acc_ref, = scratch
        k = pl.program_id(2)
        for p in range(len(pairs)):
            @pl.when(jnp.logical_and(k >= starts[p], k < starts[p] + nks[p]))
            def _(p=p):
                prod = _dot(ins[2 * p][...], ins[2 * p + 1][...], dims)

                @pl.when(k == 0)
                def _():
                    acc_ref[...] = prod

                @pl.when(k > 0)
                def _():
                    acc_ref[...] += prod

        @pl.when(k == nk_total - 1)
        def _():
            o_ref[...] = acc_ref[...].astype(o_ref.dtype)

    return pl.pallas_call(
        body, name=name, grid=(m // tm, n // tn, nk_total),
        in_specs=in_specs, out_specs=pl.BlockSpec((tm, tn), lambda i, j, k: (i, j)),
        out_shape=jax.ShapeDtypeStruct((m, n), out_dtype),
        scratch_shapes=[] if nk_total == 1 else [pltpu.VMEM((tm, tn), F32)],
        compiler_params=_params(("parallel", "parallel", "arbitrary")),
    )(*args)


def rms_fwd(x, g, name):
    s, d = x.shape
    ts = _tile(s, 512, 16)

    def body(x_ref, g_ref, o_ref):
        xh, _ = _rms_hat(x_ref[...].astype(F32))
        o_ref[...] = (xh * g_ref[...]).astype(o_ref.dtype)

    return pl.pallas_call(
        body, name=name, grid=(s // ts,),
        in_specs=[pl.BlockSpec((ts, d), lambda i: (i, 0)), pl.BlockSpec((1, d), lambda i: (0, 0))],
        out_specs=pl.BlockSpec((ts, d), lambda i: (i, 0)),
        out_shape=jax.ShapeDtypeStruct((s, d), MD),
        compiler_params=_params(("parallel",)),
    )(x, g)


def rms_bwd(x, g, dn, name):
    s, d = x.shape
    ts = _tile(s, 256, 16)

    def body(x_ref, g_ref, dn_ref, dx_ref, dg_ref):
        dx, dg = _rms_bwd(x_ref[...].astype(F32), g_ref[...], dn_ref[...].astype(F32))
        dx_ref[...] = dx.astype(dx_ref.dtype)
        _acc_out(dg_ref, dg, pl.program_id(0) == 0)

    row = pl.BlockSpec((ts, d), lambda i: (i, 0))
    vec = pl.BlockSpec((1, d), lambda i: (0, 0))
    return pl.pallas_call(
        body, name=name, grid=(s // ts,), in_specs=[row, vec, row], out_specs=[row, vec],
        out_shape=[jax.ShapeDtypeStruct((s, d), MD), jax.ShapeDtypeStruct((1, d), F32)],
        compiler_params=_params(("arbitrary",)),
    )(x, g, dn)


def post_pre_fwd(x, y, g_post, g_pre, name):
    s, d = x.shape
    ts = _tile(s, 256, 16)

    def body(x_ref, y_ref, gp_ref, gq_ref, xn_ref, h_ref):
        yh, _ = _rms_hat(y_ref[...])
        xn = x_ref[...] + yh * gp_ref[...]
        xn_ref[...] = xn
        xh, _ = _rms_hat(xn)
        h_ref[...] = (xh * gq_ref[...]).astype(h_ref.dtype)

    row = pl.BlockSpec((ts, d), lambda i: (i, 0))
    vec = pl.BlockSpec((1, d), lambda i: (0, 0))
    return pl.pallas_call(
        body, name=name, grid=(s // ts,), in_specs=[row, row, vec, vec], out_specs=[row, row],
        out_shape=[jax.ShapeDtypeStruct((s, d), F32), jax.ShapeDtypeStruct((s, d), MD)],
        compiler_params=_params(("parallel",)),
    )(x, y, g_post, g_pre)


def loss_head(x, y, target, g_post, name):
    s, d = x.shape
    ts = _tile(s, 256, 16)

    def body(x_ref, y_ref, t_ref, g_ref, loss_ref, dx_ref, dy_ref, dg_ref):
        first = pl.program_id(0) == 0
        y = y_ref[...]
        yh, _ = _rms_hat(y)
        diff = x_ref[...] + yh * g_ref[...] - t_ref[...]
        part = 0.5 * jnp.sum(jnp.sum(diff * diff, axis=1, keepdims=True) * (1.0 / d), axis=0, keepdims=True)
        _acc_out(loss_ref, part, first)
        dx = diff * (1.0 / d)
        dx_ref[...] = dx
        dy, dg = _rms_bwd(y, g_ref[...], dx)
        dy_ref[...] = dy.astype(dy_ref.dtype)
        _acc_out(dg_ref, dg, first)

    row = pl.BlockSpec((ts, d), lambda i: (i, 0))
    vec = pl.BlockSpec((1, d), lambda i: (0, 0))
    one = pl.BlockSpec((1, 1), lambda i: (0, 0))
    return pl.pallas_call(
        body, name=name, grid=(s // ts,), in_specs=[row, row, row, vec], out_specs=[one, row, row, vec],
        out_shape=[jax.ShapeDtypeStruct((1, 1), F32), jax.ShapeDtypeStruct((s, d), F32),
                   jax.ShapeDtypeStruct((s, d), MD), jax.ShapeDtypeStruct((1, d), F32)],
        compiler_params=_params(("arbitrary",)),
    )(x, y, target, g_post)


def pre_post_bwd(x, y, dx_res, dh, g_pre, g_post, name):
    s, d = x.shape
    ts = _tile(s, 256, 16)

    def body(x_ref, y_ref, dr_ref, dh_ref, gq_ref, gp_ref, dx_ref, dy_ref, dgq_ref, dgp_ref):
        first = pl.program_id(0) == 0
        dxa, dgq = _rms_bwd(x_ref[...], gq_ref[...], dh_ref[...])
        dx = dr_ref[...] + dxa
        dx_ref[...] = dx
        dy, dgp = _rms_bwd(y_ref[...], gp_ref[...], dx)
        dy_ref[...] = dy.astype(dy_ref.dtype)
        _acc_out(dgq_ref, dgq, first)
        _acc_out(dgp_ref, dgp, first)

    row = pl.BlockSpec((ts, d), lambda i: (i, 0))
    vec = pl.BlockSpec((1, d), lambda i: (0, 0))
    return pl.pallas_call(
        body, name=name, grid=(s // ts,), in_specs=[row, row, row, row, vec, vec], out_specs=[row, row, vec, vec],
        out_shape=[jax.ShapeDtypeStruct((s, d), F32), jax.ShapeDtypeStruct((s, d), MD),
                   jax.ShapeDtypeStruct((1, d), F32), jax.ShapeDtypeStruct((1, d), F32)],
        compiler_params=_params(("arbitrary",)),
    )(x, y, dx_res, dh, g_pre, g_post)


def pre_bwd(x, dx_res, dh, g_pre, name):
    s, d = x.shape
    ts = _tile(s, 256, 16)

    def body(x_ref, dr_ref, dh_ref, g_ref, dx_ref, dg_ref):
        dxa, dg = _rms_bwd(x_ref[...], g_ref[...], dh_ref[...])
        dx_ref[...] = dr_ref[...] + dxa
        _acc_out(dg_ref, dg, pl.program_id(0) == 0)

    row = pl.BlockSpec((ts, d), lambda i: (i, 0))
    vec = pl.BlockSpec((1, d), lambda i: (0, 0))
    return pl.pallas_call(
        body, name=name, grid=(s // ts,), in_specs=[row, row, row, vec], out_specs=[row, vec],
        out_shape=[jax.ShapeDtypeStruct((s, d), F32), jax.ShapeDtypeStruct((1, d), F32)],
        compiler_params=_params(("arbitrary",)),
    )(x, dx_res, dh, g_pre)


def _swap_halves(x):
    lane = lax.broadcasted_iota(jnp.int32, x.shape, 1)
    return jnp.where((lane % QK_ROPE) < QK_ROPE // 2, pltpu.roll(x, LANES - QK_ROPE // 2, 1),
                     pltpu.roll(x, QK_ROPE // 2, 1))


def _rope_tables(ang_ref, sign):
    lane = lax.broadcasted_iota(jnp.int32, ang_ref.shape, 1)
    sgn = jnp.where((lane % QK_ROPE) < QK_ROPE // 2, -sign, sign)
    ang = ang_ref[...]
    return jnp.cos(ang), jnp.sin(ang) * sgn


def rope_fwd(proj, kr, kn, ang2, name):
    s = proj.shape[0]
    ts = _tile(s, 256, 16)
    wn, wr = HEADS * QK_NOPE, HEADS * QK_ROPE

    def body(qn_ref, qr_ref, kr_ref, kn_ref, ang_ref, qo_ref, ko_ref):
        c, sn = _rope_tables(ang_ref, 1.0)
        lane = lax.broadcasted_iota(jnp.int32, (ts, LANES), 1)
        lo = lane < QK_ROPE
        x = kr_ref[...].astype(F32)
        k_rot = jnp.where(lo, x * c + _swap_halves(x) * sn, 0.0).astype(ko_ref.dtype)
        for j in range(HEADS // 2):
            x = qr_ref[:, j * LANES:(j + 1) * LANES].astype(F32)
            r = x * c + _swap_halves(x) * sn
            pair = (jnp.where(lo, r, 0.0), jnp.where(lo, pltpu.roll(r, QK_ROPE, 1), 0.0))
            for e in range(2):
                h = 2 * j + e
                qo_ref[:, h * QK_PAD:h * QK_PAD + LANES] = qn_ref[:, h * LANES:(h + 1) * LANES]
                qo_ref[:, h * QK_PAD + LANES:(h + 1) * QK_PAD] = pair[e].astype(qo_ref.dtype)
                ko_ref[:, h * QK_PAD:h * QK_PAD + LANES] = kn_ref[:, h * LANES:(h + 1) * LANES]
                ko_ref[:, h * QK_PAD + LANES:(h + 1) * QK_PAD] = k_rot

    row = lambda w, blk: pl.BlockSpec((ts, w), lambda i: (i, blk))
    return pl.pallas_call(
        body, name=name, grid=(s // ts,),
        in_specs=[row(wn, 0), row(wr, wn // wr), row(LANES, 0), row(wn, 0), row(LANES, 0)],
        out_specs=[row(HEADS * QK_PAD, 0), row(HEADS * QK_PAD, 0)],
        out_shape=[jax.ShapeDtypeStruct((s, HEADS * QK_PAD), MD), jax.ShapeDtypeStruct((s, HEADS * QK_PAD), MD)],
        compiler_params=_params(("parallel",)),
    )(proj, proj, kr, kn, ang2)


def rope_bwd(dqf, dkf, ang2, name):
    s = dqf.shape[0]
    ts = _tile(s, 256, 16)
    wn, wr = HEADS * QK_NOPE, HEADS * QK_ROPE

    def body(dq_ref, dk_ref, ang_ref, qn_ref, qr_ref, kn_ref, kr_ref):
        c, sn = _rope_tables(ang_ref, -1.0)
        lane = lax.broadcasted_iota(jnp.int32, (ts, LANES), 1)
        lo = lane < QK_ROPE
        rot = lambda dy: dy * c + _swap_halves(dy) * sn
        dk_rope = None
        for j in range(HEADS // 2):
            halves = []
            for e in range(2):
                h = 2 * j + e
                qn_ref[:, h * LANES:(h + 1) * LANES] = dq_ref[:, h * QK_PAD:h * QK_PAD + LANES]
                kn_ref[:, h * LANES:(h + 1) * LANES] = dk_ref[:, h * QK_PAD:h * QK_PAD + LANES]
                halves.append(jnp.where(lo, dq_ref[:, h * QK_PAD + LANES:(h + 1) * QK_PAD].astype(F32), 0.0))
                dk_h = dk_ref[:, h * QK_PAD + LANES:(h + 1) * QK_PAD].astype(F32)
                dk_rope = dk_h if dk_rope is None else dk_rope + dk_h
            qr_ref[:, j * LANES:(j + 1) * LANES] = rot(halves[0] + pltpu.roll(halves[1], QK_ROPE, 1)).astype(qr_ref.dtype)
        kr_ref[...] = jnp.where(lo, rot(jnp.where(lo, dk_rope, 0.0)), 0.0).astype(kr_ref.dtype)

    row = lambda w: pl.BlockSpec((ts, w), lambda i: (i, 0))
    return pl.pallas_call(
        body, name=name, grid=(s // ts,),
        in_specs=[row(HEADS * QK_PAD), row(HEADS * QK_PAD), row(LANES)],
        out_specs=[row(wn), row(wr), row(wn), row(LANES)],
        out_shape=[jax.ShapeDtypeStruct((s, wn), MD), jax.ShapeDtypeStruct((s, wr), MD),
                   jax.ShapeDtypeStruct((s, wn), MD), jax.ShapeDtypeStruct((s, LANES), MD)],
        compiler_params=_params(("parallel",)),
    )(dqf, dkf, ang2)


_EXP2_SCALE = SOFTMAX_SCALE * LOG2E


def flash_fwd(qf, kf, v, name):
    s = qf.shape[0]
    t = _tile(s, 1024)
    nq = s // t
    half = t // 2
    pairs = [(i, j) for i in range(nq) for j in range(i + 1)]
    it = jnp.asarray(np.array([p[0] for p in pairs], np.int32))
    jt = jnp.asarray(np.array([p[1] for p in pairs], np.int32))
    tq = tk = t

    def body(it_ref, jt_ref, q_ref, k_ref, v_ref, o_ref, lse_ref, m_s, l_s, acc_s):
        p = pl.program_id(1)
        i, j = it_ref[p], jt_ref[p]

        @pl.when(j == 0)
        def _():
            m_s[...] = jnp.full(m_s.shape, -jnp.inf, F32)
            l_s[...] = jnp.zeros(l_s.shape, F32)
            acc_s[...] = jnp.zeros(acc_s.shape, F32)

        def step(q0, nq_rows, nk_rows, masked):
            rows, keys = pl.ds(q0, nq_rows), pl.ds(0, nk_rows)
            sc = _dot(q_ref[rows, :], k_ref[keys, :], NT)
            if masked:
                row = lax.broadcasted_iota(jnp.int32, (nq_rows, nk_rows), 0) + q0
                col = lax.broadcasted_iota(jnp.int32, (nq_rows, nk_rows), 1)
                sc = jnp.where(col <= row, sc, -jnp.inf)
            groups = nk_rows // LANES
            cols = [sc[:, g * LANES:(g + 1) * LANES] for g in range(groups)]
            mx = cols[0]
            for g in range(1, groups):
                mx = jnp.maximum(mx, cols[g])
            m_prev = m_s[rows, :]
            m_new = jnp.maximum(m_prev, jnp.max(mx, axis=1, keepdims=True))
            alpha = jnp.exp2((m_prev - m_new) * _EXP2_SCALE)
            ps, psum = [], None
            for g in range(groups):
                pg = jnp.exp2((cols[g] - m_new) * _EXP2_SCALE)
                psum = pg if psum is None else psum + pg
                ps.append(pg.astype(MD))
            l_s[rows, :] = alpha * l_s[rows, :] + psum
            acc_s[rows, :] = acc_s[rows, :] * alpha + _dot(jnp.concatenate(ps, axis=1), v_ref[keys, :])
            m_s[rows, :] = m_new

        @pl.when(j == i)
        def _():
            step(0, half, half, True)
            step(half, half, t, True)

        @pl.when(j != i)
        def _():
            step(0, t, t, False)

        @pl.when(j == i)
        def _():
            l = jnp.sum(l_s[...], axis=1, keepdims=True)
            o_ref[...] = (acc_s[...] / l).astype(o_ref.dtype)
            lse_rep = m_s[...] * _EXP2_SCALE + jnp.log(l) * LOG2E
            lse_ref[...] = _lane_sums_as_row(lse_rep * (1.0 / LANES))

    grid_spec = pltpu.PrefetchScalarGridSpec(
        num_scalar_prefetch=2, grid=(HEADS, len(pairs)),
        in_specs=[pl.BlockSpec((tq, QK_PAD), lambda h, p, it, jt: (it[p], h)),
                  pl.BlockSpec((tk, QK_PAD), lambda h, p, it, jt: (jt[p], h)),
                  pl.BlockSpec((tk, V_DIM), lambda h, p, it, jt: (jt[p], h))],
        out_specs=[pl.BlockSpec((tq, V_DIM), lambda h, p, it, jt: (it[p], h)),
                   pl.BlockSpec((None, 1, tq), lambda h, p, it, jt: (h, 0, it[p]))],
        scratch_shapes=[pltpu.VMEM((tq, LANES), F32), pltpu.VMEM((tq, LANES), F32), pltpu.VMEM((tq, V_DIM), F32)])
    return pl.pallas_call(
        body, name=name, grid_spec=grid_spec,
        out_shape=[jax.ShapeDtypeStruct((s, HEADS * V_DIM), MD), jax.ShapeDtypeStruct((HEADS, 1, s), F32)],
        compiler_params=_params(("parallel", "arbitrary")),
    )(it, jt, qf, kf, v)


def flash_bwd(qf, kf, v, o, do, lse2, name):
    s = qf.shape[0]
    t = _tile(s, 1024)
    tq = tk = t
    nq = s // t
    half = t // 2
    pairs = [(i, j) for j in range(nq) for i in range(j, nq)]
    it = jnp.asarray(np.array([p[0] for p in pairs], np.int32))
    jt = jnp.asarray(np.array([p[1] for p in pairs], np.int32))
    n_pairs = len(pairs)

    def body(it_ref, jt_ref, q_ref, k_ref, v_ref, o_ref, do_ref, lse_ref, dq_ref, dk_ref, dv_ref, dq_s, dk_s, dv_s, dl_s):
        p = pl.program_id(1)
        i, j = it_ref[p], jt_ref[p]

        @pl.when(p == 0)
        def _():
            dq_s[...] = jnp.zeros(dq_s.shape, F32)

        @pl.when(j == 0)
        def _():
            dl_s[i] = _lane_sums_as_row(o_ref[...].astype(F32) * do_ref[...].astype(F32))

        @pl.when(i == j)
        def _():
            dk_s[...] = jnp.zeros(dk_s.shape, F32)
            dv_s[...] = jnp.zeros(dv_s.shape, F32)

        def step(k0, nk_rows, q0, nq_rows, masked):
            keys, rows = pl.ds(k0, nk_rows), pl.ds(q0, nq_rows)
            q, k, do_b = q_ref[rows, :], k_ref[keys, :], do_ref[rows, :]
            pt = jnp.exp2(_dot(k, q, NT) * _EXP2_SCALE - lse_ref[:, rows])
            if masked:
                krow = lax.broadcasted_iota(jnp.int32, (nk_rows, nq_rows), 0) + k0
                qcol = lax.broadcasted_iota(jnp.int32, (nk_rows, nq_rows), 1) + q0
                pt = jnp.where(krow <= qcol, pt, 0.0)
            dv_s[keys, :] += _dot(pt.astype(MD), do_b)
            dst = (pt * (_dot(v_ref[keys, :], do_b, NT) - dl_s[i, :, rows])).astype(MD)
            dk_s[keys, :] += _dot(dst, q)
            row0 = pl.multiple_of(i * t + q0, half)
            dq_s[pl.ds(row0, nq_rows), :] += _dot(dst, k, TN)

        @pl.when(i == j)
        def _():
            step(0, half, 0, t, True)
            step(half, half, half, half, True)

        @pl.when(i != j)
        def _():
            step(0, t, 0, t, False)

        @pl.when(i == nq - 1)
        def _():
            dk_ref[...] = (dk_s[...] * SOFTMAX_SCALE).astype(dk_ref.dtype)
            dv_ref[...] = dv_s[...].astype(dv_ref.dtype)

        @pl.when(p == n_pairs - 1)
        def _():
            dq_ref[...] = (dq_s[...] * SOFTMAX_SCALE).astype(dq_ref.dtype)

    vec = pl.BlockSpec((None, 1, tq), lambda h, p, it, jt: (h, 0, it[p]))
    grid_spec = pltpu.PrefetchScalarGridSpec(
        num_scalar_prefetch=2, grid=(HEADS, n_pairs),
        in_specs=[pl.BlockSpec((tq, QK_PAD), lambda h, p, it, jt: (it[p], h)),
                  pl.BlockSpec((tk, QK_PAD), lambda h, p, it, jt: (jt[p], h)),
                  pl.BlockSpec((tk, V_DIM), lambda h, p, it, jt: (jt[p], h)),
                  pl.BlockSpec((tq, V_DIM), lambda h, p, it, jt: (it[p], h)),
                  pl.BlockSpec((tq, V_DIM), lambda h, p, it, jt: (it[p], h)), vec],
        out_specs=[pl.BlockSpec((s, QK_PAD), lambda h, p, it, jt: (0, h)),
                   pl.BlockSpec((tk, QK_PAD), lambda h, p, it, jt: (jt[p], h)),
                   pl.BlockSpec((tk, V_DIM), lambda h, p, it, jt: (jt[p], h))],
        scratch_shapes=[pltpu.VMEM((s, QK_PAD), F32), pltpu.VMEM((tk, QK_PAD), F32), pltpu.VMEM((tk, V_DIM), F32),
                        pltpu.VMEM((nq, 1, t), F32)])
    wide = jax.ShapeDtypeStruct((s, HEADS * QK_PAD), MD)
    return pl.pallas_call(
        body, name=name, grid_spec=grid_spec,
        out_shape=[wide, wide, jax.ShapeDtypeStruct((s, HEADS * V_DIM), MD)],
        compiler_params=_params(("parallel", "arbitrary")),
    )(it, jt, qf, kf, v, o, do, lse2)


def _halo_rows(dtype):
    return 8 if jnp.dtype(dtype).itemsize == 4 else 16


def _prev_spec(ts, tc, hr, col_off):
    return pl.BlockSpec((hr, tc), lambda j, i: (jnp.maximum(i * (ts // hr) - 1, 0), j + col_off))


def _conv_taps(x, halo, w, taps, first):
    hr, ts = halo.shape[0], x.shape[0]
    xe = jnp.concatenate([jnp.where(first, 0.0, halo), x], axis=0)
    shifted = []
    acc = None
    for k in range(taps):
        sh = taps - 1 - k
        xs = xe if sh == 0 else pltpu.roll(xe, sh, 0)
        xs = xs[hr:hr + ts, :]
        shifted.append(xs)
        term = xs * w[k:k + 1, :]
        acc = term if acc is None else acc + term
    return acc, shifted


def conv_bwd(x, dy, w8, taps, x_col_off, name, ts_pref=1024):
    halves = dy.ndim == 3
    s = dy.shape[-2]
    c = dy.shape[-1] * (2 if halves else 1)
    tc = _tile(dy.shape[-1], 512)
    ts = _tile(s, ts_pref, 16)
    hx, hy = _halo_rows(x.dtype), _halo_rows(dy.dtype)
    ns = s // ts
    nh = dy.shape[-1] // tc
    xo = x_col_off // tc
    assert x_col_off % tc == 0

    def body(x_ref, xp_ref, dy_ref, dyn_ref, w_ref, dx_ref, dw_ref, db_ref):
        i = pl.program_id(1)
        first = i == 0
        xv = x_ref[...].astype(F32)
        dyv = dy_ref[...].astype(F32)
        w = w_ref[...]
        _, shifted = _conv_taps(xv, xp_ref[...].astype(F32), w, taps, first)
        dye = jnp.concatenate([dyv, jnp.where(i == ns - 1, 0.0, dyn_ref[...].astype(F32))], axis=0)
        dx = None
        for k in range(taps):
            sh = taps - 1 - k
            ds_ = dye if sh == 0 else pltpu.roll(dye, ts + hy - sh, 0)
            term = ds_[0:ts, :] * w[k:k + 1, :]
            dx = term if dx is None else dx + term
        dx_ref[...] = dx.astype(dx_ref.dtype)
        rows = [jnp.sum(dyv * shifted[k], axis=0, keepdims=True) for k in range(taps)]
        rows.append(jnp.zeros((8 - taps, tc), F32))
        _acc_out(dw_ref, jnp.concatenate(rows, axis=0), first)
        _acc_out(db_ref, jnp.sum(dyv, axis=0, keepdims=True), first)

    last = s // hy - 1
    nxt = lambda i: jnp.minimum((i + 1) * (ts // hy), last)
    if halves:
        dy_spec = pl.BlockSpec((None, ts, tc), lambda j, i: (j // nh, i, j % nh))
        dyn_spec = pl.BlockSpec((None, hy, tc), lambda j, i: (j // nh, nxt(i), j % nh))
    else:
        dy_spec = pl.BlockSpec((ts, tc), lambda j, i: (i, j))
        dyn_spec = pl.BlockSpec((hy, tc), lambda j, i: (nxt(i), j))
    return pl.pallas_call(
        body, name=name, grid=(c // tc, ns),
        in_specs=[pl.BlockSpec((ts, tc), lambda j, i: (i, j + xo)), _prev_spec(ts, tc, hx, xo), dy_spec, dyn_spec,
                  pl.BlockSpec((8, tc), lambda j, i: (0, j))],
        out_specs=[pl.BlockSpec((ts, tc), lambda j, i: (i, j)), pl.BlockSpec((8, tc), lambda j, i: (0, j)),
                   pl.BlockSpec((1, tc), lambda j, i: (0, j))],
        out_shape=[jax.ShapeDtypeStruct((s, c), MD), jax.ShapeDtypeStruct((8, c), F32), jax.ShapeDtypeStruct((1, c), F32)],
        compiler_params=_params(("parallel", "arbitrary")),
    )(x, x, dy, dy, w8)


def _lru_gates(xc, wr_ref, br, wi_ref, bi, sp, nblk):
    xb = xc.astype(MD)
    zr, zi = [], []
    for b in range(nblk):
        blk = xb[:, b * LANES:(b + 1) * LANES]
        zr.append(_dot(blk, wr_ref[b]))
        zi.append(_dot(blk, wi_ref[b]))
    r = _sigmoid(jnp.concatenate(zr, axis=1) + br)
    ig = _sigmoid(jnp.concatenate(zi, axis=1) + bi)
    log_a = -LRU_C * r * sp
    a = jnp.exp(log_a)
    mult = jnp.sqrt(-_expm1(2.0 * log_a))
    return r, ig, a, mult


def lru_fwd(proj, x_off, y_off, w8, b_conv, w_rg, b_rg, w_ig, b_ig, lam, name):
    s = proj.shape[0]
    c = lam.shape[1]
    tc, ts = _tile(c, 512), _tile(s, 256, 16)
    nblk = tc // LANES
    xo, yo = x_off // tc, y_off // tc
    hr = _halo_rows(proj.dtype)

    def body(x_ref, xp_ref, y_ref, w_ref, bc_ref, wr_ref, br_ref, wi_ref, bi_ref, lam_ref, h_ref, hg_ref, carry):
        i = pl.program_id(1)
        first = i == 0
        xc, _ = _conv_taps(x_ref[...].astype(F32), xp_ref[...].astype(F32), w_ref[...], LRU_TAPS, first)
        xc = xc + bc_ref[...]
        sp = _softplus(-lam_ref[...])
        _, ig, a, mult = _lru_gates(xc, wr_ref, br_ref[...], wi_ref, bi_ref[...], sp, nblk)
        b = mult * (ig * xc)
        row = lax.broadcasted_iota(jnp.int32, (ts, tc), 0)
        d = 1
        while d < ts:
            keep = row >= d
            a_s, b_s = pltpu.roll(a, d, 0), pltpu.roll(b, d, 0)
            b = jnp.where(keep, a * b_s + b, b)
            a = jnp.where(keep, a * a_s, a)
            d *= 2
        h0 = jnp.where(first, 0.0, carry[7:8, :])
        h = a * h0 + b
        carry[...] = h[ts - 8:ts, :]
        h_ref[...] = h
        hg_ref[...] = (h * _gelu(y_ref[...].astype(F32))).astype(hg_ref.dtype)

    tile = lambda off: pl.BlockSpec((ts, tc), lambda j, i: (i, j + off))
    vec = pl.BlockSpec((1, tc), lambda j, i: (0, j))
    wblk = pl.BlockSpec((nblk, LANES, LANES), lambda j, i: (j, 0, 0))
    out = pl.BlockSpec((ts, tc), lambda j, i: (i, j))
    return pl.pallas_call(
        body, name=name, grid=(c // tc, s // ts),
        in_specs=[tile(xo), _prev_spec(ts, tc, hr, xo), tile(yo), pl.BlockSpec((8, tc), lambda j, i: (0, j)), vec,
                  wblk, vec, wblk, vec, vec],
        out_specs=[out, out],
        out_shape=[jax.ShapeDtypeStruct((s, c), F32), jax.ShapeDtypeStruct((s, c), MD)],
        scratch_shapes=[pltpu.VMEM((8, tc), F32)],
        compiler_params=_params(("parallel", "arbitrary")),
    )(proj, proj, proj, w8, b_conv, w_rg, b_rg, w_ig, b_ig, lam)


def lru_bwd(proj, x_off, y_off, h, dhg, w8, b_conv, w_rg, b_rg, w_ig, b_ig, lam, name):
    s = proj.shape[0]
    c = lam.shape[1]
    tc, ts = _tile(c, 512), _tile(s, 256, 16)
    nblk = tc // LANES
    ns = s // ts
    xo, yo = x_off // tc, y_off // tc
    hr = _halo_rows(proj.dtype)
    rev = lambda i: ns - 1 - i

    def body(x_ref, xp_ref, y_ref, h_ref, hp_ref, dhg_ref, w_ref, bc_ref, wr_ref, br_ref, wi_ref, bi_ref, lam_ref,
             dxc_ref, dy_ref, dwr_ref, dwi_ref, dbr_ref, dbi_ref, dlam_ref, a_carry, g_carry):
        step = pl.program_id(1)
        i = rev(step)
        start = step == 0
        top = i == 0
        xc, _ = _conv_taps(x_ref[...].astype(F32), xp_ref[...].astype(F32), w_ref[...], LRU_TAPS, top)
        xc = xc + bc_ref[...]
        lam_v = lam_ref[...]
        sp = _softplus(-lam_v)
        r, ig, a, mult = _lru_gates(xc, wr_ref, br_ref[...], wi_ref, bi_ref[...], sp, nblk)
        hv = h_ref[...]
        gy, dgy = _gelu_and_grad(y_ref[...].astype(F32))
        dhg_v = dhg_ref[...]
        dy_ref[...] = (dhg_v * hv * dgy).astype(dy_ref.dtype)
        row = lax.broadcasted_iota(jnp.int32, (ts, tc), 0)
        a_next = jnp.where(row == ts - 1, jnp.where(start, 0.0, a_carry[0:1, :]), pltpu.roll(a, ts - 1, 0))
        ca, cb = a_next, dhg_v * gy
        d = 1
        while d < ts:
            keep = row < ts - d
            a_s, b_s = pltpu.roll(ca, ts - d, 0), pltpu.roll(cb, ts - d, 0)
            cb = jnp.where(keep, ca * b_s + cb, cb)
            ca = jnp.where(keep, ca * a_s, ca)
            d *= 2
        g = ca * jnp.where(start, 0.0, g_carry[0:1, :]) + cb
        a_carry[...] = a[0:8, :]
        g_carry[...] = g[0:8, :]
        h_prev = jnp.where(row == 0, jnp.where(top, 0.0, hp_ref[7:8, :]), pltpu.roll(hv, 1, 0))
        da = g * h_prev
        ixc = ig * xc
        d_ixc = g * mult
        dlog_a = da * a - g * ixc * (a * a) / mult
        dzr = dlog_a * (-LRU_C * sp) * r * (1.0 - r)
        dzi = d_ixc * xc * ig * (1.0 - ig)
        dxc = d_ixc * ig
        xb, dzr_b, dzi_b = xc.astype(MD), dzr.astype(MD), dzi.astype(MD)
        parts, dwr, dwi = [], [], []
        for b in range(nblk):
            sl = slice(b * LANES, (b + 1) * LANES)
            parts.append(_dot(dzr_b[:, sl], wr_ref[b], NT) + _dot(dzi_b[:, sl], wi_ref[b], NT))
            dwr.append(_dot(xb[:, sl], dzr_b[:, sl], TN)[None])
            dwi.append(_dot(xb[:, sl], dzi_b[:, sl], TN)[None])
        dxc_ref[...] = dxc + jnp.concatenate(parts, axis=1)
        _acc_out(dwr_ref, jnp.concatenate(dwr, axis=0), start)
        _acc_out(dwi_ref, jnp.concatenate(dwi, axis=0), start)
        _acc_out(dbr_ref, jnp.sum(dzr, axis=0, keepdims=True), start)
        _acc_out(dbi_ref, jnp.sum(dzi, axis=0, keepdims=True), start)
        dsp = jnp.sum(dlog_a * (-LRU_C) * r, axis=0, keepdims=True)
        _acc_out(dlam_ref, -dsp * _sigmoid(-lam_v), start)

    vec = pl.BlockSpec((1, tc), lambda j, i: (0, j))
    wblk = pl.BlockSpec((nblk, LANES, LANES), lambda j, i: (j, 0, 0))
    tile = lambda off: pl.BlockSpec((ts, tc), lambda j, i: (rev(i), j + off))
    x_prev = pl.BlockSpec((hr, tc), lambda j, i: (jnp.maximum(rev(i) * (ts // hr) - 1, 0), j + xo))
    h_prev = pl.BlockSpec((8, tc), lambda j, i: (jnp.maximum(rev(i) * (ts // 8) - 1, 0), j))
    return pl.pallas_call(
        body, name=name, grid=(c // tc, ns),
        in_specs=[tile(xo), x_prev, tile(yo), tile(0), h_prev, tile(0), pl.BlockSpec((8, tc), lambda j, i: (0, j)), vec,
                  wblk, vec, wblk, vec, vec],
        out_specs=[tile(0), tile(0), wblk, wblk, vec, vec, vec],
        out_shape=[jax.ShapeDtypeStruct((s, c), F32), jax.ShapeDtypeStruct((s, c), MD),
                   jax.ShapeDtypeStruct(w_rg.shape, F32), jax.ShapeDtypeStruct(w_ig.shape, F32),
                   jax.ShapeDtypeStruct((1, c), F32), jax.ShapeDtypeStruct((1, c), F32), jax.ShapeDtypeStruct((1, c), F32)],
        scratch_shapes=[pltpu.VMEM((8, tc), F32), pltpu.VMEM((8, tc), F32)],
        compiler_params=_params(("parallel", "arbitrary")),
    )(proj, proj, proj, h, h, dhg, w8, b_conv, w_rg, b_rg, w_ig, b_ig, lam)


def merge_fwd(proj, gm_off, gl_off, o_mla, o_lru, name):
    s, c = o_mla.shape
    tc, ts = _tile(c, 1024), _tile(s, 512, 16)

    def body(gm_ref, gl_ref, om_ref, ol_ref, out_ref):
        out_ref[...] = (_sigmoid(gm_ref[...].astype(F32)) * om_ref[...] + _sigmoid(gl_ref[...].astype(F32)) * ol_ref[...]).astype(out_ref.dtype)

    tile = lambda off: pl.BlockSpec((ts, tc), lambda i, j: (i, j + off // tc))
    return pl.pallas_call(
        body, name=name, grid=(s // ts, c // tc), in_specs=[tile(gm_off), tile(gl_off), tile(0), tile(0)], out_specs=tile(0),
        out_shape=jax.ShapeDtypeStruct((s, c), MD), compiler_params=_params(("parallel", "parallel")),
    )(proj, proj, o_mla, o_lru)


def merge_bwd(proj, gm_off, gl_off, o_mla, o_lru, dmerged, name):
    s, c = o_mla.shape
    tc, ts = _tile(c, 1024), _tile(s, 512, 16)

    def body(gm_ref, gl_ref, om_ref, ol_ref, dm_ref, dom_ref, dol_ref, dgm_ref, dgl_ref):
        sm, sl = _sigmoid(gm_ref[...].astype(F32)), _sigmoid(gl_ref[...].astype(F32))
        dm = dm_ref[...]
        dom_ref[...] = (dm * sm).astype(dom_ref.dtype)
        dol_ref[...] = (dm * sl).astype(dol_ref.dtype)
        dgm_ref[...] = (dm * om_ref[...] * sm * (1.0 - sm)).astype(dgm_ref.dtype)
        dgl_ref[...] = (dm * ol_ref[...] * sl * (1.0 - sl)).astype(dgl_ref.dtype)

    tile = lambda off: pl.BlockSpec((ts, tc), lambda i, j: (i, j + off // tc))
    out = jax.ShapeDtypeStruct((s, c), MD)
    return pl.pallas_call(
        body, name=name, grid=(s // ts, c // tc), in_specs=[tile(gm_off), tile(gl_off), tile(0), tile(0), tile(0)],
        out_specs=[tile(0)] * 4, out_shape=[out] * 4, compiler_params=_params(("parallel", "parallel")),
    )(proj, proj, o_mla, o_lru, dmerged)


def xattn_fwd(cq, ck, cv, name):
    s = cq.shape[0]
    m = ck.shape[0]
    ts = _tile(s, 512, 16)
    scale = X_DIM ** -0.5

    def body(q_ref, k_ref, v_ref, o_ref):
        for h in range(X_HEADS):
            sl = slice(h * X_DIM, (h + 1) * X_DIM)
            sc = _dot(q_ref[:, sl], k_ref[:, sl], NT) * scale
            e = jnp.exp(sc - jnp.max(sc, axis=1, keepdims=True))
            p = e / jnp.sum(e, axis=1, keepdims=True)
            o_ref[:, sl] = _dot(p.astype(MD), v_ref[:, sl]).astype(o_ref.dtype)

    w = X_HEADS * X_DIM
    return pl.pallas_call(
        body, name=name, grid=(s // ts,),
        in_specs=[pl.BlockSpec((ts, w), lambda i: (i, 0)), pl.BlockSpec((m, w), lambda i: (0, 0)), pl.BlockSpec((m, w), lambda i: (0, 0))],
        out_specs=pl.BlockSpec((ts, w), lambda i: (i, 0)), out_shape=jax.ShapeDtypeStruct((s, w), MD),
        compiler_params=_params(("parallel",)),
    )(cq, ck, cv)


def xattn_bwd(cq, ck, cv, dco, name):
    s = cq.shape[0]
    m = ck.shape[0]
    ts = _tile(s, 512, 16)
    scale = X_DIM ** -0.5

    def body(q_ref, k_ref, v_ref, do_ref, dq_ref, dk_ref, dv_ref):
        first = pl.program_id(0) == 0
        dks, dvs = [], []
        for h in range(X_HEADS):
            sl = slice(h * X_DIM, (h + 1) * X_DIM)
            q, k, v, do = q_ref[:, sl], k_ref[:, sl], v_ref[:, sl], do_ref[:, sl]
            sc = _dot(q, k, NT) * scale
            e = jnp.exp(sc - jnp.max(sc, axis=1, keepdims=True))
            p = e / jnp.sum(e, axis=1, keepdims=True)
            dvs.append(_dot(p.astype(MD), do, TN))
            dp = _dot(do, v, NT)
            ds = (p * (dp - jnp.sum(dp * p, axis=1, keepdims=True)) * scale).astype(MD)
            dq_ref[:, sl] = _dot(ds, k).astype(dq_ref.dtype)
            dks.append(_dot(ds, q, TN))
        _acc_out(dk_ref, jnp.concatenate(dks, axis=1), first)
        _acc_out(dv_ref, jnp.concatenate(dvs, axis=1), first)

    w = X_HEADS * X_DIM
    row = pl.BlockSpec((ts, w), lambda i: (i, 0))
    full = pl.BlockSpec((m, w), lambda i: (0, 0))
    return pl.pallas_call(
        body, name=name, grid=(s // ts,), in_specs=[row, full, full, row], out_specs=[row, full, full],
        out_shape=[jax.ShapeDtypeStruct((s, w), MD), jax.ShapeDtypeStruct((m, w), F32), jax.ShapeDtypeStruct((m, w), F32)],
        compiler_params=_params(("arbitrary",)),
    )(cq, ck, cv, dco)


def _ffn_specs(up_pre):
    s, c2 = up_pre.shape
    f = c2 // 2
    tc, ts = _tile(f, 512), _tile(s, 1024, 16)
    nf = f // tc
    hr = _halo_rows(up_pre.dtype)
    tile = lambda off: pl.BlockSpec((ts, tc), lambda j, i: (i, j + off))
    w = lambda off: pl.BlockSpec((8, tc), lambda j, i: (0, j + off))
    vec = lambda off: pl.BlockSpec((1, tc), lambda j, i: (0, j + off))
    specs = [tile(0), _prev_spec(ts, tc, hr, 0), tile(nf), _prev_spec(ts, tc, hr, nf), w(0), w(nf), vec(0), vec(nf)]
    return s, f, tc, ts, nf, tile, specs


def _ffn_gate_val(g_ref, gp_ref, v_ref, vp_ref, wg_ref, wv_ref, bg_ref, bv_ref, first):
    gate, _ = _conv_taps(g_ref[...].astype(F32), gp_ref[...].astype(F32), wg_ref[...], FFN_TAPS, first)
    val, _ = _conv_taps(v_ref[...].astype(F32), vp_ref[...].astype(F32), wv_ref[...], FFN_TAPS, first)
    return gate + bg_ref[...], val + bv_ref[...]


def ffn_act_fwd(up_pre, w8, b, name):
    s, f, tc, ts, nf, tile, specs = _ffn_specs(up_pre)

    def body(*refs):
        o_ref = refs[8]
        gate, val = _ffn_gate_val(*refs[:8], pl.program_id(1) == 0)
        o_ref[...] = (_gelu(gate) * val).astype(o_ref.dtype)

    return pl.pallas_call(
        body, name=name, grid=(nf, s // ts), in_specs=specs, out_specs=tile(0),
        out_shape=jax.ShapeDtypeStruct((s, f), MD), compiler_params=_params(("parallel", "parallel")),
    )(up_pre, up_pre, up_pre, up_pre, w8, w8, b, b)


def ffn_act_bwd(up_pre, w8, b, dact, name):
    s, f, tc, ts, nf, tile, specs = _ffn_specs(up_pre)

    def body(*refs):
        da_ref, o_ref = refs[8], refs[9]
        gate, val = _ffn_gate_val(*refs[:8], pl.program_id(1) == 0)
        ge, dge = _gelu_and_grad(gate)
        da = da_ref[...].astype(F32)
        o_ref[0] = (da * val * dge).astype(o_ref.dtype)
        o_ref[1] = (da * ge).astype(o_ref.dtype)

    return pl.pallas_call(
        body, name=name, grid=(nf, s // ts), in_specs=specs + [tile(0)],
        out_specs=pl.BlockSpec((2, ts, tc), lambda j, i: (0, i, j)), out_shape=jax.ShapeDtypeStruct((2, s, f), MD),
        compiler_params=_params(("parallel", "parallel")),
    )(up_pre, up_pre, up_pre, up_pre, w8, w8, b, b, dact)


def _adam(w, g, m, v):
    m = ADAM_B1 * m + (1.0 - ADAM_B1) * g
    v = ADAM_B2 * v + (1.0 - ADAM_B2) * (g * g)
    m_hat = m / (1.0 - ADAM_B1 ** ADAM_STEP)
    v_hat = v / (1.0 - ADAM_B2 ** ADAM_STEP)
    delta = -ADAM_LR * (m_hat / (jnp.sqrt(v_hat) + ADAM_EPS) + ADAM_WD * w)
    return delta, m, v


def _rows_tile(r, c):
    return _tile(r, max(16, (1 << 18) // c // 16 * 16), 16)


def adamw_sharded(mine, peers, w, m, v, name):
    r, c = w.shape
    tr = _rows_tile(r, c)

    def body(g_ref, p_ref, w_ref, m_ref, v_ref, go_ref, d_ref, mo_ref, vo_ref):
        grad = g_ref[...].astype(F32)
        for k in range(N_DEV - 1):
            grad = grad + p_ref[k].astype(F32)
        go_ref[...] = grad
        d_ref[...], mo_ref[...], vo_ref[...] = _adam(w_ref[...], grad, m_ref[...], v_ref[...])

    row = pl.BlockSpec((tr, c), lambda i: (i, 0))
    out = jax.ShapeDtypeStruct((r, c), F32)
    return pl.pallas_call(
        body, name=name, grid=(r // tr,),
        in_specs=[row, pl.BlockSpec((N_DEV - 1, tr, c), lambda i: (0, i, 0)), row, row, row],
        out_specs=[row] * 4, out_shape=[out] * 4, compiler_params=_params(("parallel",)),
    )(mine, peers, w, m, v)


def adamw_plain(g, w, m, v, name):
    r, c = w.shape
    tr = _tile(r, 512, 8)

    def body(g_ref, w_ref, m_ref, v_ref, d_ref, mo_ref, vo_ref):
        d_ref[...], mo_ref[...], vo_ref[...] = _adam(w_ref[...], g_ref[...], m_ref[...], v_ref[...])

    row = pl.BlockSpec((tr, c), lambda i: (i, 0))
    out = jax.ShapeDtypeStruct((r, c), F32)
    return pl.pallas_call(body, name=name, grid=(r // tr,), in_specs=[row] * 4, out_specs=[row] * 3, out_shape=[out] * 3,
                          compiler_params=_params(("parallel",)))(g, w, m, v)


def sum_devices(parts, name):
    _, r, c = parts.shape
    tr = _tile(r, 256, 8)

    def body(p_ref, o_ref):
        acc = p_ref[0]
        for d in range(1, N_DEV):
            acc = acc + p_ref[d]
        o_ref[...] = acc

    return pl.pallas_call(
        body, name=name, grid=(r // tr,), in_specs=[pl.BlockSpec((N_DEV, tr, c), lambda i: (0, i, 0))],
        out_specs=pl.BlockSpec((tr, c), lambda i: (i, 0)), out_shape=jax.ShapeDtypeStruct((r, c), F32),
        compiler_params=_params(("parallel",)))(parts)


def _place():
    return lax.axis_index("x"), lax.axis_index("y"), lax.axis_index("c")


def _other_chips(x, y):
    return [(1 - x, y), (x, 1 - y), (1 - x, 1 - y)]


_ANY = pl.BlockSpec(memory_space=pl.ANY)


def all_gather(shards, name):
    n = len(shards)

    def body(*refs):
        x_refs, out_refs = refs[:n], refs[n:2 * n]
        send_sems, recv_sems, local_sems = refs[2 * n:]
        x, y, cc = _place()
        me, sibling = (x, y, cc), (x, y, 1 - cc)
        chips = _other_chips(x, y)

        def slot(w, px, py, pc):
            return out_refs[w].at[4 * px + 2 * py + pc]

        def copy(k, w, block, to, src=None):
            return pltpu.make_async_remote_copy(
                src_ref=slot(w, *block) if src is None else src, dst_ref=slot(w, *block),
                send_sem=send_sems.at[k, w], recv_sem=recv_sems.at[k, w], device_id=to, device_id_type=MESH)

        mine = [pltpu.make_async_copy(x_refs[w], slot(w, *me), local_sems.at[w]) for w in range(n)]
        for cp in mine:
            cp.start()
        first = [copy(0, w, me, sibling, src=x_refs[w]) for w in range(n)]
        first += [copy(1 + j, w, me, (*chip, cc), src=x_refs[w]) for j, chip in enumerate(chips) for w in range(n)]
        for cp in first:
            cp.start()
        passed = []
        for j, chip in enumerate(chips):
            for w in range(n):
                copy(1 + j, w, (*chip, cc), me).wait_recv()
                passed.append(copy(4 + j, w, (*chip, cc), sibling))
                passed[-1].start()
        for w in range(n):
            copy(0, w, sibling, me).wait_recv()
        for j, chip in enumerate(chips):
            for w in range(n):
                copy(4 + j, w, (*chip, 1 - cc), me).wait_recv()
        for cp in first + passed:
            cp.wait_send()
        for cp in mine:
            cp.wait()

    return pl.pallas_call(
        body, name=name, out_shape=[jax.ShapeDtypeStruct((N_DEV, *a.shape), a.dtype) for a in shards],
        in_specs=[_ANY] * n, out_specs=[_ANY] * n,
        scratch_shapes=[pltpu.SemaphoreType.DMA((7, n)), pltpu.SemaphoreType.DMA((7, n)), pltpu.SemaphoreType.DMA((n,))],
    )(*shards)


_HBM = pl.BlockSpec(memory_space=pltpu.HBM)
_SEM = pl.BlockSpec(memory_space=pltpu.SEMAPHORE)
_DATAFLOW = pltpu.SideEffectType.DATAFLOW_SIDE_EFFECTING
_PEER_FLIPS = ((0, 0, 1), (1, 0, 0), (1, 0, 1), (0, 1, 0), (0, 1, 1), (1, 1, 0), (1, 1, 1))
N_PEERS = len(_PEER_FLIPS)


def _peer_copies(kind, src_refs, land_refs, send_sems, recv_sems):
    x, y, c = _place()
    n = len(src_refs)
    copies = []
    for w in range(n):
        for k, (fx, fy, fc) in enumerate(_PEER_FLIPS):
            px, py, pc = (1 - x if fx else x), (1 - y if fy else y), (1 - c if fc else c)
            if kind == "scatter":
                src, dst = src_refs[w].at[4 * px + 2 * py + pc], land_refs[w].at[k]
            else:
                src, dst = src_refs[w], land_refs[w].at[4 * x + 2 * y + c]
            copies.append(pltpu.make_async_remote_copy(
                src_ref=src, dst_ref=dst, send_sem=send_sems.at[k * n + w], recv_sem=recv_sems.at[k * n + w],
                device_id=(px, py, pc), device_id_type=MESH))
    return copies


def send_start(kind, srcs, name):
    n = len(srcs)
    if kind == "scatter":
        lands = [lax.empty((N_PEERS, *a.shape[1:]), a.dtype) for a in srcs]
    else:
        lands = [lax.empty((N_DEV, *a.shape), a.dtype) for a in srcs]

    def body(*refs):
        src_refs, land_refs = refs[:n], refs[n:2 * n]
        send_sems, recv_sems = refs[2 * n], refs[2 * n + 1]
        token = refs[-1]
        for cp in _peer_copies(kind, src_refs, land_refs, send_sems, recv_sems):
            cp.start()
        token[...] = jnp.zeros_like(token)

    hbm = lambda a: pltpu.with_memory_space_constraint(a, pltpu.HBM)
    outs = pl.pallas_call(
        body, name=name,
        out_shape=(pltpu.SemaphoreType.DMA((N_PEERS * n,)), pltpu.SemaphoreType.DMA((N_PEERS * n,)),
                   *[pltpu.HBM(a.shape, a.dtype) for a in srcs + lands], jax.ShapeDtypeStruct((8, LANES), F32)),
        in_specs=[_HBM] * (2 * n),
        out_specs=(_SEM, _SEM, *[_HBM] * (2 * n), pl.BlockSpec(memory_space=pltpu.VMEM)),
        input_output_aliases={i: 2 + i for i in range(2 * n)},
        compiler_params=pltpu.CompilerParams(has_side_effects=_DATAFLOW),
    )(*[hbm(a) for a in srcs + lands])
    return (kind, outs[0], outs[1], list(outs[2:2 + n]), list(outs[2 + n:2 + 2 * n])), outs[-1]


def send_wait(handle, after, name):
    kind, send_sems, recv_sems, srcs, lands = handle
    n = len(srcs)

    def body(*refs):
        src_refs, land_refs = refs[:n], refs[n:2 * n]
        for cp in _peer_copies(kind, src_refs, land_refs, refs[2 * n], refs[2 * n + 1]):
            cp.wait_send()
            cp.wait_recv()

    outs = pl.pallas_call(
        body, name=name, out_shape=tuple(pltpu.HBM(a.shape, a.dtype) for a in srcs + lands),
        in_specs=[_HBM] * (2 * n) + [_SEM, _SEM, _ANY], out_specs=tuple([_HBM] * (2 * n)),
        input_output_aliases={i: i for i in range(2 * n)},
        compiler_params=pltpu.CompilerParams(has_side_effects=_DATAFLOW),
    )(*srcs, *lands, send_sems, recv_sems, after)
    return list(outs[:n]), list(outs[n:])


class _NoExchange:
    def other_weights(self, after):
        return {}

    def send_grads(self, G, names):
        return None

    def send_small_grads(self, G):
        return None


def _after(a, token):
    return a if token is None else a + token[0:1, 0:1].astype(a.dtype)


def _step_local(x, mem, ang2, target, W, hooks=None):
    hooks = hooks or _NoExchange()
    W = dict(W)
    d = x.shape[1]
    wq = HEADS * QK_ROPE
    o_lx, o_ly, o_gm, o_gl = d + wq, 2 * d + wq, 3 * d + wq, 4 * d + wq
    one = lambda a, b, mode, dt, name, **kw: mm([(a, b)], mode, dt, name, **kw)

    h1 = rms_fwd(x, W["g_pre_mix"], "h1")
    proj = one(h1, W["w_main"], "nn", MD, "proj")
    ckv = one(h1, W["w_ckv"], "nn", MD, "ckv")
    kr = one(h1, W["w_kr"], "nn", MD, "k_rope")
    ckv_n = rms_fwd(ckv, W["g_ckv"], "ckv_n")
    kn =one(ckv_n, W["w_uk"], "nn", MD, "k_nope")
    v = one(ckv_n, W["w_uv"], "nn", MD, "v")
    qf, kf = rope_fwd(proj, kr, kn, ang2, "rope")
    o, lse2 = flash_fwd(qf, kf, v, "attn")
    W.update(hooks.other_weights(o))
    o_mla = one(o, W["w_o_mla"], "nn", F32, "o_mla")
    lru_w = (W["w_conv8"], W["b_conv_lru"], W["w_rg"], W["b_rg"], W["w_ig"], W["b_ig"], W["lru_lambda"])
    hr, hg = lru_fwd(proj, o_lx, o_ly, *lru_w, "lru")
    o_lru = one(hg, W["w_o_lru"], "nn", F32, "o_lru")
    merged = merge_fwd(proj, o_gm, o_gl, o_mla, o_lru, "merge")
    y1 = one(merged, W["w_out"], "nn", F32, "y1")
    x1, h2 = post_pre_fwd(x, y1, W["g_post_mix"], W["g_pre_x"], "x1")
    mn = rms_fwd(mem, W["g_mem"], "mem_n")
    ck = one(mn, W["w_ck"], "nn", MD, "ck")
    cv = one(mn, W["w_cv"], "nn", MD, "cv")
    cq = one(h2, W["w_cq"], "nn", MD, "cq")
    co = xattn_fwd(cq, ck, cv, "xattn")
    y2 = one(co, W["w_co"], "nn", F32, "y2")
    x2, h3 = post_pre_fwd(x1, y2, W["g_post_x"], W["g_pre_ffn"], "x2")
    up_pre = one(h3, W["w_up"], "nn", MD, "up_pre")
    act = ffn_act_fwd(up_pre, W["w_fconv8"], W["b_fconv"], "act")
    y3 = one(act, W["w_down"], "nn", F32, "y3", tk=2816)
    G = {}
    loss, dx3, dy3, G["g_post_ffn"] = loss_head(x2, y3, target, W["g_post_ffn"], "loss")

    G["w_down"] = one(act, dy3, "tn", MD, "dw_down")
    sent = hooks.send_grads(G, ("w_down",))
    dact = one(dy3, W["w_down"], "nt", MD, "dact")
    dup = ffn_act_bwd(up_pre, W["w_fconv8"], _after(W["b_fconv"], sent), dact, "dup")
    dup_pre, dwf8, G["b_fconv"] = conv_bwd(up_pre, dup, W["w_fconv8"], FFN_TAPS, 0, "ffn_conv_bwd")
    G["w_fconv"] = dwf8[:FFN_TAPS]
    G["w_up"] = one(h3, dup_pre, "tn", MD, "dw_up")
    sent = hooks.send_grads(G, ("w_up",))
    dh3 = one(dup_pre, W["w_up"], "nt", F32, "dh3", tn=2048, tk=1024)
    dx2, dy2, G["g_pre_ffn"], G["g_post_x"] = pre_post_bwd(x2, y2, dx3, dh3, _after(W["g_pre_ffn"], sent), W["g_post_x"], "dx2")
    G["w_co"] = one(co, dy2, "tn", MD, "dw_co")
    dco = one(dy2, W["w_co"], "nt", MD, "dco")
    dcq, dck, dcv = xattn_bwd(cq, ck, cv, dco, "xattn_bwd")
    dck, dcv = dck.astype(MD), dcv.astype(MD)
    G["w_cq"] = one(h2, dcq, "tn", MD, "dw_cq")
    G["w_ck"] = one(mn, dck, "tn", MD, "dw_ck")
    G["w_cv"] = one(mn, dcv, "tn", MD, "dw_cv")
    sent = hooks.send_grads(G, ("w_co", "w_cq", "w_ck", "w_cv"))
    dmn = mm([(dck, W["w_ck"]), (dcv, W["w_cv"])], "nt", F32, "dmem_n")
    _, G["g_mem"] = rms_bwd(mem, W["g_mem"], dmn, "dg_mem")
    dh2 = one(dcq, W["w_cq"], "nt", F32, "dh2")
    dx1, dy1, G["g_pre_x"], G["g_post_mix"] = pre_post_bwd(x1, y1, dx2, dh2, _after(W["g_pre_x"], sent), W["g_post_mix"], "dx1")
    G["w_out"] = one(merged, dy1, "tn", MD, "dw_out")
    dmerged = one(dy1, W["w_out"], "nt", F32, "dmerged")
    do_mla, do_lru, dgm, dgl = merge_bwd(proj, o_gm, o_gl, o_mla, o_lru, dmerged, "merge_bwd")
    G["w_o_mla"] = one(o, do_mla, "tn", MD, "dw_o_mla")
    do = one(do_mla, W["w_o_mla"], "nt", MD, "do")
    G["w_o_lru"] = one(hg, do_lru, "tn", MD, "dw_o_lru")
    sent = hooks.send_grads(G, ("w_out", "w_o_mla", "w_o_lru"))
    dhg = one(do_lru, W["w_o_lru"], "nt", F32, "dhg")
    lru_w = lru_w[:-1] + (_after(lru_w[-1], sent),)
    dxc, dly, G["w_rg"], G["w_ig"], G["b_rg"], G["b_ig"], G["lru_lambda"] = lru_bwd(proj, o_lx, o_ly, hr, dhg, *lru_w, "lru_bwd")
    dlx, dwc8, G["b_conv_lru"] = conv_bwd(proj, dxc, W["w_conv8"], LRU_TAPS, o_lx, "lru_conv_bwd")
    G["w_conv_lru"] = dwc8[:LRU_TAPS]
    dqf, dkf, dv = flash_bwd(qf, kf, v, o, do, lse2, "attn_bwd")
    dqn, dqr, dkn, dkr = rope_bwd(dqf, dkf, ang2, "rope_bwd")
    G["w_uk"] = one(ckv_n, dkn, "tn", MD, "dw_uk")
    G["w_uv"] = one(ckv_n, dv, "tn", MD, "dw_uv")
    dckv_n = mm([(dkn, W["w_uk"]), (dv, W["w_uv"])], "nt", F32, "dckv_n")
    dckv, G["g_ckv"] = rms_bwd(ckv, W["g_ckv"], dckv_n, "dckv")
    dckv = _after(dckv, hooks.send_small_grads(G))
    dproj = jnp.concatenate([dqn, dqr, dlx, dly, dgm, dgl], axis=1)
    parts = [("w_main", dproj), ("w_ckv", dckv), ("w_kr", dkr)]
    for n, dpart in parts:
        G[n] = one(h1, dpart, "tn", MD, "d" + n)
    sent = hooks.send_grads(G, ("w_ukv", "w_in"))
    parts[-1] = ("w_kr", _after(dkr, sent))
    dh1 = mm([(dpart, W[n]) for n, dpart in parts], "nt", F32, "dh1", tm=512, tn=2048, tk=1024)
    grad_x, G["g_pre_mix"] = pre_bwd(x, dx1, dh1, W["g_pre_mix"], "grad_x")
    return loss, grad_x, G


FLAT_COLS = 1024
BIG = (("w_in", 1), ("w_ukv", 1), ("w_o_mla", 0), ("w_o_lru", 0), ("w_out", 0), ("w_cq", 0), ("w_ck", 0), ("w_cv", 0),
       ("w_co", 1), ("w_up", 1), ("w_down", 0))
SMALL_SHARDED = ("w_conv_lru", "w_fconv")
REPLICATED = ("g_pre_mix", "g_post_mix", "g_ckv", "b_conv_lru", "w_rg", "b_rg", "w_ig", "b_ig", "lru_lambda",
              "g_pre_x", "g_post_x", "g_mem", "g_pre_ffn", "g_post_ffn", "b_fconv")
WEIGHTS = ("g_pre_mix", "g_post_mix", "w_in", "g_ckv", "w_ukv", "w_o_mla", "w_conv_lru", "b_conv_lru", "w_rg", "b_rg",
           "w_ig", "b_ig", "lru_lambda", "w_o_lru", "w_out", "g_pre_x", "g_post_x", "g_mem", "w_cq", "w_ck", "w_cv",
           "w_co", "g_pre_ffn", "g_post_ffn", "w_up", "w_fconv", "b_fconv", "w_down")


def _round_up(n, k):
    return (n + k - 1) // k * k


def _pack_vec(parts, row_mult):
    flat = jnp.concatenate([p.reshape(-1) for p in parts])
    n = flat.shape[0]
    total = _round_up(n, row_mult * FLAT_COLS)
    return jnp.pad(flat, (0, total - n)).reshape(-1, FLAT_COLS)


def _unpack_vec(flat2d, shapes):
    flat = flat2d.reshape(-1)
    out, off = [], 0
    for shp in shapes:
        n = int(np.prod(shp))
        out.append(flat[off:off + n].reshape(shp))
        off += n
    return out


def _gathered_to_full(blocks, axis):
    n, r, c = blocks.shape
    return blocks.reshape(n * r, c) if axis == 0 else blocks.transpose(1, 0, 2).reshape(r, n * c)


def _full_to_shards(full, axis):
    r, c = full.shape
    if axis == 0:
        return full.reshape(N_DEV, r // N_DEV, c)
    return full.reshape(r, N_DEV, c // N_DEV).transpose(1, 0, 2)


def _split_w_in(w_in, d):
    hq = HEADS * (QK_NOPE + QK_ROPE)
    q = w_in[:, :hq].reshape(d, HEADS, QK_NOPE + QK_ROPE)
    q_nope, q_rope = q[:, :, :QK_NOPE].reshape(d, HEADS * QK_NOPE), q[:, :, QK_NOPE:].reshape(d, HEADS * QK_ROPE)
    out = {"w_ckv": w_in[:, hq:hq + KV_RANK]}
    off = hq + KV_RANK
    out["w_kr"] = jnp.pad(w_in[:, off:off + QK_ROPE], ((0, 0), (0, LANES - QK_ROPE)))
    out["w_main"] = jnp.concatenate([q_nope, q_rope, w_in[:, off + QK_ROPE:]], axis=1)
    return out


def _join_w_in(G, d):
    gm = G["w_main"]
    n_qn, n_qr = HEADS * QK_NOPE, HEADS * QK_ROPE
    q = jnp.concatenate([gm[:, :n_qn].reshape(d, HEADS, QK_NOPE), gm[:, n_qn:n_qn + n_qr].reshape(d, HEADS, QK_ROPE)], axis=2)
    return jnp.concatenate([q.reshape(d, -1), G["w_ckv"], G["w_kr"][:, :QK_ROPE], gm[:, n_qn + n_qr:]], axis=1)


def kernel(x, mem, positions, g_pre_mix, g_post_mix, w_in, g_ckv, w_ukv, w_o_mla, w_conv_lru, b_conv_lru, w_rg, b_rg, w_ig, b_ig, lru_lambda, w_o_lru, w_out, g_pre_x, g_post_x, g_mem, w_cq, w_ck, w_cv, w_co, g_pre_ffn, g_post_ffn, w_up, w_fconv, b_fconv, w_down, loss_target, m_g_pre_mix, m_g_post_mix, m_w_in, m_g_ckv, m_w_ukv, m_w_o_mla, m_w_conv_lru, m_b_conv_lru, m_w_rg, m_b_rg, m_w_ig, m_b_ig, m_lru_lambda, m_w_o_lru, m_w_out, m_g_pre_x, m_g_post_x, m_g_mem, m_w_cq, m_w_ck, m_w_cv, m_w_co, m_g_pre_ffn, m_g_post_ffn, m_w_up, m_w_fconv, m_b_fconv, m_w_down, v_g_pre_mix, v_g_post_mix, v_w_in, v_g_ckv, v_w_ukv, v_w_o_mla, v_w_conv_lru, v_b_conv_lru, v_w_rg, v_b_rg, v_w_ig, v_b_ig, v_lru_lambda, v_w_o_lru, v_w_out, v_g_pre_x, v_g_post_x, v_g_mem, v_w_cq, v_w_ck, v_w_cv, v_w_co, v_g_pre_ffn, v_g_post_ffn, v_w_up, v_w_fconv, v_b_fconv, v_w_down):
    given = dict(locals())
    P = {n: given[n][0] for n in WEIGHTS}
    M = {n: given["m_" + n][0] for n in WEIGHTS}
    V = {n: given["v_" + n][0] for n in WEIGHTS}
    xs, mems, tgt = x[0], mem[0], loss_target[0]
    s, d = xs.shape
    ax, ay, ac = _place()
    dev = 4 * ax + 2 * ay + ac

    axis_of = dict(BIG)
    axis_of.update({n: 1 for n in SMALL_SHARDED})
    wire = lambda n: P[n] if n in SMALL_SHARDED else P[n].astype(MD)
    gathered_in, gathered_kv = all_gather([wire("w_in"), wire("w_ukv")], "gather_first")
    later_names = ("w_o_mla", "w_conv_lru", "w_o_lru", "w_out", "w_ck", "w_cv", "w_cq", "w_co", "w_up", "w_fconv", "w_down")
    gathered_in, later_srcs = lax.optimization_barrier((gathered_in, [wire(n) for n in later_names]))
    later_handle, started_later = send_start("gather", later_srcs, "start_gather_rest")

    def landed(names, handle, after, name):
        srcs, lands = send_wait(handle, after, name)
        full = {}
        for n, src, land in zip(names, srcs, lands):
            full[n] = _gathered_to_full(lax.dynamic_update_slice(land, src[None], (dev, 0, 0)), axis_of[n])
        return full

    W = _split_w_in(_gathered_to_full(gathered_in, 1), d)
    W["w_rg"], W["w_ig"] = P["w_rg"].astype(MD), P["w_ig"].astype(MD)
    for n in ("g_pre_mix", "g_post_mix", "g_ckv", "b_conv_lru", "b_rg", "b_ig", "lru_lambda", "g_pre_x", "g_post_x", "g_mem",
              "g_pre_ffn", "g_post_ffn", "b_fconv"):
        W[n] = P[n].reshape(1, -1)
    kv = _gathered_to_full(gathered_kv, 1).reshape(KV_RANK, HEADS, QK_NOPE + V_DIM)
    W["w_uk"] = kv[:, :, :QK_NOPE].reshape(KV_RANK, HEADS * QK_NOPE)
    W["w_uv"] = kv[:, :, QK_NOPE:].reshape(KV_RANK, HEADS * V_DIM)
    W["g_pre_mix"] = _after(W["g_pre_mix"], started_later)

    inv_freq = ROPE_THETA ** (-jnp.arange(0, QK_ROPE, 2, dtype=F32) / QK_ROPE)
    ang = positions[0].astype(F32)[:, None] * inv_freq
    ang2 = jnp.tile(ang, (1, LANES // (QK_ROPE // 2)))

    pending = []
    small_pending = []
    gate_names = ("w_rg", "w_ig")
    small_names = tuple(n for n in REPLICATED + SMALL_SHARDED if n not in gate_names)
    early_small = tuple(n for n in small_names if n != "g_pre_mix")
    as_rows = lambda a: a.reshape(-1, LANES)

    class Exchange:
        def other_weights(self, after):
            full = landed(later_names, later_handle, after, "wait_gather_rest")
            full["w_conv8"] = jnp.pad(full.pop("w_conv_lru"), ((0, 8 - LRU_TAPS), (0, 0)))
            full["w_fconv8"] = jnp.pad(full.pop("w_fconv"), ((0, 8 - FFN_TAPS), (0, 0)))
            return full

        def send_grads(self, G, names):
            if "w_in" in names:
                G["w_in"] = _join_w_in(G, d)
                G["w_ukv"] = jnp.concatenate([G["w_uk"].reshape(KV_RANK, HEADS, QK_NOPE),
                                              G["w_uv"].reshape(KV_RANK, HEADS, V_DIM)], axis=2).reshape(KV_RANK, -1)
            handle, started = send_start("scatter", [_full_to_shards(G[n], axis_of[n]) for n in names], "start_grads_" + names[0])
            pending.append((names, handle))
            return started

        def send_small_grads(self, G):
            srcs = [_pack_vec([G[n] for n in early_small], 8)] + [as_rows(G[n]) for n in gate_names]
            handle, started = send_start("gather", srcs, "start_small_grads")
            small_pending.append(handle)
            return started

    loss, grad_x, G = _step_local(xs, mems, ang2, tgt, W, Exchange())
    loss = lax.psum(loss[0, 0], ("x", "y", "c"))

    res = {k: {} for k in ("grad", "delta", "m", "v")}
    for names, handle in pending:
        slabs, peers = send_wait(handle, grad_x, "wait_grads_" + names[0])
        for n, slab, peer in zip(names, slabs, peers):
            mine = lax.dynamic_index_in_dim(slab, dev, axis=0, keepdims=False)
            outs = adamw_sharded(mine, peer, P[n], M[n], V[n], "adamw_" + n)
            for kind, o in zip(("grad", "delta", "m", "v"), outs):
                res[kind][n] = o

    owns, lands = send_wait(small_pending[0], grad_x, "wait_small_grads")
    sums = [sum_devices(lax.dynamic_update_slice(land, own[None], (dev, 0, 0)), "sum_small_grads_%d" % k)
            for k, (own, land) in enumerate(zip(owns, lands))]
    summed = dict(zip(early_small, _unpack_vec(sums[0], [G[n].shape for n in early_small])))
    for n, total in zip(gate_names, sums[1:]):
        outs = adamw_plain(total, as_rows(P[n]), as_rows(M[n]), as_rows(V[n]), "adamw_" + n)
        for kind, o in zip(("grad", "delta", "m", "v"), (total,) + tuple(outs)):
            res[kind][n] = o.reshape(P[n].shape)
    last, = all_gather([_pack_vec([G["g_pre_mix"]], 8)], "gather_last_grad")
    summed["g_pre_mix"], = _unpack_vec(sum_devices(last, "sum_last_grad"), [G["g_pre_mix"].shape])
    for n in SMALL_SHARDED:
        t, c = summed[n].shape
        summed[n] = lax.dynamic_index_in_dim(summed[n].reshape(t, N_DEV, c // N_DEV), dev, axis=1, keepdims=False)
    sg = {n: summed[n].reshape(P[n].shape) for n in small_names}
    vec = lambda src: _pack_vec([src[n] for n in small_names], 8)
    small_out = adamw_plain(vec(sg), vec(P), vec(M), vec(V), "adamw_small")
    shapes = [P[n].shape for n in small_names]
    for kind, flat in zip(("delta", "m", "v"), small_out):
        res[kind].update(zip(small_names, _unpack_vec(flat, shapes)))
    res["grad"].update(sg)

    outs = [loss, grad_x[None]]
    for kind in ("grad", "delta", "m", "v"):
        outs += [res[kind][n][None] for n in WEIGHTS]
    return tuple(outs)
```

```python
import numpy as np
import jax
import jax.numpy as jnp
from jax import lax
from jax.experimental import pallas as pl
from jax.experimental.pallas import tpu as pltpu

MD = jnp.bfloat16
F32 = jnp.float32
EPS = 1e-6
N_DEV = 8
HEADS = 16
QK_NOPE = 128
QK_ROPE = 64
V_DIM = 128
QK_PAD = 256
KV_RANK = 512
SOFTMAX_SCALE = (QK_NOPE + QK_ROPE) ** -0.5
LOG2E = 1.4426950408889634
ROPE_THETA = 10000.0
LRU_BLOCKS = 16
LRU_C = 8.0
X_HEADS = 4
X_DIM = 128
LRU_TAPS = 4
FFN_TAPS = 3
ADAM_LR, ADAM_B1, ADAM_B2, ADAM_EPS, ADAM_WD, ADAM_STEP = 0.001, 0.9, 0.999, 1e-08, 0.01, 10
LANES = 128
VMEM_LIMIT = 52 * 1024 * 1024
MESH = pl.DeviceIdType.MESH

NN = (((1,), (0,)), ((), ()))
NT = (((1,), (1,)), ((), ()))
TN = (((0,), (0,)), ((), ()))


def _tile(n, pref, align=LANES):
    if n <= pref:
        return n
    d = (pref // align) * align
    while d > align and n % d:
        d -= align
    assert n % d == 0, (n, pref, align)
    return d


def _params(sem):
    return pltpu.CompilerParams(dimension_semantics=sem, vmem_limit_bytes=VMEM_LIMIT)


def _dot(a, b, dims=NN):
    return lax.dot_general(a, b, dims, preferred_element_type=F32)


def _sigmoid(x):
    return 1.0 / (1.0 + jnp.exp(-x))


_GELU_C = 0.7978845608028654
_GELU_A = 0.044715


def _gelu(x):
    return 0.5 * x * (1.0 + jnp.tanh(x * (_GELU_C + (_GELU_C * _GELU_A) * (x * x))))


def _gelu_and_grad(x):
    x2 = x * x
    t = jnp.tanh(x * (_GELU_C + (_GELU_C * _GELU_A) * x2))
    hx = 0.5 * x
    g = hx + hx * t
    dg = 0.5 + 0.5 * t + hx * (1.0 - t * t) * (_GELU_C + (3.0 * _GELU_C * _GELU_A) * x2)
    return g, dg


def _expm1(x):
    small = x * (1.0 + x * (0.5 + x * (1.0 / 6.0 + x * (1.0 / 24.0 + x * (1.0 / 120.0)))))
    return jnp.where(jnp.abs(x) < 0.05, small, jnp.exp(x) - 1.0)


def _softplus(x):
    return jnp.maximum(x, 0.0) + jnp.log(1.0 + jnp.exp(-jnp.abs(x)))


def _rms_hat(x):
    r = lax.rsqrt(jnp.mean(x * x, axis=-1, keepdims=True) + EPS)
    return x * r, r


def _rms_bwd(x, g, dn):
    xh, r = _rms_hat(x)
    dg = jnp.sum(dn * xh, axis=0, keepdims=True)
    dxh = dn * g
    dx = r * (dxh - xh * jnp.mean(dxh * xh, axis=-1, keepdims=True))
    return dx, dg


def _acc_out(ref, val, first):
    @pl.when(first)
    def _():
        ref[...] = val

    @pl.when(jnp.logical_not(first))
    def _():
        ref[...] += val


def _lane_sums_as_row(x):
    ones = jnp.ones((8, LANES), F32)
    return lax.dot_general(ones, x, NT, precision=lax.Precision.HIGHEST, preferred_element_type=F32)[0:1, :]


def mm(pairs, mode, out_dtype, name, tm=1024, tn=1024, tk=2048):
    dims = {"nn": NN, "nt": NT, "tn": TN}[mode]
    shapes = []
    for a, b in pairs:
        if mode == "nn":
            (m, k), (k2, n) = a.shape, b.shape
        elif mode == "nt":
            (m, k), (n, k2) = a.shape, b.shape
        else:
            (k, m), (k2, n) = a.shape, b.shape
        assert k == k2, (name, a.shape, b.shape)
        shapes.append((m, n, k))
    m, n = shapes[0][0], shapes[0][1]
    assert all(s[0] == m and s[1] == n for s in shapes)
    tm, tn = _tile(m, tm), _tile(n, tn)
    tks = [_tile(s[2], tk) for s in shapes]
    nks = [s[2] // t for s, t in zip(shapes, tks)]
    starts = [sum(nks[:p]) for p in range(len(pairs))]
    nk_total = sum(nks)

    in_specs, args = [], []
    for p, (a, b) in enumerate(pairs):
        def kk(k, p=p):
            return jnp.clip(k - starts[p], 0, nks[p] - 1)
        if mode == "tn":
            in_specs.append(pl.BlockSpec((tks[p], tm), lambda i, j, k, kk=kk: (kk(k), i)))
        else:
            in_specs.append(pl.BlockSpec((tm, tks[p]), lambda i, j, k, kk=kk: (i, kk(k))))
        if mode == "nt":
            in_specs.append(pl.BlockSpec((tn, tks[p]), lambda i, j, k, kk=kk: (j, kk(k))))
        else:
            in_specs.append(pl.BlockSpec((tks[p], tn), lambda i, j, k, kk=kk: (kk(k), j)))
        args += [a, b]

    def body(*refs):
        ins, o_ref, scratch = refs[:2 * len(pairs)], refs[2 * len(pairs)], refs[2 * len(pairs) + 1:]
        if nk_total == 1:
            o_ref[...] = _dot(ins[0][...], ins[1][...], dims).astype(o_ref.dtype)
            return
        acc_ref, = scratch
        k = pl.program_id(2)
        for p in range(len(pairs)):
            @pl.when(jnp.logical_and(k >= starts[p], k < starts[p] + nks[p]))
            def _(p=p):
                prod = _dot(ins[2 * p][...], ins[2 * p + 1][...], dims)

                @pl.when(k == 0)
                def _():
                    acc_ref[...] = prod

                @pl.when(k > 0)
                def _():
                    acc_ref[...] += prod

        @pl.when(k == nk_total - 1)
        def _():
            o_ref[...] = acc_ref[...].astype(o_ref.dtype)

    return pl.pallas_call(
        body, name=name, grid=(m // tm, n // tn, nk_total),
        in_specs=in_specs, out_specs=pl.BlockSpec((tm, tn), lambda i, j, k: (i, j)),
        out_shape=jax.ShapeDtypeStruct((m, n), out_dtype),
        scratch_shapes=[] if nk_total == 1 else [pltpu.VMEM((tm, tn), F32)],
        compiler_params=_params(("parallel", "parallel", "arbitrary")),
    )(*args)


def rms_fwd(x, g, name):
    s, d = x.shape
    ts = _tile(s, 512, 16)

    def body(x_ref, g_ref, o_ref):
        xh, _ = _rms_hat(x_ref[...].astype(F32))
        o_ref[...] = (xh * g_ref[...]).astype(o_ref.dtype)

    return pl.pallas_call(
        body, name=name, grid=(s // ts,),
        in_specs=[pl.BlockSpec((ts, d), lambda i: (i, 0)), pl.BlockSpec((1, d), lambda i: (0, 0))],
        out_specs=pl.BlockSpec((ts, d), lambda i: (i, 0)),
        out_shape=jax.ShapeDtypeStruct((s, d), MD),
        compiler_params=_params(("parallel",)),
    )(x, g)


def rms_bwd(x, g, dn, name):
    s, d = x.shape
    ts = _tile(s, 256, 16)

    def body(x_ref, g_ref, dn_ref, dx_ref, dg_ref):
        dx, dg = _rms_bwd(x_ref[...].astype(F32), g_ref[...], dn_ref[...].astype(F32))
        dx_ref[...] = dx.astype(dx_ref.dtype)
        _acc_out(dg_ref, dg, pl.program_id(0) == 0)

    row = pl.BlockSpec((ts, d), lambda i: (i, 0))
    vec = pl.BlockSpec((1, d), lambda i: (0, 0))
    return pl.pallas_call(
        body, name=name, grid=(s // ts,), in_specs=[row, vec, row], out_specs=[row, vec],
        out_shape=[jax.ShapeDtypeStruct((s, d), MD), jax.ShapeDtypeStruct((1, d), F32)],
        compiler_params=_params(("arbitrary",)),
    )(x, g, dn)


def post_pre_fwd(x, y, g_post, g_pre, name):
    s, d = x.shape
    ts = _tile(s, 256, 16)

    def body(x_ref, y_ref, gp_ref, gq_ref, xn_ref, h_ref):
        yh, _ = _rms_hat(y_ref[...])
        xn = x_ref[...] + yh * gp_ref[...]
        xn_ref[...] = xn
        xh, _ = _rms_hat(xn)
        h_ref[...] = (xh * gq_ref[...]).astype(h_ref.dtype)

    row = pl.BlockSpec((ts, d), lambda i: (i, 0))
    vec = pl.BlockSpec((1, d), lambda i: (0, 0))
    return pl.pallas_call(
        body, name=name, grid=(s // ts,), in_specs=[row, row, vec, vec], out_specs=[row, row],
        out_shape=[jax.ShapeDtypeStruct((s, d), F32), jax.ShapeDtypeStruct((s, d), MD)],
        compiler_params=_params(("parallel",)),
    )(x, y, g_post, g_pre)


def loss_head(x, y, target, g_post, name):
    s, d = x.shape
    ts = _tile(s, 256, 16)

    def body(x_ref, y_ref, t_ref, g_ref, loss_ref, dx_ref, dy_ref, dg_ref):
        first = pl.program_id(0) == 0
        y = y_ref[...]
        yh, _ = _rms_hat(y)
        diff = x_ref[...] + yh * g_ref[...] - t_ref[...]
        part = 0.5 * jnp.sum(jnp.sum(diff * diff, axis=1, keepdims=True) * (1.0 / d), axis=0, keepdims=True)
        _acc_out(loss_ref, part, first)
        dx = diff * (1.0 / d)
        dx_ref[...] = dx
        dy, dg = _rms_bwd(y, g_ref[...], dx)
        dy_ref[...] = dy.astype(dy_ref.dtype)
        _acc_out(dg_ref, dg, first)

    row = pl.BlockSpec((ts, d), lambda i: (i, 0))
    vec = pl.BlockSpec((1, d), lambda i: (0, 0))
    one = pl.BlockSpec((1, 1), lambda i: (0, 0))
    return pl.pallas_call(
        body, name=name, grid=(s // ts,), in_specs=[row, row, row, vec], out_specs=[one, row, row, vec],
        out_shape=[jax.ShapeDtypeStruct((1, 1), F32), jax.ShapeDtypeStruct((s, d), F32),
                   jax.ShapeDtypeStruct((s, d), MD), jax.ShapeDtypeStruct((1, d), F32)],
        compiler_params=_params(("arbitrary",)),
    )(x, y, target, g_post)


def pre_post_bwd(x, y, dx_res, dh, g_pre, g_post, name):
    s, d = x.shape
    ts = _tile(s, 256, 16)

    def body(x_ref, y_ref, dr_ref, dh_ref, gq_ref, gp_ref, dx_ref, dy_ref, dgq_ref, dgp_ref):
        first = pl.program_id(0) == 0
        dxa, dgq = _rms_bwd(x_ref[...], gq_ref[...], dh_ref[...])
        dx = dr_ref[...] + dxa
        dx_ref[...] = dx
        dy, dgp = _rms_bwd(y_ref[...], gp_ref[...], dx)
        dy_ref[...] = dy.astype(dy_ref.dtype)
        _acc_out(dgq_ref, dgq, first)
        _acc_out(dgp_ref, dgp, first)

    row = pl.BlockSpec((ts, d), lambda i: (i, 0))
    vec = pl.BlockSpec((1, d), lambda i: (0, 0))
    return pl.pallas_call(
        body, name=name, grid=(s // ts,), in_specs=[row, row, row, row, vec, vec], out_specs=[row, row, vec, vec],
        out_shape=[jax.ShapeDtypeStruct((s, d), F32), jax.ShapeDtypeStruct((s, d), MD),
                   jax.ShapeDtypeStruct((1, d), F32), jax.ShapeDtypeStruct((1, d), F32)],
        compiler_params=_params(("arbitrary",)),
    )(x, y, dx_res, dh, g_pre, g_post)


def pre_bwd(x, dx_res, dh_a, dh_b, g_pre, name):
    s, d = x.shape
    ts = _tile(s, 256, 16)

    def body(x_ref, dr_ref, dha_ref, dhb_ref, g_ref, dx_ref, dg_ref):
        dxa, dg = _rms_bwd(x_ref[...], g_ref[...], dha_ref[...] + dhb_ref[...])
        dx_ref[...] = dr_ref[...] + dxa
        _acc_out(dg_ref, dg, pl.program_id(0) == 0)

    row = pl.BlockSpec((ts, d), lambda i: (i, 0))
    vec = pl.BlockSpec((1, d), lambda i: (0, 0))
    return pl.pallas_call(
        body, name=name, grid=(s // ts,), in_specs=[row, row, row, row, vec], out_specs=[row, vec],
        out_shape=[jax.ShapeDtypeStruct((s, d), F32), jax.ShapeDtypeStruct((1, d), F32)],
        compiler_params=_params(("arbitrary",)),
    )(x, dx_res, dh_a, dh_b, g_pre)


def _swap_halves(x):
    lane = lax.broadcasted_iota(jnp.int32, x.shape, 1)
    return jnp.where((lane % QK_ROPE) < QK_ROPE // 2, pltpu.roll(x, LANES - QK_ROPE // 2, 1),
                     pltpu.roll(x, QK_ROPE // 2, 1))


def _rope_tables(ang_ref, sign):
    lane = lax.broadcasted_iota(jnp.int32, ang_ref.shape, 1)
    sgn = jnp.where((lane % QK_ROPE) < QK_ROPE // 2, -sign, sign)
    ang = ang_ref[...]
    return jnp.cos(ang), jnp.sin(ang) * sgn


def rope_fwd(proj, kr, kn, ang2, name):
    s = proj.shape[0]
    ts = _tile(s, 256, 16)
    wn, wr = HEADS * QK_NOPE, HEADS * QK_ROPE

    def body(qn_ref, qr_ref, kr_ref, kn_ref, ang_ref, qo_ref, ko_ref):
        c, sn = _rope_tables(ang_ref, 1.0)
        lane = lax.broadcasted_iota(jnp.int32, (ts, LANES), 1)
        lo = lane < QK_ROPE
        x = kr_ref[...].astype(F32)
        k_rot = jnp.where(lo, x * c + _swap_halves(x) * sn, 0.0).astype(ko_ref.dtype)
        for j in range(HEADS // 2):
            x = qr_ref[:, j * LANES:(j + 1) * LANES].astype(F32)
            r = x * c + _swap_halves(x) * sn
            pair = (jnp.where(lo, r, 0.0), jnp.where(lo, pltpu.roll(r, QK_ROPE, 1), 0.0))
            for e in range(2):
                h = 2 * j + e
                qo_ref[:, h * QK_PAD:h * QK_PAD + LANES] = qn_ref[:, h * LANES:(h + 1) * LANES]
                qo_ref[:, h * QK_PAD + LANES:(h + 1) * QK_PAD] = pair[e].astype(qo_ref.dtype)
                ko_ref[:, h * QK_PAD:h * QK_PAD + LANES] = kn_ref[:, h * LANES:(h + 1) * LANES]
                ko_ref[:, h * QK_PAD + LANES:(h + 1) * QK_PAD] = k_rot

    row = lambda w, blk: pl.BlockSpec((ts, w), lambda i: (i, blk))
    return pl.pallas_call(
        body, name=name, grid=(s // ts,),
        in_specs=[row(wn, 0), row(wr, wn // wr), row(LANES, 0), row(wn, 0), row(LANES, 0)],
        out_specs=[row(HEADS * QK_PAD, 0), row(HEADS * QK_PAD, 0)],
        out_shape=[jax.ShapeDtypeStruct((s, HEADS * QK_PAD), MD), jax.ShapeDtypeStruct((s, HEADS * QK_PAD), MD)],
        compiler_params=_params(("parallel",)),
    )(proj, proj, kr, kn, ang2)


def rope_bwd(dqf, dkf, ang2, name):
    s = dqf.shape[0]
    ts = _tile(s, 256, 16)
    wn, wr = HEADS * QK_NOPE, HEADS * QK_ROPE

    def body(dq_ref, dk_ref, ang_ref, qn_ref, qr_ref, kn_ref, kr_ref):
        c, sn = _rope_tables(ang_ref, -1.0)
        lane = lax.broadcasted_iota(jnp.int32, (ts, LANES), 1)
        lo = lane < QK_ROPE
        rot = lambda dy: dy * c + _swap_halves(dy) * sn
        dk_rope = None
        for j in range(HEADS // 2):
            halves = []
            for e in range(2):
                h = 2 * j + e
                qn_ref[:, h * LANES:(h + 1) * LANES] = dq_ref[:, h * QK_PAD:h * QK_PAD + LANES]
                kn_ref[:, h * LANES:(h + 1) * LANES] = dk_ref[:, h * QK_PAD:h * QK_PAD + LANES]
                halves.append(jnp.where(lo, dq_ref[:, h * QK_PAD + LANES:(h + 1) * QK_PAD].astype(F32), 0.0))
                dk_h = dk_ref[:, h * QK_PAD + LANES:(h + 1) * QK_PAD].astype(F32)
                dk_rope = dk_h if dk_rope is None else dk_rope + dk_h
            qr_ref[:, j * LANES:(j + 1) * LANES] = rot(halves[0] + pltpu.roll(halves[1], QK_ROPE, 1)).astype(qr_ref.dtype)
        kr_ref[...] = jnp.where(lo, rot(jnp.where(lo, dk_rope, 0.0)), 0.0).astype(kr_ref.dtype)

    row = lambda w: pl.BlockSpec((ts, w), lambda i: (i, 0))
    return pl.pallas_call(
        body, name=name, grid=(s // ts,),
        in_specs=[row(HEADS * QK_PAD), row(HEADS * QK_PAD), row(LANES)],
        out_specs=[row(wn), row(wr), row(wn), row(LANES)],
        out_shape=[jax.ShapeDtypeStruct((s, wn), MD), jax.ShapeDtypeStruct((s, wr), MD),
                   jax.ShapeDtypeStruct((s, wn), MD), jax.ShapeDtypeStruct((s, LANES), MD)],
        compiler_params=_params(("parallel",)),
    )(dqf, dkf, ang2)


_EXP2_SCALE = SOFTMAX_SCALE * LOG2E


def flash_fwd(qf, kf, v, name):
    s = qf.shape[0]
    t = _tile(s, 1024)
    nq = s // t
    half = t // 2
    pairs = [(i, j) for i in range(nq) for j in range(i + 1)]
    it = jnp.asarray(np.array([p[0] for p in pairs], np.int32))
    jt = jnp.asarray(np.array([p[1] for p in pairs], np.int32))
    tq = tk = t

    def body(it_ref, jt_ref, q_ref, k_ref, v_ref, o_ref, lse_ref, m_s, l_s, acc_s):
        p = pl.program_id(1)
        i, j = it_ref[p], jt_ref[p]

        @pl.when(j == 0)
        def _():
            m_s[...] = jnp.full(m_s.shape, -jnp.inf, F32)
            l_s[...] = jnp.zeros(l_s.shape, F32)
            acc_s[...] = jnp.zeros(acc_s.shape, F32)

        def step(q0, nq_rows, nk_rows, masked):
            rows, keys = pl.ds(q0, nq_rows), pl.ds(0, nk_rows)
            sc = _dot(q_ref[rows, :], k_ref[keys, :], NT)
            if masked:
                row = lax.broadcasted_iota(jnp.int32, (nq_rows, nk_rows), 0) + q0
                col = lax.broadcasted_iota(jnp.int32, (nq_rows, nk_rows), 1)
                sc = jnp.where(col <= row, sc, -jnp.inf)
            groups = nk_rows // LANES
            cols = [sc[:, g * LANES:(g + 1) * LANES] for g in range(groups)]
            mx = cols[0]
            for g in range(1, groups):
                mx = jnp.maximum(mx, cols[g])
            m_prev = m_s[rows, :]
            m_new = jnp.maximum(m_prev, jnp.max(mx, axis=1, keepdims=True))
            alpha = jnp.exp2((m_prev - m_new) * _EXP2_SCALE)
            ps, psum = [], None
            for g in range(groups):
                pg = jnp.exp2((cols[g] - m_new) * _EXP2_SCALE)
                psum = pg if psum is None else psum + pg
                ps.append(pg.astype(MD))
            l_s[rows, :] = alpha * l_s[rows, :] + psum
            acc_s[rows, :] = acc_s[rows, :] * alpha + _dot(jnp.concatenate(ps, axis=1), v_ref[keys, :])
            m_s[rows, :] = m_new

        @pl.when(j == i)
        def _():
            step(0, half, half, True)
            step(half, half, t, True)

        @pl.when(j != i)
        def _():
            step(0, t, t, False)

        @pl.when(j == i)
        def _():
            l = jnp.sum(l_s[...], axis=1, keepdims=True)
            o_ref[...] = (acc_s[...] / l).astype(o_ref.dtype)
            lse_rep = m_s[...] * _EXP2_SCALE + jnp.log(l) * LOG2E
            lse_ref[...] = _lane_sums_as_row(lse_rep * (1.0 / LANES))

    grid_spec = pltpu.PrefetchScalarGridSpec(
        num_scalar_prefetch=2, grid=(HEADS, len(pairs)),
        in_specs=[pl.BlockSpec((tq, QK_PAD), lambda h, p, it, jt: (it[p], h)),
                  pl.BlockSpec((tk, QK_PAD), lambda h, p, it, jt: (jt[p], h)),
                  pl.BlockSpec((tk, V_DIM), lambda h, p, it, jt: (jt[p], h))],
        out_specs=[pl.BlockSpec((tq, V_DIM), lambda h, p, it, jt: (it[p], h)),
                   pl.BlockSpec((None, 1, tq), lambda h, p, it, jt: (h, 0, it[p]))],
        scratch_shapes=[pltpu.VMEM((tq, LANES), F32), pltpu.VMEM((tq, LANES), F32), pltpu.VMEM((tq, V_DIM), F32)])
    return pl.pallas_call(
        body, name=name, grid_spec=grid_spec,
        out_shape=[jax.ShapeDtypeStruct((s, HEADS * V_DIM), MD), jax.ShapeDtypeStruct((HEADS, 1, s), F32)],
        compiler_params=_params(("parallel", "arbitrary")),
    )(it, jt, qf, kf, v)


def flash_bwd(qf, kf, v, o, do, lse2, name):
    s = qf.shape[0]
    t = _tile(s, 1024)
    tq = tk = t
    nq = s // t
    half = t // 2
    pairs = [(i, j) for j in range(nq) for i in range(j, nq)]
    it = jnp.asarray(np.array([p[0] for p in pairs], np.int32))
    jt = jnp.asarray(np.array([p[1] for p in pairs], np.int32))
    n_pairs = len(pairs)

    def body(it_ref, jt_ref, q_ref, k_ref, v_ref, o_ref, do_ref, lse_ref, dq_ref, dk_ref, dv_ref, dq_s, dk_s, dv_s, dl_s):
        p = pl.program_id(1)
        i, j = it_ref[p], jt_ref[p]

        @pl.when(p == 0)
        def _():
            dq_s[...] = jnp.zeros(dq_s.shape, F32)

        @pl.when(j == 0)
        def _():
            dl_s[i] = _lane_sums_as_row(o_ref[...].astype(F32) * do_ref[...].astype(F32))

        @pl.when(i == j)
        def _():
            dk_s[...] = jnp.zeros(dk_s.shape, F32)
            dv_s[...] = jnp.zeros(dv_s.shape, F32)

        def step(k0, nk_rows, q0, nq_rows, masked):
            keys, rows = pl.ds(k0, nk_rows), pl.ds(q0, nq_rows)
            q, k, do_b = q_ref[rows, :], k_ref[keys, :], do_ref[rows, :]
            pt = jnp.exp2(_dot(k, q, NT) * _EXP2_SCALE - lse_ref[:, rows])
            if masked:
                krow = lax.broadcasted_iota(jnp.int32, (nk_rows, nq_rows), 0) + k0
                qcol = lax.broadcasted_iota(jnp.int32, (nk_rows, nq_rows), 1) + q0
                pt = jnp.where(krow <= qcol, pt, 0.0)
            dv_s[keys, :] += _dot(pt.astype(MD), do_b)
            dst = (pt * (_dot(v_ref[keys, :], do_b, NT) - dl_s[i, :, rows])).astype(MD)
            dk_s[keys, :] += _dot(dst, q)
            row0 = pl.multiple_of(i * t + q0, half)
            dq_s[pl.ds(row0, nq_rows), :] += _dot(dst, k, TN)

        @pl.when(i == j)
        def _():
            step(0, half, 0, t, True)
            step(half, half, half, half, True)

        @pl.when(i != j)
        def _():
            step(0, t, 0, t, False)

        @pl.when(i == nq - 1)
        def _():
            dk_ref[...] = (dk_s[...] * SOFTMAX_SCALE).astype(dk_ref.dtype)
            dv_ref[...] = dv_s[...].astype(dv_ref.dtype)

        @pl.when(p == n_pairs - 1)
        def _():
            dq_ref[...] = (dq_s[...] * SOFTMAX_SCALE).astype(dq_ref.dtype)

    vec = pl.BlockSpec((None, 1, tq), lambda h, p, it, jt: (h, 0, it[p]))
    grid_spec = pltpu.PrefetchScalarGridSpec(
        num_scalar_prefetch=2, grid=(HEADS, n_pairs),
        in_specs=[pl.BlockSpec((tq, QK_PAD), lambda h, p, it, jt: (it[p], h)),
                  pl.BlockSpec((tk, QK_PAD), lambda h, p, it, jt: (jt[p], h)),
                  pl.BlockSpec((tk, V_DIM), lambda h, p, it, jt: (jt[p], h)),
                  pl.BlockSpec((tq, V_DIM), lambda h, p, it, jt: (it[p], h)),
                  pl.BlockSpec((tq, V_DIM), lambda h, p, it, jt: (it[p], h)), vec],
        out_specs=[pl.BlockSpec((s, QK_PAD), lambda h, p, it, jt: (0, h)),
                   pl.BlockSpec((tk, QK_PAD), lambda h, p, it, jt: (jt[p], h)),
                   pl.BlockSpec((tk, V_DIM), lambda h, p, it, jt: (jt[p], h))],
        scratch_shapes=[pltpu.VMEM((s, QK_PAD), F32), pltpu.VMEM((tk, QK_PAD), F32), pltpu.VMEM((tk, V_DIM), F32),
                        pltpu.VMEM((nq, 1, t), F32)])
    wide = jax.ShapeDtypeStruct((s, HEADS * QK_PAD), MD)
    return pl.pallas_call(
        body, name=name, grid_spec=grid_spec,
        out_shape=[wide, wide, jax.ShapeDtypeStruct((s, HEADS * V_DIM), MD)],
        compiler_params=_params(("parallel", "arbitrary")),
    )(it, jt, qf, kf, v, o, do, lse2)


def _halo_rows(dtype):
    return 8 if jnp.dtype(dtype).itemsize == 4 else 16


def _prev_spec(ts, tc, hr, col_off):
    return pl.BlockSpec((hr, tc), lambda j, i: (jnp.maximum(i * (ts // hr) - 1, 0), j + col_off))


def _conv_taps(x, halo, w, taps, first):
    hr, ts = halo.shape[0], x.shape[0]
    xe = jnp.concatenate([jnp.where(first, 0.0, halo), x], axis=0)
    shifted = []
    acc = None
    for k in range(taps):
        sh = taps - 1 - k
        xs = xe if sh == 0 else pltpu.roll(xe, sh, 0)
        xs = xs[hr:hr + ts, :]
        shifted.append(xs)
        term = xs * w[k:k + 1, :]
        acc = term if acc is None else acc + term
    return acc, shifted


def conv_bwd(x, dy, w8, taps, x_col_off, name, ts_pref=1024):
    halves = dy.ndim == 3
    s = dy.shape[-2]
    c = dy.shape[-1] * (2 if halves else 1)
    tc = _tile(dy.shape[-1], 512)
    ts = _tile(s, ts_pref, 16)
    hx, hy = _halo_rows(x.dtype), _halo_rows(dy.dtype)
    ns = s // ts
    nh = dy.shape[-1] // tc
    xo = x_col_off // tc
    assert x_col_off % tc == 0

    def body(x_ref, xp_ref, dy_ref, dyn_ref, w_ref, dx_ref, dw_ref, db_ref):
        i = pl.program_id(1)
        first = i == 0
        xv = x_ref[...].astype(F32)
        dyv = dy_ref[...].astype(F32)
        w = w_ref[...]
        _, shifted = _conv_taps(xv, xp_ref[...].astype(F32), w, taps, first)
        dye = jnp.concatenate([dyv, jnp.where(i == ns - 1, 0.0, dyn_ref[...].astype(F32))], axis=0)
        dx = None
        for k in range(taps):
            sh = taps - 1 - k
            ds_ = dye if sh == 0 else pltpu.roll(dye, ts + hy - sh, 0)
            term = ds_[0:ts, :] * w[k:k + 1, :]
            dx = term if dx is None else dx + term
        dx_ref[...] = dx.astype(dx_ref.dtype)
        rows = [jnp.sum(dyv * shifted[k], axis=0, keepdims=True) for k in range(taps)]
        rows.append(jnp.zeros((8 - taps, tc), F32))
        _acc_out(dw_ref, jnp.concatenate(rows, axis=0), first)
        _acc_out(db_ref, jnp.sum(dyv, axis=0, keepdims=True), first)

    last = s // hy - 1
    nxt = lambda i: jnp.minimum((i + 1) * (ts // hy), last)
    if halves:
        dy_spec = pl.BlockSpec((None, ts, tc), lambda j, i: (j // nh, i, j % nh))
        dyn_spec = pl.BlockSpec((None, hy, tc), lambda j, i: (j // nh, nxt(i), j % nh))
    else:
        dy_spec = pl.BlockSpec((ts, tc), lambda j, i: (i, j))
        dyn_spec = pl.BlockSpec((hy, tc), lambda j, i: (nxt(i), j))
    return pl.pallas_call(
        body, name=name, grid=(c // tc, ns),
        in_specs=[pl.BlockSpec((ts, tc), lambda j, i: (i, j + xo)), _prev_spec(ts, tc, hx, xo), dy_spec, dyn_spec,
                  pl.BlockSpec((8, tc), lambda j, i: (0, j))],
        out_specs=[pl.BlockSpec((ts, tc), lambda j, i: (i, j)), pl.BlockSpec((8, tc), lambda j, i: (0, j)),
                   pl.BlockSpec((1, tc), lambda j, i: (0, j))],
        out_shape=[jax.ShapeDtypeStruct((s, c), MD), jax.ShapeDtypeStruct((8, c), F32), jax.ShapeDtypeStruct((1, c), F32)],
        compiler_params=_params(("parallel", "arbitrary")),
    )(x, x, dy, dy, w8)


def _lru_gates(xc, wr_ref, br, wi_ref, bi, sp, nblk):
    xb = xc.astype(MD)
    zr, zi = [], []
    for b in range(nblk):
        blk = xb[:, b * LANES:(b + 1) * LANES]
        zr.append(_dot(blk, wr_ref[b]))
        zi.append(_dot(blk, wi_ref[b]))
    r = _sigmoid(jnp.concatenate(zr, axis=1) + br)
    ig = _sigmoid(jnp.concatenate(zi, axis=1) + bi)
    log_a = -LRU_C * r * sp
    a = jnp.exp(log_a)
    mult = jnp.sqrt(-_expm1(2.0 * log_a))
    return r, ig, a, mult


def lru_fwd(proj, x_off, y_off, w8, b_conv, w_rg, b_rg, w_ig, b_ig, lam, name):
    s = proj.shape[0]
    c = lam.shape[1]
    tc, ts = _tile(c, 512), _tile(s, 256, 16)
    nblk = tc // LANES
    xo, yo = x_off // tc, y_off // tc
    hr = _halo_rows(proj.dtype)

    def body(x_ref, xp_ref, y_ref, w_ref, bc_ref, wr_ref, br_ref, wi_ref, bi_ref, lam_ref, h_ref, hg_ref, carry):
        i = pl.program_id(1)
        first = i == 0
        xc, _ = _conv_taps(x_ref[...].astype(F32), xp_ref[...].astype(F32), w_ref[...], LRU_TAPS, first)
        xc = xc + bc_ref[...]
        sp = _softplus(-lam_ref[...])
        _, ig, a, mult = _lru_gates(xc, wr_ref, br_ref[...], wi_ref, bi_ref[...], sp, nblk)
        b = mult * (ig * xc)
        row = lax.broadcasted_iota(jnp.int32, (ts, tc), 0)
        d = 1
        while d < ts:
            keep = row >= d
            a_s, b_s = pltpu.roll(a, d, 0), pltpu.roll(b, d, 0)
            b = jnp.where(keep, a * b_s + b, b)
            a = jnp.where(keep, a * a_s, a)
            d *= 2
        h0 = jnp.where(first, 0.0, carry[7:8, :])
        h = a * h0 + b
        carry[...] = h[ts - 8:ts, :]
        h_ref[...] = h
        hg_ref[...] = (h * _gelu(y_ref[...].astype(F32))).astype(hg_ref.dtype)

    tile = lambda off: pl.BlockSpec((ts, tc), lambda j, i: (i, j + off))
    vec = pl.BlockSpec((1, tc), lambda j, i: (0, j))
    wblk = pl.BlockSpec((nblk, LANES, LANES), lambda j, i: (j, 0, 0))
    out = pl.BlockSpec((ts, tc), lambda j, i: (i, j))
    return pl.pallas_call(
        body, name=name, grid=(c // tc, s // ts),
        in_specs=[tile(xo), _prev_spec(ts, tc, hr, xo), tile(yo), pl.BlockSpec((8, tc), lambda j, i: (0, j)), vec,
                  wblk, vec, wblk, vec, vec],
        out_specs=[out, out],
        out_shape=[jax.ShapeDtypeStruct((s, c), F32), jax.ShapeDtypeStruct((s, c), MD)],
        scratch_shapes=[pltpu.VMEM((8, tc), F32)],
        compiler_params=_params(("parallel", "arbitrary")),
    )(proj, proj, proj, w8, b_conv, w_rg, b_rg, w_ig, b_ig, lam)


def lru_bwd(proj, x_off, y_off, h, dhg, w8, b_conv, w_rg, b_rg, w_ig, b_ig, lam, name):
    s = proj.shape[0]
    c = lam.shape[1]
    tc, ts = _tile(c, 512), _tile(s, 256, 16)
    nblk = tc // LANES
    ns = s // ts
    xo, yo = x_off // tc, y_off // tc
    hr = _halo_rows(proj.dtype)
    rev = lambda i: ns - 1 - i

    def body(x_ref, xp_ref, y_ref, h_ref, hp_ref, dhg_ref, w_ref, bc_ref, wr_ref, br_ref, wi_ref, bi_ref, lam_ref,
             dxc_ref, dy_ref, dwr_ref, dwi_ref, dbr_ref, dbi_ref, dlam_ref, a_carry, g_carry):
        step = pl.program_id(1)
        i = rev(step)
        start = step == 0
        top = i == 0
        xc, _ = _conv_taps(x_ref[...].astype(F32), xp_ref[...].astype(F32), w_ref[...], LRU_TAPS, top)
        xc = xc + bc_ref[...]
        lam_v = lam_ref[...]
        sp = _softplus(-lam_v)
        r, ig, a, mult = _lru_gates(xc, wr_ref, br_ref[...], wi_ref, bi_ref[...], sp, nblk)
        hv = h_ref[...]
        gy, dgy = _gelu_and_grad(y_ref[...].astype(F32))
        dhg_v = dhg_ref[...]
        dy_ref[...] = (dhg_v * hv * dgy).astype(dy_ref.dtype)
        row = lax.broadcasted_iota(jnp.int32, (ts, tc), 0)
        a_next = jnp.where(row == ts - 1, jnp.where(start, 0.0, a_carry[0:1, :]), pltpu.roll(a, ts - 1, 0))
        ca, cb = a_next, dhg_v * gy
        d = 1
        while d < ts:
            keep = row < ts - d
            a_s, b_s = pltpu.roll(ca, ts - d, 0), pltpu.roll(cb, ts - d, 0)
            cb = jnp.where(keep, ca * b_s + cb, cb)
            ca = jnp.where(keep, ca * a_s, ca)
            d *= 2
        g = ca * jnp.where(start, 0.0, g_carry[0:1, :]) + cb
        a_carry[...] = a[0:8, :]
        g_carry[...] = g[0:8, :]
        h_prev = jnp.where(row == 0, jnp.where(top, 0.0, hp_ref[7:8, :]), pltpu.roll(hv, 1, 0))
        da = g * h_prev
        ixc = ig * xc
        d_ixc = g * mult
        dlog_a = da * a - g * ixc * (a * a) / mult
        dzr = dlog_a * (-LRU_C * sp) * r * (1.0 - r)
        dzi = d_ixc * xc * ig * (1.0 - ig)
        dxc = d_ixc * ig
        xb, dzr_b, dzi_b = xc.astype(MD), dzr.astype(MD), dzi.astype(MD)
        parts, dwr, dwi = [], [], []
        for b in range(nblk):
            sl = slice(b * LANES, (b + 1) * LANES)
            parts.append(_dot(dzr_b[:, sl], wr_ref[b], NT) + _dot(dzi_b[:, sl], wi_ref[b], NT))
            dwr.append(_dot(xb[:, sl], dzr_b[:, sl], TN)[None])
            dwi.append(_dot(xb[:, sl], dzi_b[:, sl], TN)[None])
        dxc_ref[...] = dxc + jnp.concatenate(parts, axis=1)
        _acc_out(dwr_ref, jnp.concatenate(dwr, axis=0), start)
        _acc_out(dwi_ref, jnp.concatenate(dwi, axis=0), start)
        _acc_out(dbr_ref, jnp.sum(dzr, axis=0, keepdims=True), start)
        _acc_out(dbi_ref, jnp.sum(dzi, axis=0, keepdims=True), start)
        dsp = jnp.sum(dlog_a * (-LRU_C) * r, axis=0, keepdims=True)
        _acc_out(dlam_ref, -dsp * _sigmoid(-lam_v), start)

    vec = pl.BlockSpec((1, tc), lambda j, i: (0, j))
    wblk = pl.BlockSpec((nblk, LANES, LANES), lambda j, i: (j, 0, 0))
    tile = lambda off: pl.BlockSpec((ts, tc), lambda j, i: (rev(i), j + off))
    x_prev = pl.BlockSpec((hr, tc), lambda j, i: (jnp.maximum(rev(i) * (ts // hr) - 1, 0), j + xo))
    h_prev = pl.BlockSpec((8, tc), lambda j, i: (jnp.maximum(rev(i) * (ts // 8) - 1, 0), j))
    return pl.pallas_call(
        body, name=name, grid=(c // tc, ns),
        in_specs=[tile(xo), x_prev, tile(yo), tile(0), h_prev, tile(0), pl.BlockSpec((8, tc), lambda j, i: (0, j)), vec,
                  wblk, vec, wblk, vec, vec],
        out_specs=[tile(0), tile(0), wblk, wblk, vec, vec, vec],
        out_shape=[jax.ShapeDtypeStruct((s, c), F32), jax.ShapeDtypeStruct((s, c), MD),
                   jax.ShapeDtypeStruct(w_rg.shape, F32), jax.ShapeDtypeStruct(w_ig.shape, F32),
                   jax.ShapeDtypeStruct((1, c), F32), jax.ShapeDtypeStruct((1, c), F32), jax.ShapeDtypeStruct((1, c), F32)],
        scratch_shapes=[pltpu.VMEM((8, tc), F32), pltpu.VMEM((8, tc), F32)],
        compiler_params=_params(("parallel", "arbitrary")),
    )(proj, proj, proj, h, h, dhg, w8, b_conv, w_rg, b_rg, w_ig, b_ig, lam)


def merge_fwd(proj, gm_off, gl_off, o_mla, o_lru, name):
    s, c = o_mla.shape
    tc, ts = _tile(c, 1024), _tile(s, 512, 16)

    def body(gm_ref, gl_ref, om_ref, ol_ref, out_ref):
        out_ref[...] = (_sigmoid(gm_ref[...].astype(F32)) * om_ref[...] + _sigmoid(gl_ref[...].astype(F32)) * ol_ref[...]).astype(out_ref.dtype)

    tile = lambda off: pl.BlockSpec((ts, tc), lambda i, j: (i, j + off // tc))
    return pl.pallas_call(
        body, name=name, grid=(s // ts, c // tc), in_specs=[tile(gm_off), tile(gl_off), tile(0), tile(0)], out_specs=tile(0),
        out_shape=jax.ShapeDtypeStruct((s, c), MD), compiler_params=_params(("parallel", "parallel")),
    )(proj, proj, o_mla, o_lru)


def merge_bwd(proj, gm_off, gl_off, o_mla, o_lru, dmerged, name):
    s, c = o_mla.shape
    tc, ts = _tile(c, 1024), _tile(s, 512, 16)

    def body(gm_ref, gl_ref, om_ref, ol_ref, dm_ref, dom_ref, dol_ref, dgm_ref, dgl_ref):
        sm, sl = _sigmoid(gm_ref[...].astype(F32)), _sigmoid(gl_ref[...].astype(F32))
        dm = dm_ref[...]
        dom_ref[...] = (dm * sm).astype(dom_ref.dtype)
        dol_ref[...] = (dm * sl).astype(dol_ref.dtype)
        dgm_ref[...] = (dm * om_ref[...] * sm * (1.0 - sm)).astype(dgm_ref.dtype)
        dgl_ref[...] = (dm * ol_ref[...] * sl * (1.0 - sl)).astype(dgl_ref.dtype)

    tile = lambda off: pl.BlockSpec((ts, tc), lambda i, j: (i, j + off // tc))
    out = jax.ShapeDtypeStruct((s, c), MD)
    return pl.pallas_call(
        body, name=name, grid=(s // ts, c // tc), in_specs=[tile(gm_off), tile(gl_off), tile(0), tile(0), tile(0)],
        out_specs=[tile(0)] * 4, out_shape=[out] * 4, compiler_params=_params(("parallel", "parallel")),
    )(proj, proj, o_mla, o_lru, dmerged)


def xattn_fwd(cq, ck, cv, name):
    s = cq.shape[0]
    m = ck.shape[0]
    ts = _tile(s, 512, 16)
    scale = X_DIM ** -0.5

    def body(q_ref, k_ref, v_ref, o_ref):
        for h in range(X_HEADS):
            sl = slice(h * X_DIM, (h + 1) * X_DIM)
            sc = _dot(q_ref[:, sl], k_ref[:, sl], NT) * scale
            e = jnp.exp(sc - jnp.max(sc, axis=1, keepdims=True))
            p = e / jnp.sum(e, axis=1, keepdims=True)
            o_ref[:, sl] = _dot(p.astype(MD), v_ref[:, sl]).astype(o_ref.dtype)

    w = X_HEADS * X_DIM
    return pl.pallas_call(
        body, name=name, grid=(s // ts,),
        in_specs=[pl.BlockSpec((ts, w), lambda i: (i, 0)), pl.BlockSpec((m, w), lambda i: (0, 0)), pl.BlockSpec((m, w), lambda i: (0, 0))],
        out_specs=pl.BlockSpec((ts, w), lambda i: (i, 0)), out_shape=jax.ShapeDtypeStruct((s, w), MD),
        compiler_params=_params(("parallel",)),
    )(cq, ck, cv)


def xattn_bwd(cq, ck, cv, dco, name):
    s = cq.shape[0]
    m = ck.shape[0]
    ts = _tile(s, 512, 16)
    scale = X_DIM ** -0.5

    def body(q_ref, k_ref, v_ref, do_ref, dq_ref, dk_ref, dv_ref):
        first = pl.program_id(0) == 0
        dks, dvs = [], []
        for h in range(X_HEADS):
            sl = slice(h * X_DIM, (h + 1) * X_DIM)
            q, k, v, do = q_ref[:, sl], k_ref[:, sl], v_ref[:, sl], do_ref[:, sl]
            sc = _dot(q, k, NT) * scale
            e = jnp.exp(sc - jnp.max(sc, axis=1, keepdims=True))
            p = e / jnp.sum(e, axis=1, keepdims=True)
            dvs.append(_dot(p.astype(MD), do, TN))
            dp = _dot(do, v, NT)
            ds = (p * (dp - jnp.sum(dp * p, axis=1, keepdims=True)) * scale).astype(MD)
            dq_ref[:, sl] = _dot(ds, k).astype(dq_ref.dtype)
            dks.append(_dot(ds, q, TN))
        _acc_out(dk_ref, jnp.concatenate(dks, axis=1), first)
        _acc_out(dv_ref, jnp.concatenate(dvs, axis=1), first)

    w = X_HEADS * X_DIM
    row = pl.BlockSpec((ts, w), lambda i: (i, 0))
    full = pl.BlockSpec((m, w), lambda i: (0, 0))
    return pl.pallas_call(
        body, name=name, grid=(s // ts,), in_specs=[row, full, full, row], out_specs=[row, full, full],
        out_shape=[jax.ShapeDtypeStruct((s, w), MD), jax.ShapeDtypeStruct((m, w), F32), jax.ShapeDtypeStruct((m, w), F32)],
        compiler_params=_params(("arbitrary",)),
    )(cq, ck, cv, dco)


def _ffn_specs(up_pre):
    s, c2 = up_pre.shape
    f = c2 // 2
    tc, ts = _tile(f, 512), _tile(s, 1024, 16)
    nf = f // tc
    hr = _halo_rows(up_pre.dtype)
    tile = lambda off: pl.BlockSpec((ts, tc), lambda j, i: (i, j + off))
    w = lambda off: pl.BlockSpec((8, tc), lambda j, i: (0, j + off))
    vec = lambda off: pl.BlockSpec((1, tc), lambda j, i: (0, j + off))
    specs = [tile(0), _prev_spec(ts, tc, hr, 0), tile(nf), _prev_spec(ts, tc, hr, nf), w(0), w(nf), vec(0), vec(nf)]
    return s, f, tc, ts, nf, tile, specs


def _ffn_gate_val(g_ref, gp_ref, v_ref, vp_ref, wg_ref, wv_ref, bg_ref, bv_ref, first):
    gate, _ = _conv_taps(g_ref[...].astype(F32), gp_ref[...].astype(F32), wg_ref[...], FFN_TAPS, first)
    val, _ = _conv_taps(v_ref[...].astype(F32), vp_ref[...].astype(F32), wv_ref[...], FFN_TAPS, first)
    return gate + bg_ref[...], val + bv_ref[...]


def ffn_act_fwd(up_pre, w8, b, name):
    s, f, tc, ts, nf, tile, specs = _ffn_specs(up_pre)

    def body(*refs):
        o_ref = refs[8]
        gate, val = _ffn_gate_val(*refs[:8], pl.program_id(1) == 0)
        o_ref[...] = (_gelu(gate) * val).astype(o_ref.dtype)

    return pl.pallas_call(
        body, name=name, grid=(nf, s // ts), in_specs=specs, out_specs=tile(0),
        out_shape=jax.ShapeDtypeStruct((s, f), MD), compiler_params=_params(("parallel", "parallel")),
    )(up_pre, up_pre, up_pre, up_pre, w8, w8, b, b)


def ffn_act_bwd(up_pre, w8, b, dact, name):
    s, f, tc, ts, nf, tile, specs = _ffn_specs(up_pre)

    def body(*refs):
        da_ref, o_ref = refs[8], refs[9]
        gate, val = _ffn_gate_val(*refs[:8], pl.program_id(1) == 0)
        ge, dge = _gelu_and_grad(gate)
        da = da_ref[...].astype(F32)
        o_ref[0] = (da * val * dge).astype(o_ref.dtype)
        o_ref[1] = (da * ge).astype(o_ref.dtype)

    return pl.pallas_call(
        body, name=name, grid=(nf, s // ts), in_specs=specs + [tile(0)],
        out_specs=pl.BlockSpec((2, ts, tc), lambda j, i: (0, i, j)), out_shape=jax.ShapeDtypeStruct((2, s, f), MD),
        compiler_params=_params(("parallel", "parallel")),
    )(up_pre, up_pre, up_pre, up_pre, w8, w8, b, b, dact)


def _adam(w, g, m, v):
    m = ADAM_B1 * m + (1.0 - ADAM_B1) * g
    v = ADAM_B2 * v + (1.0 - ADAM_B2) * (g * g)
    m_hat = m / (1.0 - ADAM_B1 ** ADAM_STEP)
    v_hat = v / (1.0 - ADAM_B2 ** ADAM_STEP)
    delta = -ADAM_LR * (m_hat / (jnp.sqrt(v_hat) + ADAM_EPS) + ADAM_WD * w)
    return delta, m, v


def _rows_tile(r, c):
    return _tile(r, max(16, (1 << 18) // c // 16 * 16), 16)


def adamw_sharded(mine, peers, w, m, v, name):
    r, c = w.shape
    tr = _rows_tile(r, c)

    def body(g_ref, p_ref, w_ref, m_ref, v_ref, go_ref, d_ref, mo_ref, vo_ref):
        grad = g_ref[...].astype(F32)
        for k in range(N_DEV - 1):
            grad = grad + p_ref[k].astype(F32)
        go_ref[...] = grad
        d_ref[...], mo_ref[...], vo_ref[...] = _adam(w_ref[...], grad, m_ref[...], v_ref[...])

    row = pl.BlockSpec((tr, c), lambda i: (i, 0))
    out = jax.ShapeDtypeStruct((r, c), F32)
    return pl.pallas_call(
        body, name=name, grid=(r // tr,),
        in_specs=[row, pl.BlockSpec((N_DEV - 1, tr, c), lambda i: (0, i, 0)), row, row, row],
        out_specs=[row] * 4, out_shape=[out] * 4, compiler_params=_params(("parallel",)),
    )(mine, peers, w, m, v)


def adamw_plain(g, w, m, v, name):
    r, c = w.shape
    tr = _tile(r, 512, 8)

    def body(g_ref, w_ref, m_ref, v_ref, d_ref, mo_ref, vo_ref):
        d_ref[...], mo_ref[...], vo_ref[...] = _adam(w_ref[...], g_ref[...], m_ref[...], v_ref[...])

    row = pl.BlockSpec((tr, c), lambda i: (i, 0))
    out = jax.ShapeDtypeStruct((r, c), F32)
    return pl.pallas_call(body, name=name, grid=(r // tr,), in_specs=[row] * 4, out_specs=[row] * 3, out_shape=[out] * 3,
                          compiler_params=_params(("parallel",)))(g, w, m, v)


def sum_devices(parts, name):
    _, r, c = parts.shape
    tr = _tile(r, 256, 8)

    def body(p_ref, o_ref):
        acc = p_ref[0]
        for d in range(1, N_DEV):
            acc = acc + p_ref[d]
        o_ref[...] = acc

    return pl.pallas_call(
        body, name=name, grid=(r // tr,), in_specs=[pl.BlockSpec((N_DEV, tr, c), lambda i: (0, i, 0))],
        out_specs=pl.BlockSpec((tr, c), lambda i: (i, 0)), out_shape=jax.ShapeDtypeStruct((r, c), F32),
        compiler_params=_params(("parallel",)))(parts)


def _place():
    return lax.axis_index("x"), lax.axis_index("y"), lax.axis_index("c")


def _other_chips(x, y):
    return [(1 - x, y), (x, 1 - y), (1 - x, 1 - y)]


_ANY = pl.BlockSpec(memory_space=pl.ANY)


def all_gather(shards, name):
    n = len(shards)

    def body(*refs):
        x_refs, out_refs = refs[:n], refs[n:2 * n]
        send_sems, recv_sems, local_sems = refs[2 * n:]
        x, y, cc = _place()
        me, sibling = (x, y, cc), (x, y, 1 - cc)
        chips = _other_chips(x, y)

        def slot(w, px, py, pc):
            return out_refs[w].at[4 * px + 2 * py + pc]

        def copy(k, w, block, to, src=None):
            return pltpu.make_async_remote_copy(
                src_ref=slot(w, *block) if src is None else src, dst_ref=slot(w, *block),
                send_sem=send_sems.at[k, w], recv_sem=recv_sems.at[k, w], device_id=to, device_id_type=MESH)

        mine = [pltpu.make_async_copy(x_refs[w], slot(w, *me), local_sems.at[w]) for w in range(n)]
        for cp in mine:
            cp.start()
        first = [copy(0, w, me, sibling, src=x_refs[w]) for w in range(n)]
        first += [copy(1 + j, w, me, (*chip, cc), src=x_refs[w]) for j, chip in enumerate(chips) for w in range(n)]
        for cp in first:
            cp.start()
        passed = []
        for j, chip in enumerate(chips):
            for w in range(n):
                copy(1 + j, w, (*chip, cc), me).wait_recv()
                passed.append(copy(4 + j, w, (*chip, cc), sibling))
                passed[-1].start()
        for w in range(n):
            copy(0, w, sibling, me).wait_recv()
        for j, chip in enumerate(chips):
            for w in range(n):
                copy(4 + j, w, (*chip, 1 - cc), me).wait_recv()
        for cp in first + passed:
            cp.wait_send()
        for cp in mine:
            cp.wait()

    return pl.pallas_call(
        body, name=name, out_shape=[jax.ShapeDtypeStruct((N_DEV, *a.shape), a.dtype) for a in shards],
        in_specs=[_ANY] * n, out_specs=[_ANY] * n,
        scratch_shapes=[pltpu.SemaphoreType.DMA((7, n)), pltpu.SemaphoreType.DMA((7, n)), pltpu.SemaphoreType.DMA((n,))],
    )(*shards)


_HBM = pl.BlockSpec(memory_space=pltpu.HBM)
_SEM = pl.BlockSpec(memory_space=pltpu.SEMAPHORE)
_DATAFLOW = pltpu.SideEffectType.DATAFLOW_SIDE_EFFECTING
_PEER_FLIPS = ((0, 0, 1), (1, 0, 0), (1, 0, 1), (0, 1, 0), (0, 1, 1), (1, 1, 0), (1, 1, 1))
N_PEERS = len(_PEER_FLIPS)


def _peer_copies(kind, src_refs, land_refs, send_sems, recv_sems):
    x, y, c = _place()
    n = len(src_refs)
    copies = []
    for w in range(n):
        for k, (fx, fy, fc) in enumerate(_PEER_FLIPS):
            px, py, pc = (1 - x if fx else x), (1 - y if fy else y), (1 - c if fc else c)
            if kind == "scatter":
                src, dst = src_refs[w].at[4 * px + 2 * py + pc], land_refs[w].at[k]
            else:
                src, dst = src_refs[w], land_refs[w].at[4 * x + 2 * y + c]
            copies.append(pltpu.make_async_remote_copy(
                src_ref=src, dst_ref=dst, send_sem=send_sems.at[k * n + w], recv_sem=recv_sems.at[k * n + w],
                device_id=(px, py, pc), device_id_type=MESH))
    return copies


def send_start(kind, srcs, name):
    n = len(srcs)
    if kind == "scatter":
        lands = [lax.empty((N_PEERS, *a.shape[1:]), a.dtype) for a in srcs]
    else:
        lands = [lax.empty((N_DEV, *a.shape), a.dtype) for a in srcs]

    def body(*refs):
        src_refs, land_refs = refs[:n], refs[n:2 * n]
        send_sems, recv_sems = refs[2 * n], refs[2 * n + 1]
        token = refs[-1]
        for cp in _peer_copies(kind, src_refs, land_refs, send_sems, recv_sems):
            cp.start()
        token[...] = jnp.zeros_like(token)

    hbm = lambda a: pltpu.with_memory_space_constraint(a, pltpu.HBM)
    outs = pl.pallas_call(
        body, name=name,
        out_shape=(pltpu.SemaphoreType.DMA((N_PEERS * n,)), pltpu.SemaphoreType.DMA((N_PEERS * n,)),
                   *[pltpu.HBM(a.shape, a.dtype) for a in srcs + lands], jax.ShapeDtypeStruct((8, LANES), F32)),
        in_specs=[_HBM] * (2 * n),
        out_specs=(_SEM, _SEM, *[_HBM] * (2 * n), pl.BlockSpec(memory_space=pltpu.VMEM)),
        input_output_aliases={i: 2 + i for i in range(2 * n)},
        compiler_params=pltpu.CompilerParams(has_side_effects=_DATAFLOW),
    )(*[hbm(a) for a in srcs + lands])
    return (kind, outs[0], outs[1], list(outs[2:2 + n]), list(outs[2 + n:2 + 2 * n])), outs[-1]


def send_wait(handle, after, name):
    kind, send_sems, recv_sems, srcs, lands = handle
    n = len(srcs)

    def body(*refs):
        src_refs, land_refs = refs[:n], refs[n:2 * n]
        for cp in _peer_copies(kind, src_refs, land_refs, refs[2 * n], refs[2 * n + 1]):
            cp.wait_send()
            cp.wait_recv()

    outs = pl.pallas_call(
        body, name=name, out_shape=tuple(pltpu.HBM(a.shape, a.dtype) for a in srcs + lands),
        in_specs=[_HBM] * (2 * n) + [_SEM, _SEM, _ANY], out_specs=tuple([_HBM] * (2 * n)),
        input_output_aliases={i: i for i in range(2 * n)},
        compiler_params=pltpu.CompilerParams(has_side_effects=_DATAFLOW),
    )(*srcs, *lands, send_sems, recv_sems, after)
    return list(outs[:n]), list(outs[n:])


class _NoExchange:
    def other_weights(self, after):
        return {}

    def send_grads(self, G, names):
        return None

    def send_small_grads(self, G):
        return None


def _after(a, token):
    return a if token is None else a + token[0:1, 0:1].astype(a.dtype)


def _step_local(x, mem, ang2, target, W, hooks=None):
    hooks = hooks or _NoExchange()
    W = dict(W)
    d = x.shape[1]
    wq = HEADS * QK_ROPE
    o_lx, o_ly, o_gm, o_gl = d + wq, 2 * d + wq, 3 * d + wq, 4 * d + wq
    one = lambda a, b, mode, dt, name, **kw: mm([(a, b)], mode, dt, name, **kw)

    h1 = rms_fwd(x, W["g_pre_mix"], "h1")
    proj = one(h1, W["w_main"], "nn", MD, "proj")
    ckv = one(h1, W["w_ckv"], "nn", MD, "ckv")
    kr = one(h1, W["w_kr"], "nn", MD, "k_rope")
    ckv_n = rms_fwd(ckv, W["g_ckv"], "ckv_n")
    kn =one(ckv_n, W["w_uk"], "nn", MD, "k_nope")
    v = one(ckv_n, W["w_uv"], "nn", MD, "v")
    qf, kf = rope_fwd(proj, kr, kn, ang2, "rope")
    o, lse2 = flash_fwd(qf, kf, v, "attn")
    W.update(hooks.other_weights(o))
    o_mla = one(o, W["w_o_mla"], "nn", F32, "o_mla")
    lru_w = (W["w_conv8"], W["b_conv_lru"], W["w_rg"], W["b_rg"], W["w_ig"], W["b_ig"], W["lru_lambda"])
    hr, hg = lru_fwd(proj, o_lx, o_ly, *lru_w, "lru")
    o_lru = one(hg, W["w_o_lru"], "nn", F32, "o_lru")
    merged = merge_fwd(proj, o_gm, o_gl, o_mla, o_lru, "merge")
    y1 = one(merged, W["w_out"], "nn", F32, "y1")
    x1, h2 = post_pre_fwd(x, y1, W["g_post_mix"], W["g_pre_x"], "x1")
    mn = rms_fwd(mem, W["g_mem"], "mem_n")
    ck = one(mn, W["w_ck"], "nn", MD, "ck")
    cv = one(mn, W["w_cv"], "nn", MD, "cv")
    cq = one(h2, W["w_cq"], "nn", MD, "cq")
    co = xattn_fwd(cq, ck, cv, "xattn")
    y2 = one(co, W["w_co"], "nn", F32, "y2")
    x2, h3 = post_pre_fwd(x1, y2, W["g_post_x"], W["g_pre_ffn"], "x2")
    up_pre = one(h3, W["w_up"], "nn", MD, "up_pre")
    act = ffn_act_fwd(up_pre, W["w_fconv8"], W["b_fconv"], "act")
    y3 = one(act, W["w_down"], "nn", F32, "y3", tk=2816)
    G = {}
    loss, dx3, dy3, G["g_post_ffn"] = loss_head(x2, y3, target, W["g_post_ffn"], "loss")

    G["w_down"] = one(act, dy3, "tn", MD, "dw_down")
    sent = hooks.send_grads(G, ("w_down",))
    dact = one(dy3, W["w_down"], "nt", MD, "dact")
    dup = ffn_act_bwd(up_pre, W["w_fconv8"], _after(W["b_fconv"], sent), dact, "dup")
    dup_pre, dwf8, G["b_fconv"] = conv_bwd(up_pre, dup, W["w_fconv8"], FFN_TAPS, 0, "ffn_conv_bwd")
    G["w_fconv"] = dwf8[:FFN_TAPS]
    G["w_up"] = one(h3, dup_pre, "tn", MD, "dw_up")
    sent = hooks.send_grads(G, ("w_up",))
    dh3 = one(dup_pre, W["w_up"], "nt", F32, "dh3", tm=512, tn=2048, tk=2816)
    dx2, dy2, G["g_pre_ffn"], G["g_post_x"] = pre_post_bwd(x2, y2, dx3, dh3, _after(W["g_pre_ffn"], sent), W["g_post_x"], "dx2")
    G["w_co"] = one(co, dy2, "tn", MD, "dw_co")
    dco = one(dy2, W["w_co"], "nt", MD, "dco")
    dcq, dck, dcv = xattn_bwd(cq, ck, cv, dco, "xattn_bwd")
    dck, dcv = dck.astype(MD), dcv.astype(MD)
    G["w_cq"] = one(h2, dcq, "tn", MD, "dw_cq")
    G["w_ck"] = one(mn, dck, "tn", MD, "dw_ck")
    G["w_cv"] = one(mn, dcv, "tn", MD, "dw_cv")
    sent = hooks.send_grads(G, ("w_co", "w_cq", "w_ck", "w_cv"))
    dmn = mm([(dck, W["w_ck"]), (dcv, W["w_cv"])], "nt", F32, "dmem_n")
    _, G["g_mem"] = rms_bwd(mem, W["g_mem"], dmn, "dg_mem")
    dh2 = one(dcq, W["w_cq"], "nt", F32, "dh2")
    dx1, dy1, G["g_pre_x"], G["g_post_mix"] = pre_post_bwd(x1, y1, dx2, dh2, _after(W["g_pre_x"], sent), W["g_post_mix"], "dx1")
    G["w_out"] = one(merged, dy1, "tn", MD, "dw_out")
    dmerged = one(dy1, W["w_out"], "nt", F32, "dmerged")
    do_mla, do_lru, dgm, dgl = merge_bwd(proj, o_gm, o_gl, o_mla, o_lru, dmerged, "merge_bwd")
    G["w_o_mla"] = one(o, do_mla, "tn", MD, "dw_o_mla")
    do = one(do_mla, W["w_o_mla"], "nt", MD, "do")
    G["w_o_lru"] = one(hg, do_lru, "tn", MD, "dw_o_lru")
    sent = hooks.send_grads(G, ("w_out", "w_o_mla", "w_o_lru"))
    dhg = one(do_lru, W["w_o_lru"], "nt", F32, "dhg")
    lru_w = lru_w[:-1] + (_after(lru_w[-1], sent),)
    dxc, dly, G["w_rg"], G["w_ig"], G["b_rg"], G["b_ig"], G["lru_lambda"] = lru_bwd(proj, o_lx, o_ly, hr, dhg, *lru_w, "lru_bwd")
    dlx, dwc8, G["b_conv_lru"] = conv_bwd(proj, dxc, W["w_conv8"], LRU_TAPS, o_lx, "lru_conv_bwd")
    G["w_conv_lru"] = dwc8[:LRU_TAPS]
    dqf, dkf, dv = flash_bwd(qf, kf, v, o, do, lse2, "attn_bwd")
    dqn, dqr, dkn, dkr = rope_bwd(dqf, dkf, ang2, "rope_bwd")
    G["w_uk"] = one(ckv_n, dkn, "tn", MD, "dw_uk")
    G["w_uv"] = one(ckv_n, dv, "tn", MD, "dw_uv")
    dckv_n = mm([(dkn, W["w_uk"]), (dv, W["w_uv"])], "nt", F32, "dckv_n")
    dckv, G["g_ckv"] = rms_bwd(ckv, W["g_ckv"], dckv_n, "dckv")
    dckv = _after(dckv, hooks.send_small_grads(G))
    dproj = jnp.concatenate([dqn, dqr, dlx, dly, dgm, dgl], axis=1)
    parts = [("w_main", dproj), ("w_ckv", dckv), ("w_kr", dkr)]
    for n, dpart in parts:
        G[n] = one(h1, dpart, "tn", MD, "d" + n)
    sent = hooks.send_grads(G, ("w_ukv", "w_in"))
    parts[-1] = ("w_kr", _after(dkr, sent))
    dh1_short = mm([(dpart, W[n]) for n, dpart in parts[1:]], "nt", F32, "dh1_short")
    dh1_short, dproj = lax.optimization_barrier((dh1_short, dproj))
    dh1 = one(dproj, W["w_main"], "nt", F32, "dh1", tm=512, tn=2048, tk=2816)
    grad_x, G["g_pre_mix"] = pre_bwd(x, dx1, dh1, dh1_short, W["g_pre_mix"], "grad_x")
    return loss, grad_x, G


FLAT_COLS = 1024
BIG = (("w_in", 1), ("w_ukv", 1), ("w_o_mla", 0), ("w_o_lru", 0), ("w_out", 0), ("w_cq", 0), ("w_ck", 0), ("w_cv", 0),
       ("w_co", 1), ("w_up", 1), ("w_down", 0))
SMALL_SHARDED = ("w_conv_lru", "w_fconv")
REPLICATED = ("g_pre_mix", "g_post_mix", "g_ckv", "b_conv_lru", "w_rg", "b_rg", "w_ig", "b_ig", "lru_lambda",
              "g_pre_x", "g_post_x", "g_mem", "g_pre_ffn", "g_post_ffn", "b_fconv")
WEIGHTS = ("g_pre_mix", "g_post_mix", "w_in", "g_ckv", "w_ukv", "w_o_mla", "w_conv_lru", "b_conv_lru", "w_rg", "b_rg",
           "w_ig", "b_ig", "lru_lambda", "w_o_lru", "w_out", "g_pre_x", "g_post_x", "g_mem", "w_cq", "w_ck", "w_cv",
           "w_co", "g_pre_ffn", "g_post_ffn", "w_up", "w_fconv", "b_fconv", "w_down")


def _round_up(n, k):
    return (n + k - 1) // k * k


def _pack_vec(parts, row_mult):
    flat = jnp.concatenate([p.reshape(-1) for p in parts])
    n = flat.shape[0]
    total = _round_up(n, row_mult * FLAT_COLS)
    return jnp.pad(flat, (0, total - n)).reshape(-1, FLAT_COLS)


def _unpack_vec(flat2d, shapes):
    flat = flat2d.reshape(-1)
    out, off = [], 0
    for shp in shapes:
        n = int(np.prod(shp))
        out.append(flat[off:off + n].reshape(shp))
        off += n
    return out


def _gathered_to_full(blocks, axis):
    n, r, c = blocks.shape
    return blocks.reshape(n * r, c) if axis == 0 else blocks.transpose(1, 0, 2).reshape(r, n * c)


def _full_to_shards(full, axis):
    r, c = full.shape
    if axis == 0:
        return full.reshape(N_DEV, r // N_DEV, c)
    return full.reshape(r, N_DEV, c // N_DEV).transpose(1, 0, 2)


def _split_w_in(w_in, d):
    hq = HEADS * (QK_NOPE + QK_ROPE)
    q = w_in[:, :hq].reshape(d, HEADS, QK_NOPE + QK_ROPE)
    q_nope, q_rope = q[:, :, :QK_NOPE].reshape(d, HEADS * QK_NOPE), q[:, :, QK_NOPE:].reshape(d, HEADS * QK_ROPE)
    out = {"w_ckv": w_in[:, hq:hq + KV_RANK]}
    off = hq + KV_RANK
    out["w_kr"] = jnp.pad(w_in[:, off:off + QK_ROPE], ((0, 0), (0, LANES - QK_ROPE)))
    out["w_main"] = jnp.concatenate([q_nope, q_rope, w_in[:, off + QK_ROPE:]], axis=1)
    return out


def _join_w_in(G, d):
    gm = G["w_main"]
    n_qn, n_qr = HEADS * QK_NOPE, HEADS * QK_ROPE
    q = jnp.concatenate([gm[:, :n_qn].reshape(d, HEADS, QK_NOPE), gm[:, n_qn:n_qn + n_qr].reshape(d, HEADS, QK_ROPE)], axis=2)
    return jnp.concatenate([q.reshape(d, -1), G["w_ckv"], G["w_kr"][:, :QK_ROPE], gm[:, n_qn + n_qr:]], axis=1)


def kernel(x, mem, positions, g_pre_mix, g_post_mix, w_in, g_ckv, w_ukv, w_o_mla, w_conv_lru, b_conv_lru, w_rg, b_rg, w_ig, b_ig, lru_lambda, w_o_lru, w_out, g_pre_x, g_post_x, g_mem, w_cq, w_ck, w_cv, w_co, g_pre_ffn, g_post_ffn, w_up, w_fconv, b_fconv, w_down, loss_target, m_g_pre_mix, m_g_post_mix, m_w_in, m_g_ckv, m_w_ukv, m_w_o_mla, m_w_conv_lru, m_b_conv_lru, m_w_rg, m_b_rg, m_w_ig, m_b_ig, m_lru_lambda, m_w_o_lru, m_w_out, m_g_pre_x, m_g_post_x, m_g_mem, m_w_cq, m_w_ck, m_w_cv, m_w_co, m_g_pre_ffn, m_g_post_ffn, m_w_up, m_w_fconv, m_b_fconv, m_w_down, v_g_pre_mix, v_g_post_mix, v_w_in, v_g_ckv, v_w_ukv, v_w_o_mla, v_w_conv_lru, v_b_conv_lru, v_w_rg, v_b_rg, v_w_ig, v_b_ig, v_lru_lambda, v_w_o_lru, v_w_out, v_g_pre_x, v_g_post_x, v_g_mem, v_w_cq, v_w_ck, v_w_cv, v_w_co, v_g_pre_ffn, v_g_post_ffn, v_w_up, v_w_fconv, v_b_fconv, v_w_down):
    given = dict(locals())
    P = {n: given[n][0] for n in WEIGHTS}
    M = {n: given["m_" + n][0] for n in WEIGHTS}
    V = {n: given["v_" + n][0] for n in WEIGHTS}
    xs, mems, tgt = x[0], mem[0], loss_target[0]
    s, d = xs.shape
    ax, ay, ac = _place()
    dev = 4 * ax + 2 * ay + ac

    axis_of = dict(BIG)
    axis_of.update({n: 1 for n in SMALL_SHARDED})
    wire = lambda n: P[n] if n in SMALL_SHARDED else P[n].astype(MD)
    gathered_in, gathered_kv = all_gather([wire("w_in"), wire("w_ukv")], "gather_first")
    later_names = ("w_o_mla", "w_conv_lru", "w_o_lru", "w_out", "w_ck", "w_cv", "w_cq", "w_co", "w_up", "w_fconv", "w_down")
    gathered_in, later_srcs = lax.optimization_barrier((gathered_in, [wire(n) for n in later_names]))
    later_handle, started_later = send_start("gather", later_srcs, "start_gather_rest")

    def landed(names, handle, after, name):
        srcs, lands = send_wait(handle, after, name)
        full = {}
        for n, src, land in zip(names, srcs, lands):
            full[n] = _gathered_to_full(lax.dynamic_update_slice(land, src[None], (dev, 0, 0)), axis_of[n])
        return full

    W = _split_w_in(_gathered_to_full(gathered_in, 1), d)
    W["w_rg"], W["w_ig"] = P["w_rg"].astype(MD), P["w_ig"].astype(MD)
    for n in ("g_pre_mix", "g_post_mix", "g_ckv", "b_conv_lru", "b_rg", "b_ig", "lru_lambda", "g_pre_x", "g_post_x", "g_mem",
              "g_pre_ffn", "g_post_ffn", "b_fconv"):
        W[n] = P[n].reshape(1, -1)
    kv = _gathered_to_full(gathered_kv, 1).reshape(KV_RANK, HEADS, QK_NOPE + V_DIM)
    W["w_uk"] = kv[:, :, :QK_NOPE].reshape(KV_RANK, HEADS * QK_NOPE)
    W["w_uv"] = kv[:, :, QK_NOPE:].reshape(KV_RANK, HEADS * V_DIM)
    W["g_pre_mix"] = _after(W["g_pre_mix"], started_later)

    inv_freq = ROPE_THETA ** (-jnp.arange(0, QK_ROPE, 2, dtype=F32) / QK_ROPE)
    ang = positions[0].astype(F32)[:, None] * inv_freq
    ang2 = jnp.tile(ang, (1, LANES // (QK_ROPE // 2)))

    pending = []
    small_pending = []
    gate_names = ("w_rg", "w_ig")
    small_names = tuple(n for n in REPLICATED + SMALL_SHARDED if n not in gate_names)
    early_small = tuple(n for n in small_names if n != "g_pre_mix")
    as_rows = lambda a: a.reshape(-1, LANES)

    class Exchange:
        def other_weights(self, after):
            full = landed(later_names, later_handle, after, "wait_gather_rest")
            full["w_conv8"] = jnp.pad(full.pop("w_conv_lru"), ((0, 8 - LRU_TAPS), (0, 0)))
            full["w_fconv8"] = jnp.pad(full.pop("w_fconv"), ((0, 8 - FFN_TAPS), (0, 0)))
            return full

        def send_grads(self, G, names):
            if "w_in" in names:
                G["w_in"] = _join_w_in(G, d)
                G["w_ukv"] = jnp.concatenate([G["w_uk"].reshape(KV_RANK, HEADS, QK_NOPE),
                                              G["w_uv"].reshape(KV_RANK, HEADS, V_DIM)], axis=2).reshape(KV_RANK, -1)
            handle, started = send_start("scatter", [_full_to_shards(G[n], axis_of[n]) for n in names], "start_grads_" + names[0])
            pending.append((names, handle))
            return started

        def send_small_grads(self, G):
            srcs = [_pack_vec([G[n] for n in early_small], 8)] + [as_rows(G[n]) for n in gate_names]
            handle, started = send_start("gather", srcs, "start_small_grads")
            small_pending.append(handle)
            return started

    loss, grad_x, G = _step_local(xs, mems, ang2, tgt, W, Exchange())
    loss = lax.psum(loss[0, 0], ("x", "y", "c"))

    res = {k: {} for k in ("grad", "delta", "m", "v")}
    for names, handle in pending:
        slabs, peers = send_wait(handle, grad_x, "wait_grads_" + names[0])
        for n, slab, peer in zip(names, slabs, peers):
            mine = lax.dynamic_index_in_dim(slab, dev, axis=0, keepdims=False)
            outs = adamw_sharded(mine, peer, P[n], M[n], V[n], "adamw_" + n)
            for kind, o in zip(("grad", "delta", "m", "v"), outs):
                res[kind][n] = o

    owns, lands = send_wait(small_pending[0], grad_x, "wait_small_grads")
    sums = [sum_devices(lax.dynamic_update_slice(land, own[None], (dev, 0, 0)), "sum_small_grads_%d" % k)
            for k, (own, land) in enumerate(zip(owns, lands))]
    summed = dict(zip(early_small, _unpack_vec(sums[0], [G[n].shape for n in early_small])))
    for n, total in zip(gate_names, sums[1:]):
        outs = adamw_plain(total, as_rows(P[n]), as_rows(M[n]), as_rows(V[n]), "adamw_" + n)
        for kind, o in zip(("grad", "delta", "m", "v"), (total,) + tuple(outs)):
            res[kind][n] = o.reshape(P[n].shape)
    last, = all_gather([_pack_vec([G["g_pre_mix"]], 8)], "gather_last_grad")
    summed["g_pre_mix"], = _unpack_vec(sum_devices(last, "sum_last_grad"), [G["g_pre_mix"].shape])
    for n in SMALL_SHARDED:
        t, c = summed[n].shape
        summed[n] = lax.dynamic_index_in_dim(summed[n].reshape(t, N_DEV, c // N_DEV), dev, axis=1, keepdims=False)
    sg = {n: summed[n].reshape(P[n].shape) for n in small_names}
    vec = lambda src: _pack_vec([src[n] for n in small_names], 8)
    small_out = adamw_plain(vec(sg), vec(P), vec(M), vec(V), "adamw_small")
    shapes = [P[n].shape for n in small_names]
    for kind, flat in zip(("delta", "m", "v"), small_out):
        res[kind].update(zip(small_names, _unpack_vec(flat, shapes)))
    res["grad"].update(sg)

    outs = [loss, grad_x[None]]
    for kind in ("grad", "delta", "m", "v"):
        outs += [res[kind][n][None] for n in WEIGHTS]
    return tuple(outs)
```
